```python
import jax, jax.numpy as jnp
from jax import lax
import numpy as np

D_MODEL = 2048
BATCH = 1
SEQ = 8192
DEPTH = 1
DEC_BATCH = 32
DEC_SEQ = 4
PAST_LEN = 8192
PAGE_SIZE = 128

HEAD_DIM = 128
HEADS_PER_GROUP = 4
DIL_GROUPS = ((128, 1), (512, 4), (2048, 16))
N_GROUPS = len(DIL_GROUPS)
ATT_HEADS = HEADS_PER_GROUP * N_GROUPS
ATT_W = ATT_HEADS * HEAD_DIM
ATT_OUT_W = HEADS_PER_GROUP * HEAD_DIM
CONV_W = D_MODEL // 2
CONV_K = 3
MEM_TOKENS = 256
MEM_HEADS = 4
MEM_W = MEM_HEADS * HEAD_DIM
D_FF = 5632
BAND = 128
IN_SPLITS = (CONV_W, CONV_W, CONV_W, ATT_W, ATT_W, ATT_W, MEM_W, D_MODEL, D_MODEL, D_MODEL)
IN_W = sum(IN_SPLITS)
ALPHA = (2.0 * DEPTH) ** 0.25
BETA = (8.0 * DEPTH) ** -0.25
LN_EPS = 1e-5
NEG_INF = -1e30

kernel_name = "hybrid_conv_dilated_attn_memory_step"


def window_buffer_len(win):
    return min(win, PAST_LEN)


def alibi_slopes():
    h = jnp.arange(1, ATT_HEADS + 1, dtype=jnp.float32)
    return jnp.exp2(-8.0 * h / ATT_HEADS).reshape(N_GROUPS, HEADS_PER_GROUP)


def layer_norm(x, g, b):
    xf = x.astype(jnp.float32)
    mu = jnp.mean(xf, axis=-1, keepdims=True)
    var = jnp.mean(jnp.square(xf - mu), axis=-1, keepdims=True)
    y = (xf - mu) * lax.rsqrt(var + LN_EPS) * g.astype(jnp.float32) + b.astype(jnp.float32)
    return y.astype(x.dtype)


def swiglu(h, w_gu, w_down):
    gate, up = jnp.split(h @ w_gu, 2, axis=-1)
    return (jax.nn.silu(gate) * up) @ w_down


def in_projection(h, w_in):
    return jnp.split(h @ w_in, np.cumsum(IN_SPLITS)[:-1].tolist(), axis=-1)


def split_heads(a, n):
    return a.reshape(*a.shape[:-1], n, HEAD_DIM)


def causal_conv3(u_ext, w):
    t = u_ext.shape[1] - (CONV_K - 1)
    return w[0] * u_ext[:, :t] + w[1] * u_ext[:, 1:t + 1] + w[2] * u_ext[:, 2:]


def dilated_attention_prompt(q, k, v, win, dil, slopes):
    b, s, h, e = q.shape
    n_sub = s // dil
    n_blk = -(-n_sub // BAND)
    pad = n_blk * BAND - n_sub

    def to_blocks(a):
        a = a.reshape(b, n_sub, dil, h, e).transpose(0, 2, 1, 3, 4)
        a = jnp.pad(a, ((0, 0), (0, 0), (0, pad), (0, 0), (0, 0)))
        return a.reshape(b, dil, n_blk, BAND, h, e)

    def with_prev(a):
        prev = jnp.pad(a, ((0, 0), (0, 0), (1, 0), (0, 0), (0, 0), (0, 0)))[:, :, :-1]
        return jnp.concatenate([prev, a], axis=3)

    qb = to_blocks(q)
    kk, vv = with_prev(to_blocks(k)), with_prev(to_blocks(v))
    scores = jnp.einsum("brnqhe,brnkhe->brnhqk", qb, kk).astype(jnp.float32) * (HEAD_DIM ** -0.5)
    qi = jnp.arange(BAND)[:, None]
    ki = jnp.arange(2 * BAND)[None, :]
    rel = BAND + qi - ki
    blk = jnp.arange(n_blk)[:, None, None]
    valid = (rel >= 0) & (rel <= win // dil) & ((blk > 0) | (ki >= BAND))
    bias = -slopes[:, None, None] * (rel * dil).astype(jnp.float32)
    scores = jnp.where(valid[:, None], scores + bias, NEG_INF)
    lse = jax.nn.logsumexp(scores, axis=-1)
    p = jnp.exp(scores - lse[..., None])
    o = jnp.einsum("brnhqk,brnkhe->brnqhe", p.astype(v.dtype), vv)
    o = o.reshape(b, dil, n_blk * BAND, h, e)[:, :, :n_sub].transpose(0, 2, 1, 3, 4).reshape(b, s, h, e)
    lse = lse.transpose(0, 1, 2, 4, 3).reshape(b, dil, n_blk * BAND, h)[:, :, :n_sub]
    lse = lse.transpose(0, 2, 1, 3).reshape(b, s, h)
    return o, lse


def dilated_attention_sample(q, k_all, v_all, n_past, win, dil, slopes):
    t = q.shape[1]
    j = jnp.arange(win // dil + 1)
    idx = n_past + jnp.arange(t)[:, None] - j[None, :] * dil
    valid = idx >= 0
    idx = jnp.maximum(idx, 0)
    kg = jnp.take(k_all, idx, axis=1)
    vg = jnp.take(v_all, idx, axis=1)
    scores = jnp.einsum("bthe,btjhe->bhtj", q, kg).astype(jnp.float32) * (HEAD_DIM ** -0.5)
    bias = -slopes[:, None, None] * (j * dil).astype(jnp.float32)[None, None, :]
    scores = jnp.where(valid[None, None], scores + bias, NEG_INF)
    lse = jax.nn.logsumexp(scores, axis=-1)
    p = jnp.exp(scores - lse[..., None])
    o = jnp.einsum("bhtj,btjhe->bthe", p.astype(v_all.dtype), vg)
    return o, lse.transpose(0, 2, 1)


def combine_groups(outs, lses):
    w = jax.nn.softmax(jnp.stack(lses), axis=0)
    o = jnp.stack(outs)
    out = jnp.sum(w[..., None].astype(o.dtype) * o, axis=0)
    return out.reshape(*out.shape[:2], ATT_OUT_W)


def memory_attention(qm, mk, mv):
    q = split_heads(qm, MEM_HEADS)
    scores = jnp.einsum("bthe,bmhe->bhtm", q, mk).astype(jnp.float32) * (HEAD_DIM ** -0.5)
    p = jax.nn.softmax(scores, axis=-1)
    o = jnp.einsum("bhtm,bmhe->bthe", p.astype(mv.dtype), mv)
    return o.reshape(*o.shape[:2], MEM_W)


def memory_kv(mem, w_mem_kv):
    mk, mv = jnp.split(mem @ w_mem_kv, 2, axis=-1)
    return split_heads(mk, MEM_HEADS), split_heads(mv, MEM_HEADS)


def merge_branches(o_conv, o_attn, o_mem, g_conv, g_attn, g_mem, lw):
    m = (jax.nn.sigmoid(g_conv) * (o_conv @ lw["w_br_conv"])
         + jax.nn.sigmoid(g_attn) * (o_attn @ lw["w_br_attn"])
         + jax.nn.sigmoid(g_mem) * (o_mem @ lw["w_br_mem"]))
    return m @ lw["w_o"]


def ffn_sublayer(h, w_gu, w_down, g, b):
    return layer_norm(ALPHA * h + 0.5 * swiglu(h, w_gu, w_down), g, b)


def layer_prompt(h, mem, lw):
    h = ffn_sublayer(h, lw["w_ffn1_gu"], lw["w_ffn1_down"], lw["ln1_g"], lw["ln1_b"])
    bg, cg, vc, q, k, v, qm, g_conv, g_attn, g_mem = in_projection(h, lw["w_in"])
    u_ext = jnp.pad(cg * vc, ((0, 0), (CONV_K - 1, 0), (0, 0)))
    o_conv = bg * causal_conv3(u_ext, lw["conv_w"])
    conv_state = u_ext[:, -(CONV_K - 1):]
    q, k, v = split_heads(q, ATT_HEADS), split_heads(k, ATT_HEADS), split_heads(v, ATT_HEADS)
    slopes = alibi_slopes()
    outs, lses, win_states = [], [], []
    for g, (win, dil) in enumerate(DIL_GROUPS):
        sl = slice(g * HEADS_PER_GROUP, (g + 1) * HEADS_PER_GROUP)
        o, lse = dilated_attention_prompt(q[:, :, sl], k[:, :, sl], v[:, :, sl], win, dil, slopes[g])
        outs.append(o)
        lses.append(lse)
        keep = min(win, h.shape[1])
        win_states += [k[:, -keep:, sl], v[:, -keep:, sl]]
    o_attn = combine_groups(outs, lses)
    mk, mv = memory_kv(mem, lw["w_mem_kv"])
    o_mem = memory_attention(qm, mk, mv)
    h = layer_norm(ALPHA * h + merge_branches(o_conv, o_attn, o_mem, g_conv, g_attn, g_mem, lw),
                   lw["ln2_g"], lw["ln2_b"])
    h = ffn_sublayer(h, lw["w_ffn2_gu"], lw["w_ffn2_down"], lw["ln3_g"], lw["ln3_b"])
    return h, (conv_state, *win_states, mk, mv)


def layer_sample(h, conv_state, buf_k, buf_v, mem_k, mem_v, lw):
    h = ffn_sublayer(h, lw["w_ffn1_gu"], lw["w_ffn1_down"], lw["ln1_g"], lw["ln1_b"])
    bg, cg, vc, q, k, v, qm, g_conv, g_attn, g_mem = in_projection(h, lw["w_in"])
    u_ext = jnp.concatenate([conv_state.astype(vc.dtype), cg * vc], axis=1)
    o_conv = bg * causal_conv3(u_ext, lw["conv_w"])
    new_conv = u_ext[:, -(CONV_K - 1):]
    q, k, v = split_heads(q, ATT_HEADS), split_heads(k, ATT_HEADS), split_heads(v, ATT_HEADS)
    slopes = alibi_slopes()
    outs, lses, new_win = [], [], []
    for g, (win, dil) in enumerate(DIL_GROUPS):
        sl = slice(g * HEADS_PER_GROUP, (g + 1) * HEADS_PER_GROUP)
        k_all = jnp.concatenate([buf_k[g].astype(k.dtype), k[:, :, sl]], axis=1)
        v_all = jnp.concatenate([buf_v[g].astype(v.dtype), v[:, :, sl]], axis=1)
        o, lse = dilated_attention_sample(q[:, :, sl], k_all, v_all, buf_k[g].shape[1], win, dil, slopes[g])
        outs.append(o)
        lses.append(lse)
        keep = min(win, k_all.shape[1])
        new_win += [k_all[:, -keep:], v_all[:, -keep:]]
    o_attn = combine_groups(outs, lses)
    o_mem = memory_attention(qm, mem_k.astype(qm.dtype), mem_v.astype(qm.dtype))
    h = layer_norm(ALPHA * h + merge_branches(o_conv, o_attn, o_mem, g_conv, g_attn, g_mem, lw),
                   lw["ln2_g"], lw["ln2_b"])
    h = ffn_sublayer(h, lw["w_ffn2_gu"], lw["w_ffn2_down"], lw["ln3_g"], lw["ln3_b"])
    return h, (new_conv, *new_win)


def setup_inputs(seed: int = 0) -> dict:
    key = jax.random.key(seed)
    ks = iter(jax.random.split(key, 40))

    def nrm(shape, scale):
        return jax.random.normal(next(ks), shape, jnp.float32) * scale

    def gain():
        return 1.0 + nrm((DEPTH, D_MODEL), 0.02)

    def bias():
        return nrm((DEPTH, D_MODEL), 0.02)

    lw0, lw1, lw2 = [window_buffer_len(w) for w, _ in DIL_GROUPS]
    hkv = (HEADS_PER_GROUP, HEAD_DIM)
    return {
        "x_prompt": nrm((BATCH, SEQ, D_MODEL), 1.0),
        "x_sample": nrm((DEC_BATCH, DEC_SEQ, D_MODEL), 1.0),
        "mem_prompt": nrm((BATCH, MEM_TOKENS, D_MODEL), 1.0),
        "state_conv": nrm((DEPTH, DEC_BATCH, CONV_K - 1, CONV_W), 1.0),
        "state_k_w128": nrm((DEPTH, DEC_BATCH, lw0) + hkv, 1.0),
        "state_v_w128": nrm((DEPTH, DEC_BATCH, lw0) + hkv, 1.0),
        "state_k_w512": nrm((DEPTH, DEC_BATCH, lw1) + hkv, 1.0),
        "state_v_w512": nrm((DEPTH, DEC_BATCH, lw1) + hkv, 1.0),
        "state_k_w2048": nrm((DEPTH, DEC_BATCH, lw2) + hkv, 1.0),
        "state_v_w2048": nrm((DEPTH, DEC_BATCH, lw2) + hkv, 1.0),
        "cache_mem_k": nrm((DEPTH, DEC_BATCH, MEM_TOKENS, MEM_HEADS, HEAD_DIM), 1.0),
        "cache_mem_v": nrm((DEPTH, DEC_BATCH, MEM_TOKENS, MEM_HEADS, HEAD_DIM), 1.0),
        "ln1_g": gain(),
        "ln1_b": bias(),
        "w_ffn1_gu": nrm((DEPTH, D_MODEL, 2 * D_FF), D_MODEL ** -0.5),
        "w_ffn1_down": nrm((DEPTH, D_FF, D_MODEL), BETA * D_FF ** -0.5),
        "w_in": nrm((DEPTH, D_MODEL, IN_W), D_MODEL ** -0.5),
        "conv_w": nrm((DEPTH, CONV_K, CONV_W), CONV_K ** -0.5),
        "w_mem_kv": nrm((DEPTH, D_MODEL, 2 * MEM_W), D_MODEL ** -0.5),
        "w_br_conv": nrm((DEPTH, CONV_W, D_MODEL), BETA * CONV_W ** -0.5),
        "w_br_attn": nrm((DEPTH, ATT_OUT_W, D_MODEL), BETA * ATT_OUT_W ** -0.5),
        "w_br_mem": nrm((DEPTH, MEM_W, D_MODEL), BETA * MEM_W ** -0.5),
        "w_o": nrm((DEPTH, D_MODEL, D_MODEL), BETA * D_MODEL ** -0.5),
        "ln2_g": gain(),
        "ln2_b": bias(),
        "w_ffn2_gu": nrm((DEPTH, D_MODEL, 2 * D_FF), D_MODEL ** -0.5),
        "w_ffn2_down": nrm((DEPTH, D_FF, D_MODEL), BETA * D_FF ** -0.5),
        "ln3_g": gain(),
        "ln3_b": bias(),
    }


def reference(x_prompt, x_sample, mem_prompt, state_conv, state_k_w128, state_v_w128,
              state_k_w512, state_v_w512, state_k_w2048, state_v_w2048, cache_mem_k, cache_mem_v,
              ln1_g, ln1_b, w_ffn1_gu, w_ffn1_down, w_in, conv_w, w_mem_kv, w_br_conv, w_br_attn,
              w_br_mem, w_o, ln2_g, ln2_b, w_ffn2_gu, w_ffn2_down, ln3_g, ln3_b):
    hp, hs = x_prompt, x_sample
    p_states, s_states = [], []
    for l in range(DEPTH):
        lw = {"ln1_g": ln1_g[l], "ln1_b": ln1_b[l], "w_ffn1_gu": w_ffn1_gu[l],
              "w_ffn1_down": w_ffn1_down[l], "w_in": w_in[l], "conv_w": conv_w[l],
              "w_mem_kv": w_mem_kv[l], "w_br_conv": w_br_conv[l], "w_br_attn": w_br_attn[l],
              "w_br_mem": w_br_mem[l], "w_o": w_o[l], "ln2_g": ln2_g[l], "ln2_b": ln2_b[l],
              "w_ffn2_gu": w_ffn2_gu[l], "w_ffn2_down": w_ffn2_down[l],
              "ln3_g": ln3_g[l], "ln3_b": ln3_b[l]}
        hp, st_p = layer_prompt(hp, mem_prompt, lw)
        p_states.append(st_p)
        hs, st_s = layer_sample(hs, state_conv[l],
                                (state_k_w128[l], state_k_w512[l], state_k_w2048[l]),
                                (state_v_w128[l], state_v_w512[l], state_v_w2048[l]),
                                cache_mem_k[l], cache_mem_v[l], lw)
        s_states.append(st_s)
    (conv_p, k128_p, v128_p, k512_p, v512_p, k2048_p, v2048_p, mem_k_p, mem_v_p) = [
        jnp.stack(z) for z in zip(*p_states)]
    (conv_s, k128_s, v128_s, k512_s, v512_s, k2048_s, v2048_s) = [
        jnp.stack(z) for z in zip(*s_states)]
    return (hp, hs, conv_p, k128_p, v128_p, k512_p, v512_p, k2048_p, v2048_p, mem_k_p, mem_v_p,
            conv_s, k128_s, v128_s, k512_s, v512_s, k2048_s, v2048_s)
```

```python
import functools

import jax
import jax.numpy as jnp
import numpy as np
from jax import lax
from jax.experimental import pallas as pl
from jax.experimental.pallas import tpu as pltpu

F32 = jnp.float32
BF16 = jnp.bfloat16

D_MODEL = 2048
D_FF = 5632
HEAD_DIM = 128
HEADS = 4
GROUP_W = HEADS * HEAD_DIM
DIL_GROUPS = ((128, 1), (512, 4), (2048, 16))
BAND = 128
CONV_W = 1024
MEM_TOKENS = 256
IN_W = 14336
COL_B, COL_C, COL_V = 0, 1024, 2048
COL_Q, COL_K, COL_VA = 3072, 4608, 6144
COL_QM = 7680
COL_GC, COL_GA, COL_GM = 8192, 10240, 12288
ALPHA = 2.0 ** 0.25
LN_EPS = 1e-5
NEG_INF = -1e30
SCALE = HEAD_DIM ** -0.5
SLOPES = np.exp2(np.float32(-8.0) * np.arange(1, 13, dtype=np.float32) / np.float32(12)).reshape(3, 4)

SAMPLE_PAD = 8
VMEM_LIMIT = 56 * 1024 * 1024


def _params(n_axes, vmem=VMEM_LIMIT):
    return pltpu.CompilerParams(dimension_semantics=("arbitrary",) * n_axes, vmem_limit_bytes=vmem)


def _layer_norm_rows(z, g, b):
    mu = jnp.mean(z, axis=-1, keepdims=True)
    zc = z - mu
    var = jnp.mean(zc * zc, axis=-1, keepdims=True)
    return zc * lax.rsqrt(var + LN_EPS) * g + b


def _ffn_kernel(x_ref, wg_ref, wu_ref, wd_ref, g_ref, b_ref, y_ref, yb_ref, xb_ref, *, n_ff, row_chunk):
    j = pl.program_id(1)

    @pl.when(j == 0)
    def _():
        xb_ref[...] = x_ref[...].astype(BF16)

    xb = xb_ref[...]
    gate = jnp.dot(xb, wg_ref[...], preferred_element_type=F32)
    up = jnp.dot(xb, wu_ref[...], preferred_element_type=F32)
    act = (gate * jax.nn.sigmoid(gate) * up).astype(BF16)
    part = jnp.dot(act, wd_ref[...], preferred_element_type=F32)

    @pl.when(j == 0)
    def _():
        y_ref[...] = part

    @pl.when(j > 0)
    def _():
        y_ref[...] += part

    @pl.when(j == n_ff - 1)
    def _():
        def chunk(c, carry):
            rows = pl.ds(pl.multiple_of(c * row_chunk, row_chunk), row_chunk)
            z = ALPHA * x_ref[rows, :] + 0.5 * y_ref[rows, :]
            y = _layer_norm_rows(z, g_ref[...], b_ref[...])
            y_ref[rows, :] = y
            yb_ref[rows, :] = y.astype(BF16)
            return carry
        lax.fori_loop(0, x_ref.shape[0] // row_chunk, chunk, 0)


def _ffn(x, w_gu, w_down, g, b, *, tm, tf):
    rows = x.shape[0]
    n_ff = D_FF // tf
    single = pl.Buffered(1)
    return pl.pallas_call(
        functools.partial(_ffn_kernel, n_ff=n_ff, row_chunk=min(tm, 128)),
        grid=(rows // tm, n_ff),
        in_specs=[
            pl.BlockSpec((tm, D_MODEL), lambda i, j: (i, 0), pipeline_mode=single),
            pl.BlockSpec((D_MODEL, tf), lambda i, j: (0, j)),
            pl.BlockSpec((D_MODEL, tf), lambda i, j: (0, j + n_ff)),
            pl.BlockSpec((tf, D_MODEL), lambda i, j: (j, 0)),
            pl.BlockSpec((1, D_MODEL), lambda i, j: (0, 0)),
            pl.BlockSpec((1, D_MODEL), lambda i, j: (0, 0)),
        ],
        out_specs=[
            pl.BlockSpec((tm, D_MODEL), lambda i, j: (i, 0), pipeline_mode=single),
            pl.BlockSpec((tm, D_MODEL), lambda i, j: (i, 0), pipeline_mode=single),
        ],
        out_shape=[jax.ShapeDtypeStruct((rows, D_MODEL), F32),
                   jax.ShapeDtypeStruct((rows, D_MODEL), BF16)],
        scratch_shapes=[pltpu.VMEM((tm, D_MODEL), BF16)],
        compiler_params=_params(2),
        name="ffn",
    )(x, w_gu, w_gu, w_down, g, b)


def _mm_kernel(x_ref, w_ref, o_ref):
    o_ref[...] = jnp.dot(x_ref[...], w_ref[...], preferred_element_type=F32).astype(o_ref.dtype)


def _matmul(x, w, *, tm, tn, n_cols, out_dtype, row_blk0=0, n_rows=None, col_blk0=0, name="mm"):
    k = x.shape[1]
    n_rows = x.shape[0] if n_rows is None else n_rows
    return pl.pallas_call(
        _mm_kernel,
        grid=(n_rows // tm, n_cols // tn),
        in_specs=[pl.BlockSpec((tm, k), lambda i, j: (i + row_blk0, 0)),
                  pl.BlockSpec((k, tn), lambda i, j: (0, j + col_blk0))],
        out_specs=pl.BlockSpec((tm, tn), lambda i, j: (i, j)),
        out_shape=jax.ShapeDtypeStruct((n_rows, n_cols), out_dtype),
        compiler_params=_params(2),
        name=name,
    )(x, w)


def _band_attn_kernel(q_ref, kp_ref, kc_ref, vp_ref, vc_ref, o_ref, l_ref, *, dil, slopes):
    n = pl.program_id(1)
    qi = lax.broadcasted_iota(jnp.int32, (BAND, 2 * BAND), 0)
    ki = lax.broadcasted_iota(jnp.int32, (BAND, 2 * BAND), 1)
    rel = BAND + qi - ki
    k_min = jnp.where(n > 0, 0, BAND)
    valid = (rel >= 0) & (rel <= BAND) & (ki >= k_min)
    dist = (rel * dil).astype(F32)
    for h in range(HEADS):
        hs = slice(h * HEAD_DIM, (h + 1) * HEAD_DIM)
        q = q_ref[:, hs]
        k = jnp.concatenate([kp_ref[:, hs], kc_ref[:, hs]], axis=0)
        v = jnp.concatenate([vp_ref[:, hs], vc_ref[:, hs]], axis=0)
        s = lax.dot_general(q, k, (((1,), (1,)), ((), ())), preferred_element_type=F32) * SCALE
        s = jnp.where(valid, s - slopes[h] * dist, NEG_INF)
        m = jnp.max(s, axis=-1, keepdims=True)
        p = jnp.exp(s - m)
        l = jnp.sum(p, axis=-1, keepdims=True)
        o = jnp.dot(p.astype(BF16), v, preferred_element_type=F32) / l
        o_ref[:, hs] = o
        l_ref[:, hs] = jnp.broadcast_to(m + jnp.log(l), (BAND, HEAD_DIM))


def _band_attention(p_all, g):
    win, dil = DIL_GROUPS[g]
    seq = p_all.shape[0]
    n_sub = seq // dil
    n_blk = n_sub // BAND
    cols = IN_W // GROUP_W
    pv = p_all.reshape(n_sub, dil * IN_W)
    qb, kb, vb = COL_Q // GROUP_W + g, COL_K // GROUP_W + g, COL_VA // GROUP_W + g
    blk = (BAND, GROUP_W)
    out = pl.pallas_call(
        functools.partial(_band_attn_kernel, dil=dil, slopes=tuple(float(s) for s in SLOPES[g])),
        grid=(dil, n_blk),
        in_specs=[
            pl.BlockSpec(blk, lambda r, n: (n, r * cols + qb)),
            pl.BlockSpec(blk, lambda r, n: (jnp.maximum(n - 1, 0), r * cols + kb)),
            pl.BlockSpec(blk, lambda r, n: (n, r * cols + kb)),
            pl.BlockSpec(blk, lambda r, n: (jnp.maximum(n - 1, 0), r * cols + vb)),
            pl.BlockSpec(blk, lambda r, n: (n, r * cols + vb)),
        ],
        out_specs=[pl.BlockSpec(blk, lambda r, n: (n, r)), pl.BlockSpec(blk, lambda r, n: (n, r))],
        out_shape=[jax.ShapeDtypeStruct((n_sub, dil * GROUP_W), F32)] * 2,
        compiler_params=_params(2),
        name=f"band_attn_g{g}",
    )(pv, pv, pv, pv, pv)
    return out[0].reshape(seq, GROUP_W), out[1].reshape(seq, GROUP_W)


def _mem_attn_kernel(q_ref, mk_ref, mv_ref, o_ref):
    for h in range(HEADS):
        hs = slice(h * HEAD_DIM, (h + 1) * HEAD_DIM)
        q = q_ref[0, :, hs].astype(BF16)
        k = mk_ref[0, :, hs].astype(BF16)
        v = mv_ref[0, :, hs].astype(BF16)
        s = lax.dot_general(q, k, (((1,), (1,)), ((), ())), preferred_element_type=F32) * SCALE
        m = jnp.max(s, axis=-1, keepdims=True)
        p = jnp.exp(s - m)
        l = jnp.sum(p, axis=-1, keepdims=True)
        o = jnp.dot(p.astype(BF16), v, preferred_element_type=F32) / l
        o_ref[0, :, hs] = o


def _mem_attention(p3, mk, mv, mk_blk, mv_blk, *, tq):
    nb, t = p3.shape[0], p3.shape[1]
    return pl.pallas_call(
        _mem_attn_kernel,
        grid=(nb, t // tq),
        in_specs=[
            pl.BlockSpec((1, tq, GROUP_W), lambda b, i: (b, i, COL_QM // GROUP_W)),
            pl.BlockSpec((1, MEM_TOKENS, GROUP_W), lambda b, i: (b, 0, mk_blk)),
            pl.BlockSpec((1, MEM_TOKENS, GROUP_W), lambda b, i: (b, 0, mv_blk)),
        ],
        out_specs=pl.BlockSpec((1, tq, GROUP_W), lambda b, i: (b, i, 0)),
        out_shape=jax.ShapeDtypeStruct((nb, t, GROUP_W), F32),
        compiler_params=_params(2),
        name="mem_attn",
    )(p3, mk, mv)


def _conv_kernel(b_ref, c_ref, v_ref, st_ref, w_ref, o_ref, ns_ref, carry_ref, *, t_last):
    i = pl.program_id(1)

    @pl.when(i == 0)
    def _():
        carry_ref[0:2, :] = st_ref[0]

    u = c_ref[0].astype(F32) * v_ref[0].astype(F32)
    c2 = carry_ref[0:1, :]
    c1 = carry_ref[1:2, :]
    row = lax.broadcasted_iota(jnp.int32, u.shape, 0)
    u1 = jnp.where(row == 0, c1, pltpu.roll(u, 1, 0))
    u2 = jnp.where(row == 0, c2, jnp.where(row == 1, c1, pltpu.roll(u, 2, 0)))
    o = b_ref[0].astype(F32) * (w_ref[0:1, :] * u2 + w_ref[1:2, :] * u1 + w_ref[2:3, :] * u)
    o_ref[0] = o.astype(o_ref.dtype)
    last2 = u[t_last - 2:t_last, :]
    carry_ref[0:2, :] = last2
    ns_ref[0] = last2


def _conv(p3, state, conv_w, *, tm, t_last):
    nb, t = p3.shape[0], p3.shape[1]
    blk = (1, tm, CONV_W)
    return pl.pallas_call(
        functools.partial(_conv_kernel, t_last=t_last),
        grid=(nb, t // tm),
        in_specs=[
            pl.BlockSpec(blk, lambda b, i: (b, i, COL_B // CONV_W)),
            pl.BlockSpec(blk, lambda b, i: (b, i, COL_C // CONV_W)),
            pl.BlockSpec(blk, lambda b, i: (b, i, COL_V // CONV_W)),
            pl.BlockSpec((1, 2, CONV_W), lambda b, i: (b, 0, 0)),
            pl.BlockSpec((3, CONV_W), lambda b, i: (0, 0)),
        ],
        out_specs=[pl.BlockSpec(blk, lambda b, i: (b, i, 0)),
                   pl.BlockSpec((1, 2, CONV_W), lambda b, i: (b, 0, 0))],
        out_shape=[jax.ShapeDtypeStruct((nb, t, CONV_W), BF16),
                   jax.ShapeDtypeStruct((nb, 2, CONV_W), F32)],
        scratch_shapes=[pltpu.VMEM((8, CONV_W), F32)],
        compiler_params=_params(2),
        name="conv",
    )(p3, p3, p3, state, conv_w)


def _merge_kernel(oc_ref, o1_ref, l1_ref, o2_ref, l2_ref, o3_ref, l3_ref, om_ref,
                  gc_ref, ga_ref, gm_ref, h_ref, wc_ref, wa_ref, wm_ref, wo_ref, g_ref, b_ref,
                  y_ref, yb_ref):
    l1, l2, l3 = l1_ref[...], l2_ref[...], l3_ref[...]
    m = jnp.maximum(jnp.maximum(l1, l2), l3)
    e1, e2, e3 = jnp.exp(l1 - m), jnp.exp(l2 - m), jnp.exp(l3 - m)
    o_attn = (e1 * o1_ref[...] + e2 * o2_ref[...] + e3 * o3_ref[...]) / (e1 + e2 + e3)
    mc = jnp.dot(oc_ref[...].astype(BF16), wc_ref[...], preferred_element_type=F32)
    ma = jnp.dot(o_attn.astype(BF16), wa_ref[...], preferred_element_type=F32)
    mm = jnp.dot(om_ref[...].astype(BF16), wm_ref[...], preferred_element_type=F32)
    mix = (jax.nn.sigmoid(gc_ref[...].astype(F32)) * mc
           + jax.nn.sigmoid(ga_ref[...].astype(F32)) * ma
           + jax.nn.sigmoid(gm_ref[...].astype(F32)) * mm)
    z = ALPHA * h_ref[...] + jnp.dot(mix.astype(BF16), wo_ref[...], preferred_element_type=F32)
    y = _layer_norm_rows(z, g_ref[...], b_ref[...])
    y_ref[...] = y
    yb_ref[...] = y.astype(BF16)


def _merge(o_conv, attn, o_mem, p_all, h, w_c, w_a, w_m, w_o, g, b, *, tm):
    rows = h.shape[0]
    single = pl.Buffered(1)
    row_blk = lambda w: pl.BlockSpec((tm, w), lambda i: (i, 0))
    gate_blk = lambda col: pl.BlockSpec((tm, D_MODEL), lambda i: (i, col // D_MODEL))
    whole = lambda a: pl.BlockSpec(a.shape, lambda i: (0, 0), pipeline_mode=single)
    (o1, l1), (o2, l2), (o3, l3) = attn
    return pl.pallas_call(
        _merge_kernel,
        grid=(rows // tm,),
        in_specs=[row_blk(CONV_W)] + [row_blk(GROUP_W)] * 7
                 + [gate_blk(COL_GC), gate_blk(COL_GA), gate_blk(COL_GM), row_blk(D_MODEL),
                    whole(w_c), whole(w_a), whole(w_m), whole(w_o), whole(g), whole(b)],
        out_specs=[row_blk(D_MODEL), row_blk(D_MODEL)],
        out_shape=[jax.ShapeDtypeStruct((rows, D_MODEL), F32),
                   jax.ShapeDtypeStruct((rows, D_MODEL), BF16)],
        compiler_params=_params(1),
        name="merge",
    )(o_conv, o1, l1, o2, l2, o3, l3, o_mem, p_all, p_all, p_all, h, w_c, w_a, w_m, w_o, g, b)


def _sample_attn_kernel(q_ref, kn_ref, vn_ref,
                        k0_ref, v0_ref, k1_ref, v1_ref, k2_ref, v2_ref,
                        o0_ref, l0_ref, o1_ref, l1_ref, o2_ref, l2_ref,
                        nk0_ref, nv0_ref, nk1_ref, nv1_ref, nk2_ref, nv2_ref, *, t_new):
    k_bufs = (k0_ref, k1_ref, k2_ref)
    v_bufs = (v0_ref, v1_ref, v2_ref)
    o_refs = (o0_ref, o1_ref, o2_ref)
    l_refs = (l0_ref, l1_ref, l2_ref)
    row_b = lax.broadcasted_iota(jnp.int32, (BAND, 1), 0)
    row_n = lax.broadcasted_iota(jnp.int32, (SAMPLE_PAD, 1), 0)
    for g, (win, dil) in enumerate(DIL_GROUPS):
        for o_ref in (o_refs[g], l_refs[g]):
            o_ref[...] = jnp.zeros(o_ref.shape, F32)
        gs = slice(g * GROUP_W, (g + 1) * GROUP_W)
        k_new = kn_ref[0, :, gs]
        v_new = vn_ref[0, :, gs]
        for t in range(t_new):
            lane0 = (t % dil) * GROUP_W if dil > 1 else 0
            q_t = q_ref[0, t:t + 1, gs]
            kb = k_bufs[g][0, :, lane0:lane0 + GROUP_W]
            vb = v_bufs[g][0, :, lane0:lane0 + GROUP_W]
            prod_b = kb * q_t
            prod_n = k_new * q_t
            dist_b = (win + t - row_b * dil - (t % dil if dil > 1 else 0)).astype(F32)
            dist_n = (t - row_n).astype(F32)
            if dil == 1:
                ok_b = row_b >= t
                ok_n = row_n <= t
            else:
                ok_b = row_b >= 0
                ok_n = row_n == t
            for h in range(HEADS):
                hs = slice(h * HEAD_DIM, (h + 1) * HEAD_DIM)
                slope = float(SLOPES[g, h])
                s_b = jnp.sum(prod_b[:, hs], axis=-1, keepdims=True) * SCALE - slope * dist_b
                s_n = jnp.sum(prod_n[:, hs], axis=-1, keepdims=True) * SCALE - slope * dist_n
                s_b = jnp.where(ok_b, s_b, NEG_INF)
                s_n = jnp.where(ok_n, s_n, NEG_INF)
                m = jnp.maximum(jnp.max(s_b, axis=0, keepdims=True), jnp.max(s_n, axis=0, keepdims=True))
                e_b = jnp.exp(s_b - m)
                e_n = jnp.exp(s_n - m)
                l = jnp.sum(e_b, axis=0, keepdims=True) + jnp.sum(e_n, axis=0, keepdims=True)
                acc = (jnp.sum(e_b * vb[:, hs], axis=0, keepdims=True)
                       + jnp.sum(e_n * v_new[:, hs], axis=0, keepdims=True))
                o_refs[g][0, t:t + 1, hs] = acc / l
                l_refs[g][0, t:t + 1, hs] = jnp.broadcast_to(m + jnp.log(l), (1, HEAD_DIM))

    def shift(buf_ref, new_ref, out_ref, g):
        win, dil = DIL_GROUPS[g]
        gs0 = g * GROUP_W
        if dil == 1:
            out_ref[0, 0:win - t_new, :] = buf_ref[0, t_new:win, :]
            out_ref[0, win - t_new:win, :] = new_ref[0, 0:t_new, gs0:gs0 + GROUP_W]
            return
        keep = (dil - t_new) * GROUP_W
        width = dil * GROUP_W
        if keep:
            out_ref[0, :, 0:keep] = buf_ref[0, :, width - keep:width]
        out_ref[0, 0:BAND - 1, keep:width] = buf_ref[0, 1:BAND, 0:width - keep]
        for u in range(t_new):
            out_ref[0, BAND - 1:BAND, keep + u * GROUP_W:keep + (u + 1) * GROUP_W] = (
                new_ref[0, u:u + 1, gs0:gs0 + GROUP_W])

    shift(k0_ref, kn_ref, nk0_ref, 0)
    shift(v0_ref, vn_ref, nv0_ref, 0)
    shift(k1_ref, kn_ref, nk1_ref, 1)
    shift(v1_ref, vn_ref, nv1_ref, 1)
    shift(k2_ref, kn_ref, nk2_ref, 2)
    shift(v2_ref, vn_ref, nv2_ref, 2)


def _sample_attention(p3, k_states, v_states, *, t_new):
    nb = p3.shape[0]
    att_w = 3 * GROUP_W
    views = []
    for g, (win, dil) in enumerate(DIL_GROUPS):
        views += [k_states[g].reshape(nb, win // dil, dil * GROUP_W),
                  v_states[g].reshape(nb, win // dil, dil * GROUP_W)]
    view_specs = [pl.BlockSpec((1,) + v.shape[1:], lambda b: (b, 0, 0)) for v in views]
    o_spec = pl.BlockSpec((1, SAMPLE_PAD, GROUP_W), lambda b: (b, 0, 0))
    o_shape = jax.ShapeDtypeStruct((nb, SAMPLE_PAD, GROUP_W), F32)
    outs = pl.pallas_call(
        functools.partial(_sample_attn_kernel, t_new=t_new),
        grid=(nb,),
        in_specs=[pl.BlockSpec((1, SAMPLE_PAD, att_w), lambda b: (b, 0, COL_Q // att_w)),
                  pl.BlockSpec((1, SAMPLE_PAD, att_w), lambda b: (b, 0, COL_K // att_w)),
                  pl.BlockSpec((1, SAMPLE_PAD, att_w), lambda b: (b, 0, COL_VA // att_w))] + view_specs,
        out_specs=[o_spec] * 6 + view_specs,
        out_shape=[o_shape] * 6 + [jax.ShapeDtypeStruct(v.shape, F32) for v in views],
        compiler_params=_params(1),
        name="sample_attn",
    )(p3, p3, p3, *views)
    attn = [(outs[2 * g].reshape(nb * SAMPLE_PAD, GROUP_W), outs[2 * g + 1].reshape(nb * SAMPLE_PAD, GROUP_W))
            for g in range(3)]
    new_states = [outs[6 + i].reshape(nb, DIL_GROUPS[i // 2][0], GROUP_W) for i in range(6)]
    return attn, new_states


def kernel(x_prompt, x_sample, mem_prompt, state_conv, state_k_w128, state_v_w128, state_k_w512, state_v_w512, state_k_w2048, state_v_w2048, cache_mem_k, cache_mem_v, ln1_g, ln1_b, w_ffn1_gu, w_ffn1_down, w_in, conv_w, w_mem_kv, w_br_conv, w_br_attn, w_br_mem, w_o, ln2_g, ln2_b, w_ffn2_gu, w_ffn2_down, ln3_g, ln3_b):
    depth = w_in.shape[0]
    assert depth == 1 and x_prompt.shape[0] == 1
    seq = x_prompt.shape[1]
    nb, t_new = x_sample.shape[0], x_sample.shape[1]

    wb = lambda w: w[0].astype(BF16)
    w1gu, w1d, w2gu, w2d = wb(w_ffn1_gu), wb(w_ffn1_down), wb(w_ffn2_gu), wb(w_ffn2_down)
    win_b, wmem_b = wb(w_in), wb(w_mem_kv)
    wc_b, wa_b, wm_b, wo_b = wb(w_br_conv), wb(w_br_attn), wb(w_br_mem), wb(w_o)
    cw = conv_w[0]

    xp = x_prompt[0]
    h1, h1b = _ffn(xp, w1gu, w1d, ln1_g, ln1_b, tm=1024, tf=512)
    p_all = _matmul(h1b, win_b, tm=1024, tn=512, n_cols=IN_W, out_dtype=BF16, name="in_proj")
    keep = DIL_GROUPS[-1][0]
    kv_tail = _matmul(h1b, win_b, tm=1024, tn=512, n_cols=2 * 3 * GROUP_W, out_dtype=F32,
                      row_blk0=(seq - keep) // 1024, n_rows=keep, col_blk0=COL_K // 512, name="kv_tail")
    mem_kv = _matmul(mem_prompt[0].astype(BF16), wmem_b, tm=MEM_TOKENS, tn=512, n_cols=2 * GROUP_W,
                     out_dtype=F32, name="mem_kv")
    attn_p = [_band_attention(p_all, g) for g in range(3)]
    p3 = p_all.reshape(1, seq, IN_W)
    o_mem = _mem_attention(p3, mem_kv[None], mem_kv[None], 0, 1, tq=512)[0]
    o_conv, conv_p = _conv(p3, jnp.zeros((1, 2, CONV_W), F32), cw, tm=512, t_last=512)
    h2, h2b = _merge(o_conv[0], attn_p, o_mem, p_all, h1, wc_b, wa_b, wm_b, wo_b, ln2_g, ln2_b, tm=256)
    del h2b
    y_prompt, _ = _ffn(h2, w2gu, w2d, ln3_g, ln3_b, tm=1024, tf=512)

    prompt_states = []
    for g, (win, _) in enumerate(DIL_GROUPS):
        for base in (0, 3 * GROUP_W):
            c0 = base + g * GROUP_W
            prompt_states.append(kv_tail[keep - win:, c0:c0 + GROUP_W].reshape(1, 1, win, HEADS, HEAD_DIM))
    mem_k_p = mem_kv[:, :GROUP_W].reshape(1, 1, MEM_TOKENS, HEADS, HEAD_DIM)
    mem_v_p = mem_kv[:, GROUP_W:].reshape(1, 1, MEM_TOKENS, HEADS, HEAD_DIM)

    rows_s = nb * SAMPLE_PAD
    xs = jnp.pad(x_sample, ((0, 0), (0, SAMPLE_PAD - t_new), (0, 0))).reshape(rows_s, D_MODEL)
    s1, s1b = _ffn(xs, w1gu, w1d, ln1_g, ln1_b, tm=rows_s, tf=512)
    ps_all = _matmul(s1b, win_b, tm=rows_s, tn=512, n_cols=IN_W, out_dtype=F32, name="in_proj_s")
    ps3 = ps_all.reshape(nb, SAMPLE_PAD, IN_W)
    k_states = [s[0].reshape(nb, -1, GROUP_W) for s in (state_k_w128, state_k_w512, state_k_w2048)]
    v_states = [s[0].reshape(nb, -1, GROUP_W) for s in (state_v_w128, state_v_w512, state_v_w2048)]
    attn_s, new_states = _sample_attention(ps3, k_states, v_states, t_new=t_new)
    o_mem_s = _mem_attention(ps3, cache_mem_k[0].reshape(nb, MEM_TOKENS, GROUP_W),
                             cache_mem_v[0].reshape(nb, MEM_TOKENS, GROUP_W), 0, 0, tq=SAMPLE_PAD)
    o_conv_s, conv_s = _conv(ps3, state_conv[0], cw, tm=SAMPLE_PAD, t_last=t_new)
    s2, _ = _merge(o_conv_s.reshape(rows_s, CONV_W), attn_s, o_mem_s.reshape(rows_s, GROUP_W), ps_all, s1,
                   wc_b, wa_b, wm_b, wo_b, ln2_g, ln2_b, tm=rows_s)
    y_s, _ = _ffn(s2, w2gu, w2d, ln3_g, ln3_b, tm=rows_s, tf=512)
    y_sample = y_s.reshape(nb, SAMPLE_PAD, D_MODEL)[:, :t_new]

    sample_states = [s.reshape(1, nb, s.shape[1], HEADS, HEAD_DIM) for s in new_states]
    return (y_prompt[None], y_sample, conv_p[None], *prompt_states, mem_k_p, mem_v_p,
            conv_s[None], *sample_states)
```

```python
import functools

import jax
import jax.numpy as jnp
import numpy as np
from jax import lax
from jax.experimental import pallas as pl
from jax.experimental.pallas import tpu as pltpu

F32 = jnp.float32
BF16 = jnp.bfloat16

D_MODEL = 2048
D_FF = 5632
HEAD_DIM = 128
HEADS = 4
GROUP_W = HEADS * HEAD_DIM
DIL_GROUPS = ((128, 1), (512, 4), (2048, 16))
N_GROUPS = len(DIL_GROUPS)
BAND = 128
CHUNK = BAND * DIL_GROUPS[-1][1]
CONV_W = 1024
MEM_TOKENS = 256
IN_W = 14336
COL_B, COL_C, COL_V = 0, 1024, 2048
COL_Q, COL_K, COL_VA = 3072, 4608, 6144
COL_QM = 7680
COL_GC, COL_GA, COL_GM = 8192, 10240, 12288
QKV_W = COL_QM - COL_Q
ALPHA = 2.0 ** 0.25
LN_EPS = 1e-5
NEG_INF = -1e30
SCALE = HEAD_DIM ** -0.5
SLOPES = np.exp2(np.float32(-8.0) * np.arange(1, 13, dtype=np.float32) / np.float32(12)).reshape(3, 4)

SAMPLE_PAD = 8
ROWS16 = 16
VMEM_LIMIT = 56 * 1024 * 1024


def _params(n_axes, vmem=VMEM_LIMIT):
    return pltpu.CompilerParams(dimension_semantics=("arbitrary",) * n_axes, vmem_limit_bytes=vmem)


def _layer_norm_rows(z, g, b):
    mu = jnp.mean(z, axis=-1, keepdims=True)
    zc = z - mu
    var = jnp.mean(zc * zc, axis=-1, keepdims=True)
    return zc * lax.rsqrt(var + LN_EPS) * g + b


def _ffn_kernel(x_ref, wg_ref, wu_ref, wd_ref, g_ref, b_ref, y_ref, yb_ref, xb_ref, *, n_ff, row_chunk):
    j = pl.program_id(1)

    @pl.when(j == 0)
    def _():
        xb_ref[...] = x_ref[...].astype(BF16)

    xb = xb_ref[...]
    gate = jnp.dot(xb, wg_ref[...], preferred_element_type=F32)
    up = jnp.dot(xb, wu_ref[...], preferred_element_type=F32)
    act = (gate * jax.nn.sigmoid(gate) * up).astype(BF16)
    part = jnp.dot(act, wd_ref[...], preferred_element_type=F32)

    @pl.when(j == 0)
    def _():
        y_ref[...] = part

    @pl.when(j > 0)
    def _():
        y_ref[...] += part

    @pl.when(j == n_ff - 1)
    def _():
        def chunk(c, carry):
            rows = pl.ds(pl.multiple_of(c * row_chunk, row_chunk), row_chunk)
            z = ALPHA * x_ref[rows, :] + 0.5 * y_ref[rows, :]
            y = _layer_norm_rows(z, g_ref[...], b_ref[...])
            y_ref[rows, :] = y
            yb_ref[rows, :] = y.astype(BF16)
            return carry
        lax.fori_loop(0, x_ref.shape[0] // row_chunk, chunk, 0)


def _ffn(x, w_gu, w_down, g, b, *, tm, tf):
    rows = x.shape[0]
    n_ff = D_FF // tf
    single = pl.Buffered(1)
    return pl.pallas_call(
        functools.partial(_ffn_kernel, n_ff=n_ff, row_chunk=min(tm, 128)),
        grid=(rows // tm, n_ff),
        in_specs=[
            pl.BlockSpec((tm, D_MODEL), lambda i, j: (i, 0), pipeline_mode=single),
            pl.BlockSpec((D_MODEL, tf), lambda i, j: (0, j)),
            pl.BlockSpec((D_MODEL, tf), lambda i, j: (0, j + n_ff)),
            pl.BlockSpec((tf, D_MODEL), lambda i, j: (j, 0)),
            pl.BlockSpec((1, D_MODEL), lambda i, j: (0, 0)),
            pl.BlockSpec((1, D_MODEL), lambda i, j: (0, 0)),
        ],
        out_specs=[
            pl.BlockSpec((tm, D_MODEL), lambda i, j: (i, 0), pipeline_mode=single),
            pl.BlockSpec((tm, D_MODEL), lambda i, j: (i, 0), pipeline_mode=single),
        ],
        out_shape=[jax.ShapeDtypeStruct((rows, D_MODEL), F32),
                   jax.ShapeDtypeStruct((rows, D_MODEL), BF16)],
        scratch_shapes=[pltpu.VMEM((tm, D_MODEL), BF16)],
        compiler_params=_params(2),
        name="ffn",
    )(x, w_gu, w_gu, w_down, g, b)


def _mm_kernel(x_ref, w_ref, o_ref):
    o_ref[...] = jnp.dot(x_ref[...], w_ref[...], preferred_element_type=F32).astype(o_ref.dtype)


def _matmul(x, w, *, tm, tn, out_dtype, name):
    rows, k = x.shape
    n_cols = w.shape[1]
    return pl.pallas_call(
        _mm_kernel,
        grid=(rows // tm, n_cols // tn),
        in_specs=[pl.BlockSpec((tm, k), lambda i, j: (i, 0)),
                  pl.BlockSpec((k, tn), lambda i, j: (0, j))],
        out_specs=pl.BlockSpec((tm, tn), lambda i, j: (i, j)),
        out_shape=jax.ShapeDtypeStruct((rows, n_cols), out_dtype),
        compiler_params=_params(2),
        name=name,
    )(x, w)


def _in_proj_kernel(x_ref, w_ref, p_ref, qkv_ref, *, j_lo, j_hi):
    j = pl.program_id(1)
    acc = jnp.dot(x_ref[...], w_ref[...], preferred_element_type=F32)
    p_ref[...] = acc.astype(BF16)

    @pl.when((j >= j_lo) & (j < j_hi))
    def _():
        qkv_ref[...] = acc


def _in_proj(x, w, *, tm, tn):
    rows, k = x.shape
    j_lo, j_hi = COL_Q // tn, COL_QM // tn
    return pl.pallas_call(
        functools.partial(_in_proj_kernel, j_lo=j_lo, j_hi=j_hi),
        grid=(rows // tm, IN_W // tn),
        in_specs=[pl.BlockSpec((tm, k), lambda i, j: (i, 0)),
                  pl.BlockSpec((k, tn), lambda i, j: (0, j))],
        out_specs=[pl.BlockSpec((tm, tn), lambda i, j: (i, j)),
                   pl.BlockSpec((tm, tn), lambda i, j: (i, jnp.clip(j - j_lo, 0, j_hi - j_lo - 1)))],
        out_shape=[jax.ShapeDtypeStruct((rows, IN_W), BF16),
                   jax.ShapeDtypeStruct((rows, QKV_W), F32)],
        compiler_params=_params(2),
        name="in_proj",
    )(x, w)


def _band_kernel(sl_ref, q0_ref, k0_ref, v0_ref, q1_ref, k1_ref, v1_ref, q2_ref, k2_ref, v2_ref,
                 o_ref, kk_ref, vv_ref, og_ref, lg_ref):
    c = pl.program_id(1)
    q_refs = (q0_ref, q1_ref, q2_ref)
    k_refs = (k0_ref, k1_ref, k2_ref)
    v_refs = (v0_ref, v1_ref, v2_ref)
    for g in range(N_GROUPS):
        @pl.when(c == 0)
        def _():
            kk_ref[g, 0:CHUNK, :] = jnp.zeros((CHUNK, HEAD_DIM), F32)
            vv_ref[g, 0:CHUNK, :] = jnp.zeros((CHUNK, HEAD_DIM), F32)

        @pl.when(c > 0)
        def _():
            kk_ref[g, 0:CHUNK, :] = kk_ref[g, CHUNK:2 * CHUNK, :]
            vv_ref[g, 0:CHUNK, :] = vv_ref[g, CHUNK:2 * CHUNK, :]

        kk_ref[g, CHUNK:2 * CHUNK, :] = k_refs[g][...]
        vv_ref[g, CHUNK:2 * CHUNK, :] = v_refs[g][...]

    qi = lax.broadcasted_iota(jnp.int32, (BAND, 2 * BAND), 0)
    ki = lax.broadcasted_iota(jnp.int32, (BAND, 2 * BAND), 1)
    rel = BAND + qi - ki
    valid = (rel >= 0) & (rel <= BAND)
    prev_keys = (ki < BAND).astype(F32)

    for g, (_, dil) in enumerate(DIL_GROUPS):
        slope = sl_ref[0, g:g + 1, 0:1]
        bias = jnp.where(valid, -(slope * (rel * dil).astype(F32)), NEG_INF)
        blocks_per_residue = CHUNK // (BAND * dil)

        def block(idx, carry, g=g, dil=dil, bias=bias, blocks_per_residue=blocks_per_residue):
            r = idx // blocks_per_residue
            n = idx % blocks_per_residue
            start = r + n * (BAND * dil)
            no_prev = jnp.where((c == 0) & (n == 0), NEG_INF, 0.0)
            q = q_refs[g][pl.ds(start, BAND, stride=dil), :].astype(BF16)
            k = kk_ref[g, pl.ds(CHUNK - BAND * dil + start, 2 * BAND, stride=dil), :].astype(BF16)
            v = vv_ref[g, pl.ds(CHUNK - BAND * dil + start, 2 * BAND, stride=dil), :].astype(BF16)
            s = lax.dot_general(q, k, (((1,), (1,)), ((), ())), preferred_element_type=F32) * SCALE
            s = s + bias + no_prev * prev_keys
            m = jnp.max(s, axis=-1, keepdims=True)
            p = jnp.exp(s - m)
            l = jnp.sum(p, axis=-1, keepdims=True)
            o = jnp.dot(p.astype(BF16), v, preferred_element_type=F32) / l
            og_ref[g, pl.ds(start, BAND, stride=dil), :] = o
            lg_ref[g, pl.ds(start, BAND, stride=dil), :] = jnp.broadcast_to(m + jnp.log(l), (BAND, HEAD_DIM))
            return carry

        lax.fori_loop(0, CHUNK // BAND, block, 0)

    rows_per_step = 256

    def combine(i, carry):
        rows = pl.ds(pl.multiple_of(i * rows_per_step, rows_per_step), rows_per_step)
        l0, l1, l2 = lg_ref[0, rows, :], lg_ref[1, rows, :], lg_ref[2, rows, :]
        m = jnp.maximum(jnp.maximum(l0, l1), l2)
        e0, e1, e2 = jnp.exp(l0 - m), jnp.exp(l1 - m), jnp.exp(l2 - m)
        o = (e0 * og_ref[0, rows, :] + e1 * og_ref[1, rows, :] + e2 * og_ref[2, rows, :]) / (e0 + e1 + e2)
        o_ref[rows, :] = o.astype(o_ref.dtype)
        return carry

    lax.fori_loop(0, CHUNK // rows_per_step, combine, 0)


def _band_attention(qkv, slopes):
    seq = qkv.shape[0]
    blk = (CHUNK, HEAD_DIM)
    per_tensor = N_GROUPS * HEADS
    in_specs = [pl.BlockSpec((1, 8, HEAD_DIM), lambda h, c: (h, 0, 0))]
    for g in range(N_GROUPS):
        for tensor in range(3):
            in_specs.append(pl.BlockSpec(
                blk, lambda h, c, g=g, tensor=tensor: (c, tensor * per_tensor + g * HEADS + h)))
    return pl.pallas_call(
        _band_kernel,
        grid=(HEADS, seq // CHUNK),
        in_specs=in_specs,
        out_specs=pl.BlockSpec(blk, lambda h, c: (c, h)),
        out_shape=jax.ShapeDtypeStruct((seq, GROUP_W), BF16),
        scratch_shapes=[pltpu.VMEM((N_GROUPS, 2 * CHUNK, HEAD_DIM), F32),
                        pltpu.VMEM((N_GROUPS, 2 * CHUNK, HEAD_DIM), F32),
                        pltpu.VMEM((N_GROUPS, CHUNK, HEAD_DIM), F32),
                        pltpu.VMEM((N_GROUPS, CHUNK, HEAD_DIM), F32)],
        compiler_params=_params(2),
        name="band_attn",
    )(slopes, *([qkv] * 9))


def _mem_attn_kernel(q_ref, mk_ref, mv_ref, o_ref):
    for h in range(HEADS):
        hs = slice(h * HEAD_DIM, (h + 1) * HEAD_DIM)
        q = q_ref[:, hs]
        k = mk_ref[:, hs].astype(BF16)
        v = mv_ref[:, hs].astype(BF16)
        s = lax.dot_general(q, k, (((1,), (1,)), ((), ())), preferred_element_type=F32) * SCALE
        m = jnp.max(s, axis=-1, keepdims=True)
        p = jnp.exp(s - m)
        l = jnp.sum(p, axis=-1, keepdims=True)
        o = jnp.dot(p.astype(BF16), v, preferred_element_type=F32) / l
        o_ref[:, hs] = o.astype(o_ref.dtype)


def _mem_attention(p_all, mem_kv, *, tq):
    seq = p_all.shape[0]
    return pl.pallas_call(
        _mem_attn_kernel,
        grid=(seq // tq,),
        in_specs=[pl.BlockSpec((tq, GROUP_W), lambda i: (i, COL_QM // GROUP_W)),
                  pl.BlockSpec((MEM_TOKENS, GROUP_W), lambda i: (0, 0)),
                  pl.BlockSpec((MEM_TOKENS, GROUP_W), lambda i: (0, 1))],
        out_specs=pl.BlockSpec((tq, GROUP_W), lambda i: (i, 0)),
        out_shape=jax.ShapeDtypeStruct((seq, GROUP_W), BF16),
        compiler_params=_params(1),
        name="mem_attn",
    )(p_all, mem_kv, mem_kv)


def _conv_kernel(b_ref, c_ref, v_ref, st_ref, w_ref, o_ref, ns_ref, carry_ref, *, t_last):
    i = pl.program_id(1)

    @pl.when(i == 0)
    def _():
        carry_ref[0:2, :] = st_ref[0]

    u = c_ref[0].astype(F32) * v_ref[0].astype(F32)
    c2 = carry_ref[0:1, :]
    c1 = carry_ref[1:2, :]
    row = lax.broadcasted_iota(jnp.int32, u.shape, 0)
    u1 = jnp.where(row == 0, c1, pltpu.roll(u, 1, 0))
    u2 = jnp.where(row == 0, c2, jnp.where(row == 1, c1, pltpu.roll(u, 2, 0)))
    o = b_ref[0].astype(F32) * (w_ref[0:1, :] * u2 + w_ref[1:2, :] * u1 + w_ref[2:3, :] * u)
    o_ref[0] = o.astype(o_ref.dtype)
    last2 = u[t_last - 2:t_last, :]
    carry_ref[0:2, :] = last2
    ns_ref[0] = last2


def _conv(p3, state, conv_w, *, tm, t_last):
    nb, t = p3.shape[0], p3.shape[1]
    blk = (1, tm, CONV_W)
    return pl.pallas_call(
        functools.partial(_conv_kernel, t_last=t_last),
        grid=(nb, t // tm),
        in_specs=[
            pl.BlockSpec(blk, lambda b, i: (b, i, COL_B // CONV_W)),
            pl.BlockSpec(blk, lambda b, i: (b, i, COL_C // CONV_W)),
            pl.BlockSpec(blk, lambda b, i: (b, i, COL_V // CONV_W)),
            pl.BlockSpec((1, 2, CONV_W), lambda b, i: (b, 0, 0)),
            pl.BlockSpec((3, CONV_W), lambda b, i: (0, 0)),
        ],
        out_specs=[pl.BlockSpec(blk, lambda b, i: (b, i, 0)),
                   pl.BlockSpec((1, 2, CONV_W), lambda b, i: (b, 0, 0))],
        out_shape=[jax.ShapeDtypeStruct((nb, t, CONV_W), BF16),
                   jax.ShapeDtypeStruct((nb, 2, CONV_W), F32)],
        scratch_shapes=[pltpu.VMEM((8, CONV_W), F32)],
        compiler_params=_params(2),
        name="conv",
    )(p3, p3, p3, state, conv_w)


def _merge_kernel(oc_ref, oa_ref, om_ref, gc_ref, ga_ref, gm_ref, h_ref,
                  wc_ref, wa_ref, wm_ref, wo_ref, g_ref, b_ref, y_ref, yb_ref):
    mc = jnp.dot(oc_ref[...], wc_ref[...], preferred_element_type=F32)
    ma = jnp.dot(oa_ref[...], wa_ref[...], preferred_element_type=F32)
    mm = jnp.dot(om_ref[...], wm_ref[...], preferred_element_type=F32)
    mix = (jax.nn.sigmoid(gc_ref[...].astype(F32)) * mc
           + jax.nn.sigmoid(ga_ref[...].astype(F32)) * ma
           + jax.nn.sigmoid(gm_ref[...].astype(F32)) * mm)
    z = ALPHA * h_ref[...] + jnp.dot(mix.astype(BF16), wo_ref[...], preferred_element_type=F32)
    y = _layer_norm_rows(z, g_ref[...], b_ref[...])
    y_ref[...] = y
    yb_ref[...] = y.astype(BF16)


def _merge(o_conv, o_attn, o_mem, p_all, h, w_c, w_a, w_m, w_o, g, b, *, tm):
    rows = h.shape[0]
    single = pl.Buffered(1)
    row_blk = lambda w: pl.BlockSpec((tm, w), lambda i: (i, 0))
    gate_blk = lambda col: pl.BlockSpec((tm, D_MODEL), lambda i: (i, col // D_MODEL))
    whole = lambda a: pl.BlockSpec(a.shape, lambda i: (0, 0), pipeline_mode=single)
    return pl.pallas_call(
        _merge_kernel,
        grid=(rows // tm,),
        in_specs=[row_blk(CONV_W), row_blk(GROUP_W), row_blk(GROUP_W),
                  gate_blk(COL_GC), gate_blk(COL_GA), gate_blk(COL_GM), row_blk(D_MODEL),
                  whole(w_c), whole(w_a), whole(w_m), whole(w_o), whole(g), whole(b)],
        out_specs=[row_blk(D_MODEL), row_blk(D_MODEL)],
        out_shape=[jax.ShapeDtypeStruct((rows, D_MODEL), F32),
                   jax.ShapeDtypeStruct((rows, D_MODEL), BF16)],
        compiler_params=_params(1),
        name="merge",
    )(o_conv, o_attn, o_mem, p_all, p_all, p_all, h, w_c, w_a, w_m, w_o, g, b)


def _over_quad(x, op):
    x = op(x, pltpu.roll(x, HEADS, 0))
    return op(x, pltpu.roll(x, 2 * HEADS, 0))


def _lane_dot(k3, q):
    return jnp.sum(k3 * q, axis=-1, keepdims=True) * SCALE


def _quad_attention(k3, v3, qt, bias3, k_new, v_new, bias_new):
    s = _lane_dot(k3, qt[None]) + bias3
    m = jnp.max(s, axis=0)
    if k_new is not None:
        s_new = _lane_dot(k_new, qt) + bias_new
        m = jnp.maximum(m, s_new)
    m = _over_quad(m, jnp.maximum)
    e = jnp.exp(s - m[None])
    l = jnp.sum(e, axis=0)
    acc = jnp.sum(e * v3, axis=0)
    if k_new is not None:
        e_new = jnp.exp(s_new - m)
        l = l + e_new
        acc = acc + e_new * v_new
    l = _over_quad(l, jnp.add)
    acc = _over_quad(acc, jnp.add)
    return acc / l, m + jnp.log(l)


def _sample_kernel(qd_ref, qt_ref, kn_ref, vn_ref, sl_ref,
                   k0_ref, v0_ref, k1_ref, v1_ref, k2_ref, v2_ref, mk_ref, mv_ref,
                   oa_ref, om_ref, nk0_ref, nv0_ref, nk1_ref, nv1_ref, nk2_ref, nv2_ref, *, t_new):
    k_refs = (k0_ref, k1_ref, k2_ref)
    v_refs = (v0_ref, v1_ref, v2_ref)
    quad16 = lax.broadcasted_iota(jnp.int32, (ROWS16, HEAD_DIM), 0) >> 2

    outs, lses = [], []

    win0 = DIL_GROUPS[0][0]
    a_idx = lax.broadcasted_iota(jnp.int32, (win0 // 4, ROWS16, HEAD_DIM), 0)
    b_idx = lax.broadcasted_iota(jnp.int32, (win0 // 4, ROWS16, HEAD_DIM), 1) >> 2
    tok = a_idx * 4 + b_idx
    slope0 = sl_ref[0]
    k3, v3 = k0_ref[0], v0_ref[0]
    o0 = jnp.zeros((ROWS16, HEAD_DIM), F32)
    l0 = jnp.zeros((ROWS16, HEAD_DIM), F32)
    for t in range(t_new):
        bias3 = jnp.where(tok >= t, -(slope0[None] * (win0 + t - tok).astype(F32)), NEG_INF)
        bias_new = jnp.where(quad16 <= t, -(slope0 * (t - quad16).astype(F32)), NEG_INF)
        o_t, l_t = _quad_attention(k3, v3, qt_ref[0, 0, t], bias3, kn_ref[0, 0], vn_ref[0, 0], bias_new)
        o0 = jnp.where(quad16 == t, o_t, o0)
        l0 = jnp.where(quad16 == t, l_t, l0)
    outs.append(o0)
    lses.append(l0)

    for g in (1, 2):
        win, dil = DIL_GROUPS[g]
        q16 = qd_ref[0, g - 1]
        k3 = k_refs[g][0, :, 0:ROWS16, :]
        v3 = v_refs[g][0, :, 0:ROWS16, :]
        i_idx = lax.broadcasted_iota(jnp.int32, (BAND, ROWS16, HEAD_DIM), 0)
        bias3 = -(sl_ref[g][None] * (win - i_idx * dil).astype(F32))
        s = _lane_dot(k3, q16[None]) + bias3
        s_new = _lane_dot(kn_ref[0, g], q16)
        m = jnp.maximum(jnp.max(s, axis=0), s_new)
        e = jnp.exp(s - m[None])
        e_new = jnp.exp(s_new - m)
        l = jnp.sum(e, axis=0) + e_new
        acc = jnp.sum(e * v3, axis=0) + e_new * vn_ref[0, g]
        outs.append(acc / l)
        lses.append(m + jnp.log(l))

    m = jnp.maximum(jnp.maximum(lses[0], lses[1]), lses[2])
    w = [jnp.exp(x - m) for x in lses]
    oa_ref[0] = (w[0] * outs[0] + w[1] * outs[1] + w[2] * outs[2]) / (w[0] + w[1] + w[2])

    k3, v3 = mk_ref[0], mv_ref[0]
    om = jnp.zeros((ROWS16, HEAD_DIM), F32)
    for t in range(t_new):
        o_t, _ = _quad_attention(k3, v3, qt_ref[0, 1, t], jnp.zeros((1, ROWS16, HEAD_DIM), F32), None, None, None)
        om = jnp.where(quad16 == t, o_t, om)
    om_ref[0] = om

    for g, (buf_ref, new_ref, out_ref) in enumerate(((k0_ref, kn_ref, nk0_ref), (v0_ref, vn_ref, nv0_ref),
                                                     (k1_ref, kn_ref, nk1_ref), (v1_ref, vn_ref, nv1_ref),
                                                     (k2_ref, kn_ref, nk2_ref), (v2_ref, vn_ref, nv2_ref))):
        grp = g // 2
        tiles, rows = buf_ref.shape[1], buf_ref.shape[2]
        keep = rows - ROWS16
        if keep:
            out_ref[0, :, 0:keep, :] = buf_ref[0, :, ROWS16:rows, :]
        out_ref[0, 0:tiles - 1, keep:rows, :] = buf_ref[0, 1:tiles, 0:ROWS16, :]
        out_ref[0, tiles - 1, keep:rows, :] = new_ref[0, grp]


def _sample_attention(qd, qt, kn, vn, slopes16, k_states, v_states, mem_k, mem_v, *, t_new):
    nb = qd.shape[0]
    states = []
    for g in range(N_GROUPS):
        states += [k_states[g], v_states[g]]
    whole = lambda a: pl.BlockSpec((1,) + a.shape[1:], lambda b: (b,) + (0,) * (a.ndim - 1))
    o_spec = pl.BlockSpec((1, ROWS16, HEAD_DIM), lambda b: (b, 0, 0))
    o_shape = jax.ShapeDtypeStruct((nb, ROWS16, HEAD_DIM), F32)
    return pl.pallas_call(
        functools.partial(_sample_kernel, t_new=t_new),
        grid=(nb,),
        in_specs=[whole(qd), whole(qt), whole(kn), whole(vn),
                  pl.BlockSpec(slopes16.shape, lambda b: (0, 0, 0))]
                 + [whole(s) for s in states] + [whole(mem_k), whole(mem_v)],
        out_specs=[o_spec, o_spec] + [whole(s) for s in states],
        out_shape=[o_shape, o_shape] + [jax.ShapeDtypeStruct(s.shape, F32) for s in states],
        compiler_params=_params(1),
        name="sample_attn",
    )(qd, qt, kn, vn, slopes16, *states, mem_k, mem_v)


def kernel(x_prompt, x_sample, mem_prompt, state_conv, state_k_w128, state_v_w128, state_k_w512, state_v_w512, state_k_w2048, state_v_w2048, cache_mem_k, cache_mem_v, ln1_g, ln1_b, w_ffn1_gu, w_ffn1_down, w_in, conv_w, w_mem_kv, w_br_conv, w_br_attn, w_br_mem, w_o, ln2_g, ln2_b, w_ffn2_gu, w_ffn2_down, ln3_g, ln3_b):
    depth = w_in.shape[0]
    seq = x_prompt.shape[1]
    nb, t_new = x_sample.shape[0], x_sample.shape[1]
    assert depth == 1 and x_prompt.shape[0] == 1 and seq % CHUNK == 0
    assert t_new * HEADS == ROWS16 and t_new <= DIL_GROUPS[1][1]

    wb = lambda w: w[0].astype(BF16)
    w1gu, w1d, w2gu, w2d = wb(w_ffn1_gu), wb(w_ffn1_down), wb(w_ffn2_gu), wb(w_ffn2_down)
    win_b, wmem_b = wb(w_in), wb(w_mem_kv)
    wc_b, wa_b, wm_b, wo_b = wb(w_br_conv), wb(w_br_attn), wb(w_br_mem), wb(w_o)
    cw = conv_w[0]
    slopes_lane = np.broadcast_to(SLOPES[:, :, None], (N_GROUPS, HEADS, HEAD_DIM))
    slopes_band = jnp.asarray(np.pad(slopes_lane.transpose(1, 0, 2), ((0, 0), (0, 8 - N_GROUPS), (0, 0))))
    slopes16 = jnp.asarray(np.tile(slopes_lane, (1, ROWS16 // HEADS, 1)))

    h1, h1b = _ffn(x_prompt[0], w1gu, w1d, ln1_g, ln1_b, tm=1024, tf=512)
    p_all, qkv = _in_proj(h1b, win_b, tm=1024, tn=512)
    mem_kv = _matmul(mem_prompt[0].astype(BF16), wmem_b, tm=MEM_TOKENS, tn=512, out_dtype=F32, name="mem_kv")
    o_attn = _band_attention(qkv, slopes_band)
    o_mem = _mem_attention(p_all, mem_kv, tq=512)
    o_conv, conv_p = _conv(p_all[None], jnp.zeros((1, 2, CONV_W), F32), cw, tm=512, t_last=512)
    h2, _ = _merge(o_conv[0], o_attn, o_mem, p_all, h1, wc_b, wa_b, wm_b, wo_b, ln2_g, ln2_b, tm=256)
    y_prompt, _ = _ffn(h2, w2gu, w2d, ln3_g, ln3_b, tm=1024, tf=512)

    prompt_states = []
    for g, (win, _) in enumerate(DIL_GROUPS):
        for tensor in (1, 2):
            c0 = tensor * N_GROUPS * GROUP_W + g * GROUP_W
            prompt_states.append(qkv[seq - win:, c0:c0 + GROUP_W].reshape(1, 1, win, HEADS, HEAD_DIM))
    mem_k_p = mem_kv[:, :GROUP_W].reshape(1, 1, MEM_TOKENS, HEADS, HEAD_DIM)
    mem_v_p = mem_kv[:, GROUP_W:].reshape(1, 1, MEM_TOKENS, HEADS, HEAD_DIM)

    rows_s = nb * SAMPLE_PAD
    xs = jnp.pad(x_sample, ((0, 0), (0, SAMPLE_PAD - t_new), (0, 0))).reshape(rows_s, D_MODEL)
    s1, s1b = _ffn(xs, w1gu, w1d, ln1_g, ln1_b, tm=rows_s, tf=512)
    ps_all = _matmul(s1b, win_b, tm=rows_s, tn=512, out_dtype=F32, name="in_proj_s")
    ps3 = ps_all.reshape(nb, SAMPLE_PAD, IN_W)

    real = ps3[:, :t_new]
    heads = lambda a, n: a.reshape(nb, t_new, n, HEADS, HEAD_DIM)
    q, k, v = (heads(real[..., c:c + N_GROUPS * GROUP_W], N_GROUPS) for c in (COL_Q, COL_K, COL_VA))
    rows_th = lambda a: a.transpose(0, 2, 1, 3, 4).reshape(nb, a.shape[2], ROWS16, HEAD_DIM)
    over_quad = lambda a: jnp.broadcast_to(a[:, :, None], (nb, t_new, ROWS16 // HEADS, HEADS, HEAD_DIM)
                                           ).reshape(nb, t_new, ROWS16, HEAD_DIM)
    qd = rows_th(q[:, :, 1:])
    qm = real[..., COL_QM:COL_QM + GROUP_W].reshape(nb, t_new, HEADS, HEAD_DIM)
    qt = jnp.stack([over_quad(q[:, :, 0]), over_quad(qm)], axis=1)
    kn, vn = rows_th(k), rows_th(v)

    def tiles(state):
        win = state.shape[2]
        per_tile = 4 if win <= DIL_GROUPS[1][0] else DIL_GROUPS[2][1]
        return state.reshape(nb, win // per_tile, per_tile * HEADS, HEAD_DIM)

    k_states = [tiles(s) for s in (state_k_w128, state_k_w512, state_k_w2048)]
    v_states = [tiles(s) for s in (state_v_w128, state_v_w512, state_v_w2048)]
    outs = _sample_attention(qd, qt, kn, vn, slopes16, k_states, v_states,
                             tiles(cache_mem_k), tiles(cache_mem_v), t_new=t_new)
    pad_rows = lambda a, w: jnp.pad(a.reshape(nb, t_new, w), ((0, 0), (0, SAMPLE_PAD - t_new), (0, 0))
                                    ).reshape(rows_s, w).astype(BF16)
    o_attn_s, o_mem_s = pad_rows(outs[0], GROUP_W), pad_rows(outs[1], GROUP_W)
    sample_states = [s.reshape(1, nb, -1, HEADS, HEAD_DIM) for s in outs[2:]]

    o_conv_s, conv_s = _conv(ps3, state_conv[0], cw, tm=SAMPLE_PAD, t_last=t_new)
    s2, _ = _merge(o_conv_s.reshape(rows_s, CONV_W), o_attn_s, o_mem_s, ps_all, s1,
                   wc_b, wa_b, wm_b, wo_b, ln2_g, ln2_b, tm=rows_s)
    y_s, _ = _ffn(s2, w2gu, w2d, ln3_g, ln3_b, tm=rows_s, tf=512)
    y_sample = y_s.reshape(nb, SAMPLE_PAD, D_MODEL)[:, :t_new]

    return (y_prompt[None], y_sample, conv_p[None], *prompt_states, mem_k_p, mem_v_p,
            conv_s[None], *sample_states)
```

```python
import functools

import jax
import jax.numpy as jnp
import numpy as np
from jax import lax
from jax.experimental import pallas as pl
from jax.experimental.pallas import tpu as pltpu

F32 = jnp.float32
BF16 = jnp.bfloat16

D_MODEL = 2048
D_FF = 5632
HEAD_DIM = 128
HEADS = 4
GROUP_W = HEADS * HEAD_DIM
DIL_GROUPS = ((128, 1), (512, 4), (2048, 16))
N_GROUPS = len(DIL_GROUPS)
BAND = 128
CHUNK = BAND * DIL_GROUPS[-1][1]
CONV_W = 1024
MEM_TOKENS = 256
IN_W = 14336
COL_B, COL_C, COL_V = 0, 1024, 2048
COL_Q, COL_K, COL_VA = 3072, 4608, 6144
COL_QM = 7680
COL_GC, COL_GA, COL_GM = 8192, 10240, 12288
QKV_W = COL_QM - COL_Q
ALPHA = 2.0 ** 0.25
LN_EPS = 1e-5
NEG_INF = -1e30
SCALE = HEAD_DIM ** -0.5
SLOPES = np.exp2(np.float32(-8.0) * np.arange(1, 13, dtype=np.float32) / np.float32(12)).reshape(3, 4)

SAMPLE_PAD = 8
ROWS16 = 16
VMEM_LIMIT = 56 * 1024 * 1024


def _params(n_axes, vmem=VMEM_LIMIT):
    return pltpu.CompilerParams(dimension_semantics=("arbitrary",) * n_axes, vmem_limit_bytes=vmem)


def _layer_norm_rows(z, g, b):
    mu = jnp.mean(z, axis=-1, keepdims=True)
    zc = z - mu
    var = jnp.mean(zc * zc, axis=-1, keepdims=True)
    return zc * lax.rsqrt(var + LN_EPS) * g + b


def _ffn_kernel(x_ref, wg_ref, wu_ref, wd_ref, g_ref, b_ref, y_ref, yb_ref, *rest, n_ff, row_chunk, emit_bf16):
    xb_ref = rest[-1]
    j = pl.program_id(1)

    @pl.when(j == 0)
    def _():
        xb_ref[...] = x_ref[...].astype(BF16)
        y_ref[...] = jnp.zeros(y_ref.shape, F32)

    wg, wu, wd = wg_ref[...], wu_ref[...], wd_ref[...]
    if emit_bf16:
        wg, wu, wd = wg.astype(BF16), wu.astype(BF16), wd.astype(BF16)
        rest[0][...], rest[1][...], rest[2][...] = wg, wu, wd
    xb = xb_ref[...]
    gate = jnp.dot(xb, wg, preferred_element_type=F32)
    up = jnp.dot(xb, wu, preferred_element_type=F32)
    act = (gate * jax.nn.sigmoid(gate) * up).astype(BF16)
    y_ref[...] += jnp.dot(act, wd, preferred_element_type=F32)

    @pl.when(j == n_ff - 1)
    def _():
        def chunk(c, carry):
            rows = pl.ds(pl.multiple_of(c * row_chunk, row_chunk), row_chunk)
            z = ALPHA * x_ref[rows, :] + 0.5 * y_ref[rows, :]
            y = _layer_norm_rows(z, g_ref[...], b_ref[...])
            y_ref[rows, :] = y
            yb_ref[rows, :] = y.astype(BF16)
            return carry
        lax.fori_loop(0, x_ref.shape[0] // row_chunk, chunk, 0)


def _ffn(x, w_gate, w_up, w_down, g, b, *, tm, tf, up_blk0=0, emit_bf16=False):
    rows = x.shape[0]
    n_ff = D_FF // tf
    assert not emit_bf16 or rows == tm
    single = pl.Buffered(1)
    w_col = pl.BlockSpec((D_MODEL, tf), lambda i, j: (0, j))
    w_row = pl.BlockSpec((tf, D_MODEL), lambda i, j: (j, 0))
    row_tile = pl.BlockSpec((tm, D_MODEL), lambda i, j: (i, 0), pipeline_mode=single)
    out_specs = [row_tile, row_tile]
    out_shape = [jax.ShapeDtypeStruct((rows, D_MODEL), F32), jax.ShapeDtypeStruct((rows, D_MODEL), BF16)]
    if emit_bf16:
        out_specs += [w_col, w_col, w_row]
        out_shape += [jax.ShapeDtypeStruct((D_MODEL, D_FF), BF16)] * 2 + [jax.ShapeDtypeStruct((D_FF, D_MODEL), BF16)]
    return pl.pallas_call(
        functools.partial(_ffn_kernel, n_ff=n_ff, row_chunk=min(tm, 128), emit_bf16=emit_bf16),
        grid=(rows // tm, n_ff),
        in_specs=[
            row_tile,
            w_col,
            pl.BlockSpec((D_MODEL, tf), lambda i, j: (0, j + up_blk0)),
            w_row,
            pl.BlockSpec((1, D_MODEL), lambda i, j: (0, 0)),
            pl.BlockSpec((1, D_MODEL), lambda i, j: (0, 0)),
        ],
        out_specs=out_specs,
        out_shape=out_shape,
        scratch_shapes=[pltpu.VMEM((tm, D_MODEL), BF16)],
        compiler_params=_params(2),
        name="ffn_cast" if emit_bf16 else "ffn",
    )(x, w_gate, w_up, w_down, g, b)


def _mm_kernel(x_ref, w_ref, o_ref, *wb_ref):
    w = w_ref[...].astype(BF16)
    if wb_ref:
        wb_ref[0][...] = w
    o_ref[...] = jnp.dot(x_ref[...], w, preferred_element_type=F32).astype(o_ref.dtype)


def _matmul(x, w, *, tm, tn, out_dtype, name, emit_bf16=False):
    rows, k = x.shape
    n_cols = w.shape[1]
    assert not emit_bf16 or rows == tm
    w_spec = pl.BlockSpec((k, tn), lambda i, j: (0, j))
    out_specs = [pl.BlockSpec((tm, tn), lambda i, j: (i, j))]
    out_shape = [jax.ShapeDtypeStruct((rows, n_cols), out_dtype)]
    if emit_bf16:
        out_specs.append(w_spec)
        out_shape.append(jax.ShapeDtypeStruct(w.shape, BF16))
    out = pl.pallas_call(
        _mm_kernel,
        grid=(rows // tm, n_cols // tn),
        in_specs=[pl.BlockSpec((tm, k), lambda i, j: (i, 0)), w_spec],
        out_specs=out_specs,
        out_shape=out_shape,
        compiler_params=_params(2),
        name=name,
    )(x, w)
    return out if emit_bf16 else out[0]


def _in_proj_kernel(x_ref, w_ref, p_ref, qkv_ref, *, j_lo, j_hi):
    j = pl.program_id(1)
    acc = jnp.dot(x_ref[...], w_ref[...], preferred_element_type=F32)
    p_ref[...] = acc.astype(BF16)

    @pl.when((j >= j_lo) & (j < j_hi))
    def _():
        qkv_ref[...] = acc


def _in_proj(x, w, *, tm, tn):
    rows, k = x.shape
    j_lo, j_hi = COL_Q // tn, COL_QM // tn
    return pl.pallas_call(
        functools.partial(_in_proj_kernel, j_lo=j_lo, j_hi=j_hi),
        grid=(rows // tm, IN_W // tn),
        in_specs=[pl.BlockSpec((tm, k), lambda i, j: (i, 0)),
                  pl.BlockSpec((k, tn), lambda i, j: (0, j))],
        out_specs=[pl.BlockSpec((tm, tn), lambda i, j: (i, j)),
                   pl.BlockSpec((tm, tn), lambda i, j: (i, jnp.clip(j - j_lo, 0, j_hi - j_lo - 1)))],
        out_shape=[jax.ShapeDtypeStruct((rows, IN_W), BF16),
                   jax.ShapeDtypeStruct((rows, QKV_W), F32)],
        compiler_params=_params(2),
        name="in_proj",
    )(x, w)


def _rows(ref, start, dil):
    if dil == 1:
        return ref[start:start + BAND, :]
    return ref[pl.ds(start, BAND, stride=dil), :]


def _band_kernel(sl_ref, q0_ref, k0_ref, v0_ref, q1_ref, k1_ref, v1_ref, q2_ref, k2_ref, v2_ref,
                 o_ref, kt0_ref, vt0_ref, kt1_ref, vt1_ref, kt2_ref, vt2_ref, og_ref, lg_ref):
    c = pl.program_id(1)
    q_refs = (q0_ref, q1_ref, q2_ref)
    k_refs = (k0_ref, k1_ref, k2_ref)
    v_refs = (v0_ref, v1_ref, v2_ref)
    kt_refs = (kt0_ref, kt1_ref, kt2_ref)
    vt_refs = (vt0_ref, vt1_ref, vt2_ref)

    @pl.when(c == 0)
    def _():
        for ref in kt_refs + vt_refs:
            ref[...] = jnp.zeros(ref.shape, F32)

    qi = lax.broadcasted_iota(jnp.int32, (BAND, 2 * BAND), 0)
    ki = lax.broadcasted_iota(jnp.int32, (BAND, 2 * BAND), 1)
    rel = BAND + qi - ki
    valid = (rel >= 0) & (rel <= BAND)
    no_prev = jnp.where(ki < BAND, jnp.where(c == 0, NEG_INF, 0.0), 0.0)

    for g, (_, dil) in enumerate(DIL_GROUPS):
        slope = sl_ref[0, g:g + 1, 0:1]
        bias = jnp.where(valid, -(slope * (rel * dil).astype(F32)), NEG_INF)
        bias_first = bias + no_prev
        span = BAND * dil
        for r in range(dil):
            for n in range(CHUNK // span):
                start = r + n * span
                q = _rows(q_refs[g], start, dil).astype(BF16)
                if n == 0:
                    k_prev, v_prev = _rows(kt_refs[g], r, dil), _rows(vt_refs[g], r, dil)
                else:
                    k_prev, v_prev = _rows(k_refs[g], start - span, dil), _rows(v_refs[g], start - span, dil)
                k = jnp.concatenate([k_prev, _rows(k_refs[g], start, dil)], axis=0).astype(BF16)
                v = jnp.concatenate([v_prev, _rows(v_refs[g], start, dil)], axis=0).astype(BF16)
                s = lax.dot_general(q, k, (((1,), (1,)), ((), ())), preferred_element_type=F32) * SCALE
                s = s + (bias_first if n == 0 else bias)
                m = jnp.max(s, axis=-1, keepdims=True)
                p = jnp.exp(s - m)
                l = jnp.sum(p, axis=-1, keepdims=True)
                o = jnp.dot(p.astype(BF16), v, preferred_element_type=F32) / l
                lse = jnp.broadcast_to(m + jnp.log(l), (BAND, HEAD_DIM))
                if dil == 1:
                    og_ref[g, start:start + BAND, :] = o
                    lg_ref[g, start:start + BAND, :] = lse
                else:
                    og_ref[g, pl.ds(start, BAND, stride=dil), :] = o
                    lg_ref[g, pl.ds(start, BAND, stride=dil), :] = lse
        kt_refs[g][...] = k_refs[g][CHUNK - span:CHUNK, :]
        vt_refs[g][...] = v_refs[g][CHUNK - span:CHUNK, :]

    rows_per_step = 256

    def combine(i, carry):
        rows = pl.ds(pl.multiple_of(i * rows_per_step, rows_per_step), rows_per_step)
        l0, l1, l2 = lg_ref[0, rows, :], lg_ref[1, rows, :], lg_ref[2, rows, :]
        m = jnp.maximum(jnp.maximum(l0, l1), l2)
        e0, e1, e2 = jnp.exp(l0 - m), jnp.exp(l1 - m), jnp.exp(l2 - m)
        o = (e0 * og_ref[0, rows, :] + e1 * og_ref[1, rows, :] + e2 * og_ref[2, rows, :]) / (e0 + e1 + e2)
        o_ref[rows, :] = o.astype(o_ref.dtype)
        return carry

    lax.fori_loop(0, CHUNK // rows_per_step, combine, 0)


def _band_attention(qkv, slopes):
    seq = qkv.shape[0]
    blk = (CHUNK, HEAD_DIM)
    per_tensor = N_GROUPS * HEADS
    in_specs = [pl.BlockSpec((1, 8, HEAD_DIM), lambda h, c: (h, 0, 0))]
    for g in range(N_GROUPS):
        for tensor in range(3):
            in_specs.append(pl.BlockSpec(
                blk, lambda h, c, g=g, tensor=tensor: (c, tensor * per_tensor + g * HEADS + h)))
    tails = []
    for _, dil in DIL_GROUPS:
        tails += [pltpu.VMEM((BAND * dil, HEAD_DIM), F32)] * 2
    return pl.pallas_call(
        _band_kernel,
        grid=(HEADS, seq // CHUNK),
        in_specs=in_specs,
        out_specs=pl.BlockSpec(blk, lambda h, c: (c, h)),
        out_shape=jax.ShapeDtypeStruct((seq, GROUP_W), BF16),
        scratch_shapes=tails + [pltpu.VMEM((N_GROUPS, CHUNK, HEAD_DIM), F32),
                                pltpu.VMEM((N_GROUPS, CHUNK, HEAD_DIM), F32)],
        compiler_params=_params(2),
        name="band_attn",
    )(slopes, *([qkv] * 9))


def _mem_attn_kernel(q_ref, mk_ref, mv_ref, o_ref):
    for h in range(HEADS):
        hs = slice(h * HEAD_DIM, (h + 1) * HEAD_DIM)
        q = q_ref[:, hs]
        k = mk_ref[:, hs].astype(BF16)
        v = mv_ref[:, hs].astype(BF16)
        s = lax.dot_general(q, k, (((1,), (1,)), ((), ())), preferred_element_type=F32) * SCALE
        m = jnp.max(s, axis=-1, keepdims=True)
        p = jnp.exp(s - m)
        l = jnp.sum(p, axis=-1, keepdims=True)
        o = jnp.dot(p.astype(BF16), v, preferred_element_type=F32) / l
        o_ref[:, hs] = o.astype(o_ref.dtype)


def _mem_attention(p_all, mem_kv, *, tq):
    seq = p_all.shape[0]
    return pl.pallas_call(
        _mem_attn_kernel,
        grid=(seq // tq,),
        in_specs=[pl.BlockSpec((tq, GROUP_W), lambda i: (i, COL_QM // GROUP_W)),
                  pl.BlockSpec((MEM_TOKENS, GROUP_W), lambda i: (0, 0)),
                  pl.BlockSpec((MEM_TOKENS, GROUP_W), lambda i: (0, 1))],
        out_specs=pl.BlockSpec((tq, GROUP_W), lambda i: (i, 0)),
        out_shape=jax.ShapeDtypeStruct((seq, GROUP_W), BF16),
        compiler_params=_params(1),
        name="mem_attn",
    )(p_all, mem_kv, mem_kv)


def _conv_kernel(b_ref, c_ref, v_ref, st_ref, w_ref, o_ref, ns_ref, carry_ref, *, t_last):
    i = pl.program_id(1)

    @pl.when(i == 0)
    def _():
        carry_ref[0:2, :] = st_ref[0]

    u = c_ref[0].astype(F32) * v_ref[0].astype(F32)
    c2 = carry_ref[0:1, :]
    c1 = carry_ref[1:2, :]
    row = lax.broadcasted_iota(jnp.int32, u.shape, 0)
    u1 = jnp.where(row == 0, c1, pltpu.roll(u, 1, 0))
    u2 = jnp.where(row == 0, c2, jnp.where(row == 1, c1, pltpu.roll(u, 2, 0)))
    o = b_ref[0].astype(F32) * (w_ref[0:1, :] * u2 + w_ref[1:2, :] * u1 + w_ref[2:3, :] * u)
    o_ref[0] = o.astype(o_ref.dtype)
    last2 = u[t_last - 2:t_last, :]
    carry_ref[0:2, :] = last2
    ns_ref[0] = last2


def _conv(p3, state, conv_w, *, tm, t_last):
    nb, t = p3.shape[0], p3.shape[1]
    blk = (1, tm, CONV_W)
    return pl.pallas_call(
        functools.partial(_conv_kernel, t_last=t_last),
        grid=(nb, t // tm),
        in_specs=[
            pl.BlockSpec(blk, lambda b, i: (b, i, COL_B // CONV_W)),
            pl.BlockSpec(blk, lambda b, i: (b, i, COL_C // CONV_W)),
            pl.BlockSpec(blk, lambda b, i: (b, i, COL_V // CONV_W)),
            pl.BlockSpec((1, 2, CONV_W), lambda b, i: (b, 0, 0)),
            pl.BlockSpec((3, CONV_W), lambda b, i: (0, 0)),
        ],
        out_specs=[pl.BlockSpec(blk, lambda b, i: (b, i, 0)),
                   pl.BlockSpec((1, 2, CONV_W), lambda b, i: (b, 0, 0))],
        out_shape=[jax.ShapeDtypeStruct((nb, t, CONV_W), BF16),
                   jax.ShapeDtypeStruct((nb, 2, CONV_W), F32)],
        scratch_shapes=[pltpu.VMEM((8, CONV_W), F32)],
        compiler_params=_params(2),
        name="conv",
    )(p3, p3, p3, state, conv_w)


def _merge_kernel(oc_ref, oa_ref, om_ref, gc_ref, ga_ref, gm_ref, h_ref,
                  wc_ref, wa_ref, wm_ref, wo_ref, g_ref, b_ref, y_ref, yb_ref):
    mc = jnp.dot(oc_ref[...], wc_ref[...], preferred_element_type=F32)
    ma = jnp.dot(oa_ref[...], wa_ref[...], preferred_element_type=F32)
    mm = jnp.dot(om_ref[...], wm_ref[...], preferred_element_type=F32)
    mix = (jax.nn.sigmoid(gc_ref[...].astype(F32)) * mc
           + jax.nn.sigmoid(ga_ref[...].astype(F32)) * ma
           + jax.nn.sigmoid(gm_ref[...].astype(F32)) * mm)
    z = ALPHA * h_ref[...] + jnp.dot(mix.astype(BF16), wo_ref[...], preferred_element_type=F32)
    y = _layer_norm_rows(z, g_ref[...], b_ref[...])
    y_ref[...] = y
    yb_ref[...] = y.astype(BF16)


def _merge(o_conv, o_attn, o_mem, p_all, h, w_c, w_a, w_m, w_o, g, b, *, tm):
    rows = h.shape[0]
    single = pl.Buffered(1)
    row_blk = lambda w: pl.BlockSpec((tm, w), lambda i: (i, 0))
    gate_blk = lambda col: pl.BlockSpec((tm, D_MODEL), lambda i: (i, col // D_MODEL))
    whole = lambda a: pl.BlockSpec(a.shape, lambda i: (0, 0), pipeline_mode=single)
    return pl.pallas_call(
        _merge_kernel,
        grid=(rows // tm,),
        in_specs=[row_blk(CONV_W), row_blk(GROUP_W), row_blk(GROUP_W),
                  gate_blk(COL_GC), gate_blk(COL_GA), gate_blk(COL_GM), row_blk(D_MODEL),
                  whole(w_c), whole(w_a), whole(w_m), whole(w_o), whole(g), whole(b)],
        out_specs=[row_blk(D_MODEL), row_blk(D_MODEL)],
        out_shape=[jax.ShapeDtypeStruct((rows, D_MODEL), F32),
                   jax.ShapeDtypeStruct((rows, D_MODEL), BF16)],
        compiler_params=_params(1),
        name="merge",
    )(o_conv, o_attn, o_mem, p_all, p_all, p_all, h, w_c, w_a, w_m, w_o, g, b)


def _over_quad(x, op):
    x = op(x, pltpu.roll(x, HEADS, 0))
    return op(x, pltpu.roll(x, 2 * HEADS, 0))


def _lane_dot(k3, q):
    return jnp.sum(k3 * q, axis=-1, keepdims=True) * SCALE


def _quad_attention(k3, v3, qt, bias3, k_new, v_new, bias_new):
    s = _lane_dot(k3, qt[None]) + bias3
    m = jnp.max(s, axis=0)
    if k_new is not None:
        s_new = _lane_dot(k_new, qt) + bias_new
        m = jnp.maximum(m, s_new)
    m = _over_quad(m, jnp.maximum)
    e = jnp.exp(s - m[None])
    l = jnp.sum(e, axis=0)
    acc = jnp.sum(e * v3, axis=0)
    if k_new is not None:
        e_new = jnp.exp(s_new - m)
        l = l + e_new
        acc = acc + e_new * v_new
    l = _over_quad(l, jnp.add)
    acc = _over_quad(acc, jnp.add)
    return acc / l, m + jnp.log(l)


def _sample_kernel(qd_ref, qt_ref, kn_ref, vn_ref, sl_ref,
                   k0_ref, v0_ref, k1_ref, v1_ref, k2_ref, v2_ref, mk_ref, mv_ref,
                   oa_ref, om_ref, nk0_ref, nv0_ref, nk1_ref, nv1_ref, nk2_ref, nv2_ref, *, t_new):
    k_refs = (k0_ref, k1_ref, k2_ref)
    v_refs = (v0_ref, v1_ref, v2_ref)
    quad16 = lax.broadcasted_iota(jnp.int32, (ROWS16, HEAD_DIM), 0) >> 2

    outs, lses = [], []

    win0 = DIL_GROUPS[0][0]
    a_idx = lax.broadcasted_iota(jnp.int32, (win0 // 4, ROWS16, HEAD_DIM), 0)
    b_idx = lax.broadcasted_iota(jnp.int32, (win0 // 4, ROWS16, HEAD_DIM), 1) >> 2
    tok = a_idx * 4 + b_idx
    slope0 = sl_ref[0]
    k3, v3 = k0_ref[0], v0_ref[0]
    o0 = jnp.zeros((ROWS16, HEAD_DIM), F32)
    l0 = jnp.zeros((ROWS16, HEAD_DIM), F32)
    for t in range(t_new):
        bias3 = jnp.where(tok >= t, -(slope0[None] * (win0 + t - tok).astype(F32)), NEG_INF)
        bias_new = jnp.where(quad16 <= t, -(slope0 * (t - quad16).astype(F32)), NEG_INF)
        o_t, l_t = _quad_attention(k3, v3, qt_ref[0, 0, t], bias3, kn_ref[0, 0], vn_ref[0, 0], bias_new)
        o0 = jnp.where(quad16 == t, o_t, o0)
        l0 = jnp.where(quad16 == t, l_t, l0)
    outs.append(o0)
    lses.append(l0)

    for g in (1, 2):
        win, dil = DIL_GROUPS[g]
        q16 = qd_ref[0, g - 1]
        k3 = k_refs[g][0, :, 0:ROWS16, :]
        v3 = v_refs[g][0, :, 0:ROWS16, :]
        i_idx = lax.broadcasted_iota(jnp.int32, (BAND, ROWS16, HEAD_DIM), 0)
        bias3 = -(sl_ref[g][None] * (win - i_idx * dil).astype(F32))
        s = _lane_dot(k3, q16[None]) + bias3
        s_new = _lane_dot(kn_ref[0, g], q16)
        m = jnp.maximum(jnp.max(s, axis=0), s_new)
        e = jnp.exp(s - m[None])
        e_new = jnp.exp(s_new - m)
        l = jnp.sum(e, axis=0) + e_new
        acc = jnp.sum(e * v3, axis=0) + e_new * vn_ref[0, g]
        outs.append(acc / l)
        lses.append(m + jnp.log(l))

    m = jnp.maximum(jnp.maximum(lses[0], lses[1]), lses[2])
    w = [jnp.exp(x - m) for x in lses]
    oa_ref[0] = (w[0] * outs[0] + w[1] * outs[1] + w[2] * outs[2]) / (w[0] + w[1] + w[2])

    k3, v3 = mk_ref[0], mv_ref[0]
    om = jnp.zeros((ROWS16, HEAD_DIM), F32)
    for t in range(t_new):
        o_t, _ = _quad_attention(k3, v3, qt_ref[0, 1, t], jnp.zeros((1, ROWS16, HEAD_DIM), F32), None, None, None)
        om = jnp.where(quad16 == t, o_t, om)
    om_ref[0] = om

    for g, (buf_ref, new_ref, out_ref) in enumerate(((k0_ref, kn_ref, nk0_ref), (v0_ref, vn_ref, nv0_ref),
                                                     (k1_ref, kn_ref, nk1_ref), (v1_ref, vn_ref, nv1_ref),
                                                     (k2_ref, kn_ref, nk2_ref), (v2_ref, vn_ref, nv2_ref))):
        grp = g // 2
        tiles, rows = buf_ref.shape[1], buf_ref.shape[2]
        keep = rows - ROWS16
        if keep:
            out_ref[0, :, 0:keep, :] = buf_ref[0, :, ROWS16:rows, :]
        out_ref[0, 0:tiles - 1, keep:rows, :] = buf_ref[0, 1:tiles, 0:ROWS16, :]
        out_ref[0, tiles - 1, keep:rows, :] = new_ref[0, grp]


def _sample_attention(qd, qt, kn, vn, slopes16, k_states, v_states, mem_k, mem_v, *, t_new):
    nb = qd.shape[0]
    states = []
    for g in range(N_GROUPS):
        states += [k_states[g], v_states[g]]
    whole = lambda a: pl.BlockSpec((1,) + a.shape[1:], lambda b: (b,) + (0,) * (a.ndim - 1))
    o_spec = pl.BlockSpec((1, ROWS16, HEAD_DIM), lambda b: (b, 0, 0))
    o_shape = jax.ShapeDtypeStruct((nb, ROWS16, HEAD_DIM), F32)
    return pl.pallas_call(
        functools.partial(_sample_kernel, t_new=t_new),
        grid=(nb,),
        in_specs=[whole(qd), whole(qt), whole(kn), whole(vn),
                  pl.BlockSpec(slopes16.shape, lambda b: (0, 0, 0))]
                 + [whole(s) for s in states] + [whole(mem_k), whole(mem_v)],
        out_specs=[o_spec, o_spec] + [whole(s) for s in states],
        out_shape=[o_shape, o_shape] + [jax.ShapeDtypeStruct(s.shape, F32) for s in states],
        compiler_params=_params(1),
        name="sample_attn",
    )(qd, qt, kn, vn, slopes16, *states, mem_k, mem_v)


def kernel(x_prompt, x_sample, mem_prompt, state_conv, state_k_w128, state_v_w128, state_k_w512, state_v_w512, state_k_w2048, state_v_w2048, cache_mem_k, cache_mem_v, ln1_g, ln1_b, w_ffn1_gu, w_ffn1_down, w_in, conv_w, w_mem_kv, w_br_conv, w_br_attn, w_br_mem, w_o, ln2_g, ln2_b, w_ffn2_gu, w_ffn2_down, ln3_g, ln3_b):
    depth = w_in.shape[0]
    seq = x_prompt.shape[1]
    nb, t_new = x_sample.shape[0], x_sample.shape[1]
    assert depth == 1 and x_prompt.shape[0] == 1 and seq % CHUNK == 0
    assert t_new * HEADS == ROWS16 and t_new <= DIL_GROUPS[1][1]

    wb = lambda w: w[0].astype(BF16)
    wc_b, wa_b, wm_b, wo_b = wb(w_br_conv), wb(w_br_attn), wb(w_br_mem), wb(w_o)
    n_ff_blocks = D_FF // 512
    cw = conv_w[0]
    slopes_lane = np.broadcast_to(SLOPES[:, :, None], (N_GROUPS, HEADS, HEAD_DIM))
    slopes_band = jnp.asarray(np.pad(slopes_lane.transpose(1, 0, 2), ((0, 0), (0, 8 - N_GROUPS), (0, 0))))
    slopes16 = jnp.asarray(np.tile(slopes_lane, (1, ROWS16 // HEADS, 1)))

    rows_s = nb * SAMPLE_PAD
    xs = jnp.pad(x_sample, ((0, 0), (0, SAMPLE_PAD - t_new), (0, 0))).reshape(rows_s, D_MODEL)
    s1, s1b, w1g, w1u, w1d = _ffn(xs, w_ffn1_gu[0], w_ffn1_gu[0], w_ffn1_down[0], ln1_g, ln1_b,
                                  tm=rows_s, tf=512, up_blk0=n_ff_blocks, emit_bf16=True)
    ps_all, win_b = _matmul(s1b, w_in[0], tm=rows_s, tn=512, out_dtype=F32, name="in_proj_s", emit_bf16=True)

    h1, h1b = _ffn(x_prompt[0], w1g, w1u, w1d, ln1_g, ln1_b, tm=1024, tf=512)
    p_all, qkv = _in_proj(h1b, win_b, tm=1024, tn=512)
    mem_kv = _matmul(mem_prompt[0].astype(BF16), w_mem_kv[0], tm=MEM_TOKENS, tn=512, out_dtype=F32, name="mem_kv")
    o_attn = _band_attention(qkv, slopes_band)
    o_mem = _mem_attention(p_all, mem_kv, tq=512)
    o_conv, conv_p = _conv(p_all[None], jnp.zeros((1, 2, CONV_W), F32), cw, tm=512, t_last=512)
    h2, _ = _merge(o_conv[0], o_attn, o_mem, p_all, h1, wc_b, wa_b, wm_b, wo_b, ln2_g, ln2_b, tm=256)

    prompt_states = []
    for g, (win, _) in enumerate(DIL_GROUPS):
        for tensor in (1, 2):
            c0 = tensor * N_GROUPS * GROUP_W + g * GROUP_W
            prompt_states.append(qkv[seq - win:, c0:c0 + GROUP_W].reshape(1, 1, win, HEADS, HEAD_DIM))
    mem_k_p = mem_kv[:, :GROUP_W].reshape(1, 1, MEM_TOKENS, HEADS, HEAD_DIM)
    mem_v_p = mem_kv[:, GROUP_W:].reshape(1, 1, MEM_TOKENS, HEADS, HEAD_DIM)

    ps3 = ps_all.reshape(nb, SAMPLE_PAD, IN_W)

    real = ps3[:, :t_new]
    heads = lambda a, n: a.reshape(nb, t_new, n, HEADS, HEAD_DIM)
    q, k, v = (heads(real[..., c:c + N_GROUPS * GROUP_W], N_GROUPS) for c in (COL_Q, COL_K, COL_VA))
    rows_th = lambda a: a.transpose(0, 2, 1, 3, 4).reshape(nb, a.shape[2], ROWS16, HEAD_DIM)
    over_quad = lambda a: jnp.broadcast_to(a[:, :, None], (nb, t_new, ROWS16 // HEADS, HEADS, HEAD_DIM)
                                           ).reshape(nb, t_new, ROWS16, HEAD_DIM)
    qd = rows_th(q[:, :, 1:])
    qm = real[..., COL_QM:COL_QM + GROUP_W].reshape(nb, t_new, HEADS, HEAD_DIM)
    qt = jnp.stack([over_quad(q[:, :, 0]), over_quad(qm)], axis=1)
    kn, vn = rows_th(k), rows_th(v)

    def tiles(state):
        win = state.shape[2]
        per_tile = 4 if win <= DIL_GROUPS[1][0] else DIL_GROUPS[2][1]
        return state.reshape(nb, win // per_tile, per_tile * HEADS, HEAD_DIM)

    k_states = [tiles(s) for s in (state_k_w128, state_k_w512, state_k_w2048)]
    v_states = [tiles(s) for s in (state_v_w128, state_v_w512, state_v_w2048)]
    outs = _sample_attention(qd, qt, kn, vn, slopes16, k_states, v_states,
                             tiles(cache_mem_k), tiles(cache_mem_v), t_new=t_new)
    pad_rows = lambda a, w: jnp.pad(a.reshape(nb, t_new, w), ((0, 0), (0, SAMPLE_PAD - t_new), (0, 0))
                                    ).reshape(rows_s, w).astype(BF16)
    o_attn_s, o_mem_s = pad_rows(outs[0], GROUP_W), pad_rows(outs[1], GROUP_W)
    sample_states = [s.reshape(1, nb, -1, HEADS, HEAD_DIM) for s in outs[2:]]

    o_conv_s, conv_s = _conv(ps3, state_conv[0], cw, tm=SAMPLE_PAD, t_last=t_new)
    s2, _ = _merge(o_conv_s.reshape(rows_s, CONV_W), o_attn_s, o_mem_s, ps_all, s1,
                   wc_b, wa_b, wm_b, wo_b, ln2_g, ln2_b, tm=rows_s)
    y_s, _, w2g, w2u, w2d = _ffn(s2, w_ffn2_gu[0], w_ffn2_gu[0], w_ffn2_down[0], ln3_g, ln3_b,
                                 tm=rows_s, tf=512, up_blk0=n_ff_blocks, emit_bf16=True)
    y_sample = y_s.reshape(nb, SAMPLE_PAD, D_MODEL)[:, :t_new]
    y_prompt, _ = _ffn(h2, w2g, w2u, w2d, ln3_g, ln3_b, tm=1024, tf=512)

    return (y_prompt[None], y_sample, conv_p[None], *prompt_states, mem_k_p, mem_v_p,
            conv_s[None], *sample_states)
```

```python
import functools

import jax
import jax.numpy as jnp
import numpy as np
from jax import lax
from jax.experimental import pallas as pl
from jax.experimental.pallas import tpu as pltpu

F32 = jnp.float32
BF16 = jnp.bfloat16

D_MODEL = 2048
D_FF = 5632
HEAD_DIM = 128
HEADS = 4
GROUP_W = HEADS * HEAD_DIM
DIL_GROUPS = ((128, 1), (512, 4), (2048, 16))
N_GROUPS = len(DIL_GROUPS)
BAND = 128
CHUNK = BAND * DIL_GROUPS[-1][1]
CONV_W = 1024
MEM_TOKENS = 256
IN_W = 14336
COL_B, COL_C, COL_V = 0, 1024, 2048
COL_Q, COL_K, COL_VA = 3072, 4608, 6144
COL_QM = 7680
COL_GC, COL_GA, COL_GM = 8192, 10240, 12288
QKV_W = COL_QM - COL_Q
ALPHA = 2.0 ** 0.25
LN_EPS = 1e-5
NEG_INF = -1e30
SCALE = HEAD_DIM ** -0.5
SLOPES = np.exp2(np.float32(-8.0) * np.arange(1, 13, dtype=np.float32) / np.float32(12)).reshape(3, 4)

SAMPLE_PAD = 8
ROWS16 = 16
VMEM_LIMIT = 56 * 1024 * 1024


def _params(n_axes, vmem=VMEM_LIMIT):
    return pltpu.CompilerParams(dimension_semantics=("arbitrary",) * n_axes, vmem_limit_bytes=vmem)


def _layer_norm_rows(z, g, b):
    mu = jnp.mean(z, axis=-1, keepdims=True)
    zc = z - mu
    var = jnp.mean(zc * zc, axis=-1, keepdims=True)
    return zc * lax.rsqrt(var + LN_EPS) * g + b


def _window_copies(b, src_refs, dst_refs, sems):
    copies = []
    for src, dst in zip(src_refs, dst_refs):
        tiles, rows = src.shape[1], src.shape[2]
        keep = rows - ROWS16
        if keep:
            copies.append((src.at[b, :, pl.ds(ROWS16, keep), :], dst.at[b, :, pl.ds(0, keep), :]))
        copies.append((src.at[b, pl.ds(1, tiles - 1), pl.ds(0, ROWS16), :],
                       dst.at[b, pl.ds(0, tiles - 1), pl.ds(keep, ROWS16), :]))
        copies.append((src.at[b, pl.ds(tiles - 1, 1), pl.ds(keep, ROWS16), :],
                       dst.at[b, pl.ds(tiles - 1, 1), pl.ds(keep, ROWS16), :]))
    return [pltpu.make_async_copy(s, d, sems.at[n]) for n, (s, d) in enumerate(copies)]


def _ffn_kernel(*refs, n_ff, row_chunk, emit_bf16, n_windows, n_requests):
    x_ref, wg_ref, wu_ref, wd_ref, g_ref, b_ref = refs[:6]
    win_src = refs[6:6 + n_windows]
    y_ref, yb_ref = refs[6 + n_windows:8 + n_windows]
    rest = refs[8 + n_windows:]
    if emit_bf16:
        wb_refs, rest = rest[:3], rest[3:]
    win_dst, rest = rest[:n_windows], rest[n_windows:]
    xb_ref = rest[0]
    j = pl.program_id(1)

    if n_windows:
        sems = rest[1]
        half = n_windows // 2
        step = pl.program_id(0) * n_ff + j
        for part in (0, 1):
            part_refs = win_src[part * half:(part + 1) * half], win_dst[part * half:(part + 1) * half]

            @pl.when((step < 2 * n_requests) & (step % 2 == part))
            def _():
                for copy in _window_copies(step // 2, *part_refs, sems.at[part]):
                    copy.start()

            @pl.when((step >= 1) & (step <= 2 * n_requests) & (step % 2 != part))
            def _():
                for copy in _window_copies((step - 1) // 2, *part_refs, sems.at[part]):
                    copy.wait()

    @pl.when(j == 0)
    def _():
        xb_ref[...] = x_ref[...].astype(BF16)
        y_ref[...] = jnp.zeros(y_ref.shape, F32)

    wg, wu, wd = wg_ref[...], wu_ref[...], wd_ref[...]
    if emit_bf16:
        wg, wu, wd = wg.astype(BF16), wu.astype(BF16), wd.astype(BF16)
        wb_refs[0][...], wb_refs[1][...], wb_refs[2][...] = wg, wu, wd
    xb = xb_ref[...]
    gate = jnp.dot(xb, wg, preferred_element_type=F32)
    up = jnp.dot(xb, wu, preferred_element_type=F32)
    act = (gate * jax.nn.sigmoid(gate) * up).astype(BF16)
    y_ref[...] += jnp.dot(act, wd, preferred_element_type=F32)

    @pl.when(j == n_ff - 1)
    def _():
        def chunk(c, carry):
            rows = pl.ds(pl.multiple_of(c * row_chunk, row_chunk), row_chunk)
            z = ALPHA * x_ref[rows, :] + 0.5 * y_ref[rows, :]
            y = _layer_norm_rows(z, g_ref[...], b_ref[...])
            y_ref[rows, :] = y
            yb_ref[rows, :] = y.astype(BF16)
            return carry
        lax.fori_loop(0, x_ref.shape[0] // row_chunk, chunk, 0)


def _ffn(x, w_gate, w_up, w_down, g, b, *, tm, tf, up_blk0=0, emit_bf16=False, windows=()):
    rows = x.shape[0]
    n_ff = D_FF // tf
    n_steps = (rows // tm) * n_ff
    n_requests = windows[0].shape[0] if windows else 0
    assert not emit_bf16 or rows == tm
    assert n_steps > 2 * n_requests
    single = pl.Buffered(1)
    hbm = pl.BlockSpec(memory_space=pl.ANY)
    w_col = pl.BlockSpec((D_MODEL, tf), lambda i, j: (0, j))
    w_row = pl.BlockSpec((tf, D_MODEL), lambda i, j: (j, 0))
    row_tile = pl.BlockSpec((tm, D_MODEL), lambda i, j: (i, 0), pipeline_mode=single)
    out_specs = [row_tile, row_tile]
    out_shape = [jax.ShapeDtypeStruct((rows, D_MODEL), F32), jax.ShapeDtypeStruct((rows, D_MODEL), BF16)]
    scratch = [pltpu.VMEM((tm, D_MODEL), BF16)]
    if emit_bf16:
        out_specs += [w_col, w_col, w_row]
        out_shape += [jax.ShapeDtypeStruct((D_MODEL, D_FF), BF16)] * 2 + [jax.ShapeDtypeStruct((D_FF, D_MODEL), BF16)]
    if windows:
        out_specs += [hbm] * len(windows)
        out_shape += [jax.ShapeDtypeStruct(w.shape, w.dtype) for w in windows]
        copies_per_part = sum(2 + (w.shape[2] > ROWS16) for w in windows[:len(windows) // 2])
        scratch.append(pltpu.SemaphoreType.DMA((2, copies_per_part)))
    return pl.pallas_call(
        functools.partial(_ffn_kernel, n_ff=n_ff, row_chunk=min(tm, 128), emit_bf16=emit_bf16,
                          n_windows=len(windows), n_requests=n_requests),
        grid=(rows // tm, n_ff),
        in_specs=[
            row_tile,
            w_col,
            pl.BlockSpec((D_MODEL, tf), lambda i, j: (0, j + up_blk0)),
            w_row,
            pl.BlockSpec((1, D_MODEL), lambda i, j: (0, 0)),
            pl.BlockSpec((1, D_MODEL), lambda i, j: (0, 0)),
        ] + [hbm] * len(windows),
        out_specs=out_specs,
        out_shape=out_shape,
        scratch_shapes=scratch,
        compiler_params=_params(2),
        name="ffn_cast" if emit_bf16 else "ffn",
    )(x, w_gate, w_up, w_down, g, b, *windows)


def _mm_kernel(x_ref, w_ref, o_ref, *wb_ref):
    w = w_ref[...].astype(BF16)
    if wb_ref:
        wb_ref[0][...] = w
    o_ref[...] = jnp.dot(x_ref[...], w, preferred_element_type=F32).astype(o_ref.dtype)


def _matmul(x, w, *, tm, tn, out_dtype, name, emit_bf16=False):
    rows, k = x.shape
    n_cols = w.shape[1]
    assert not emit_bf16 or rows == tm
    w_spec = pl.BlockSpec((k, tn), lambda i, j: (0, j))
    out_specs = [pl.BlockSpec((tm, tn), lambda i, j: (i, j))]
    out_shape = [jax.ShapeDtypeStruct((rows, n_cols), out_dtype)]
    if emit_bf16:
        out_specs.append(w_spec)
        out_shape.append(jax.ShapeDtypeStruct(w.shape, BF16))
    out = pl.pallas_call(
        _mm_kernel,
        grid=(rows // tm, n_cols // tn),
        in_specs=[pl.BlockSpec((tm, k), lambda i, j: (i, 0)), w_spec],
        out_specs=out_specs,
        out_shape=out_shape,
        compiler_params=_params(2),
        name=name,
    )(x, w)
    return out if emit_bf16 else out[0]


def _in_proj_kernel(x_ref, w_ref, p_ref, qkv_ref, *, j_lo, j_hi):
    j = pl.program_id(1)
    acc = jnp.dot(x_ref[...], w_ref[...], preferred_element_type=F32)
    p_ref[...] = acc.astype(BF16)

    @pl.when((j >= j_lo) & (j < j_hi))
    def _():
        qkv_ref[...] = acc


def _in_proj(x, w, *, tm, tn):
    rows, k = x.shape
    j_lo, j_hi = COL_Q // tn, COL_QM // tn
    return pl.pallas_call(
        functools.partial(_in_proj_kernel, j_lo=j_lo, j_hi=j_hi),
        grid=(rows // tm, IN_W // tn),
        in_specs=[pl.BlockSpec((tm, k), lambda i, j: (i, 0)),
                  pl.BlockSpec((k, tn), lambda i, j: (0, j))],
        out_specs=[pl.BlockSpec((tm, tn), lambda i, j: (i, j)),
                   pl.BlockSpec((tm, tn), lambda i, j: (i, jnp.clip(j - j_lo, 0, j_hi - j_lo - 1)))],
        out_shape=[jax.ShapeDtypeStruct((rows, IN_W), BF16),
                   jax.ShapeDtypeStruct((rows, QKV_W), F32)],
        compiler_params=_params(2),
        name="in_proj",
    )(x, w)


def _rows(ref, start, dil):
    if dil == 1:
        return ref[start:start + BAND, :]
    return ref[pl.ds(start, BAND, stride=dil), :]


def _band_kernel(sl_ref, q0_ref, k0_ref, v0_ref, q1_ref, k1_ref, v1_ref, q2_ref, k2_ref, v2_ref,
                 o_ref, kt0_ref, vt0_ref, kt1_ref, vt1_ref, kt2_ref, vt2_ref, og_ref, lg_ref):
    c = pl.program_id(1)
    q_refs = (q0_ref, q1_ref, q2_ref)
    k_refs = (k0_ref, k1_ref, k2_ref)
    v_refs = (v0_ref, v1_ref, v2_ref)
    kt_refs = (kt0_ref, kt1_ref, kt2_ref)
    vt_refs = (vt0_ref, vt1_ref, vt2_ref)

    @pl.when(c == 0)
    def _():
        for ref in kt_refs + vt_refs:
            ref[...] = jnp.zeros(ref.shape, F32)

    qi = lax.broadcasted_iota(jnp.int32, (BAND, 2 * BAND), 0)
    ki = lax.broadcasted_iota(jnp.int32, (BAND, 2 * BAND), 1)
    rel = BAND + qi - ki
    valid = (rel >= 0) & (rel <= BAND)
    no_prev = jnp.where(ki < BAND, jnp.where(c == 0, NEG_INF, 0.0), 0.0)

    for g, (_, dil) in enumerate(DIL_GROUPS):
        slope = sl_ref[0, g:g + 1, 0:1]
        bias = jnp.where(valid, -(slope * (rel * dil).astype(F32)), NEG_INF)
        bias_first = bias + no_prev
        span = BAND * dil
        for r in range(dil):
            for n in range(CHUNK // span):
                start = r + n * span
                q = _rows(q_refs[g], start, dil).astype(BF16)
                if n == 0:
                    k_prev, v_prev = _rows(kt_refs[g], r, dil), _rows(vt_refs[g], r, dil)
                else:
                    k_prev, v_prev = _rows(k_refs[g], start - span, dil), _rows(v_refs[g], start - span, dil)
                k = jnp.concatenate([k_prev, _rows(k_refs[g], start, dil)], axis=0).astype(BF16)
                v = jnp.concatenate([v_prev, _rows(v_refs[g], start, dil)], axis=0).astype(BF16)
                s = lax.dot_general(q, k, (((1,), (1,)), ((), ())), preferred_element_type=F32) * SCALE
                s = s + (bias_first if n == 0 else bias)
                m = jnp.max(s, axis=-1, keepdims=True)
                p = jnp.exp(s - m)
                l = jnp.sum(p, axis=-1, keepdims=True)
                o = jnp.dot(p.astype(BF16), v, preferred_element_type=F32) / l
                lse = jnp.broadcast_to(m + jnp.log(l), (BAND, HEAD_DIM))
                if dil == 1:
                    og_ref[g, start:start + BAND, :] = o
                    lg_ref[g, start:start + BAND, :] = lse
                else:
                    og_ref[g, pl.ds(start, BAND, stride=dil), :] = o
                    lg_ref[g, pl.ds(start, BAND, stride=dil), :] = lse
        kt_refs[g][...] = k_refs[g][CHUNK - span:CHUNK, :]
        vt_refs[g][...] = v_refs[g][CHUNK - span:CHUNK, :]

    rows_per_step = 256

    def combine(i, carry):
        rows = pl.ds(pl.multiple_of(i * rows_per_step, rows_per_step), rows_per_step)
        l0, l1, l2 = lg_ref[0, rows, :], lg_ref[1, rows, :], lg_ref[2, rows, :]
        m = jnp.maximum(jnp.maximum(l0, l1), l2)
        e0, e1, e2 = jnp.exp(l0 - m), jnp.exp(l1 - m), jnp.exp(l2 - m)
        o = (e0 * og_ref[0, rows, :] + e1 * og_ref[1, rows, :] + e2 * og_ref[2, rows, :]) / (e0 + e1 + e2)
        o_ref[rows, :] = o.astype(o_ref.dtype)
        return carry

    lax.fori_loop(0, CHUNK // rows_per_step, combine, 0)


def _band_attention(qkv, slopes):
    seq = qkv.shape[0]
    blk = (CHUNK, HEAD_DIM)
    per_tensor = N_GROUPS * HEADS
    in_specs = [pl.BlockSpec((1, 8, HEAD_DIM), lambda h, c: (h, 0, 0))]
    for g in range(N_GROUPS):
        for tensor in range(3):
            in_specs.append(pl.BlockSpec(
                blk, lambda h, c, g=g, tensor=tensor: (c, tensor * per_tensor + g * HEADS + h)))
    tails = []
    for _, dil in DIL_GROUPS:
        tails += [pltpu.VMEM((BAND * dil, HEAD_DIM), F32)] * 2
    return pl.pallas_call(
        _band_kernel,
        grid=(HEADS, seq // CHUNK),
        in_specs=in_specs,
        out_specs=pl.BlockSpec(blk, lambda h, c: (c, h)),
        out_shape=jax.ShapeDtypeStruct((seq, GROUP_W), BF16),
        scratch_shapes=tails + [pltpu.VMEM((N_GROUPS, CHUNK, HEAD_DIM), F32),
                                pltpu.VMEM((N_GROUPS, CHUNK, HEAD_DIM), F32)],
        compiler_params=_params(2),
        name="band_attn",
    )(slopes, *([qkv] * 9))


def _mem_attn_kernel(q_ref, mk_ref, mv_ref, o_ref):
    for h in range(HEADS):
        hs = slice(h * HEAD_DIM, (h + 1) * HEAD_DIM)
        q = q_ref[:, hs]
        k = mk_ref[:, hs].astype(BF16)
        v = mv_ref[:, hs].astype(BF16)
        s = lax.dot_general(q, k, (((1,), (1,)), ((), ())), preferred_element_type=F32) * SCALE
        m = jnp.max(s, axis=-1, keepdims=True)
        p = jnp.exp(s - m)
        l = jnp.sum(p, axis=-1, keepdims=True)
        o = jnp.dot(p.astype(BF16), v, preferred_element_type=F32) / l
        o_ref[:, hs] = o.astype(o_ref.dtype)


def _mem_attention(p_all, mem_kv, *, tq):
    seq = p_all.shape[0]
    return pl.pallas_call(
        _mem_attn_kernel,
        grid=(seq // tq,),
        in_specs=[pl.BlockSpec((tq, GROUP_W), lambda i: (i, COL_QM // GROUP_W)),
                  pl.BlockSpec((MEM_TOKENS, GROUP_W), lambda i: (0, 0)),
                  pl.BlockSpec((MEM_TOKENS, GROUP_W), lambda i: (0, 1))],
        out_specs=pl.BlockSpec((tq, GROUP_W), lambda i: (i, 0)),
        out_shape=jax.ShapeDtypeStruct((seq, GROUP_W), BF16),
        compiler_params=_params(1),
        name="mem_attn",
    )(p_all, mem_kv, mem_kv)


def _conv_kernel(b_ref, c_ref, v_ref, st_ref, w_ref, o_ref, ns_ref, carry_ref, *, t_last):
    i = pl.program_id(1)

    @pl.when(i == 0)
    def _():
        carry_ref[0:2, :] = st_ref[0]

    u = c_ref[0].astype(F32) * v_ref[0].astype(F32)
    c2 = carry_ref[0:1, :]
    c1 = carry_ref[1:2, :]
    row = lax.broadcasted_iota(jnp.int32, u.shape, 0)
    u1 = jnp.where(row == 0, c1, pltpu.roll(u, 1, 0))
    u2 = jnp.where(row == 0, c2, jnp.where(row == 1, c1, pltpu.roll(u, 2, 0)))
    o = b_ref[0].astype(F32) * (w_ref[0:1, :] * u2 + w_ref[1:2, :] * u1 + w_ref[2:3, :] * u)
    o_ref[0] = o.astype(o_ref.dtype)
    last2 = u[t_last - 2:t_last, :]
    carry_ref[0:2, :] = last2
    ns_ref[0] = last2


def _conv(p3, state, conv_w, *, tm, t_last):
    nb, t = p3.shape[0], p3.shape[1]
    blk = (1, tm, CONV_W)
    return pl.pallas_call(
        functools.partial(_conv_kernel, t_last=t_last),
        grid=(nb, t // tm),
        in_specs=[
            pl.BlockSpec(blk, lambda b, i: (b, i, COL_B // CONV_W)),
            pl.BlockSpec(blk, lambda b, i: (b, i, COL_C // CONV_W)),
            pl.BlockSpec(blk, lambda b, i: (b, i, COL_V // CONV_W)),
            pl.BlockSpec((1, 2, CONV_W), lambda b, i: (b, 0, 0)),
            pl.BlockSpec((3, CONV_W), lambda b, i: (0, 0)),
        ],
        out_specs=[pl.BlockSpec(blk, lambda b, i: (b, i, 0)),
                   pl.BlockSpec((1, 2, CONV_W), lambda b, i: (b, 0, 0))],
        out_shape=[jax.ShapeDtypeStruct((nb, t, CONV_W), BF16),
                   jax.ShapeDtypeStruct((nb, 2, CONV_W), F32)],
        scratch_shapes=[pltpu.VMEM((8, CONV_W), F32)],
        compiler_params=_params(2),
        name="conv",
    )(p3, p3, p3, state, conv_w)


def _merge_kernel(oc_ref, oa_ref, om_ref, gc_ref, ga_ref, gm_ref, h_ref,
                  wc_ref, wa_ref, wm_ref, wo_ref, g_ref, b_ref, y_ref, yb_ref):
    mc = jnp.dot(oc_ref[...], wc_ref[...], preferred_element_type=F32)
    ma = jnp.dot(oa_ref[...], wa_ref[...], preferred_element_type=F32)
    mm = jnp.dot(om_ref[...], wm_ref[...], preferred_element_type=F32)
    mix = (jax.nn.sigmoid(gc_ref[...].astype(F32)) * mc
           + jax.nn.sigmoid(ga_ref[...].astype(F32)) * ma
           + jax.nn.sigmoid(gm_ref[...].astype(F32)) * mm)
    z = ALPHA * h_ref[...] + jnp.dot(mix.astype(BF16), wo_ref[...], preferred_element_type=F32)
    y = _layer_norm_rows(z, g_ref[...], b_ref[...])
    y_ref[...] = y
    yb_ref[...] = y.astype(BF16)


def _merge(o_conv, o_attn, o_mem, p_all, h, w_c, w_a, w_m, w_o, g, b, *, tm):
    rows = h.shape[0]
    single = pl.Buffered(1)
    row_blk = lambda w: pl.BlockSpec((tm, w), lambda i: (i, 0))
    gate_blk = lambda col: pl.BlockSpec((tm, D_MODEL), lambda i: (i, col // D_MODEL))
    whole = lambda a: pl.BlockSpec(a.shape, lambda i: (0, 0), pipeline_mode=single)
    return pl.pallas_call(
        _merge_kernel,
        grid=(rows // tm,),
        in_specs=[row_blk(CONV_W), row_blk(GROUP_W), row_blk(GROUP_W),
                  gate_blk(COL_GC), gate_blk(COL_GA), gate_blk(COL_GM), row_blk(D_MODEL),
                  whole(w_c), whole(w_a), whole(w_m), whole(w_o), whole(g), whole(b)],
        out_specs=[row_blk(D_MODEL), row_blk(D_MODEL)],
        out_shape=[jax.ShapeDtypeStruct((rows, D_MODEL), F32),
                   jax.ShapeDtypeStruct((rows, D_MODEL), BF16)],
        compiler_params=_params(1),
        name="merge",
    )(o_conv, o_attn, o_mem, p_all, p_all, p_all, h, w_c, w_a, w_m, w_o, g, b)


def _over_quad(x, op):
    x = op(x, pltpu.roll(x, HEADS, 0))
    return op(x, pltpu.roll(x, 2 * HEADS, 0))


def _lane_dot(k3, q):
    return jnp.sum(k3 * q, axis=-1, keepdims=True) * SCALE


def _quad_attention(k3, v3, qt, bias3, k_new, v_new, bias_new):
    s = _lane_dot(k3, qt[None]) + bias3
    m = jnp.max(s, axis=0)
    if k_new is not None:
        s_new = _lane_dot(k_new, qt) + bias_new
        m = jnp.maximum(m, s_new)
    m = _over_quad(m, jnp.maximum)
    e = jnp.exp(s - m[None])
    l = jnp.sum(e, axis=0)
    acc = jnp.sum(e * v3, axis=0)
    if k_new is not None:
        e_new = jnp.exp(s_new - m)
        l = l + e_new
        acc = acc + e_new * v_new
    l = _over_quad(l, jnp.add)
    acc = _over_quad(acc, jnp.add)
    return acc / l, m + jnp.log(l)


def _sample_kernel(qd_ref, qt_ref, kn_ref, vn_ref, sl_ref,
                   k0_ref, v0_ref, k1_ref, v1_ref, k2_ref, v2_ref, mk_ref, mv_ref, *new_refs, t_new):
    k_refs = (k0_ref, k1_ref, k2_ref)
    v_refs = (v0_ref, v1_ref, v2_ref)
    oa_ref, om_ref = new_refs[N_GROUPS * 2], new_refs[N_GROUPS * 2 + 1]
    quad16 = lax.broadcasted_iota(jnp.int32, (ROWS16, HEAD_DIM), 0) >> 2

    outs, lses = [], []

    win0 = DIL_GROUPS[0][0]
    a_idx = lax.broadcasted_iota(jnp.int32, (win0 // 4, ROWS16, HEAD_DIM), 0)
    b_idx = lax.broadcasted_iota(jnp.int32, (win0 // 4, ROWS16, HEAD_DIM), 1) >> 2
    tok = a_idx * 4 + b_idx
    slope0 = sl_ref[0]
    k3, v3 = k0_ref[0], v0_ref[0]
    o0 = jnp.zeros((ROWS16, HEAD_DIM), F32)
    l0 = jnp.zeros((ROWS16, HEAD_DIM), F32)
    for t in range(t_new):
        bias3 = jnp.where(tok >= t, -(slope0[None] * (win0 + t - tok).astype(F32)), NEG_INF)
        bias_new = jnp.where(quad16 <= t, -(slope0 * (t - quad16).astype(F32)), NEG_INF)
        o_t, l_t = _quad_attention(k3, v3, qt_ref[0, 0, t], bias3, kn_ref[0, 0], vn_ref[0, 0], bias_new)
        o0 = jnp.where(quad16 == t, o_t, o0)
        l0 = jnp.where(quad16 == t, l_t, l0)
    outs.append(o0)
    lses.append(l0)

    for g in (1, 2):
        win, dil = DIL_GROUPS[g]
        q16 = qd_ref[0, g - 1]
        k3 = k_refs[g][0, :, 0:ROWS16, :]
        v3 = v_refs[g][0, :, 0:ROWS16, :]
        i_idx = lax.broadcasted_iota(jnp.int32, (BAND, ROWS16, HEAD_DIM), 0)
        bias3 = -(sl_ref[g][None] * (win - i_idx * dil).astype(F32))
        s = _lane_dot(k3, q16[None]) + bias3
        s_new = _lane_dot(kn_ref[0, g], q16)
        m = jnp.maximum(jnp.max(s, axis=0), s_new)
        e = jnp.exp(s - m[None])
        e_new = jnp.exp(s_new - m)
        l = jnp.sum(e, axis=0) + e_new
        acc = jnp.sum(e * v3, axis=0) + e_new * vn_ref[0, g]
        outs.append(acc / l)
        lses.append(m + jnp.log(l))

    m = jnp.maximum(jnp.maximum(lses[0], lses[1]), lses[2])
    w = [jnp.exp(x - m) for x in lses]
    oa_ref[0] = (w[0] * outs[0] + w[1] * outs[1] + w[2] * outs[2]) / (w[0] + w[1] + w[2])

    k3, v3 = mk_ref[0], mv_ref[0]
    om = jnp.zeros((ROWS16, HEAD_DIM), F32)
    for t in range(t_new):
        o_t, _ = _quad_attention(k3, v3, qt_ref[0, 1, t], jnp.zeros((1, ROWS16, HEAD_DIM), F32), None, None, None)
        om = jnp.where(quad16 == t, o_t, om)
    om_ref[0] = om

    for i, out_ref in enumerate(new_refs[N_GROUPS * 2 + 2:]):
        out_ref[0, 0] = (kn_ref, vn_ref)[i % 2][0, i // 2]


def _sample_attention(qd, qt, kn, vn, slopes16, k_states, v_states, shifted, mem_k, mem_v, *, t_new):
    nb = qd.shape[0]
    states = []
    for g in range(N_GROUPS):
        states += [k_states[g], v_states[g]]
    whole = lambda a: pl.BlockSpec((1,) + a.shape[1:], lambda b: (b,) + (0,) * (a.ndim - 1))
    head_rows = lambda a: pl.BlockSpec((1, a.shape[1], ROWS16, HEAD_DIM), lambda b: (b, 0, 0, 0))
    last_rows = lambda a: pl.BlockSpec((1, 1, ROWS16, HEAD_DIM),
                                       lambda b, a=a: (b, a.shape[1] - 1, a.shape[2] // ROWS16 - 1, 0))
    o_spec = pl.BlockSpec((1, ROWS16, HEAD_DIM), lambda b: (b, 0, 0))
    o_shape = jax.ShapeDtypeStruct((nb, ROWS16, HEAD_DIM), F32)
    n_in = 5 + len(states) + 2
    return pl.pallas_call(
        functools.partial(_sample_kernel, t_new=t_new),
        grid=(nb,),
        in_specs=[whole(qd), whole(qt), whole(kn), whole(vn),
                  pl.BlockSpec(slopes16.shape, lambda b: (0, 0, 0))]
                 + [head_rows(s) for s in states] + [whole(mem_k), whole(mem_v)]
                 + [pl.BlockSpec(memory_space=pl.ANY)] * len(shifted),
        out_specs=[o_spec, o_spec] + [last_rows(s) for s in shifted],
        out_shape=[o_shape, o_shape] + [jax.ShapeDtypeStruct(s.shape, F32) for s in shifted],
        input_output_aliases={n_in + i: 2 + i for i in range(len(shifted))},
        compiler_params=_params(1),
        name="sample_attn",
    )(qd, qt, kn, vn, slopes16, *states, mem_k, mem_v, *shifted)


def kernel(x_prompt, x_sample, mem_prompt, state_conv, state_k_w128, state_v_w128, state_k_w512, state_v_w512, state_k_w2048, state_v_w2048, cache_mem_k, cache_mem_v, ln1_g, ln1_b, w_ffn1_gu, w_ffn1_down, w_in, conv_w, w_mem_kv, w_br_conv, w_br_attn, w_br_mem, w_o, ln2_g, ln2_b, w_ffn2_gu, w_ffn2_down, ln3_g, ln3_b):
    depth = w_in.shape[0]
    seq = x_prompt.shape[1]
    nb, t_new = x_sample.shape[0], x_sample.shape[1]
    assert depth == 1 and x_prompt.shape[0] == 1 and seq % CHUNK == 0
    assert t_new * HEADS == ROWS16 and t_new <= DIL_GROUPS[1][1]

    wb = lambda w: w[0].astype(BF16)
    wc_b, wa_b, wm_b, wo_b = wb(w_br_conv), wb(w_br_attn), wb(w_br_mem), wb(w_o)
    n_ff_blocks = D_FF // 512
    cw = conv_w[0]
    slopes_lane = np.broadcast_to(SLOPES[:, :, None], (N_GROUPS, HEADS, HEAD_DIM))
    slopes_band = jnp.asarray(np.pad(slopes_lane.transpose(1, 0, 2), ((0, 0), (0, 8 - N_GROUPS), (0, 0))))
    slopes16 = jnp.asarray(np.tile(slopes_lane, (1, ROWS16 // HEADS, 1)))

    rows_s = nb * SAMPLE_PAD
    xs = jnp.pad(x_sample, ((0, 0), (0, SAMPLE_PAD - t_new), (0, 0))).reshape(rows_s, D_MODEL)
    s1, s1b, w1g, w1u, w1d = _ffn(xs, w_ffn1_gu[0], w_ffn1_gu[0], w_ffn1_down[0], ln1_g, ln1_b,
                                  tm=rows_s, tf=512, up_blk0=n_ff_blocks, emit_bf16=True)
    ps_all, win_b = _matmul(s1b, w_in[0], tm=rows_s, tn=512, out_dtype=F32, name="in_proj_s", emit_bf16=True)

    def tiles(state):
        win = state.shape[2]
        per_tile = 4 if win <= DIL_GROUPS[1][0] else DIL_GROUPS[2][1]
        return state.reshape(nb, win // per_tile, per_tile * HEADS, HEAD_DIM)

    k_states = [tiles(s) for s in (state_k_w128, state_k_w512, state_k_w2048)]
    v_states = [tiles(s) for s in (state_v_w128, state_v_w512, state_v_w2048)]
    h1, h1b, *shifted = _ffn(x_prompt[0], w1g, w1u, w1d, ln1_g, ln1_b, tm=1024, tf=512,
                             windows=k_states + v_states)
    shifted = [shifted[g + part * N_GROUPS] for g in range(N_GROUPS) for part in (0, 1)]
    p_all, qkv = _in_proj(h1b, win_b, tm=1024, tn=512)
    mem_kv = _matmul(mem_prompt[0].astype(BF16), w_mem_kv[0], tm=MEM_TOKENS, tn=512, out_dtype=F32, name="mem_kv")
    o_attn = _band_attention(qkv, slopes_band)
    o_mem = _mem_attention(p_all, mem_kv, tq=512)
    o_conv, conv_p = _conv(p_all[None], jnp.zeros((1, 2, CONV_W), F32), cw, tm=512, t_last=512)
    h2, _ = _merge(o_conv[0], o_attn, o_mem, p_all, h1, wc_b, wa_b, wm_b, wo_b, ln2_g, ln2_b, tm=256)

    prompt_states = []
    for g, (win, _) in enumerate(DIL_GROUPS):
        for tensor in (1, 2):
            c0 = tensor * N_GROUPS * GROUP_W + g * GROUP_W
            prompt_states.append(qkv[seq - win:, c0:c0 + GROUP_W].reshape(1, 1, win, HEADS, HEAD_DIM))
    mem_k_p = mem_kv[:, :GROUP_W].reshape(1, 1, MEM_TOKENS, HEADS, HEAD_DIM)
    mem_v_p = mem_kv[:, GROUP_W:].reshape(1, 1, MEM_TOKENS, HEADS, HEAD_DIM)

    ps3 = ps_all.reshape(nb, SAMPLE_PAD, IN_W)

    real = ps3[:, :t_new]
    heads = lambda a, n: a.reshape(nb, t_new, n, HEADS, HEAD_DIM)
    q, k, v = (heads(real[..., c:c + N_GROUPS * GROUP_W], N_GROUPS) for c in (COL_Q, COL_K, COL_VA))
    rows_th = lambda a: a.transpose(0, 2, 1, 3, 4).reshape(nb, a.shape[2], ROWS16, HEAD_DIM)
    over_quad = lambda a: jnp.broadcast_to(a[:, :, None], (nb, t_new, ROWS16 // HEADS, HEADS, HEAD_DIM)
                                           ).reshape(nb, t_new, ROWS16, HEAD_DIM)
    qd = rows_th(q[:, :, 1:])
    qm = real[..., COL_QM:COL_QM + GROUP_W].reshape(nb, t_new, HEADS, HEAD_DIM)
    qt = jnp.stack([over_quad(q[:, :, 0]), over_quad(qm)], axis=1)
    kn, vn = rows_th(k), rows_th(v)

    outs = _sample_attention(qd, qt, kn, vn, slopes16, k_states, v_states, shifted,
                             tiles(cache_mem_k), tiles(cache_mem_v), t_new=t_new)
    pad_rows = lambda a, w: jnp.pad(a.reshape(nb, t_new, w), ((0, 0), (0, SAMPLE_PAD - t_new), (0, 0))
                                    ).reshape(rows_s, w).astype(BF16)
    o_attn_s, o_mem_s = pad_rows(outs[0], GROUP_W), pad_rows(outs[1], GROUP_W)
    sample_states = [s.reshape(1, nb, -1, HEADS, HEAD_DIM) for s in outs[2:]]

    o_conv_s, conv_s = _conv(ps3, state_conv[0], cw, tm=SAMPLE_PAD, t_last=t_new)
    s2, _ = _merge(o_conv_s.reshape(rows_s, CONV_W), o_attn_s, o_mem_s, ps_all, s1,
                   wc_b, wa_b, wm_b, wo_b, ln2_g, ln2_b, tm=rows_s)
    y_s, _, w2g, w2u, w2d = _ffn(s2, w_ffn2_gu[0], w_ffn2_gu[0], w_ffn2_down[0], ln3_g, ln3_b,
                                 tm=rows_s, tf=512, up_blk0=n_ff_blocks, emit_bf16=True)
    y_sample = y_s.reshape(nb, SAMPLE_PAD, D_MODEL)[:, :t_new]
    y_prompt, _ = _ffn(h2, w2g, w2u, w2d, ln3_g, ln3_b, tm=1024, tf=512)

    return (y_prompt[None], y_sample, conv_p[None], *prompt_states, mem_k_p, mem_v_p,
            conv_s[None], *sample_states)
```

```python
import functools

import jax
import jax.numpy as jnp
import numpy as np
from jax import lax
from jax.experimental import pallas as pl
from jax.experimental.pallas import tpu as pltpu

F32 = jnp.float32
BF16 = jnp.bfloat16

D_MODEL = 2048
D_FF = 5632
HEAD_DIM = 128
HEADS = 4
GROUP_W = HEADS * HEAD_DIM
DIL_GROUPS = ((128, 1), (512, 4), (2048, 16))
N_GROUPS = len(DIL_GROUPS)
BAND = 128
CHUNK = BAND * DIL_GROUPS[-1][1]
CONV_W = 1024
MEM_TOKENS = 256
IN_W = 14336
COL_B, COL_C, COL_V = 0, 1024, 2048
COL_Q, COL_K, COL_VA = 3072, 4608, 6144
COL_QM = 7680
COL_GC, COL_GA, COL_GM = 8192, 10240, 12288
QKV_W = COL_QM - COL_Q
ALPHA = 2.0 ** 0.25
LN_EPS = 1e-5
NEG_INF = -1e30
SCALE = HEAD_DIM ** -0.5
SLOPES = np.exp2(np.float32(-8.0) * np.arange(1, 13, dtype=np.float32) / np.float32(12)).reshape(3, 4)

SAMPLE_PAD = 8
ROWS16 = 16
VMEM_LIMIT = 56 * 1024 * 1024


def _params(n_axes, vmem=VMEM_LIMIT):
    return pltpu.CompilerParams(dimension_semantics=("arbitrary",) * n_axes, vmem_limit_bytes=vmem)


def _layer_norm_rows(z, g, b):
    mu = jnp.mean(z, axis=-1, keepdims=True)
    zc = z - mu
    var = jnp.mean(zc * zc, axis=-1, keepdims=True)
    return zc * lax.rsqrt(var + LN_EPS) * g + b


def _window_loads(b, src_refs, stage_refs, slot, sems):
    return [pltpu.make_async_copy(src.at[b], stage.at[slot], sems.at[n])
            for n, (src, stage) in enumerate(zip(src_refs, stage_refs))]


def _window_stores(b, stage_refs, dst_refs, slot, sems):
    copies = []
    for stage, dst in zip(stage_refs, dst_refs):
        tiles, rows = dst.shape[1], dst.shape[2]
        keep = rows - ROWS16
        if keep:
            copies.append((stage.at[slot, :, pl.ds(ROWS16, keep), :], dst.at[b, :, pl.ds(0, keep), :]))
        copies.append((stage.at[slot, pl.ds(1, tiles - 1), pl.ds(0, ROWS16), :],
                       dst.at[b, pl.ds(0, tiles - 1), pl.ds(keep, ROWS16), :]))
        copies.append((stage.at[slot, pl.ds(tiles - 1, 1), pl.ds(keep, ROWS16), :],
                       dst.at[b, pl.ds(tiles - 1, 1), pl.ds(keep, ROWS16), :]))
    return [pltpu.make_async_copy(s, d, sems.at[n]) for n, (s, d) in enumerate(copies)]


def _ffn_kernel(*refs, n_ff, row_chunk, emit_bf16, n_windows, n_requests):
    x_ref, wg_ref, wu_ref, wd_ref, g_ref, b_ref = refs[:6]
    win_src = refs[6:6 + n_windows]
    y_ref, yb_ref = refs[6 + n_windows:8 + n_windows]
    rest = refs[8 + n_windows:]
    if emit_bf16:
        wb_refs, rest = rest[:3], rest[3:]
    win_dst, rest = rest[:n_windows], rest[n_windows:]
    xb_ref = rest[0]
    j = pl.program_id(1)

    if n_windows:
        half = n_windows // 2
        stage_refs, load_sems, store_sems = rest[1:1 + half], rest[1 + half], rest[2 + half]
        step = pl.program_id(0) * n_ff + j
        n_items = 2 * n_requests
        for part in (0, 1):
            srcs, dsts = win_src[part * half:(part + 1) * half], win_dst[part * half:(part + 1) * half]
            mine = step % 2 == part

            @pl.when(mine & (step >= 2) & (step < n_items + 2))
            def _():
                for copy in _window_stores((step - 2) // 2, stage_refs, dsts, part, store_sems.at[part]):
                    copy.wait()

            @pl.when(mine & (step < n_items))
            def _():
                for copy in _window_loads(step // 2, srcs, stage_refs, part, load_sems.at[part]):
                    copy.start()

            @pl.when(jnp.logical_not(mine) & (step >= 1) & (step < n_items + 1))
            def _():
                for copy in _window_loads((step - 1) // 2, srcs, stage_refs, part, load_sems.at[part]):
                    copy.wait()
                for copy in _window_stores((step - 1) // 2, stage_refs, dsts, part, store_sems.at[part]):
                    copy.start()

    @pl.when(j == 0)
    def _():
        xb_ref[...] = x_ref[...].astype(BF16)
        y_ref[...] = jnp.zeros(y_ref.shape, F32)

    wg, wu, wd = wg_ref[...], wu_ref[...], wd_ref[...]
    if emit_bf16:
        wg, wu, wd = wg.astype(BF16), wu.astype(BF16), wd.astype(BF16)
        wb_refs[0][...], wb_refs[1][...], wb_refs[2][...] = wg, wu, wd
    xb = xb_ref[...]
    gate = jnp.dot(xb, wg, preferred_element_type=F32)
    up = jnp.dot(xb, wu, preferred_element_type=F32)
    act = (gate * jax.nn.sigmoid(gate) * up).astype(BF16)
    y_ref[...] += jnp.dot(act, wd, preferred_element_type=F32)

    @pl.when(j == n_ff - 1)
    def _():
        def chunk(c, carry):
            rows = pl.ds(pl.multiple_of(c * row_chunk, row_chunk), row_chunk)
            z = ALPHA * x_ref[rows, :] + 0.5 * y_ref[rows, :]
            y = _layer_norm_rows(z, g_ref[...], b_ref[...])
            y_ref[rows, :] = y
            yb_ref[rows, :] = y.astype(BF16)
            return carry
        lax.fori_loop(0, x_ref.shape[0] // row_chunk, chunk, 0)


def _ffn(x, w_gate, w_up, w_down, g, b, *, tm, tf, up_blk0=0, emit_bf16=False, windows=()):
    rows = x.shape[0]
    n_ff = D_FF // tf
    n_steps = (rows // tm) * n_ff
    n_requests = windows[0].shape[0] if windows else 0
    assert not emit_bf16 or rows == tm
    assert n_steps >= 2 * n_requests + 2
    single = pl.Buffered(1)
    hbm = pl.BlockSpec(memory_space=pl.ANY)
    w_col = pl.BlockSpec((D_MODEL, tf), lambda i, j: (0, j))
    w_row = pl.BlockSpec((tf, D_MODEL), lambda i, j: (j, 0))
    row_tile = pl.BlockSpec((tm, D_MODEL), lambda i, j: (i, 0), pipeline_mode=single)
    out_specs = [row_tile, row_tile]
    out_shape = [jax.ShapeDtypeStruct((rows, D_MODEL), F32), jax.ShapeDtypeStruct((rows, D_MODEL), BF16)]
    scratch = [pltpu.VMEM((tm, D_MODEL), BF16)]
    if emit_bf16:
        out_specs += [w_col, w_col, w_row]
        out_shape += [jax.ShapeDtypeStruct((D_MODEL, D_FF), BF16)] * 2 + [jax.ShapeDtypeStruct((D_FF, D_MODEL), BF16)]
    if windows:
        out_specs += [hbm] * len(windows)
        out_shape += [jax.ShapeDtypeStruct(w.shape, w.dtype) for w in windows]
        half = windows[:len(windows) // 2]
        scratch += [pltpu.VMEM((2,) + w.shape[1:], w.dtype) for w in half]
        scratch += [pltpu.SemaphoreType.DMA((2, len(half))),
                    pltpu.SemaphoreType.DMA((2, sum(2 + (w.shape[2] > ROWS16) for w in half)))]
    return pl.pallas_call(
        functools.partial(_ffn_kernel, n_ff=n_ff, row_chunk=min(tm, 128), emit_bf16=emit_bf16,
                          n_windows=len(windows), n_requests=n_requests),
        grid=(rows // tm, n_ff),
        in_specs=[
            row_tile,
            w_col,
            pl.BlockSpec((D_MODEL, tf), lambda i, j: (0, j + up_blk0)),
            w_row,
            pl.BlockSpec((1, D_MODEL), lambda i, j: (0, 0)),
            pl.BlockSpec((1, D_MODEL), lambda i, j: (0, 0)),
        ] + [hbm] * len(windows),
        out_specs=out_specs,
        out_shape=out_shape,
        scratch_shapes=scratch,
        compiler_params=_params(2),
        name="ffn_cast" if emit_bf16 else "ffn",
    )(x, w_gate, w_up, w_down, g, b, *windows)


def _mm_kernel(x_ref, w_ref, o_ref, *wb_ref):
    w = w_ref[...].astype(BF16)
    if wb_ref:
        wb_ref[0][...] = w
    o_ref[...] = jnp.dot(x_ref[...], w, preferred_element_type=F32).astype(o_ref.dtype)


def _matmul(x, w, *, tm, tn, out_dtype, name, emit_bf16=False):
    rows, k = x.shape
    n_cols = w.shape[1]
    assert not emit_bf16 or rows == tm
    w_spec = pl.BlockSpec((k, tn), lambda i, j: (0, j))
    out_specs = [pl.BlockSpec((tm, tn), lambda i, j: (i, j))]
    out_shape = [jax.ShapeDtypeStruct((rows, n_cols), out_dtype)]
    if emit_bf16:
        out_specs.append(w_spec)
        out_shape.append(jax.ShapeDtypeStruct(w.shape, BF16))
    out = pl.pallas_call(
        _mm_kernel,
        grid=(rows // tm, n_cols // tn),
        in_specs=[pl.BlockSpec((tm, k), lambda i, j: (i, 0)), w_spec],
        out_specs=out_specs,
        out_shape=out_shape,
        compiler_params=_params(2),
        name=name,
    )(x, w)
    return out if emit_bf16 else out[0]


def _in_proj_kernel(x_ref, w_ref, p_ref, qkv_ref, *, j_lo, j_hi):
    j = pl.program_id(1)
    acc = jnp.dot(x_ref[...], w_ref[...], preferred_element_type=F32)
    p_ref[...] = acc.astype(BF16)

    @pl.when((j >= j_lo) & (j < j_hi))
    def _():
        qkv_ref[...] = acc


def _in_proj(x, w, *, tm, tn):
    rows, k = x.shape
    j_lo, j_hi = COL_Q // tn, COL_QM // tn
    return pl.pallas_call(
        functools.partial(_in_proj_kernel, j_lo=j_lo, j_hi=j_hi),
        grid=(rows // tm, IN_W // tn),
        in_specs=[pl.BlockSpec((tm, k), lambda i, j: (i, 0)),
                  pl.BlockSpec((k, tn), lambda i, j: (0, j))],
        out_specs=[pl.BlockSpec((tm, tn), lambda i, j: (i, j)),
                   pl.BlockSpec((tm, tn), lambda i, j: (i, jnp.clip(j - j_lo, 0, j_hi - j_lo - 1)))],
        out_shape=[jax.ShapeDtypeStruct((rows, IN_W), BF16),
                   jax.ShapeDtypeStruct((rows, QKV_W), F32)],
        compiler_params=_params(2),
        name="in_proj",
    )(x, w)


def _rows(ref, start, dil):
    if dil == 1:
        return ref[start:start + BAND, :]
    return ref[pl.ds(start, BAND, stride=dil), :]


def _band_kernel(sl_ref, q0_ref, k0_ref, v0_ref, q1_ref, k1_ref, v1_ref, q2_ref, k2_ref, v2_ref,
                 o_ref, kt0_ref, vt0_ref, kt1_ref, vt1_ref, kt2_ref, vt2_ref, og_ref, lg_ref):
    c = pl.program_id(1)
    q_refs = (q0_ref, q1_ref, q2_ref)
    k_refs = (k0_ref, k1_ref, k2_ref)
    v_refs = (v0_ref, v1_ref, v2_ref)
    kt_refs = (kt0_ref, kt1_ref, kt2_ref)
    vt_refs = (vt0_ref, vt1_ref, vt2_ref)

    @pl.when(c == 0)
    def _():
        for ref in kt_refs + vt_refs:
            ref[...] = jnp.zeros(ref.shape, F32)

    qi = lax.broadcasted_iota(jnp.int32, (BAND, 2 * BAND), 0)
    ki = lax.broadcasted_iota(jnp.int32, (BAND, 2 * BAND), 1)
    rel = BAND + qi - ki
    valid = (rel >= 0) & (rel <= BAND)
    no_prev = jnp.where(ki < BAND, jnp.where(c == 0, NEG_INF, 0.0), 0.0)

    for g, (_, dil) in enumerate(DIL_GROUPS):
        slope = sl_ref[0, g:g + 1, 0:1]
        bias = jnp.where(valid, -(slope * (rel * dil).astype(F32)), NEG_INF)
        bias_first = bias + no_prev
        span = BAND * dil
        for r in range(dil):
            for n in range(CHUNK // span):
                start = r + n * span
                q = _rows(q_refs[g], start, dil).astype(BF16)
                if n == 0:
                    k_prev, v_prev = _rows(kt_refs[g], r, dil), _rows(vt_refs[g], r, dil)
                else:
                    k_prev, v_prev = _rows(k_refs[g], start - span, dil), _rows(v_refs[g], start - span, dil)
                k = jnp.concatenate([k_prev, _rows(k_refs[g], start, dil)], axis=0).astype(BF16)
                v = jnp.concatenate([v_prev, _rows(v_refs[g], start, dil)], axis=0).astype(BF16)
                s = lax.dot_general(q, k, (((1,), (1,)), ((), ())), preferred_element_type=F32) * SCALE
                s = s + (bias_first if n == 0 else bias)
                m = jnp.max(s, axis=-1, keepdims=True)
                p = jnp.exp(s - m)
                l = jnp.sum(p, axis=-1, keepdims=True)
                o = jnp.dot(p.astype(BF16), v, preferred_element_type=F32) / l
                lse = jnp.broadcast_to(m + jnp.log(l), (BAND, HEAD_DIM))
                if dil == 1:
                    og_ref[g, start:start + BAND, :] = o
                    lg_ref[g, start:start + BAND, :] = lse
                else:
                    og_ref[g, pl.ds(start, BAND, stride=dil), :] = o
                    lg_ref[g, pl.ds(start, BAND, stride=dil), :] = lse
        kt_refs[g][...] = k_refs[g][CHUNK - span:CHUNK, :]
        vt_refs[g][...] = v_refs[g][CHUNK - span:CHUNK, :]

    rows_per_step = 256

    def combine(i, carry):
        rows = pl.ds(pl.multiple_of(i * rows_per_step, rows_per_step), rows_per_step)
        l0, l1, l2 = lg_ref[0, rows, :], lg_ref[1, rows, :], lg_ref[2, rows, :]
        m = jnp.maximum(jnp.maximum(l0, l1), l2)
        e0, e1, e2 = jnp.exp(l0 - m), jnp.exp(l1 - m), jnp.exp(l2 - m)
        o = (e0 * og_ref[0, rows, :] + e1 * og_ref[1, rows, :] + e2 * og_ref[2, rows, :]) / (e0 + e1 + e2)
        o_ref[rows, :] = o.astype(o_ref.dtype)
        return carry

    lax.fori_loop(0, CHUNK // rows_per_step, combine, 0)


def _band_attention(qkv, slopes):
    seq = qkv.shape[0]
    blk = (CHUNK, HEAD_DIM)
    per_tensor = N_GROUPS * HEADS
    in_specs = [pl.BlockSpec((1, 8, HEAD_DIM), lambda h, c: (h, 0, 0))]
    for g in range(N_GROUPS):
        for tensor in range(3):
            in_specs.append(pl.BlockSpec(
                blk, lambda h, c, g=g, tensor=tensor: (c, tensor * per_tensor + g * HEADS + h)))
    tails = []
    for _, dil in DIL_GROUPS:
        tails += [pltpu.VMEM((BAND * dil, HEAD_DIM), F32)] * 2
    return pl.pallas_call(
        _band_kernel,
        grid=(HEADS, seq // CHUNK),
        in_specs=in_specs,
        out_specs=pl.BlockSpec(blk, lambda h, c: (c, h)),
        out_shape=jax.ShapeDtypeStruct((seq, GROUP_W), BF16),
        scratch_shapes=tails + [pltpu.VMEM((N_GROUPS, CHUNK, HEAD_DIM), F32),
                                pltpu.VMEM((N_GROUPS, CHUNK, HEAD_DIM), F32)],
        compiler_params=_params(2),
        name="band_attn",
    )(slopes, *([qkv] * 9))


def _mem_attn_kernel(q_ref, mk_ref, mv_ref, o_ref):
    for h in range(HEADS):
        hs = slice(h * HEAD_DIM, (h + 1) * HEAD_DIM)
        q = q_ref[:, hs]
        k = mk_ref[:, hs].astype(BF16)
        v = mv_ref[:, hs].astype(BF16)
        s = lax.dot_general(q, k, (((1,), (1,)), ((), ())), preferred_element_type=F32) * SCALE
        m = jnp.max(s, axis=-1, keepdims=True)
        p = jnp.exp(s - m)
        l = jnp.sum(p, axis=-1, keepdims=True)
        o = jnp.dot(p.astype(BF16), v, preferred_element_type=F32) / l
        o_ref[:, hs] = o.astype(o_ref.dtype)


def _mem_attention(p_all, mem_kv, *, tq):
    seq = p_all.shape[0]
    return pl.pallas_call(
        _mem_attn_kernel,
        grid=(seq // tq,),
        in_specs=[pl.BlockSpec((tq, GROUP_W), lambda i: (i, COL_QM // GROUP_W)),
                  pl.BlockSpec((MEM_TOKENS, GROUP_W), lambda i: (0, 0)),
                  pl.BlockSpec((MEM_TOKENS, GROUP_W), lambda i: (0, 1))],
        out_specs=pl.BlockSpec((tq, GROUP_W), lambda i: (i, 0)),
        out_shape=jax.ShapeDtypeStruct((seq, GROUP_W), BF16),
        compiler_params=_params(1),
        name="mem_attn",
    )(p_all, mem_kv, mem_kv)


def _conv_kernel(b_ref, c_ref, v_ref, st_ref, w_ref, o_ref, ns_ref, carry_ref, *, t_last):
    i = pl.program_id(1)

    @pl.when(i == 0)
    def _():
        carry_ref[0:2, :] = st_ref[0]

    u = c_ref[0].astype(F32) * v_ref[0].astype(F32)
    c2 = carry_ref[0:1, :]
    c1 = carry_ref[1:2, :]
    row = lax.broadcasted_iota(jnp.int32, u.shape, 0)
    u1 = jnp.where(row == 0, c1, pltpu.roll(u, 1, 0))
    u2 = jnp.where(row == 0, c2, jnp.where(row == 1, c1, pltpu.roll(u, 2, 0)))
    o = b_ref[0].astype(F32) * (w_ref[0:1, :] * u2 + w_ref[1:2, :] * u1 + w_ref[2:3, :] * u)
    o_ref[0] = o.astype(o_ref.dtype)
    last2 = u[t_last - 2:t_last, :]
    carry_ref[0:2, :] = last2
    ns_ref[0] = last2


def _conv(p3, state, conv_w, *, tm, t_last):
    nb, t = p3.shape[0], p3.shape[1]
    blk = (1, tm, CONV_W)
    return pl.pallas_call(
        functools.partial(_conv_kernel, t_last=t_last),
        grid=(nb, t // tm),
        in_specs=[
            pl.BlockSpec(blk, lambda b, i: (b, i, COL_B // CONV_W)),
            pl.BlockSpec(blk, lambda b, i: (b, i, COL_C // CONV_W)),
            pl.BlockSpec(blk, lambda b, i: (b, i, COL_V // CONV_W)),
            pl.BlockSpec((1, 2, CONV_W), lambda b, i: (b, 0, 0)),
            pl.BlockSpec((3, CONV_W), lambda b, i: (0, 0)),
        ],
        out_specs=[pl.BlockSpec(blk, lambda b, i: (b, i, 0)),
                   pl.BlockSpec((1, 2, CONV_W), lambda b, i: (b, 0, 0))],
        out_shape=[jax.ShapeDtypeStruct((nb, t, CONV_W), BF16),
                   jax.ShapeDtypeStruct((nb, 2, CONV_W), F32)],
        scratch_shapes=[pltpu.VMEM((8, CONV_W), F32)],
        compiler_params=_params(2),
        name="conv",
    )(p3, p3, p3, state, conv_w)


def _merge_kernel(oc_ref, oa_ref, om_ref, gc_ref, ga_ref, gm_ref, h_ref,
                  wc_ref, wa_ref, wm_ref, wo_ref, g_ref, b_ref, y_ref, yb_ref):
    mc = jnp.dot(oc_ref[...], wc_ref[...], preferred_element_type=F32)
    ma = jnp.dot(oa_ref[...], wa_ref[...], preferred_element_type=F32)
    mm = jnp.dot(om_ref[...], wm_ref[...], preferred_element_type=F32)
    mix = (jax.nn.sigmoid(gc_ref[...].astype(F32)) * mc
           + jax.nn.sigmoid(ga_ref[...].astype(F32)) * ma
           + jax.nn.sigmoid(gm_ref[...].astype(F32)) * mm)
    z = ALPHA * h_ref[...] + jnp.dot(mix.astype(BF16), wo_ref[...], preferred_element_type=F32)
    y = _layer_norm_rows(z, g_ref[...], b_ref[...])
    y_ref[...] = y
    yb_ref[...] = y.astype(BF16)


def _merge(o_conv, o_attn, o_mem, p_all, h, w_c, w_a, w_m, w_o, g, b, *, tm):
    rows = h.shape[0]
    single = pl.Buffered(1)
    row_blk = lambda w: pl.BlockSpec((tm, w), lambda i: (i, 0))
    gate_blk = lambda col: pl.BlockSpec((tm, D_MODEL), lambda i: (i, col // D_MODEL))
    whole = lambda a: pl.BlockSpec(a.shape, lambda i: (0, 0), pipeline_mode=single)
    return pl.pallas_call(
        _merge_kernel,
        grid=(rows // tm,),
        in_specs=[row_blk(CONV_W), row_blk(GROUP_W), row_blk(GROUP_W),
                  gate_blk(COL_GC), gate_blk(COL_GA), gate_blk(COL_GM), row_blk(D_MODEL),
                  whole(w_c), whole(w_a), whole(w_m), whole(w_o), whole(g), whole(b)],
        out_specs=[row_blk(D_MODEL), row_blk(D_MODEL)],
        out_shape=[jax.ShapeDtypeStruct((rows, D_MODEL), F32),
                   jax.ShapeDtypeStruct((rows, D_MODEL), BF16)],
        compiler_params=_params(1),
        name="merge",
    )(o_conv, o_attn, o_mem, p_all, p_all, p_all, h, w_c, w_a, w_m, w_o, g, b)


def _over_quad(x, op):
    x = op(x, pltpu.roll(x, HEADS, 0))
    return op(x, pltpu.roll(x, 2 * HEADS, 0))


def _lane_dot(k3, q):
    return jnp.sum(k3 * q, axis=-1, keepdims=True) * SCALE


def _quad_attention(k3, v3, qt, bias3, k_new, v_new, bias_new):
    s = _lane_dot(k3, qt[None]) + bias3
    m = jnp.max(s, axis=0)
    if k_new is not None:
        s_new = _lane_dot(k_new, qt) + bias_new
        m = jnp.maximum(m, s_new)
    m = _over_quad(m, jnp.maximum)
    e = jnp.exp(s - m[None])
    l = jnp.sum(e, axis=0)
    acc = jnp.sum(e * v3, axis=0)
    if k_new is not None:
        e_new = jnp.exp(s_new - m)
        l = l + e_new
        acc = acc + e_new * v_new
    l = _over_quad(l, jnp.add)
    acc = _over_quad(acc, jnp.add)
    return acc / l, m + jnp.log(l)


def _sample_kernel(qd_ref, qt_ref, kn_ref, vn_ref, sl_ref,
                   k0_ref, v0_ref, k1_ref, v1_ref, k2_ref, v2_ref, mk_ref, mv_ref, *new_refs, t_new):
    k_refs = (k0_ref, k1_ref, k2_ref)
    v_refs = (v0_ref, v1_ref, v2_ref)
    oa_ref, om_ref = new_refs[N_GROUPS * 2], new_refs[N_GROUPS * 2 + 1]
    quad16 = lax.broadcasted_iota(jnp.int32, (ROWS16, HEAD_DIM), 0) >> 2

    outs, lses = [], []

    win0 = DIL_GROUPS[0][0]
    a_idx = lax.broadcasted_iota(jnp.int32, (win0 // 4, ROWS16, HEAD_DIM), 0)
    b_idx = lax.broadcasted_iota(jnp.int32, (win0 // 4, ROWS16, HEAD_DIM), 1) >> 2
    tok = a_idx * 4 + b_idx
    slope0 = sl_ref[0]
    k3, v3 = k0_ref[0], v0_ref[0]
    o0 = jnp.zeros((ROWS16, HEAD_DIM), F32)
    l0 = jnp.zeros((ROWS16, HEAD_DIM), F32)
    for t in range(t_new):
        bias3 = jnp.where(tok >= t, -(slope0[None] * (win0 + t - tok).astype(F32)), NEG_INF)
        bias_new = jnp.where(quad16 <= t, -(slope0 * (t - quad16).astype(F32)), NEG_INF)
        o_t, l_t = _quad_attention(k3, v3, qt_ref[0, 0, t], bias3, kn_ref[0, 0], vn_ref[0, 0], bias_new)
        o0 = jnp.where(quad16 == t, o_t, o0)
        l0 = jnp.where(quad16 == t, l_t, l0)
    outs.append(o0)
    lses.append(l0)

    for g in (1, 2):
        win, dil = DIL_GROUPS[g]
        q16 = qd_ref[0, g - 1]
        k3 = k_refs[g][0, :, 0:ROWS16, :]
        v3 = v_refs[g][0, :, 0:ROWS16, :]
        i_idx = lax.broadcasted_iota(jnp.int32, (BAND, ROWS16, HEAD_DIM), 0)
        bias3 = -(sl_ref[g][None] * (win - i_idx * dil).astype(F32))
        s = _lane_dot(k3, q16[None]) + bias3
        s_new = _lane_dot(kn_ref[0, g], q16)
        m = jnp.maximum(jnp.max(s, axis=0), s_new)
        e = jnp.exp(s - m[None])
        e_new = jnp.exp(s_new - m)
        l = jnp.sum(e, axis=0) + e_new
        acc = jnp.sum(e * v3, axis=0) + e_new * vn_ref[0, g]
        outs.append(acc / l)
        lses.append(m + jnp.log(l))

    m = jnp.maximum(jnp.maximum(lses[0], lses[1]), lses[2])
    w = [jnp.exp(x - m) for x in lses]
    oa_ref[0] = (w[0] * outs[0] + w[1] * outs[1] + w[2] * outs[2]) / (w[0] + w[1] + w[2])

    k3, v3 = mk_ref[0], mv_ref[0]
    om = jnp.zeros((ROWS16, HEAD_DIM), F32)
    for t in range(t_new):
        o_t, _ = _quad_attention(k3, v3, qt_ref[0, 1, t], jnp.zeros((1, ROWS16, HEAD_DIM), F32), None, None, None)
        om = jnp.where(quad16 == t, o_t, om)
    om_ref[0] = om

    for i, out_ref in enumerate(new_refs[N_GROUPS * 2 + 2:]):
        out_ref[0, 0] = (kn_ref, vn_ref)[i % 2][0, i // 2]


def _sample_attention(qd, qt, kn, vn, slopes16, k_states, v_states, shifted, mem_k, mem_v, *, t_new):
    nb = qd.shape[0]
    states = []
    for g in range(N_GROUPS):
        states += [k_states[g], v_states[g]]
    whole = lambda a: pl.BlockSpec((1,) + a.shape[1:], lambda b: (b,) + (0,) * (a.ndim - 1))
    head_rows = lambda a: pl.BlockSpec((1, a.shape[1], ROWS16, HEAD_DIM), lambda b: (b, 0, 0, 0))
    last_rows = lambda a: pl.BlockSpec((1, 1, ROWS16, HEAD_DIM),
                                       lambda b, a=a: (b, a.shape[1] - 1, a.shape[2] // ROWS16 - 1, 0))
    o_spec = pl.BlockSpec((1, ROWS16, HEAD_DIM), lambda b: (b, 0, 0))
    o_shape = jax.ShapeDtypeStruct((nb, ROWS16, HEAD_DIM), F32)
    n_in = 5 + len(states) + 2
    return pl.pallas_call(
        functools.partial(_sample_kernel, t_new=t_new),
        grid=(nb,),
        in_specs=[whole(qd), whole(qt), whole(kn), whole(vn),
                  pl.BlockSpec(slopes16.shape, lambda b: (0, 0, 0))]
                 + [head_rows(s) for s in states] + [whole(mem_k), whole(mem_v)]
                 + [pl.BlockSpec(memory_space=pl.ANY)] * len(shifted),
        out_specs=[o_spec, o_spec] + [last_rows(s) for s in shifted],
        out_shape=[o_shape, o_shape] + [jax.ShapeDtypeStruct(s.shape, F32) for s in shifted],
        input_output_aliases={n_in + i: 2 + i for i in range(len(shifted))},
        compiler_params=_params(1),
        name="sample_attn",
    )(qd, qt, kn, vn, slopes16, *states, mem_k, mem_v, *shifted)


def kernel(x_prompt, x_sample, mem_prompt, state_conv, state_k_w128, state_v_w128, state_k_w512, state_v_w512, state_k_w2048, state_v_w2048, cache_mem_k, cache_mem_v, ln1_g, ln1_b, w_ffn1_gu, w_ffn1_down, w_in, conv_w, w_mem_kv, w_br_conv, w_br_attn, w_br_mem, w_o, ln2_g, ln2_b, w_ffn2_gu, w_ffn2_down, ln3_g, ln3_b):
    depth = w_in.shape[0]
    seq = x_prompt.shape[1]
    nb, t_new = x_sample.shape[0], x_sample.shape[1]
    assert depth == 1 and x_prompt.shape[0] == 1 and seq % CHUNK == 0
    assert t_new * HEADS == ROWS16 and t_new <= DIL_GROUPS[1][1]

    wb = lambda w: w[0].astype(BF16)
    wc_b, wa_b, wm_b, wo_b = wb(w_br_conv), wb(w_br_attn), wb(w_br_mem), wb(w_o)
    n_ff_blocks = D_FF // 512
    cw = conv_w[0]
    slopes_lane = np.broadcast_to(SLOPES[:, :, None], (N_GROUPS, HEADS, HEAD_DIM))
    slopes_band = jnp.asarray(np.pad(slopes_lane.transpose(1, 0, 2), ((0, 0), (0, 8 - N_GROUPS), (0, 0))))
    slopes16 = jnp.asarray(np.tile(slopes_lane, (1, ROWS16 // HEADS, 1)))

    rows_s = nb * SAMPLE_PAD
    xs = jnp.pad(x_sample, ((0, 0), (0, SAMPLE_PAD - t_new), (0, 0))).reshape(rows_s, D_MODEL)
    s1, s1b, w1g, w1u, w1d = _ffn(xs, w_ffn1_gu[0], w_ffn1_gu[0], w_ffn1_down[0], ln1_g, ln1_b,
                                  tm=rows_s, tf=512, up_blk0=n_ff_blocks, emit_bf16=True)
    ps_all, win_b = _matmul(s1b, w_in[0], tm=rows_s, tn=512, out_dtype=F32, name="in_proj_s", emit_bf16=True)

    def tiles(state):
        win = state.shape[2]
        per_tile = 4 if win <= DIL_GROUPS[1][0] else DIL_GROUPS[2][1]
        return state.reshape(nb, win // per_tile, per_tile * HEADS, HEAD_DIM)

    k_states = [tiles(s) for s in (state_k_w128, state_k_w512, state_k_w2048)]
    v_states = [tiles(s) for s in (state_v_w128, state_v_w512, state_v_w2048)]
    h1, h1b, *shifted = _ffn(x_prompt[0], w1g, w1u, w1d, ln1_g, ln1_b, tm=1024, tf=512,
                             windows=k_states + v_states)
    shifted = [shifted[g + part * N_GROUPS] for g in range(N_GROUPS) for part in (0, 1)]
    p_all, qkv = _in_proj(h1b, win_b, tm=1024, tn=512)
    mem_kv = _matmul(mem_prompt[0].astype(BF16), w_mem_kv[0], tm=MEM_TOKENS, tn=512, out_dtype=F32, name="mem_kv")
    o_attn = _band_attention(qkv, slopes_band)
    o_mem = _mem_attention(p_all, mem_kv, tq=512)
    o_conv, conv_p = _conv(p_all[None], jnp.zeros((1, 2, CONV_W), F32), cw, tm=512, t_last=512)
    h2, _ = _merge(o_conv[0], o_attn, o_mem, p_all, h1, wc_b, wa_b, wm_b, wo_b, ln2_g, ln2_b, tm=256)

    prompt_states = []
    for g, (win, _) in enumerate(DIL_GROUPS):
        for tensor in (1, 2):
            c0 = tensor * N_GROUPS * GROUP_W + g * GROUP_W
            prompt_states.append(qkv[seq - win:, c0:c0 + GROUP_W].reshape(1, 1, win, HEADS, HEAD_DIM))
    mem_k_p = mem_kv[:, :GROUP_W].reshape(1, 1, MEM_TOKENS, HEADS, HEAD_DIM)
    mem_v_p = mem_kv[:, GROUP_W:].reshape(1, 1, MEM_TOKENS, HEADS, HEAD_DIM)

    ps3 = ps_all.reshape(nb, SAMPLE_PAD, IN_W)

    real = ps3[:, :t_new]
    heads = lambda a, n: a.reshape(nb, t_new, n, HEADS, HEAD_DIM)
    q, k, v = (heads(real[..., c:c + N_GROUPS * GROUP_W], N_GROUPS) for c in (COL_Q, COL_K, COL_VA))
    rows_th = lambda a: a.transpose(0, 2, 1, 3, 4).reshape(nb, a.shape[2], ROWS16, HEAD_DIM)
    over_quad = lambda a: jnp.broadcast_to(a[:, :, None], (nb, t_new, ROWS16 // HEADS, HEADS, HEAD_DIM)
                                           ).reshape(nb, t_new, ROWS16, HEAD_DIM)
    qd = rows_th(q[:, :, 1:])
    qm = real[..., COL_QM:COL_QM + GROUP_W].reshape(nb, t_new, HEADS, HEAD_DIM)
    qt = jnp.stack([over_quad(q[:, :, 0]), over_quad(qm)], axis=1)
    kn, vn = rows_th(k), rows_th(v)

    outs = _sample_attention(qd, qt, kn, vn, slopes16, k_states, v_states, shifted,
                             tiles(cache_mem_k), tiles(cache_mem_v), t_new=t_new)
    pad_rows = lambda a, w: jnp.pad(a.reshape(nb, t_new, w), ((0, 0), (0, SAMPLE_PAD - t_new), (0, 0))
                                    ).reshape(rows_s, w).astype(BF16)
    o_attn_s, o_mem_s = pad_rows(outs[0], GROUP_W), pad_rows(outs[1], GROUP_W)
    sample_states = [s.reshape(1, nb, -1, HEADS, HEAD_DIM) for s in outs[2:]]

    o_conv_s, conv_s = _conv(ps3, state_conv[0], cw, tm=SAMPLE_PAD, t_last=t_new)
    s2, _ = _merge(o_conv_s.reshape(rows_s, CONV_W), o_attn_s, o_mem_s, ps_all, s1,
                   wc_b, wa_b, wm_b, wo_b, ln2_g, ln2_b, tm=rows_s)
    y_s, _, w2g, w2u, w2d = _ffn(s2, w_ffn2_gu[0], w_ffn2_gu[0], w_ffn2_down[0], ln3_g, ln3_b,
                                 tm=rows_s, tf=512, up_blk0=n_ff_blocks, emit_bf16=True)
    y_sample = y_s.reshape(nb, SAMPLE_PAD, D_MODEL)[:, :t_new]
    y_prompt, _ = _ffn(h2, w2g, w2u, w2d, ln3_g, ln3_b, tm=1024, tf=512)

    return (y_prompt[None], y_sample, conv_p[None], *prompt_states, mem_k_p, mem_v_p,
            conv_s[None], *sample_states)
```

```python
import functools

import jax
import jax.numpy as jnp
import numpy as np
from jax import lax
from jax.experimental import pallas as pl
from jax.experimental.pallas import tpu as pltpu

F32 = jnp.float32
BF16 = jnp.bfloat16

D_MODEL = 2048
D_FF = 5632
HEAD_DIM = 128
HEADS = 4
GROUP_W = HEADS * HEAD_DIM
DIL_GROUPS = ((128, 1), (512, 4), (2048, 16))
N_GROUPS = len(DIL_GROUPS)
BAND = 128
CHUNK = BAND * DIL_GROUPS[-1][1]
CONV_W = 1024
MEM_TOKENS = 256
IN_W = 14336
COL_B, COL_C, COL_V = 0, 1024, 2048
COL_Q, COL_K, COL_VA = 3072, 4608, 6144
COL_QM = 7680
COL_GC, COL_GA, COL_GM = 8192, 10240, 12288
QKV_W = COL_QM - COL_Q
ALPHA = 2.0 ** 0.25
LN_EPS = 1e-5
NEG_INF = -1e30
SCALE = HEAD_DIM ** -0.5
SLOPES = np.exp2(np.float32(-8.0) * np.arange(1, 13, dtype=np.float32) / np.float32(12)).reshape(3, 4)

SAMPLE_PAD = 8
ROWS16 = 16
VMEM_LIMIT = 56 * 1024 * 1024


def _params(n_axes, vmem=VMEM_LIMIT):
    return pltpu.CompilerParams(dimension_semantics=("arbitrary",) * n_axes, vmem_limit_bytes=vmem)


def _layer_norm_rows(z, g, b):
    mu = jnp.mean(z, axis=-1, keepdims=True)
    zc = z - mu
    var = jnp.mean(zc * zc, axis=-1, keepdims=True)
    return zc * lax.rsqrt(var + LN_EPS) * g + b


def _window_loads(b, src_refs, stage_refs, slot, sems):
    return [pltpu.make_async_copy(src.at[b], stage.at[slot], sems.at[slot, n])
            for n, (src, stage) in enumerate(zip(src_refs, stage_refs))]


def _window_stores(b, stage_refs, dst_refs, slot, sems):
    copies = []
    for stage, dst in zip(stage_refs, dst_refs):
        tiles, rows = dst.shape[1], dst.shape[2]
        keep = rows - ROWS16
        if keep:
            copies.append((stage.at[slot, :, pl.ds(ROWS16, keep), :], dst.at[b, :, pl.ds(0, keep), :]))
        copies.append((stage.at[slot, pl.ds(1, tiles - 1), pl.ds(0, ROWS16), :],
                       dst.at[b, pl.ds(0, tiles - 1), pl.ds(keep, ROWS16), :]))
        copies.append((stage.at[slot, pl.ds(tiles - 1, 1), pl.ds(keep, ROWS16), :],
                       dst.at[b, pl.ds(tiles - 1, 1), pl.ds(keep, ROWS16), :]))
    return [pltpu.make_async_copy(s, d, sems.at[slot, n]) for n, (s, d) in enumerate(copies)]


def _ffn_kernel(*refs, n_ff, row_chunk, emit_bf16, n_windows, n_requests):
    x_ref, wg_ref, wu_ref, wd_ref, g_ref, b_ref = refs[:6]
    win_src = refs[6:6 + n_windows]
    y_ref, yb_ref = refs[6 + n_windows:8 + n_windows]
    rest = refs[8 + n_windows:]
    if emit_bf16:
        wb_refs, rest = rest[:3], rest[3:]
    win_dst, rest = rest[:n_windows], rest[n_windows:]
    xb_ref = rest[0]
    j = pl.program_id(1)

    if n_windows:
        half = n_windows // 2
        stage_refs, load_sems, store_sems = rest[1:1 + half], rest[1 + half], rest[2 + half]
        step = pl.program_id(0) * n_ff + j
        n_items = 2 * n_requests
        for part in (0, 1):
            srcs, dsts = win_src[part * half:(part + 1) * half], win_dst[part * half:(part + 1) * half]
            mine = step % 2 == part

            @pl.when(mine & (step >= 2) & (step < n_items + 2))
            def _():
                for copy in _window_stores((step - 2) // 2, stage_refs, dsts, part, store_sems):
                    copy.wait()

            @pl.when(mine & (step < n_items))
            def _():
                for copy in _window_loads(step // 2, srcs, stage_refs, part, load_sems):
                    copy.start()

            @pl.when(jnp.logical_not(mine) & (step >= 1) & (step < n_items + 1))
            def _():
                for copy in _window_loads((step - 1) // 2, srcs, stage_refs, part, load_sems):
                    copy.wait()
                for copy in _window_stores((step - 1) // 2, stage_refs, dsts, part, store_sems):
                    copy.start()

    @pl.when(j == 0)
    def _():
        xb_ref[...] = x_ref[...].astype(BF16)
        y_ref[...] = jnp.zeros(y_ref.shape, F32)

    wg, wu, wd = wg_ref[...], wu_ref[...], wd_ref[...]
    if emit_bf16:
        wg, wu, wd = wg.astype(BF16), wu.astype(BF16), wd.astype(BF16)
        wb_refs[0][...], wb_refs[1][...], wb_refs[2][...] = wg, wu, wd
    xb = xb_ref[...]
    gate = jnp.dot(xb, wg, preferred_element_type=F32)
    up = jnp.dot(xb, wu, preferred_element_type=F32)
    act = (gate * jax.nn.sigmoid(gate) * up).astype(BF16)
    y_ref[...] += jnp.dot(act, wd, preferred_element_type=F32)

    @pl.when(j == n_ff - 1)
    def _():
        def chunk(c, carry):
            rows = pl.ds(pl.multiple_of(c * row_chunk, row_chunk), row_chunk)
            z = ALPHA * x_ref[rows, :] + 0.5 * y_ref[rows, :]
            y = _layer_norm_rows(z, g_ref[...], b_ref[...])
            y_ref[rows, :] = y
            yb_ref[rows, :] = y.astype(BF16)
            return carry
        lax.fori_loop(0, x_ref.shape[0] // row_chunk, chunk, 0)


def _ffn(x, w_gate, w_up, w_down, g, b, *, tm, tf, up_blk0=0, emit_bf16=False, windows=()):
    rows = x.shape[0]
    n_ff = D_FF // tf
    n_steps = (rows // tm) * n_ff
    n_requests = windows[0].shape[0] if windows else 0
    assert not emit_bf16 or rows == tm
    assert n_steps >= 2 * n_requests + 2
    row_mode = pl.Buffered(1) if (rows == tm or tm >= 1024) else None
    hbm = pl.BlockSpec(memory_space=pl.ANY)
    w_col = pl.BlockSpec((D_MODEL, tf), lambda i, j: (0, j))
    w_row = pl.BlockSpec((tf, D_MODEL), lambda i, j: (j, 0))
    row_tile = pl.BlockSpec((tm, D_MODEL), lambda i, j: (i, 0), pipeline_mode=row_mode)
    out_specs = [row_tile, row_tile]
    out_shape = [jax.ShapeDtypeStruct((rows, D_MODEL), F32), jax.ShapeDtypeStruct((rows, D_MODEL), BF16)]
    scratch = [pltpu.VMEM((tm, D_MODEL), BF16)]
    if emit_bf16:
        out_specs += [w_col, w_col, w_row]
        out_shape += [jax.ShapeDtypeStruct((D_MODEL, D_FF), BF16)] * 2 + [jax.ShapeDtypeStruct((D_FF, D_MODEL), BF16)]
    if windows:
        out_specs += [hbm] * len(windows)
        out_shape += [jax.ShapeDtypeStruct(w.shape, w.dtype) for w in windows]
        half = windows[:len(windows) // 2]
        scratch += [pltpu.VMEM((2,) + w.shape[1:], w.dtype) for w in half]
        scratch += [pltpu.SemaphoreType.DMA((2, len(half))),
                    pltpu.SemaphoreType.DMA((2, sum(2 + (w.shape[2] > ROWS16) for w in half)))]
    return pl.pallas_call(
        functools.partial(_ffn_kernel, n_ff=n_ff, row_chunk=min(tm, 128), emit_bf16=emit_bf16,
                          n_windows=len(windows), n_requests=n_requests),
        grid=(rows // tm, n_ff),
        in_specs=[
            row_tile,
            w_col,
            pl.BlockSpec((D_MODEL, tf), lambda i, j: (0, j + up_blk0)),
            w_row,
            pl.BlockSpec((1, D_MODEL), lambda i, j: (0, 0)),
            pl.BlockSpec((1, D_MODEL), lambda i, j: (0, 0)),
        ] + [hbm] * len(windows),
        out_specs=out_specs,
        out_shape=out_shape,
        scratch_shapes=scratch,
        compiler_params=_params(2),
        name="ffn_cast" if emit_bf16 else "ffn",
    )(x, w_gate, w_up, w_down, g, b, *windows)


def _mm_kernel(x_ref, w_ref, o_ref, *wb_ref):
    w = w_ref[...].astype(BF16)
    if wb_ref:
        wb_ref[0][...] = w
    o_ref[...] = jnp.dot(x_ref[...], w, preferred_element_type=F32).astype(o_ref.dtype)


def _matmul(x, w, *, tm, tn, out_dtype, name, emit_bf16=False):
    rows, k = x.shape
    n_cols = w.shape[1]
    assert not emit_bf16 or rows == tm
    w_spec = pl.BlockSpec((k, tn), lambda i, j: (0, j))
    out_specs = [pl.BlockSpec((tm, tn), lambda i, j: (i, j))]
    out_shape = [jax.ShapeDtypeStruct((rows, n_cols), out_dtype)]
    if emit_bf16:
        out_specs.append(w_spec)
        out_shape.append(jax.ShapeDtypeStruct(w.shape, BF16))
    out = pl.pallas_call(
        _mm_kernel,
        grid=(rows // tm, n_cols // tn),
        in_specs=[pl.BlockSpec((tm, k), lambda i, j: (i, 0)), w_spec],
        out_specs=out_specs,
        out_shape=out_shape,
        compiler_params=_params(2),
        name=name,
    )(x, w)
    return out if emit_bf16 else out[0]


def _in_proj_kernel(x_ref, w_ref, p_ref, qkv_ref, *, j_lo, j_hi):
    j = pl.program_id(1)
    acc = jnp.dot(x_ref[...], w_ref[...], preferred_element_type=F32)
    p_ref[...] = acc.astype(BF16)

    @pl.when((j >= j_lo) & (j < j_hi))
    def _():
        qkv_ref[...] = acc


def _in_proj(x, w, *, tm, tn):
    rows, k = x.shape
    j_lo, j_hi = COL_Q // tn, COL_QM // tn
    return pl.pallas_call(
        functools.partial(_in_proj_kernel, j_lo=j_lo, j_hi=j_hi),
        grid=(rows // tm, IN_W // tn),
        in_specs=[pl.BlockSpec((tm, k), lambda i, j: (i, 0)),
                  pl.BlockSpec((k, tn), lambda i, j: (0, j))],
        out_specs=[pl.BlockSpec((tm, tn), lambda i, j: (i, j)),
                   pl.BlockSpec((tm, tn), lambda i, j: (i, jnp.clip(j - j_lo, 0, j_hi - j_lo - 1)))],
        out_shape=[jax.ShapeDtypeStruct((rows, IN_W), BF16),
                   jax.ShapeDtypeStruct((rows, QKV_W), F32)],
        compiler_params=_params(2),
        name="in_proj",
    )(x, w)


def _rows(ref, start, dil):
    if dil == 1:
        return ref[start:start + BAND, :]
    return ref[pl.ds(start, BAND, stride=dil), :]


def _band_kernel(sl_ref, q0_ref, k0_ref, v0_ref, q1_ref, k1_ref, v1_ref, q2_ref, k2_ref, v2_ref,
                 o_ref, kt0_ref, vt0_ref, kt1_ref, vt1_ref, kt2_ref, vt2_ref, og_ref, lg_ref):
    c = pl.program_id(1)
    q_refs = (q0_ref, q1_ref, q2_ref)
    k_refs = (k0_ref, k1_ref, k2_ref)
    v_refs = (v0_ref, v1_ref, v2_ref)
    kt_refs = (kt0_ref, kt1_ref, kt2_ref)
    vt_refs = (vt0_ref, vt1_ref, vt2_ref)

    @pl.when(c == 0)
    def _():
        for ref in kt_refs + vt_refs:
            ref[...] = jnp.zeros(ref.shape, F32)

    qi = lax.broadcasted_iota(jnp.int32, (BAND, 2 * BAND), 0)
    ki = lax.broadcasted_iota(jnp.int32, (BAND, 2 * BAND), 1)
    rel = BAND + qi - ki
    valid = (rel >= 0) & (rel <= BAND)
    no_prev = jnp.where(ki < BAND, jnp.where(c == 0, NEG_INF, 0.0), 0.0)

    for g, (_, dil) in enumerate(DIL_GROUPS):
        slope = sl_ref[0, g:g + 1, 0:1]
        bias = jnp.where(valid, -(slope * (rel * dil).astype(F32)), NEG_INF)
        bias_first = bias + no_prev
        span = BAND * dil
        for r in range(dil):
            for n in range(CHUNK // span):
                start = r + n * span
                q = _rows(q_refs[g], start, dil).astype(BF16)
                if n == 0:
                    k_prev, v_prev = _rows(kt_refs[g], r, dil), _rows(vt_refs[g], r, dil)
                else:
                    k_prev, v_prev = _rows(k_refs[g], start - span, dil), _rows(v_refs[g], start - span, dil)
                k = jnp.concatenate([k_prev, _rows(k_refs[g], start, dil)], axis=0).astype(BF16)
                v = jnp.concatenate([v_prev, _rows(v_refs[g], start, dil)], axis=0).astype(BF16)
                s = lax.dot_general(q, k, (((1,), (1,)), ((), ())), preferred_element_type=F32) * SCALE
                s = s + (bias_first if n == 0 else bias)
                m = jnp.max(s, axis=-1, keepdims=True)
                p = jnp.exp(s - m)
                l = jnp.sum(p, axis=-1, keepdims=True)
                o = jnp.dot(p.astype(BF16), v, preferred_element_type=F32) / l
                lse = jnp.broadcast_to(m + jnp.log(l), (BAND, HEAD_DIM))
                if dil == 1:
                    og_ref[g, start:start + BAND, :] = o
                    lg_ref[g, start:start + BAND, :] = lse
                else:
                    og_ref[g, pl.ds(start, BAND, stride=dil), :] = o
                    lg_ref[g, pl.ds(start, BAND, stride=dil), :] = lse
        kt_refs[g][...] = k_refs[g][CHUNK - span:CHUNK, :]
        vt_refs[g][...] = v_refs[g][CHUNK - span:CHUNK, :]

    rows_per_step = 256

    def combine(i, carry):
        rows = pl.ds(pl.multiple_of(i * rows_per_step, rows_per_step), rows_per_step)
        l0, l1, l2 = lg_ref[0, rows, :], lg_ref[1, rows, :], lg_ref[2, rows, :]
        m = jnp.maximum(jnp.maximum(l0, l1), l2)
        e0, e1, e2 = jnp.exp(l0 - m), jnp.exp(l1 - m), jnp.exp(l2 - m)
        o = (e0 * og_ref[0, rows, :] + e1 * og_ref[1, rows, :] + e2 * og_ref[2, rows, :]) / (e0 + e1 + e2)
        o_ref[rows, :] = o.astype(o_ref.dtype)
        return carry

    lax.fori_loop(0, CHUNK // rows_per_step, combine, 0)


def _band_attention(qkv, slopes):
    seq = qkv.shape[0]
    blk = (CHUNK, HEAD_DIM)
    per_tensor = N_GROUPS * HEADS
    in_specs = [pl.BlockSpec((1, 8, HEAD_DIM), lambda h, c: (h, 0, 0))]
    for g in range(N_GROUPS):
        for tensor in range(3):
            in_specs.append(pl.BlockSpec(
                blk, lambda h, c, g=g, tensor=tensor: (c, tensor * per_tensor + g * HEADS + h)))
    tails = []
    for _, dil in DIL_GROUPS:
        tails += [pltpu.VMEM((BAND * dil, HEAD_DIM), F32)] * 2
    return pl.pallas_call(
        _band_kernel,
        grid=(HEADS, seq // CHUNK),
        in_specs=in_specs,
        out_specs=pl.BlockSpec(blk, lambda h, c: (c, h)),
        out_shape=jax.ShapeDtypeStruct((seq, GROUP_W), BF16),
        scratch_shapes=tails + [pltpu.VMEM((N_GROUPS, CHUNK, HEAD_DIM), F32),
                                pltpu.VMEM((N_GROUPS, CHUNK, HEAD_DIM), F32)],
        compiler_params=_params(2),
        name="band_attn",
    )(slopes, *([qkv] * 9))


def _mem_attn_kernel(q_ref, mk_ref, mv_ref, o_ref):
    for h in range(HEADS):
        hs = slice(h * HEAD_DIM, (h + 1) * HEAD_DIM)
        q = q_ref[:, hs]
        k = mk_ref[:, hs].astype(BF16)
        v = mv_ref[:, hs].astype(BF16)
        s = lax.dot_general(q, k, (((1,), (1,)), ((), ())), preferred_element_type=F32) * SCALE
        m = jnp.max(s, axis=-1, keepdims=True)
        p = jnp.exp(s - m)
        l = jnp.sum(p, axis=-1, keepdims=True)
        o = jnp.dot(p.astype(BF16), v, preferred_element_type=F32) / l
        o_ref[:, hs] = o.astype(o_ref.dtype)


def _mem_attention(p_all, mem_kv, *, tq):
    seq = p_all.shape[0]
    return pl.pallas_call(
        _mem_attn_kernel,
        grid=(seq // tq,),
        in_specs=[pl.BlockSpec((tq, GROUP_W), lambda i: (i, COL_QM // GROUP_W)),
                  pl.BlockSpec((MEM_TOKENS, GROUP_W), lambda i: (0, 0)),
                  pl.BlockSpec((MEM_TOKENS, GROUP_W), lambda i: (0, 1))],
        out_specs=pl.BlockSpec((tq, GROUP_W), lambda i: (i, 0)),
        out_shape=jax.ShapeDtypeStruct((seq, GROUP_W), BF16),
        compiler_params=_params(1),
        name="mem_attn",
    )(p_all, mem_kv, mem_kv)


def _conv_kernel(b_ref, c_ref, v_ref, st_ref, w_ref, o_ref, ns_ref, carry_ref, *, t_last):
    i = pl.program_id(1)

    @pl.when(i == 0)
    def _():
        carry_ref[0:2, :] = st_ref[0]

    u = c_ref[0].astype(F32) * v_ref[0].astype(F32)
    c2 = carry_ref[0:1, :]
    c1 = carry_ref[1:2, :]
    row = lax.broadcasted_iota(jnp.int32, u.shape, 0)
    u1 = jnp.where(row == 0, c1, pltpu.roll(u, 1, 0))
    u2 = jnp.where(row == 0, c2, jnp.where(row == 1, c1, pltpu.roll(u, 2, 0)))
    o = b_ref[0].astype(F32) * (w_ref[0:1, :] * u2 + w_ref[1:2, :] * u1 + w_ref[2:3, :] * u)
    o_ref[0] = o.astype(o_ref.dtype)
    last2 = u[t_last - 2:t_last, :]
    carry_ref[0:2, :] = last2
    ns_ref[0] = last2


def _conv(p3, state, conv_w, *, tm, t_last):
    nb, t = p3.shape[0], p3.shape[1]
    blk = (1, tm, CONV_W)
    return pl.pallas_call(
        functools.partial(_conv_kernel, t_last=t_last),
        grid=(nb, t // tm),
        in_specs=[
            pl.BlockSpec(blk, lambda b, i: (b, i, COL_B // CONV_W)),
            pl.BlockSpec(blk, lambda b, i: (b, i, COL_C // CONV_W)),
            pl.BlockSpec(blk, lambda b, i: (b, i, COL_V // CONV_W)),
            pl.BlockSpec((1, 2, CONV_W), lambda b, i: (b, 0, 0)),
            pl.BlockSpec((3, CONV_W), lambda b, i: (0, 0)),
        ],
        out_specs=[pl.BlockSpec(blk, lambda b, i: (b, i, 0)),
                   pl.BlockSpec((1, 2, CONV_W), lambda b, i: (b, 0, 0))],
        out_shape=[jax.ShapeDtypeStruct((nb, t, CONV_W), BF16),
                   jax.ShapeDtypeStruct((nb, 2, CONV_W), F32)],
        scratch_shapes=[pltpu.VMEM((8, CONV_W), F32)],
        compiler_params=_params(2),
        name="conv",
    )(p3, p3, p3, state, conv_w)


def _merge_kernel(oc_ref, oa_ref, om_ref, gc_ref, ga_ref, gm_ref, h_ref,
                  wc_ref, wa_ref, wm_ref, wo_ref, g_ref, b_ref, y_ref, yb_ref, *, sub_rows):
    for r0 in range(0, h_ref.shape[0], sub_rows):
        rows = slice(r0, r0 + sub_rows)
        mc = jnp.dot(oc_ref[rows, :], wc_ref[...], preferred_element_type=F32)
        ma = jnp.dot(oa_ref[rows, :], wa_ref[...], preferred_element_type=F32)
        mm = jnp.dot(om_ref[rows, :], wm_ref[...], preferred_element_type=F32)
        mix = (jax.nn.sigmoid(gc_ref[rows, :].astype(F32)) * mc
               + jax.nn.sigmoid(ga_ref[rows, :].astype(F32)) * ma
               + jax.nn.sigmoid(gm_ref[rows, :].astype(F32)) * mm)
        z = ALPHA * h_ref[rows, :] + jnp.dot(mix.astype(BF16), wo_ref[...], preferred_element_type=F32)
        y = _layer_norm_rows(z, g_ref[...], b_ref[...])
        y_ref[rows, :] = y
        yb_ref[rows, :] = y.astype(BF16)


def _merge(o_conv, o_attn, o_mem, p_all, h, w_c, w_a, w_m, w_o, g, b, *, tm):
    rows = h.shape[0]
    single = pl.Buffered(1)
    row_blk = lambda w: pl.BlockSpec((tm, w), lambda i: (i, 0))
    gate_blk = lambda col: pl.BlockSpec((tm, D_MODEL), lambda i: (i, col // D_MODEL))
    whole = lambda a: pl.BlockSpec(a.shape, lambda i: (0, 0), pipeline_mode=single)
    return pl.pallas_call(
        functools.partial(_merge_kernel, sub_rows=min(tm, 256)),
        grid=(rows // tm,),
        in_specs=[row_blk(CONV_W), row_blk(GROUP_W), row_blk(GROUP_W),
                  gate_blk(COL_GC), gate_blk(COL_GA), gate_blk(COL_GM), row_blk(D_MODEL),
                  whole(w_c), whole(w_a), whole(w_m), whole(w_o), whole(g), whole(b)],
        out_specs=[row_blk(D_MODEL), row_blk(D_MODEL)],
        out_shape=[jax.ShapeDtypeStruct((rows, D_MODEL), F32),
                   jax.ShapeDtypeStruct((rows, D_MODEL), BF16)],
        compiler_params=_params(1),
        name="merge",
    )(o_conv, o_attn, o_mem, p_all, p_all, p_all, h, w_c, w_a, w_m, w_o, g, b)


def _over_quad(x, op):
    x = op(x, pltpu.roll(x, HEADS, 0))
    return op(x, pltpu.roll(x, 2 * HEADS, 0))


def _lane_dot(k3, q):
    return jnp.sum(k3 * q, axis=-1, keepdims=True) * SCALE


def _quad_attention(k3, v3, qt, bias3, k_new, v_new, bias_new):
    s = _lane_dot(k3, qt[None]) + bias3
    m = jnp.max(s, axis=0)
    if k_new is not None:
        s_new = _lane_dot(k_new, qt) + bias_new
        m = jnp.maximum(m, s_new)
    m = _over_quad(m, jnp.maximum)
    e = jnp.exp(s - m[None])
    l = jnp.sum(e, axis=0)
    acc = jnp.sum(e * v3, axis=0)
    if k_new is not None:
        e_new = jnp.exp(s_new - m)
        l = l + e_new
        acc = acc + e_new * v_new
    l = _over_quad(l, jnp.add)
    acc = _over_quad(acc, jnp.add)
    return acc / l, m + jnp.log(l)


def _sample_kernel(qd_ref, qt_ref, kn_ref, vn_ref, sl_ref,
                   k0_ref, v0_ref, k1_ref, v1_ref, k2_ref, v2_ref, mk_ref, mv_ref, *new_refs, t_new):
    k_refs = (k0_ref, k1_ref, k2_ref)
    v_refs = (v0_ref, v1_ref, v2_ref)
    oa_ref, om_ref = new_refs[N_GROUPS * 2], new_refs[N_GROUPS * 2 + 1]
    quad16 = lax.broadcasted_iota(jnp.int32, (ROWS16, HEAD_DIM), 0) >> 2

    outs, lses = [], []

    win0 = DIL_GROUPS[0][0]
    a_idx = lax.broadcasted_iota(jnp.int32, (win0 // 4, ROWS16, HEAD_DIM), 0)
    b_idx = lax.broadcasted_iota(jnp.int32, (win0 // 4, ROWS16, HEAD_DIM), 1) >> 2
    tok = a_idx * 4 + b_idx
    slope0 = sl_ref[0]
    k3, v3 = k0_ref[0], v0_ref[0]
    o0 = jnp.zeros((ROWS16, HEAD_DIM), F32)
    l0 = jnp.zeros((ROWS16, HEAD_DIM), F32)
    for t in range(t_new):
        bias3 = jnp.where(tok >= t, -(slope0[None] * (win0 + t - tok).astype(F32)), NEG_INF)
        bias_new = jnp.where(quad16 <= t, -(slope0 * (t - quad16).astype(F32)), NEG_INF)
        o_t, l_t = _quad_attention(k3, v3, qt_ref[0, 0, t], bias3, kn_ref[0, 0], vn_ref[0, 0], bias_new)
        o0 = jnp.where(quad16 == t, o_t, o0)
        l0 = jnp.where(quad16 == t, l_t, l0)
    outs.append(o0)
    lses.append(l0)

    for g in (1, 2):
        win, dil = DIL_GROUPS[g]
        q16 = qd_ref[0, g - 1]
        k3 = k_refs[g][0, :, 0:ROWS16, :]
        v3 = v_refs[g][0, :, 0:ROWS16, :]
        i_idx = lax.broadcasted_iota(jnp.int32, (BAND, ROWS16, HEAD_DIM), 0)
        bias3 = -(sl_ref[g][None] * (win - i_idx * dil).astype(F32))
        s = _lane_dot(k3, q16[None]) + bias3
        s_new = _lane_dot(kn_ref[0, g], q16)
        m = jnp.maximum(jnp.max(s, axis=0), s_new)
        e = jnp.exp(s - m[None])
        e_new = jnp.exp(s_new - m)
        l = jnp.sum(e, axis=0) + e_new
        acc = jnp.sum(e * v3, axis=0) + e_new * vn_ref[0, g]
        outs.append(acc / l)
        lses.append(m + jnp.log(l))

    m = jnp.maximum(jnp.maximum(lses[0], lses[1]), lses[2])
    w = [jnp.exp(x - m) for x in lses]
    oa_ref[0] = (w[0] * outs[0] + w[1] * outs[1] + w[2] * outs[2]) / (w[0] + w[1] + w[2])

    k3, v3 = mk_ref[0], mv_ref[0]
    om = jnp.zeros((ROWS16, HEAD_DIM), F32)
    for t in range(t_new):
        o_t, _ = _quad_attention(k3, v3, qt_ref[0, 1, t], jnp.zeros((1, ROWS16, HEAD_DIM), F32), None, None, None)
        om = jnp.where(quad16 == t, o_t, om)
    om_ref[0] = om

    for i, out_ref in enumerate(new_refs[N_GROUPS * 2 + 2:]):
        out_ref[0, 0] = (kn_ref, vn_ref)[i % 2][0, i // 2]


def _sample_attention(qd, qt, kn, vn, slopes16, k_states, v_states, shifted, mem_k, mem_v, *, t_new):
    nb = qd.shape[0]
    states = []
    for g in range(N_GROUPS):
        states += [k_states[g], v_states[g]]
    whole = lambda a: pl.BlockSpec((1,) + a.shape[1:], lambda b: (b,) + (0,) * (a.ndim - 1))
    head_rows = lambda a: pl.BlockSpec((1, a.shape[1], ROWS16, HEAD_DIM), lambda b: (b, 0, 0, 0))
    last_rows = lambda a: pl.BlockSpec((1, 1, ROWS16, HEAD_DIM),
                                       lambda b, a=a: (b, a.shape[1] - 1, a.shape[2] // ROWS16 - 1, 0))
    o_spec = pl.BlockSpec((1, ROWS16, HEAD_DIM), lambda b: (b, 0, 0))
    o_shape = jax.ShapeDtypeStruct((nb, ROWS16, HEAD_DIM), F32)
    n_in = 5 + len(states) + 2
    return pl.pallas_call(
        functools.partial(_sample_kernel, t_new=t_new),
        grid=(nb,),
        in_specs=[whole(qd), whole(qt), whole(kn), whole(vn),
                  pl.BlockSpec(slopes16.shape, lambda b: (0, 0, 0))]
                 + [head_rows(s) for s in states] + [whole(mem_k), whole(mem_v)]
                 + [pl.BlockSpec(memory_space=pl.ANY)] * len(shifted),
        out_specs=[o_spec, o_spec] + [last_rows(s) for s in shifted],
        out_shape=[o_shape, o_shape] + [jax.ShapeDtypeStruct(s.shape, F32) for s in shifted],
        input_output_aliases={n_in + i: 2 + i for i in range(len(shifted))},
        compiler_params=_params(1),
        name="sample_attn",
    )(qd, qt, kn, vn, slopes16, *states, mem_k, mem_v, *shifted)


def kernel(x_prompt, x_sample, mem_prompt, state_conv, state_k_w128, state_v_w128, state_k_w512, state_v_w512, state_k_w2048, state_v_w2048, cache_mem_k, cache_mem_v, ln1_g, ln1_b, w_ffn1_gu, w_ffn1_down, w_in, conv_w, w_mem_kv, w_br_conv, w_br_attn, w_br_mem, w_o, ln2_g, ln2_b, w_ffn2_gu, w_ffn2_down, ln3_g, ln3_b):
    depth = w_in.shape[0]
    seq = x_prompt.shape[1]
    nb, t_new = x_sample.shape[0], x_sample.shape[1]
    assert depth == 1 and x_prompt.shape[0] == 1 and seq % CHUNK == 0
    assert t_new * HEADS == ROWS16 and t_new <= DIL_GROUPS[1][1]

    wb = lambda w: w[0].astype(BF16)
    wc_b, wa_b, wm_b, wo_b = wb(w_br_conv), wb(w_br_attn), wb(w_br_mem), wb(w_o)
    n_ff_blocks = D_FF // 512
    cw = conv_w[0]
    slopes_lane = np.broadcast_to(SLOPES[:, :, None], (N_GROUPS, HEADS, HEAD_DIM))
    slopes_band = jnp.asarray(np.pad(slopes_lane.transpose(1, 0, 2), ((0, 0), (0, 8 - N_GROUPS), (0, 0))))
    slopes16 = jnp.asarray(np.tile(slopes_lane, (1, ROWS16 // HEADS, 1)))

    rows_s = nb * SAMPLE_PAD
    xs = jnp.pad(x_sample, ((0, 0), (0, SAMPLE_PAD - t_new), (0, 0))).reshape(rows_s, D_MODEL)
    s1, s1b, w1g, w1u, w1d = _ffn(xs, w_ffn1_gu[0], w_ffn1_gu[0], w_ffn1_down[0], ln1_g, ln1_b,
                                  tm=rows_s, tf=512, up_blk0=n_ff_blocks, emit_bf16=True)
    ps_all, win_b = _matmul(s1b, w_in[0], tm=rows_s, tn=512, out_dtype=F32, name="in_proj_s", emit_bf16=True)

    def tiles(state):
        win = state.shape[2]
        per_tile = 4 if win <= DIL_GROUPS[1][0] else DIL_GROUPS[2][1]
        return state.reshape(nb, win // per_tile, per_tile * HEADS, HEAD_DIM)

    k_states = [tiles(s) for s in (state_k_w128, state_k_w512, state_k_w2048)]
    v_states = [tiles(s) for s in (state_v_w128, state_v_w512, state_v_w2048)]
    h1, h1b, *shifted = _ffn(x_prompt[0], w1g, w1u, w1d, ln1_g, ln1_b, tm=512, tf=512,
                             windows=k_states + v_states)
    shifted = [shifted[g + part * N_GROUPS] for g in range(N_GROUPS) for part in (0, 1)]
    p_all, qkv = _in_proj(h1b, win_b, tm=1024, tn=512)
    mem_kv = _matmul(mem_prompt[0].astype(BF16), w_mem_kv[0], tm=MEM_TOKENS, tn=512, out_dtype=F32, name="mem_kv")
    o_attn = _band_attention(qkv, slopes_band)
    o_mem = _mem_attention(p_all, mem_kv, tq=512)
    o_conv, conv_p = _conv(p_all[None], jnp.zeros((1, 2, CONV_W), F32), cw, tm=512, t_last=512)
    h2, _ = _merge(o_conv[0], o_attn, o_mem, p_all, h1, wc_b, wa_b, wm_b, wo_b, ln2_g, ln2_b, tm=256)

    prompt_states = []
    for g, (win, _) in enumerate(DIL_GROUPS):
        for tensor in (1, 2):
            c0 = tensor * N_GROUPS * GROUP_W + g * GROUP_W
            prompt_states.append(qkv[seq - win:, c0:c0 + GROUP_W].reshape(1, 1, win, HEADS, HEAD_DIM))
    mem_k_p = mem_kv[:, :GROUP_W].reshape(1, 1, MEM_TOKENS, HEADS, HEAD_DIM)
    mem_v_p = mem_kv[:, GROUP_W:].reshape(1, 1, MEM_TOKENS, HEADS, HEAD_DIM)

    ps3 = ps_all.reshape(nb, SAMPLE_PAD, IN_W)

    real = ps3[:, :t_new]
    heads = lambda a, n: a.reshape(nb, t_new, n, HEADS, HEAD_DIM)
    q, k, v = (heads(real[..., c:c + N_GROUPS * GROUP_W], N_GROUPS) for c in (COL_Q, COL_K, COL_VA))
    rows_th = lambda a: a.transpose(0, 2, 1, 3, 4).reshape(nb, a.shape[2], ROWS16, HEAD_DIM)
    over_quad = lambda a: jnp.broadcast_to(a[:, :, None], (nb, t_new, ROWS16 // HEADS, HEADS, HEAD_DIM)
                                           ).reshape(nb, t_new, ROWS16, HEAD_DIM)
    qd = rows_th(q[:, :, 1:])
    qm = real[..., COL_QM:COL_QM + GROUP_W].reshape(nb, t_new, HEADS, HEAD_DIM)
    qt = jnp.stack([over_quad(q[:, :, 0]), over_quad(qm)], axis=1)
    kn, vn = rows_th(k), rows_th(v)

    outs = _sample_attention(qd, qt, kn, vn, slopes16, k_states, v_states, shifted,
                             tiles(cache_mem_k), tiles(cache_mem_v), t_new=t_new)
    pad_rows = lambda a, w: jnp.pad(a.reshape(nb, t_new, w), ((0, 0), (0, SAMPLE_PAD - t_new), (0, 0))
                                    ).reshape(rows_s, w).astype(BF16)
    o_attn_s, o_mem_s = pad_rows(outs[0], GROUP_W), pad_rows(outs[1], GROUP_W)
    sample_states = [s.reshape(1, nb, -1, HEADS, HEAD_DIM) for s in outs[2:]]

    o_conv_s, conv_s = _conv(ps3, state_conv[0], cw, tm=SAMPLE_PAD, t_last=t_new)
    s2, _ = _merge(o_conv_s.reshape(rows_s, CONV_W), o_attn_s, o_mem_s, ps_all, s1,
                   wc_b, wa_b, wm_b, wo_b, ln2_g, ln2_b, tm=rows_s)
    y_s, _, w2g, w2u, w2d = _ffn(s2, w_ffn2_gu[0], w_ffn2_gu[0], w_ffn2_down[0], ln3_g, ln3_b,
                                 tm=rows_s, tf=512, up_blk0=n_ff_blocks, emit_bf16=True)
    y_sample = y_s.reshape(nb, SAMPLE_PAD, D_MODEL)[:, :t_new]
    y_prompt, _ = _ffn(h2, w2g, w2u, w2d, ln3_g, ln3_b, tm=1024, tf=512)

    return (y_prompt[None], y_sample, conv_p[None], *prompt_states, mem_k_p, mem_v_p,
            conv_s[None], *sample_states)
```

```python
import functools

import jax
import jax.numpy as jnp
import numpy as np
from jax import lax
from jax.experimental import pallas as pl
from jax.experimental.pallas import tpu as pltpu

F32 = jnp.float32
BF16 = jnp.bfloat16

D_MODEL = 2048
D_FF = 5632
HEAD_DIM = 128
HEADS = 4
GROUP_W = HEADS * HEAD_DIM
DIL_GROUPS = ((128, 1), (512, 4), (2048, 16))
N_GROUPS = len(DIL_GROUPS)
BAND = 128
CHUNK = BAND * DIL_GROUPS[-1][1]
CONV_W = 1024
MEM_TOKENS = 256
IN_W = 14336
COL_B, COL_C, COL_V = 0, 1024, 2048
COL_Q, COL_K, COL_VA = 3072, 4608, 6144
COL_QM = 7680
COL_GC, COL_GA, COL_GM = 8192, 10240, 12288
QKV_W = COL_QM - COL_Q
ALPHA = 2.0 ** 0.25
LN_EPS = 1e-5
NEG_INF = -1e30
SCALE = HEAD_DIM ** -0.5
SLOPES = np.exp2(np.float32(-8.0) * np.arange(1, 13, dtype=np.float32) / np.float32(12)).reshape(3, 4)

SAMPLE_PAD = 8
ROWS16 = 16
VMEM_LIMIT = 56 * 1024 * 1024


def _params(n_axes, vmem=VMEM_LIMIT):
    return pltpu.CompilerParams(dimension_semantics=("arbitrary",) * n_axes, vmem_limit_bytes=vmem)


def _layer_norm_rows(z, g, b):
    mu = jnp.mean(z, axis=-1, keepdims=True)
    zc = z - mu
    var = jnp.mean(zc * zc, axis=-1, keepdims=True)
    return zc * lax.rsqrt(var + LN_EPS) * g + b


def _window_loads(b, src_refs, stage_refs, slot, sems):
    return [pltpu.make_async_copy(src.at[b], stage.at[slot], sems.at[slot, n])
            for n, (src, stage) in enumerate(zip(src_refs, stage_refs))]


def _window_stores(b, stage_refs, dst_refs, slot, sems):
    copies = []
    for stage, dst in zip(stage_refs, dst_refs):
        tiles, rows = dst.shape[1], dst.shape[2]
        keep = rows - ROWS16
        if keep:
            copies.append((stage.at[slot, :, pl.ds(ROWS16, keep), :], dst.at[b, :, pl.ds(0, keep), :]))
        copies.append((stage.at[slot, pl.ds(1, tiles - 1), pl.ds(0, ROWS16), :],
                       dst.at[b, pl.ds(0, tiles - 1), pl.ds(keep, ROWS16), :]))
        copies.append((stage.at[slot, pl.ds(tiles - 1, 1), pl.ds(keep, ROWS16), :],
                       dst.at[b, pl.ds(tiles - 1, 1), pl.ds(keep, ROWS16), :]))
    return [pltpu.make_async_copy(s, d, sems.at[slot, n]) for n, (s, d) in enumerate(copies)]


def _ffn_kernel(*refs, n_ff, row_chunk, emit_bf16, n_windows, n_requests):
    x_ref, wg_ref, wu_ref, wd_ref, g_ref, b_ref = refs[:6]
    win_src = refs[6:6 + n_windows]
    y_ref, yb_ref = refs[6 + n_windows:8 + n_windows]
    rest = refs[8 + n_windows:]
    if emit_bf16:
        wb_refs, rest = rest[:3], rest[3:]
    win_dst, rest = rest[:n_windows], rest[n_windows:]
    xb_ref = rest[0]
    j = pl.program_id(1)

    if n_windows:
        half = n_windows // 2
        stage_refs, load_sems, store_sems = rest[1:1 + half], rest[1 + half], rest[2 + half]
        step = pl.program_id(0) * n_ff + j
        n_items = 2 * n_requests
        for part in (0, 1):
            srcs, dsts = win_src[part * half:(part + 1) * half], win_dst[part * half:(part + 1) * half]
            mine = step % 2 == part

            @pl.when(mine & (step >= 2) & (step < n_items + 2))
            def _():
                for copy in _window_stores((step - 2) // 2, stage_refs, dsts, part, store_sems):
                    copy.wait()

            @pl.when(mine & (step < n_items))
            def _():
                for copy in _window_loads(step // 2, srcs, stage_refs, part, load_sems):
                    copy.start()

            @pl.when(jnp.logical_not(mine) & (step >= 1) & (step < n_items + 1))
            def _():
                for copy in _window_loads((step - 1) // 2, srcs, stage_refs, part, load_sems):
                    copy.wait()
                for copy in _window_stores((step - 1) // 2, stage_refs, dsts, part, store_sems):
                    copy.start()

    @pl.when(j == 0)
    def _():
        xb_ref[...] = x_ref[...].astype(BF16)
        y_ref[...] = jnp.zeros(y_ref.shape, F32)

    wg, wu, wd = wg_ref[...], wu_ref[...], wd_ref[...]
    if emit_bf16:
        wg, wu, wd = wg.astype(BF16), wu.astype(BF16), wd.astype(BF16)
        wb_refs[0][...], wb_refs[1][...], wb_refs[2][...] = wg, wu, wd
    xb = xb_ref[...]
    gate = jnp.dot(xb, wg, preferred_element_type=F32)
    up = jnp.dot(xb, wu, preferred_element_type=F32)
    act = (gate * jax.nn.sigmoid(gate) * up).astype(BF16)
    y_ref[...] += jnp.dot(act, wd, preferred_element_type=F32)

    @pl.when(j == n_ff - 1)
    def _():
        def chunk(c, carry):
            rows = pl.ds(pl.multiple_of(c * row_chunk, row_chunk), row_chunk)
            z = ALPHA * x_ref[rows, :] + 0.5 * y_ref[rows, :]
            y = _layer_norm_rows(z, g_ref[...], b_ref[...])
            y_ref[rows, :] = y
            yb_ref[rows, :] = y.astype(BF16)
            return carry
        lax.fori_loop(0, x_ref.shape[0] // row_chunk, chunk, 0)


def _ffn(x, w_gate, w_up, w_down, g, b, *, tm, tf, up_blk0=0, emit_bf16=False, windows=()):
    rows = x.shape[0]
    n_ff = D_FF // tf
    n_steps = (rows // tm) * n_ff
    n_requests = windows[0].shape[0] if windows else 0
    assert not emit_bf16 or rows == tm
    assert n_steps >= 2 * n_requests + 2
    row_mode = pl.Buffered(1) if (rows == tm or tm >= 1024) else None
    hbm = pl.BlockSpec(memory_space=pl.ANY)
    w_col = pl.BlockSpec((D_MODEL, tf), lambda i, j: (0, j))
    w_col_tile = pl.BlockSpec((None, D_MODEL, tf), lambda i, j: (j, 0, 0))
    tiled = w_gate.ndim == 3
    w_row = pl.BlockSpec((tf, D_MODEL), lambda i, j: (j, 0))
    row_tile = pl.BlockSpec((tm, D_MODEL), lambda i, j: (i, 0), pipeline_mode=row_mode)
    out_specs = [row_tile, row_tile]
    out_shape = [jax.ShapeDtypeStruct((rows, D_MODEL), F32), jax.ShapeDtypeStruct((rows, D_MODEL), BF16)]
    scratch = [pltpu.VMEM((tm, D_MODEL), BF16)]
    if emit_bf16:
        out_specs += [w_col_tile, w_col_tile, w_row]
        out_shape += [jax.ShapeDtypeStruct((n_ff, D_MODEL, tf), BF16)] * 2 + [jax.ShapeDtypeStruct((D_FF, D_MODEL), BF16)]
    if windows:
        out_specs += [hbm] * len(windows)
        out_shape += [jax.ShapeDtypeStruct(w.shape, w.dtype) for w in windows]
        half = windows[:len(windows) // 2]
        scratch += [pltpu.VMEM((2,) + w.shape[1:], w.dtype) for w in half]
        scratch += [pltpu.SemaphoreType.DMA((2, len(half))),
                    pltpu.SemaphoreType.DMA((2, sum(2 + (w.shape[2] > ROWS16) for w in half)))]
    return pl.pallas_call(
        functools.partial(_ffn_kernel, n_ff=n_ff, row_chunk=min(tm, 128), emit_bf16=emit_bf16,
                          n_windows=len(windows), n_requests=n_requests),
        grid=(rows // tm, n_ff),
        in_specs=[
            row_tile,
            w_col_tile if tiled else w_col,
            w_col_tile if tiled else pl.BlockSpec((D_MODEL, tf), lambda i, j: (0, j + up_blk0)),
            w_row,
            pl.BlockSpec((1, D_MODEL), lambda i, j: (0, 0)),
            pl.BlockSpec((1, D_MODEL), lambda i, j: (0, 0)),
        ] + [hbm] * len(windows),
        out_specs=out_specs,
        out_shape=out_shape,
        scratch_shapes=scratch,
        compiler_params=_params(2),
        name="ffn_cast" if emit_bf16 else "ffn",
    )(x, w_gate, w_up, w_down, g, b, *windows)


def _mm_kernel(x_ref, w_ref, o_ref, *wb_ref):
    w = w_ref[...].astype(BF16)
    if wb_ref:
        wb_ref[0][...] = w
    o_ref[...] = jnp.dot(x_ref[...], w, preferred_element_type=F32).astype(o_ref.dtype)


def _matmul(x, w, *, tm, tn, out_dtype, name, emit_bf16=False):
    rows, k = x.shape
    n_cols = w.shape[1]
    assert not emit_bf16 or rows == tm
    w_spec = pl.BlockSpec((k, tn), lambda i, j: (0, j))
    out_specs = [pl.BlockSpec((tm, tn), lambda i, j: (i, j))]
    out_shape = [jax.ShapeDtypeStruct((rows, n_cols), out_dtype)]
    if emit_bf16:
        out_specs.append(pl.BlockSpec((None, k, tn), lambda i, j: (j, 0, 0)))
        out_shape.append(jax.ShapeDtypeStruct((n_cols // tn, k, tn), BF16))
    out = pl.pallas_call(
        _mm_kernel,
        grid=(rows // tm, n_cols // tn),
        in_specs=[pl.BlockSpec((tm, k), lambda i, j: (i, 0)), w_spec],
        out_specs=out_specs,
        out_shape=out_shape,
        compiler_params=_params(2),
        name=name,
    )(x, w)
    return out if emit_bf16 else out[0]


def _in_proj_kernel(x_ref, w_ref, p_ref, qkv_ref, *, j_lo, j_hi):
    j = pl.program_id(1)
    acc = jnp.dot(x_ref[...], w_ref[...], preferred_element_type=F32)
    p_ref[...] = acc.astype(BF16)

    @pl.when((j >= j_lo) & (j < j_hi))
    def _():
        qkv_ref[...] = acc


def _in_proj(x, w, *, tm, tn):
    rows, k = x.shape
    assert w.shape == (IN_W // tn, k, tn)
    j_lo, j_hi = COL_Q // tn, COL_QM // tn
    return pl.pallas_call(
        functools.partial(_in_proj_kernel, j_lo=j_lo, j_hi=j_hi),
        grid=(rows // tm, IN_W // tn),
        in_specs=[pl.BlockSpec((tm, k), lambda i, j: (i, 0)),
                  pl.BlockSpec((None, k, tn), lambda i, j: (j, 0, 0))],
        out_specs=[pl.BlockSpec((tm, tn), lambda i, j: (i, j)),
                   pl.BlockSpec((tm, tn), lambda i, j: (i, jnp.clip(j - j_lo, 0, j_hi - j_lo - 1)))],
        out_shape=[jax.ShapeDtypeStruct((rows, IN_W), BF16),
                   jax.ShapeDtypeStruct((rows, QKV_W), F32)],
        compiler_params=_params(2),
        name="in_proj",
    )(x, w)


def _rows(ref, start, dil):
    if dil == 1:
        return ref[start:start + BAND, :]
    return ref[pl.ds(start, BAND, stride=dil), :]


def _band_kernel(sl_ref, q0_ref, k0_ref, v0_ref, q1_ref, k1_ref, v1_ref, q2_ref, k2_ref, v2_ref,
                 o_ref, kt0_ref, vt0_ref, kt1_ref, vt1_ref, kt2_ref, vt2_ref, og_ref, lg_ref):
    c = pl.program_id(1)
    q_refs = (q0_ref, q1_ref, q2_ref)
    k_refs = (k0_ref, k1_ref, k2_ref)
    v_refs = (v0_ref, v1_ref, v2_ref)
    kt_refs = (kt0_ref, kt1_ref, kt2_ref)
    vt_refs = (vt0_ref, vt1_ref, vt2_ref)

    @pl.when(c == 0)
    def _():
        for ref in kt_refs + vt_refs:
            ref[...] = jnp.zeros(ref.shape, F32)

    qi = lax.broadcasted_iota(jnp.int32, (BAND, 2 * BAND), 0)
    ki = lax.broadcasted_iota(jnp.int32, (BAND, 2 * BAND), 1)
    rel = BAND + qi - ki
    valid = (rel >= 0) & (rel <= BAND)
    no_prev = jnp.where(ki < BAND, jnp.where(c == 0, NEG_INF, 0.0), 0.0)

    for g, (_, dil) in enumerate(DIL_GROUPS):
        slope = sl_ref[0, g:g + 1, 0:1]
        bias = jnp.where(valid, -(slope * (rel * dil).astype(F32)), NEG_INF)
        bias_first = bias + no_prev
        span = BAND * dil
        for r in range(dil):
            for n in range(CHUNK // span):
                start = r + n * span
                q = _rows(q_refs[g], start, dil).astype(BF16)
                if n == 0:
                    k_prev, v_prev = _rows(kt_refs[g], r, dil), _rows(vt_refs[g], r, dil)
                else:
                    k_prev, v_prev = _rows(k_refs[g], start - span, dil), _rows(v_refs[g], start - span, dil)
                k = jnp.concatenate([k_prev, _rows(k_refs[g], start, dil)], axis=0).astype(BF16)
                v = jnp.concatenate([v_prev, _rows(v_refs[g], start, dil)], axis=0).astype(BF16)
                s = lax.dot_general(q, k, (((1,), (1,)), ((), ())), preferred_element_type=F32) * SCALE
                s = s + (bias_first if n == 0 else bias)
                m = jnp.max(s, axis=-1, keepdims=True)
                p = jnp.exp(s - m)
                l = jnp.sum(p, axis=-1, keepdims=True)
                o = jnp.dot(p.astype(BF16), v, preferred_element_type=F32) / l
                lse = jnp.broadcast_to(m + jnp.log(l), (BAND, HEAD_DIM))
                if dil == 1:
                    og_ref[g, start:start + BAND, :] = o
                    lg_ref[g, start:start + BAND, :] = lse
                else:
                    og_ref[g, pl.ds(start, BAND, stride=dil), :] = o
                    lg_ref[g, pl.ds(start, BAND, stride=dil), :] = lse
        kt_refs[g][...] = k_refs[g][CHUNK - span:CHUNK, :]
        vt_refs[g][...] = v_refs[g][CHUNK - span:CHUNK, :]

    rows_per_step = 256

    def combine(i, carry):
        rows = pl.ds(pl.multiple_of(i * rows_per_step, rows_per_step), rows_per_step)
        l0, l1, l2 = lg_ref[0, rows, :], lg_ref[1, rows, :], lg_ref[2, rows, :]
        m = jnp.maximum(jnp.maximum(l0, l1), l2)
        e0, e1, e2 = jnp.exp(l0 - m), jnp.exp(l1 - m), jnp.exp(l2 - m)
        o = (e0 * og_ref[0, rows, :] + e1 * og_ref[1, rows, :] + e2 * og_ref[2, rows, :]) / (e0 + e1 + e2)
        o_ref[rows, :] = o.astype(o_ref.dtype)
        return carry

    lax.fori_loop(0, CHUNK // rows_per_step, combine, 0)


def _band_attention(qkv, slopes):
    seq = qkv.shape[0]
    blk = (CHUNK, HEAD_DIM)
    per_tensor = N_GROUPS * HEADS
    in_specs = [pl.BlockSpec((1, 8, HEAD_DIM), lambda h, c: (h, 0, 0))]
    for g in range(N_GROUPS):
        for tensor in range(3):
            in_specs.append(pl.BlockSpec(
                blk, lambda h, c, g=g, tensor=tensor: (c, tensor * per_tensor + g * HEADS + h)))
    tails = []
    for _, dil in DIL_GROUPS:
        tails += [pltpu.VMEM((BAND * dil, HEAD_DIM), F32)] * 2
    return pl.pallas_call(
        _band_kernel,
        grid=(HEADS, seq // CHUNK),
        in_specs=in_specs,
        out_specs=pl.BlockSpec(blk, lambda h, c: (c, h)),
        out_shape=jax.ShapeDtypeStruct((seq, GROUP_W), BF16),
        scratch_shapes=tails + [pltpu.VMEM((N_GROUPS, CHUNK, HEAD_DIM), F32),
                                pltpu.VMEM((N_GROUPS, CHUNK, HEAD_DIM), F32)],
        compiler_params=_params(2),
        name="band_attn",
    )(slopes, *([qkv] * 9))


def _mem_attn_kernel(q_ref, mk_ref, mv_ref, o_ref):
    for h in range(HEADS):
        hs = slice(h * HEAD_DIM, (h + 1) * HEAD_DIM)
        q = q_ref[:, hs]
        k = mk_ref[:, hs].astype(BF16)
        v = mv_ref[:, hs].astype(BF16)
        s = lax.dot_general(q, k, (((1,), (1,)), ((), ())), preferred_element_type=F32) * SCALE
        m = jnp.max(s, axis=-1, keepdims=True)
        p = jnp.exp(s - m)
        l = jnp.sum(p, axis=-1, keepdims=True)
        o = jnp.dot(p.astype(BF16), v, preferred_element_type=F32) / l
        o_ref[:, hs] = o.astype(o_ref.dtype)


def _mem_attention(p_all, mem_kv, *, tq):
    seq = p_all.shape[0]
    return pl.pallas_call(
        _mem_attn_kernel,
        grid=(seq // tq,),
        in_specs=[pl.BlockSpec((tq, GROUP_W), lambda i: (i, COL_QM // GROUP_W)),
                  pl.BlockSpec((MEM_TOKENS, GROUP_W), lambda i: (0, 0)),
                  pl.BlockSpec((MEM_TOKENS, GROUP_W), lambda i: (0, 1))],
        out_specs=pl.BlockSpec((tq, GROUP_W), lambda i: (i, 0)),
        out_shape=jax.ShapeDtypeStruct((seq, GROUP_W), BF16),
        compiler_params=_params(1),
        name="mem_attn",
    )(p_all, mem_kv, mem_kv)


def _conv_kernel(b_ref, c_ref, v_ref, st_ref, w_ref, o_ref, ns_ref, carry_ref, *, t_last):
    i = pl.program_id(1)

    @pl.when(i == 0)
    def _():
        carry_ref[0:2, :] = st_ref[0]

    u = c_ref[0].astype(F32) * v_ref[0].astype(F32)
    c2 = carry_ref[0:1, :]
    c1 = carry_ref[1:2, :]
    row = lax.broadcasted_iota(jnp.int32, u.shape, 0)
    u1 = jnp.where(row == 0, c1, pltpu.roll(u, 1, 0))
    u2 = jnp.where(row == 0, c2, jnp.where(row == 1, c1, pltpu.roll(u, 2, 0)))
    o = b_ref[0].astype(F32) * (w_ref[0:1, :] * u2 + w_ref[1:2, :] * u1 + w_ref[2:3, :] * u)
    o_ref[0] = o.astype(o_ref.dtype)
    last2 = u[t_last - 2:t_last, :]
    carry_ref[0:2, :] = last2
    ns_ref[0] = last2


def _conv(p3, state, conv_w, *, tm, t_last):
    nb, t = p3.shape[0], p3.shape[1]
    blk = (1, tm, CONV_W)
    return pl.pallas_call(
        functools.partial(_conv_kernel, t_last=t_last),
        grid=(nb, t // tm),
        in_specs=[
            pl.BlockSpec(blk, lambda b, i: (b, i, COL_B // CONV_W)),
            pl.BlockSpec(blk, lambda b, i: (b, i, COL_C // CONV_W)),
            pl.BlockSpec(blk, lambda b, i: (b, i, COL_V // CONV_W)),
            pl.BlockSpec((1, 2, CONV_W), lambda b, i: (b, 0, 0)),
            pl.BlockSpec((3, CONV_W), lambda b, i: (0, 0)),
        ],
        out_specs=[pl.BlockSpec(blk, lambda b, i: (b, i, 0)),
                   pl.BlockSpec((1, 2, CONV_W), lambda b, i: (b, 0, 0))],
        out_shape=[jax.ShapeDtypeStruct((nb, t, CONV_W), BF16),
                   jax.ShapeDtypeStruct((nb, 2, CONV_W), F32)],
        scratch_shapes=[pltpu.VMEM((8, CONV_W), F32)],
        compiler_params=_params(2),
        name="conv",
    )(p3, p3, p3, state, conv_w)


def _merge_kernel(oc_ref, oa_ref, om_ref, gc_ref, ga_ref, gm_ref, h_ref,
                  wc_ref, wa_ref, wm_ref, wo_ref, g_ref, b_ref, y_ref, yb_ref, *, sub_rows):
    for r0 in range(0, h_ref.shape[0], sub_rows):
        rows = slice(r0, r0 + sub_rows)
        mc = jnp.dot(oc_ref[rows, :], wc_ref[...], preferred_element_type=F32)
        ma = jnp.dot(oa_ref[rows, :], wa_ref[...], preferred_element_type=F32)
        mm = jnp.dot(om_ref[rows, :], wm_ref[...], preferred_element_type=F32)
        mix = (jax.nn.sigmoid(gc_ref[rows, :].astype(F32)) * mc
               + jax.nn.sigmoid(ga_ref[rows, :].astype(F32)) * ma
               + jax.nn.sigmoid(gm_ref[rows, :].astype(F32)) * mm)
        z = ALPHA * h_ref[rows, :] + jnp.dot(mix.astype(BF16), wo_ref[...], preferred_element_type=F32)
        y = _layer_norm_rows(z, g_ref[...], b_ref[...])
        y_ref[rows, :] = y
        yb_ref[rows, :] = y.astype(BF16)


def _merge(o_conv, o_attn, o_mem, p_all, h, w_c, w_a, w_m, w_o, g, b, *, tm):
    rows = h.shape[0]
    single = pl.Buffered(1)
    row_blk = lambda w: pl.BlockSpec((tm, w), lambda i: (i, 0))
    gate_blk = lambda col: pl.BlockSpec((tm, D_MODEL), lambda i: (i, col // D_MODEL))
    whole = lambda a: pl.BlockSpec(a.shape, lambda i: (0, 0), pipeline_mode=single)
    return pl.pallas_call(
        functools.partial(_merge_kernel, sub_rows=min(tm, 256)),
        grid=(rows // tm,),
        in_specs=[row_blk(CONV_W), row_blk(GROUP_W), row_blk(GROUP_W),
                  gate_blk(COL_GC), gate_blk(COL_GA), gate_blk(COL_GM), row_blk(D_MODEL),
                  whole(w_c), whole(w_a), whole(w_m), whole(w_o), whole(g), whole(b)],
        out_specs=[row_blk(D_MODEL), row_blk(D_MODEL)],
        out_shape=[jax.ShapeDtypeStruct((rows, D_MODEL), F32),
                   jax.ShapeDtypeStruct((rows, D_MODEL), BF16)],
        compiler_params=_params(1),
        name="merge",
    )(o_conv, o_attn, o_mem, p_all, p_all, p_all, h, w_c, w_a, w_m, w_o, g, b)


def _over_quad(x, op):
    x = op(x, pltpu.roll(x, HEADS, 0))
    return op(x, pltpu.roll(x, 2 * HEADS, 0))


def _lane_dot(k3, q):
    return jnp.sum(k3 * q, axis=-1, keepdims=True) * SCALE


def _quad_attention(k3, v3, qt, bias3, k_new, v_new, bias_new):
    s = _lane_dot(k3, qt[None]) + bias3
    m = jnp.max(s, axis=0)
    if k_new is not None:
        s_new = _lane_dot(k_new, qt) + bias_new
        m = jnp.maximum(m, s_new)
    m = _over_quad(m, jnp.maximum)
    e = jnp.exp(s - m[None])
    l = jnp.sum(e, axis=0)
    acc = jnp.sum(e * v3, axis=0)
    if k_new is not None:
        e_new = jnp.exp(s_new - m)
        l = l + e_new
        acc = acc + e_new * v_new
    l = _over_quad(l, jnp.add)
    acc = _over_quad(acc, jnp.add)
    return acc / l, m + jnp.log(l)


def _sample_kernel(qd_ref, qt_ref, kn_ref, vn_ref, sl_ref,
                   k0_ref, v0_ref, k1_ref, v1_ref, k2_ref, v2_ref, mk_ref, mv_ref, *new_refs, t_new):
    k_refs = (k0_ref, k1_ref, k2_ref)
    v_refs = (v0_ref, v1_ref, v2_ref)
    oa_ref, om_ref = new_refs[N_GROUPS * 2], new_refs[N_GROUPS * 2 + 1]
    quad16 = lax.broadcasted_iota(jnp.int32, (ROWS16, HEAD_DIM), 0) >> 2

    outs, lses = [], []

    win0 = DIL_GROUPS[0][0]
    a_idx = lax.broadcasted_iota(jnp.int32, (win0 // 4, ROWS16, HEAD_DIM), 0)
    b_idx = lax.broadcasted_iota(jnp.int32, (win0 // 4, ROWS16, HEAD_DIM), 1) >> 2
    tok = a_idx * 4 + b_idx
    slope0 = sl_ref[0]
    k3, v3 = k0_ref[0], v0_ref[0]
    o0 = jnp.zeros((ROWS16, HEAD_DIM), F32)
    l0 = jnp.zeros((ROWS16, HEAD_DIM), F32)
    for t in range(t_new):
        bias3 = jnp.where(tok >= t, -(slope0[None] * (win0 + t - tok).astype(F32)), NEG_INF)
        bias_new = jnp.where(quad16 <= t, -(slope0 * (t - quad16).astype(F32)), NEG_INF)
        o_t, l_t = _quad_attention(k3, v3, qt_ref[0, 0, t], bias3, kn_ref[0, 0], vn_ref[0, 0], bias_new)
        o0 = jnp.where(quad16 == t, o_t, o0)
        l0 = jnp.where(quad16 == t, l_t, l0)
    outs.append(o0)
    lses.append(l0)

    for g in (1, 2):
        win, dil = DIL_GROUPS[g]
        q16 = qd_ref[0, g - 1]
        k3 = k_refs[g][0, :, 0:ROWS16, :]
        v3 = v_refs[g][0, :, 0:ROWS16, :]
        i_idx = lax.broadcasted_iota(jnp.int32, (BAND, ROWS16, HEAD_DIM), 0)
        bias3 = -(sl_ref[g][None] * (win - i_idx * dil).astype(F32))
        s = _lane_dot(k3, q16[None]) + bias3
        s_new = _lane_dot(kn_ref[0, g], q16)
        m = jnp.maximum(jnp.max(s, axis=0), s_new)
        e = jnp.exp(s - m[None])
        e_new = jnp.exp(s_new - m)
        l = jnp.sum(e, axis=0) + e_new
        acc = jnp.sum(e * v3, axis=0) + e_new * vn_ref[0, g]
        outs.append(acc / l)
        lses.append(m + jnp.log(l))

    m = jnp.maximum(jnp.maximum(lses[0], lses[1]), lses[2])
    w = [jnp.exp(x - m) for x in lses]
    oa_ref[0] = (w[0] * outs[0] + w[1] * outs[1] + w[2] * outs[2]) / (w[0] + w[1] + w[2])

    k3, v3 = mk_ref[0], mv_ref[0]
    om = jnp.zeros((ROWS16, HEAD_DIM), F32)
    for t in range(t_new):
        o_t, _ = _quad_attention(k3, v3, qt_ref[0, 1, t], jnp.zeros((1, ROWS16, HEAD_DIM), F32), None, None, None)
        om = jnp.where(quad16 == t, o_t, om)
    om_ref[0] = om

    for i, out_ref in enumerate(new_refs[N_GROUPS * 2 + 2:]):
        out_ref[0, 0] = (kn_ref, vn_ref)[i % 2][0, i // 2]


def _sample_attention(qd, qt, kn, vn, slopes16, k_states, v_states, shifted, mem_k, mem_v, *, t_new):
    nb = qd.shape[0]
    states = []
    for g in range(N_GROUPS):
        states += [k_states[g], v_states[g]]
    whole = lambda a: pl.BlockSpec((1,) + a.shape[1:], lambda b: (b,) + (0,) * (a.ndim - 1))
    head_rows = lambda a: pl.BlockSpec((1, a.shape[1], ROWS16, HEAD_DIM), lambda b: (b, 0, 0, 0))
    last_rows = lambda a: pl.BlockSpec((1, 1, ROWS16, HEAD_DIM),
                                       lambda b, a=a: (b, a.shape[1] - 1, a.shape[2] // ROWS16 - 1, 0))
    o_spec = pl.BlockSpec((1, ROWS16, HEAD_DIM), lambda b: (b, 0, 0))
    o_shape = jax.ShapeDtypeStruct((nb, ROWS16, HEAD_DIM), F32)
    n_in = 5 + len(states) + 2
    return pl.pallas_call(
        functools.partial(_sample_kernel, t_new=t_new),
        grid=(nb,),
        in_specs=[whole(qd), whole(qt), whole(kn), whole(vn),
                  pl.BlockSpec(slopes16.shape, lambda b: (0, 0, 0))]
                 + [head_rows(s) for s in states] + [whole(mem_k), whole(mem_v)]
                 + [pl.BlockSpec(memory_space=pl.ANY)] * len(shifted),
        out_specs=[o_spec, o_spec] + [last_rows(s) for s in shifted],
        out_shape=[o_shape, o_shape] + [jax.ShapeDtypeStruct(s.shape, F32) for s in shifted],
        input_output_aliases={n_in + i: 2 + i for i in range(len(shifted))},
        compiler_params=_params(1),
        name="sample_attn",
    )(qd, qt, kn, vn, slopes16, *states, mem_k, mem_v, *shifted)


def kernel(x_prompt, x_sample, mem_prompt, state_conv, state_k_w128, state_v_w128, state_k_w512, state_v_w512, state_k_w2048, state_v_w2048, cache_mem_k, cache_mem_v, ln1_g, ln1_b, w_ffn1_gu, w_ffn1_down, w_in, conv_w, w_mem_kv, w_br_conv, w_br_attn, w_br_mem, w_o, ln2_g, ln2_b, w_ffn2_gu, w_ffn2_down, ln3_g, ln3_b):
    depth = w_in.shape[0]
    seq = x_prompt.shape[1]
    nb, t_new = x_sample.shape[0], x_sample.shape[1]
    assert depth == 1 and x_prompt.shape[0] == 1 and seq % CHUNK == 0
    assert t_new * HEADS == ROWS16 and t_new <= DIL_GROUPS[1][1]

    wb = lambda w: w[0].astype(BF16)
    wc_b, wa_b, wm_b, wo_b = wb(w_br_conv), wb(w_br_attn), wb(w_br_mem), wb(w_o)
    n_ff_blocks = D_FF // 512
    cw = conv_w[0]
    slopes_lane = np.broadcast_to(SLOPES[:, :, None], (N_GROUPS, HEADS, HEAD_DIM))
    slopes_band = jnp.asarray(np.pad(slopes_lane.transpose(1, 0, 2), ((0, 0), (0, 8 - N_GROUPS), (0, 0))))
    slopes16 = jnp.asarray(np.tile(slopes_lane, (1, ROWS16 // HEADS, 1)))

    rows_s = nb * SAMPLE_PAD
    xs = jnp.pad(x_sample, ((0, 0), (0, SAMPLE_PAD - t_new), (0, 0))).reshape(rows_s, D_MODEL)
    s1, s1b, w1g, w1u, w1d = _ffn(xs, w_ffn1_gu[0], w_ffn1_gu[0], w_ffn1_down[0], ln1_g, ln1_b,
                                  tm=rows_s, tf=512, up_blk0=n_ff_blocks, emit_bf16=True)
    ps_all, win_b = _matmul(s1b, w_in[0], tm=rows_s, tn=512, out_dtype=F32, name="in_proj_s", emit_bf16=True)

    def tiles(state):
        win = state.shape[2]
        per_tile = 4 if win <= DIL_GROUPS[1][0] else DIL_GROUPS[2][1]
        return state.reshape(nb, win // per_tile, per_tile * HEADS, HEAD_DIM)

    k_states = [tiles(s) for s in (state_k_w128, state_k_w512, state_k_w2048)]
    v_states = [tiles(s) for s in (state_v_w128, state_v_w512, state_v_w2048)]
    h1, h1b, *shifted = _ffn(x_prompt[0], w1g, w1u, w1d, ln1_g, ln1_b, tm=1024, tf=512,
                             windows=k_states + v_states)
    shifted = [shifted[g + part * N_GROUPS] for g in range(N_GROUPS) for part in (0, 1)]
    p_all, qkv = _in_proj(h1b, win_b, tm=1024, tn=512)
    mem_kv = _matmul(mem_prompt[0].astype(BF16), w_mem_kv[0], tm=MEM_TOKENS, tn=512, out_dtype=F32, name="mem_kv")
    o_attn = _band_attention(qkv, slopes_band)
    o_mem = _mem_attention(p_all, mem_kv, tq=512)
    o_conv, conv_p = _conv(p_all[None], jnp.zeros((1, 2, CONV_W), F32), cw, tm=512, t_last=512)
    h2, _ = _merge(o_conv[0], o_attn, o_mem, p_all, h1, wc_b, wa_b, wm_b, wo_b, ln2_g, ln2_b, tm=256)

    prompt_states = []
    for g, (win, _) in enumerate(DIL_GROUPS):
        for tensor in (1, 2):
            c0 = tensor * N_GROUPS * GROUP_W + g * GROUP_W
            prompt_states.append(qkv[seq - win:, c0:c0 + GROUP_W].reshape(1, 1, win, HEADS, HEAD_DIM))
    mem_k_p = mem_kv[:, :GROUP_W].reshape(1, 1, MEM_TOKENS, HEADS, HEAD_DIM)
    mem_v_p = mem_kv[:, GROUP_W:].reshape(1, 1, MEM_TOKENS, HEADS, HEAD_DIM)

    ps3 = ps_all.reshape(nb, SAMPLE_PAD, IN_W)

    real = ps3[:, :t_new]
    heads = lambda a, n: a.reshape(nb, t_new, n, HEADS, HEAD_DIM)
    q, k, v = (heads(real[..., c:c + N_GROUPS * GROUP_W], N_GROUPS) for c in (COL_Q, COL_K, COL_VA))
    rows_th = lambda a: a.transpose(0, 2, 1, 3, 4).reshape(nb, a.shape[2], ROWS16, HEAD_DIM)
    over_quad = lambda a: jnp.broadcast_to(a[:, :, None], (nb, t_new, ROWS16 // HEADS, HEADS, HEAD_DIM)
                                           ).reshape(nb, t_new, ROWS16, HEAD_DIM)
    qd = rows_th(q[:, :, 1:])
    qm = real[..., COL_QM:COL_QM + GROUP_W].reshape(nb, t_new, HEADS, HEAD_DIM)
    qt = jnp.stack([over_quad(q[:, :, 0]), over_quad(qm)], axis=1)
    kn, vn = rows_th(k), rows_th(v)

    outs = _sample_attention(qd, qt, kn, vn, slopes16, k_states, v_states, shifted,
                             tiles(cache_mem_k), tiles(cache_mem_v), t_new=t_new)
    pad_rows = lambda a, w: jnp.pad(a.reshape(nb, t_new, w), ((0, 0), (0, SAMPLE_PAD - t_new), (0, 0))
                                    ).reshape(rows_s, w).astype(BF16)
    o_attn_s, o_mem_s = pad_rows(outs[0], GROUP_W), pad_rows(outs[1], GROUP_W)
    sample_states = [s.reshape(1, nb, -1, HEADS, HEAD_DIM) for s in outs[2:]]

    o_conv_s, conv_s = _conv(ps3, state_conv[0], cw, tm=SAMPLE_PAD, t_last=t_new)
    s2, _ = _merge(o_conv_s.reshape(rows_s, CONV_W), o_attn_s, o_mem_s, ps_all, s1,
                   wc_b, wa_b, wm_b, wo_b, ln2_g, ln2_b, tm=rows_s)
    y_s, _, w2g, w2u, w2d = _ffn(s2, w_ffn2_gu[0], w_ffn2_gu[0], w_ffn2_down[0], ln3_g, ln3_b,
                                 tm=rows_s, tf=512, up_blk0=n_ff_blocks, emit_bf16=True)
    y_sample = y_s.reshape(nb, SAMPLE_PAD, D_MODEL)[:, :t_new]
    y_prompt, _ = _ffn(h2, w2g, w2u, w2d, ln3_g, ln3_b, tm=1024, tf=512)

    return (y_prompt[None], y_sample, conv_p[None], *prompt_states, mem_k_p, mem_v_p,
            conv_s[None], *sample_states)
```

```python
import functools

import jax
import jax.numpy as jnp
import numpy as np
from jax import lax
from jax.experimental import pallas as pl
from jax.experimental.pallas import tpu as pltpu

F32 = jnp.float32
BF16 = jnp.bfloat16

D_MODEL = 2048
D_FF = 5632
HEAD_DIM = 128
HEADS = 4
GROUP_W = HEADS * HEAD_DIM
DIL_GROUPS = ((128, 1), (512, 4), (2048, 16))
N_GROUPS = len(DIL_GROUPS)
BAND = 128
CHUNK = BAND * DIL_GROUPS[-1][1]
CONV_W = 1024
MEM_TOKENS = 256
IN_W = 14336
COL_B, COL_C, COL_V = 0, 1024, 2048
COL_Q, COL_K, COL_VA = 3072, 4608, 6144
COL_QM = 7680
COL_GC, COL_GA, COL_GM = 8192, 10240, 12288
QKV_W = COL_QM - COL_Q
ALPHA = 2.0 ** 0.25
LN_EPS = 1e-5
NEG_INF = -1e30
SCALE = HEAD_DIM ** -0.5
SLOPES = np.exp2(np.float32(-8.0) * np.arange(1, 13, dtype=np.float32) / np.float32(12)).reshape(3, 4)

SAMPLE_PAD = 8
ROWS16 = 16
VMEM_LIMIT = 56 * 1024 * 1024


def _params(n_axes, vmem=VMEM_LIMIT):
    return pltpu.CompilerParams(dimension_semantics=("arbitrary",) * n_axes, vmem_limit_bytes=vmem)


def _layer_norm_rows(z, g, b):
    mu = jnp.mean(z, axis=-1, keepdims=True)
    zc = z - mu
    var = jnp.mean(zc * zc, axis=-1, keepdims=True)
    return zc * lax.rsqrt(var + LN_EPS) * g + b


def _window_loads(b, src_refs, stage_refs, slot, sems):
    return [pltpu.make_async_copy(src.at[b], stage.at[slot], sems.at[slot, n])
            for n, (src, stage) in enumerate(zip(src_refs, stage_refs))]


def _window_stores(b, stage_refs, dst_refs, slot, sems):
    copies = []
    for stage, dst in zip(stage_refs, dst_refs):
        tiles, rows = dst.shape[1], dst.shape[2]
        keep = rows - ROWS16
        if keep:
            copies.append((stage.at[slot, :, pl.ds(ROWS16, keep), :], dst.at[b, :, pl.ds(0, keep), :]))
        copies.append((stage.at[slot, pl.ds(1, tiles - 1), pl.ds(0, ROWS16), :],
                       dst.at[b, pl.ds(0, tiles - 1), pl.ds(keep, ROWS16), :]))
        copies.append((stage.at[slot, pl.ds(tiles - 1, 1), pl.ds(keep, ROWS16), :],
                       dst.at[b, pl.ds(tiles - 1, 1), pl.ds(keep, ROWS16), :]))
    return [pltpu.make_async_copy(s, d, sems.at[slot, n]) for n, (s, d) in enumerate(copies)]


def _ffn_kernel(*refs, n_ff, row_chunk, emit_bf16, bf16_out, n_windows, n_requests):
    x_ref, wg_ref, wu_ref, wd_ref, g_ref, b_ref = refs[:6]
    win_src = refs[6:6 + n_windows]
    y_ref = refs[6 + n_windows]
    yb_ref = refs[7 + n_windows] if bf16_out else None
    rest = refs[7 + bf16_out + n_windows:]
    if emit_bf16:
        wb_refs, rest = rest[:3], rest[3:]
    win_dst, rest = rest[:n_windows], rest[n_windows:]
    xb_ref = rest[0]
    j = pl.program_id(1)

    if n_windows:
        half = n_windows // 2
        stage_refs, load_sems, store_sems = rest[1:1 + half], rest[1 + half], rest[2 + half]
        step = pl.program_id(0) * n_ff + j
        n_items = 2 * n_requests
        for part in (0, 1):
            srcs, dsts = win_src[part * half:(part + 1) * half], win_dst[part * half:(part + 1) * half]
            mine = step % 2 == part

            @pl.when(mine & (step >= 2) & (step < n_items + 2))
            def _():
                for copy in _window_stores((step - 2) // 2, stage_refs, dsts, part, store_sems):
                    copy.wait()

            @pl.when(mine & (step < n_items))
            def _():
                for copy in _window_loads(step // 2, srcs, stage_refs, part, load_sems):
                    copy.start()

            @pl.when(jnp.logical_not(mine) & (step >= 1) & (step < n_items + 1))
            def _():
                for copy in _window_loads((step - 1) // 2, srcs, stage_refs, part, load_sems):
                    copy.wait()
                for copy in _window_stores((step - 1) // 2, stage_refs, dsts, part, store_sems):
                    copy.start()

    @pl.when(j == 0)
    def _():
        xb_ref[...] = x_ref[...].astype(BF16)
        y_ref[...] = jnp.zeros(y_ref.shape, F32)

    wg, wu, wd = wg_ref[...], wu_ref[...], wd_ref[...]
    if emit_bf16:
        wg, wu, wd = wg.astype(BF16), wu.astype(BF16), wd.astype(BF16)
        wb_refs[0][...], wb_refs[1][...], wb_refs[2][...] = wg, wu, wd
    xb = xb_ref[...]
    gate = jnp.dot(xb, wg, preferred_element_type=F32)
    up = jnp.dot(xb, wu, preferred_element_type=F32)
    act = (gate * jax.nn.sigmoid(gate) * up).astype(BF16)
    y_ref[...] += jnp.dot(act, wd, preferred_element_type=F32)

    @pl.when(j == n_ff - 1)
    def _():
        def chunk(c, carry):
            rows = pl.ds(pl.multiple_of(c * row_chunk, row_chunk), row_chunk)
            z = ALPHA * x_ref[rows, :] + 0.5 * y_ref[rows, :]
            y = _layer_norm_rows(z, g_ref[...], b_ref[...])
            y_ref[rows, :] = y
            if bf16_out:
                yb_ref[rows, :] = y.astype(BF16)
            return carry
        lax.fori_loop(0, x_ref.shape[0] // row_chunk, chunk, 0)


def _ffn(x, w_gate, w_up, w_down, g, b, *, tm, tf, up_blk0=0, emit_bf16=False, bf16_out=True,
         out_buffers=1, windows=()):
    rows = x.shape[0]
    n_ff = D_FF // tf
    n_steps = (rows // tm) * n_ff
    n_requests = windows[0].shape[0] if windows else 0
    assert not emit_bf16 or rows == tm
    assert n_steps >= 2 * n_requests + 2
    row_mode = pl.Buffered(1) if (rows == tm or tm >= 1024) else None
    hbm = pl.BlockSpec(memory_space=pl.ANY)
    w_col = pl.BlockSpec((D_MODEL, tf), lambda i, j: (0, j))
    w_col_tile = pl.BlockSpec((None, D_MODEL, tf), lambda i, j: (j, 0, 0))
    tiled = w_gate.ndim == 3
    w_row = pl.BlockSpec((tf, D_MODEL), lambda i, j: (j, 0))
    row_tile = pl.BlockSpec((tm, D_MODEL), lambda i, j: (i, 0), pipeline_mode=row_mode)
    out_tile = row_tile if out_buffers == 1 else pl.BlockSpec((tm, D_MODEL), lambda i, j: (i, 0))
    out_specs = [out_tile] * (1 + bf16_out)
    out_shape = [jax.ShapeDtypeStruct((rows, D_MODEL), F32), jax.ShapeDtypeStruct((rows, D_MODEL), BF16)][:1 + bf16_out]
    scratch = [pltpu.VMEM((tm, D_MODEL), BF16)]
    if emit_bf16:
        out_specs += [w_col_tile, w_col_tile, w_row]
        out_shape += [jax.ShapeDtypeStruct((n_ff, D_MODEL, tf), BF16)] * 2 + [jax.ShapeDtypeStruct((D_FF, D_MODEL), BF16)]
    if windows:
        out_specs += [hbm] * len(windows)
        out_shape += [jax.ShapeDtypeStruct(w.shape, w.dtype) for w in windows]
        half = windows[:len(windows) // 2]
        scratch += [pltpu.VMEM((2,) + w.shape[1:], w.dtype) for w in half]
        scratch += [pltpu.SemaphoreType.DMA((2, len(half))),
                    pltpu.SemaphoreType.DMA((2, sum(2 + (w.shape[2] > ROWS16) for w in half)))]
    return pl.pallas_call(
        functools.partial(_ffn_kernel, n_ff=n_ff, row_chunk=min(tm, 128), emit_bf16=emit_bf16, bf16_out=bf16_out,
                          n_windows=len(windows), n_requests=n_requests),
        grid=(rows // tm, n_ff),
        in_specs=[
            row_tile,
            w_col_tile if tiled else w_col,
            w_col_tile if tiled else pl.BlockSpec((D_MODEL, tf), lambda i, j: (0, j + up_blk0)),
            w_row,
            pl.BlockSpec((1, D_MODEL), lambda i, j: (0, 0)),
            pl.BlockSpec((1, D_MODEL), lambda i, j: (0, 0)),
        ] + [hbm] * len(windows),
        out_specs=out_specs,
        out_shape=out_shape,
        scratch_shapes=scratch,
        compiler_params=_params(2),
        name="ffn_cast" if emit_bf16 else "ffn",
    )(x, w_gate, w_up, w_down, g, b, *windows)


def _mm_kernel(x_ref, w_ref, o_ref, *wb_ref):
    w = w_ref[...].astype(BF16)
    if wb_ref:
        wb_ref[0][...] = w
    o_ref[...] = jnp.dot(x_ref[...], w, preferred_element_type=F32).astype(o_ref.dtype)


def _matmul(x, w, *, tm, tn, out_dtype, name, emit_bf16=False):
    rows, k = x.shape
    n_cols = w.shape[1]
    assert not emit_bf16 or rows == tm
    w_spec = pl.BlockSpec((k, tn), lambda i, j: (0, j))
    out_specs = [pl.BlockSpec((tm, tn), lambda i, j: (i, j))]
    out_shape = [jax.ShapeDtypeStruct((rows, n_cols), out_dtype)]
    if emit_bf16:
        out_specs.append(pl.BlockSpec((None, k, tn), lambda i, j: (j, 0, 0)))
        out_shape.append(jax.ShapeDtypeStruct((n_cols // tn, k, tn), BF16))
    out = pl.pallas_call(
        _mm_kernel,
        grid=(rows // tm, n_cols // tn),
        in_specs=[pl.BlockSpec((tm, k), lambda i, j: (i, 0)), w_spec],
        out_specs=out_specs,
        out_shape=out_shape,
        compiler_params=_params(2),
        name=name,
    )(x, w)
    return out if emit_bf16 else out[0]


def _in_proj_kernel(x_ref, w_ref, p_ref, qkv_ref, *, n_own):
    j = pl.program_id(1)
    acc = jnp.dot(x_ref[...], w_ref[...], preferred_element_type=F32)
    p_ref[...] = acc.astype(BF16)

    @pl.when(j < n_own)
    def _():
        qkv_ref[...] = acc


def _in_proj(x, w, *, tm, tn):
    rows, k = x.shape
    assert w.shape == (IN_W // tn, k, tn) and COL_Q % tn == 0 and COL_GC % tn == 0
    c_lo, n_own = COL_Q // tn, (COL_GC - COL_Q) // tn

    def col_block(j):
        return jnp.where(j < n_own, j + c_lo, jnp.where(j < n_own + c_lo, j - n_own, j))

    return pl.pallas_call(
        functools.partial(_in_proj_kernel, n_own=n_own),
        grid=(rows // tm, IN_W // tn),
        in_specs=[pl.BlockSpec((tm, k), lambda i, j: (i, 0)),
                  pl.BlockSpec((None, k, tn), lambda i, j: (col_block(j), 0, 0))],
        out_specs=[pl.BlockSpec((tm, tn), lambda i, j: (i, col_block(j))),
                   pl.BlockSpec((tm, tn), lambda i, j: (i, jnp.minimum(j, n_own - 1)))],
        out_shape=[jax.ShapeDtypeStruct((rows, IN_W), BF16),
                   jax.ShapeDtypeStruct((rows, COL_GC - COL_Q), F32)],
        compiler_params=_params(2),
        name="in_proj",
    )(x, w)


def _rows(ref, start, dil):
    if dil == 1:
        return ref[start:start + BAND, :]
    return ref[pl.ds(start, BAND, stride=dil), :]


def _band_kernel(sl_ref, q0_ref, k0_ref, v0_ref, q1_ref, k1_ref, v1_ref, q2_ref, k2_ref, v2_ref,
                 o_ref, kt0_ref, vt0_ref, kt1_ref, vt1_ref, kt2_ref, vt2_ref, og_ref, lg_ref):
    c = pl.program_id(1)
    q_refs = (q0_ref, q1_ref, q2_ref)
    k_refs = (k0_ref, k1_ref, k2_ref)
    v_refs = (v0_ref, v1_ref, v2_ref)
    kt_refs = (kt0_ref, kt1_ref, kt2_ref)
    vt_refs = (vt0_ref, vt1_ref, vt2_ref)

    @pl.when(c == 0)
    def _():
        for ref in kt_refs + vt_refs:
            ref[...] = jnp.zeros(ref.shape, F32)

    qi = lax.broadcasted_iota(jnp.int32, (BAND, 2 * BAND), 0)
    ki = lax.broadcasted_iota(jnp.int32, (BAND, 2 * BAND), 1)
    rel = BAND + qi - ki
    valid = (rel >= 0) & (rel <= BAND)
    no_prev = jnp.where(ki < BAND, jnp.where(c == 0, NEG_INF, 0.0), 0.0)

    for g, (_, dil) in enumerate(DIL_GROUPS):
        slope = sl_ref[0, g:g + 1, 0:1]
        bias = jnp.where(valid, -(slope * (rel * dil).astype(F32)), NEG_INF)
        bias_first = bias + no_prev
        span = BAND * dil
        for r in range(dil):
            for n in range(CHUNK // span):
                start = r + n * span
                q = _rows(q_refs[g], start, dil).astype(BF16)
                if n == 0:
                    k_prev, v_prev = _rows(kt_refs[g], r, dil), _rows(vt_refs[g], r, dil)
                else:
                    k_prev, v_prev = _rows(k_refs[g], start - span, dil), _rows(v_refs[g], start - span, dil)
                k = jnp.concatenate([k_prev, _rows(k_refs[g], start, dil)], axis=0).astype(BF16)
                v = jnp.concatenate([v_prev, _rows(v_refs[g], start, dil)], axis=0).astype(BF16)
                s = lax.dot_general(q, k, (((1,), (1,)), ((), ())), preferred_element_type=F32) * SCALE
                s = s + (bias_first if n == 0 else bias)
                m = jnp.max(s, axis=-1, keepdims=True)
                p = jnp.exp(s - m)
                l = jnp.sum(p, axis=-1, keepdims=True)
                o = jnp.dot(p.astype(BF16), v, preferred_element_type=F32) / l
                lse = jnp.broadcast_to(m + jnp.log(l), (BAND, HEAD_DIM))
                if dil == 1:
                    og_ref[g, start:start + BAND, :] = o
                    lg_ref[g, start:start + BAND, :] = lse
                else:
                    og_ref[g, pl.ds(start, BAND, stride=dil), :] = o
                    lg_ref[g, pl.ds(start, BAND, stride=dil), :] = lse
        kt_refs[g][...] = k_refs[g][CHUNK - span:CHUNK, :]
        vt_refs[g][...] = v_refs[g][CHUNK - span:CHUNK, :]

    rows_per_step = 256

    def combine(i, carry):
        rows = pl.ds(pl.multiple_of(i * rows_per_step, rows_per_step), rows_per_step)
        l0, l1, l2 = lg_ref[0, rows, :], lg_ref[1, rows, :], lg_ref[2, rows, :]
        m = jnp.maximum(jnp.maximum(l0, l1), l2)
        e0, e1, e2 = jnp.exp(l0 - m), jnp.exp(l1 - m), jnp.exp(l2 - m)
        o = (e0 * og_ref[0, rows, :] + e1 * og_ref[1, rows, :] + e2 * og_ref[2, rows, :]) / (e0 + e1 + e2)
        o_ref[rows, :] = o.astype(o_ref.dtype)
        return carry

    lax.fori_loop(0, CHUNK // rows_per_step, combine, 0)


def _band_attention(qkv, slopes):
    seq = qkv.shape[0]
    blk = (CHUNK, HEAD_DIM)
    per_tensor = N_GROUPS * HEADS
    in_specs = [pl.BlockSpec((1, 8, HEAD_DIM), lambda h, c: (h, 0, 0))]
    for g in range(N_GROUPS):
        for tensor in range(3):
            in_specs.append(pl.BlockSpec(
                blk, lambda h, c, g=g, tensor=tensor: (c, tensor * per_tensor + g * HEADS + h)))
    tails = []
    for _, dil in DIL_GROUPS:
        tails += [pltpu.VMEM((BAND * dil, HEAD_DIM), F32)] * 2
    return pl.pallas_call(
        _band_kernel,
        grid=(HEADS, seq // CHUNK),
        in_specs=in_specs,
        out_specs=pl.BlockSpec(blk, lambda h, c: (c, h)),
        out_shape=jax.ShapeDtypeStruct((seq, GROUP_W), BF16),
        scratch_shapes=tails + [pltpu.VMEM((N_GROUPS, CHUNK, HEAD_DIM), F32),
                                pltpu.VMEM((N_GROUPS, CHUNK, HEAD_DIM), F32)],
        compiler_params=_params(2),
        name="band_attn",
    )(slopes, *([qkv] * 9))


def _mem_attn_kernel(q_ref, mk_ref, mv_ref, o_ref):
    for h in range(HEADS):
        hs = slice(h * HEAD_DIM, (h + 1) * HEAD_DIM)
        q = q_ref[:, hs]
        k = mk_ref[:, hs].astype(BF16)
        v = mv_ref[:, hs].astype(BF16)
        s = lax.dot_general(q, k, (((1,), (1,)), ((), ())), preferred_element_type=F32) * SCALE
        m = jnp.max(s, axis=-1, keepdims=True)
        p = jnp.exp(s - m)
        l = jnp.sum(p, axis=-1, keepdims=True)
        o = jnp.dot(p.astype(BF16), v, preferred_element_type=F32) / l
        o_ref[:, hs] = o.astype(o_ref.dtype)


def _mem_attention(p_all, mem_kv, *, tq):
    seq = p_all.shape[0]
    return pl.pallas_call(
        _mem_attn_kernel,
        grid=(seq // tq,),
        in_specs=[pl.BlockSpec((tq, GROUP_W), lambda i: (i, COL_QM // GROUP_W)),
                  pl.BlockSpec((MEM_TOKENS, GROUP_W), lambda i: (0, 0)),
                  pl.BlockSpec((MEM_TOKENS, GROUP_W), lambda i: (0, 1))],
        out_specs=pl.BlockSpec((tq, GROUP_W), lambda i: (i, 0)),
        out_shape=jax.ShapeDtypeStruct((seq, GROUP_W), BF16),
        compiler_params=_params(1),
        name="mem_attn",
    )(p_all, mem_kv, mem_kv)


def _conv_kernel(b_ref, c_ref, v_ref, st_ref, w_ref, o_ref, ns_ref, carry_ref, *, t_last):
    i = pl.program_id(1)

    @pl.when(i == 0)
    def _():
        carry_ref[0:2, :] = st_ref[0]

    u = c_ref[0].astype(F32) * v_ref[0].astype(F32)
    c2 = carry_ref[0:1, :]
    c1 = carry_ref[1:2, :]
    row = lax.broadcasted_iota(jnp.int32, u.shape, 0)
    u1 = jnp.where(row == 0, c1, pltpu.roll(u, 1, 0))
    u2 = jnp.where(row == 0, c2, jnp.where(row == 1, c1, pltpu.roll(u, 2, 0)))
    o = b_ref[0].astype(F32) * (w_ref[0:1, :] * u2 + w_ref[1:2, :] * u1 + w_ref[2:3, :] * u)
    o_ref[0] = o.astype(o_ref.dtype)
    last2 = u[t_last - 2:t_last, :]
    carry_ref[0:2, :] = last2
    ns_ref[0] = last2


def _conv(p3, state, conv_w, *, tm, t_last):
    nb, t = p3.shape[0], p3.shape[1]
    blk = (1, tm, CONV_W)
    return pl.pallas_call(
        functools.partial(_conv_kernel, t_last=t_last),
        grid=(nb, t // tm),
        in_specs=[
            pl.BlockSpec(blk, lambda b, i: (b, i, COL_B // CONV_W)),
            pl.BlockSpec(blk, lambda b, i: (b, i, COL_C // CONV_W)),
            pl.BlockSpec(blk, lambda b, i: (b, i, COL_V // CONV_W)),
            pl.BlockSpec((1, 2, CONV_W), lambda b, i: (b, 0, 0)),
            pl.BlockSpec((3, CONV_W), lambda b, i: (0, 0)),
        ],
        out_specs=[pl.BlockSpec(blk, lambda b, i: (b, i, 0)),
                   pl.BlockSpec((1, 2, CONV_W), lambda b, i: (b, 0, 0))],
        out_shape=[jax.ShapeDtypeStruct((nb, t, CONV_W), BF16),
                   jax.ShapeDtypeStruct((nb, 2, CONV_W), F32)],
        scratch_shapes=[pltpu.VMEM((8, CONV_W), F32)],
        compiler_params=_params(2),
        name="conv",
    )(p3, p3, p3, state, conv_w)


def _merge_kernel(oc_ref, oa_ref, om_ref, gc_ref, ga_ref, gm_ref, h_ref,
                  wc_ref, wa_ref, wm_ref, wo_ref, g_ref, b_ref, y_ref, yb_ref, *, sub_rows):
    for r0 in range(0, h_ref.shape[0], sub_rows):
        rows = slice(r0, r0 + sub_rows)
        mc = jnp.dot(oc_ref[rows, :], wc_ref[...], preferred_element_type=F32)
        ma = jnp.dot(oa_ref[rows, :], wa_ref[...], preferred_element_type=F32)
        mm = jnp.dot(om_ref[rows, :], wm_ref[...], preferred_element_type=F32)
        mix = (jax.nn.sigmoid(gc_ref[rows, :].astype(F32)) * mc
               + jax.nn.sigmoid(ga_ref[rows, :].astype(F32)) * ma
               + jax.nn.sigmoid(gm_ref[rows, :].astype(F32)) * mm)
        z = ALPHA * h_ref[rows, :] + jnp.dot(mix.astype(BF16), wo_ref[...], preferred_element_type=F32)
        y = _layer_norm_rows(z, g_ref[...], b_ref[...])
        y_ref[rows, :] = y
        yb_ref[rows, :] = y.astype(BF16)


def _merge(o_conv, o_attn, o_mem, p_all, h, w_c, w_a, w_m, w_o, g, b, *, tm):
    rows = h.shape[0]
    single = pl.Buffered(1)
    row_blk = lambda w: pl.BlockSpec((tm, w), lambda i: (i, 0))
    gate_blk = lambda col: pl.BlockSpec((tm, D_MODEL), lambda i: (i, col // D_MODEL))
    whole = lambda a: pl.BlockSpec(a.shape, lambda i: (0, 0), pipeline_mode=single)
    return pl.pallas_call(
        functools.partial(_merge_kernel, sub_rows=min(tm, 256)),
        grid=(rows // tm,),
        in_specs=[row_blk(CONV_W), row_blk(GROUP_W), row_blk(GROUP_W),
                  gate_blk(COL_GC), gate_blk(COL_GA), gate_blk(COL_GM), row_blk(D_MODEL),
                  whole(w_c), whole(w_a), whole(w_m), whole(w_o), whole(g), whole(b)],
        out_specs=[row_blk(D_MODEL), row_blk(D_MODEL)],
        out_shape=[jax.ShapeDtypeStruct((rows, D_MODEL), F32),
                   jax.ShapeDtypeStruct((rows, D_MODEL), BF16)],
        compiler_params=_params(1),
        name="merge",
    )(o_conv, o_attn, o_mem, p_all, p_all, p_all, h, w_c, w_a, w_m, w_o, g, b)


def _over_quad(x, op):
    x = op(x, pltpu.roll(x, HEADS, 0))
    return op(x, pltpu.roll(x, 2 * HEADS, 0))


def _lane_dot(k3, q):
    return jnp.sum(k3 * q, axis=-1, keepdims=True) * SCALE


def _quad_attention(k3, v3, qt, bias3, k_new, v_new, bias_new):
    s = _lane_dot(k3, qt[None]) + bias3
    m = jnp.max(s, axis=0)
    if k_new is not None:
        s_new = _lane_dot(k_new, qt) + bias_new
        m = jnp.maximum(m, s_new)
    m = _over_quad(m, jnp.maximum)
    e = jnp.exp(s - m[None])
    l = jnp.sum(e, axis=0)
    acc = jnp.sum(e * v3, axis=0)
    if k_new is not None:
        e_new = jnp.exp(s_new - m)
        l = l + e_new
        acc = acc + e_new * v_new
    l = _over_quad(l, jnp.add)
    acc = _over_quad(acc, jnp.add)
    return acc / l, m + jnp.log(l)


def _sample_kernel(qd_ref, qt_ref, kn_ref, vn_ref, sl_ref,
                   k0_ref, v0_ref, k1_ref, v1_ref, k2_ref, v2_ref, mk_ref, mv_ref, *new_refs, t_new):
    k_refs = (k0_ref, k1_ref, k2_ref)
    v_refs = (v0_ref, v1_ref, v2_ref)
    oa_ref, om_ref = new_refs[N_GROUPS * 2], new_refs[N_GROUPS * 2 + 1]
    quad16 = lax.broadcasted_iota(jnp.int32, (ROWS16, HEAD_DIM), 0) >> 2

    outs, lses = [], []

    win0 = DIL_GROUPS[0][0]
    a_idx = lax.broadcasted_iota(jnp.int32, (win0 // 4, ROWS16, HEAD_DIM), 0)
    b_idx = lax.broadcasted_iota(jnp.int32, (win0 // 4, ROWS16, HEAD_DIM), 1) >> 2
    tok = a_idx * 4 + b_idx
    slope0 = sl_ref[0]
    k3, v3 = k0_ref[0], v0_ref[0]
    o0 = jnp.zeros((ROWS16, HEAD_DIM), F32)
    l0 = jnp.zeros((ROWS16, HEAD_DIM), F32)
    for t in range(t_new):
        bias3 = jnp.where(tok >= t, -(slope0[None] * (win0 + t - tok).astype(F32)), NEG_INF)
        bias_new = jnp.where(quad16 <= t, -(slope0 * (t - quad16).astype(F32)), NEG_INF)
        o_t, l_t = _quad_attention(k3, v3, qt_ref[0, 0, t], bias3, kn_ref[0, 0], vn_ref[0, 0], bias_new)
        o0 = jnp.where(quad16 == t, o_t, o0)
        l0 = jnp.where(quad16 == t, l_t, l0)
    outs.append(o0)
    lses.append(l0)

    for g in (1, 2):
        win, dil = DIL_GROUPS[g]
        q16 = qd_ref[0, g - 1]
        k3 = k_refs[g][0, :, 0:ROWS16, :]
        v3 = v_refs[g][0, :, 0:ROWS16, :]
        i_idx = lax.broadcasted_iota(jnp.int32, (BAND, ROWS16, HEAD_DIM), 0)
        bias3 = -(sl_ref[g][None] * (win - i_idx * dil).astype(F32))
        s = _lane_dot(k3, q16[None]) + bias3
        s_new = _lane_dot(kn_ref[0, g], q16)
        m = jnp.maximum(jnp.max(s, axis=0), s_new)
        e = jnp.exp(s - m[None])
        e_new = jnp.exp(s_new - m)
        l = jnp.sum(e, axis=0) + e_new
        acc = jnp.sum(e * v3, axis=0) + e_new * vn_ref[0, g]
        outs.append(acc / l)
        lses.append(m + jnp.log(l))

    m = jnp.maximum(jnp.maximum(lses[0], lses[1]), lses[2])
    w = [jnp.exp(x - m) for x in lses]
    oa_ref[0] = (w[0] * outs[0] + w[1] * outs[1] + w[2] * outs[2]) / (w[0] + w[1] + w[2])

    k3, v3 = mk_ref[0], mv_ref[0]
    om = jnp.zeros((ROWS16, HEAD_DIM), F32)
    for t in range(t_new):
        o_t, _ = _quad_attention(k3, v3, qt_ref[0, 1, t], jnp.zeros((1, ROWS16, HEAD_DIM), F32), None, None, None)
        om = jnp.where(quad16 == t, o_t, om)
    om_ref[0] = om

    for i, out_ref in enumerate(new_refs[N_GROUPS * 2 + 2:]):
        out_ref[0, 0] = (kn_ref, vn_ref)[i % 2][0, i // 2]


def _sample_attention(qd, qt, kn, vn, slopes16, k_states, v_states, shifted, mem_k, mem_v, *, t_new):
    nb = qd.shape[0]
    states = []
    for g in range(N_GROUPS):
        states += [k_states[g], v_states[g]]
    whole = lambda a: pl.BlockSpec((1,) + a.shape[1:], lambda b: (b,) + (0,) * (a.ndim - 1))
    head_rows = lambda a: pl.BlockSpec((1, a.shape[1], ROWS16, HEAD_DIM), lambda b: (b, 0, 0, 0))
    last_rows = lambda a: pl.BlockSpec((1, 1, ROWS16, HEAD_DIM),
                                       lambda b, a=a: (b, a.shape[1] - 1, a.shape[2] // ROWS16 - 1, 0))
    o_spec = pl.BlockSpec((1, ROWS16, HEAD_DIM), lambda b: (b, 0, 0))
    o_shape = jax.ShapeDtypeStruct((nb, ROWS16, HEAD_DIM), F32)
    n_in = 5 + len(states) + 2
    return pl.pallas_call(
        functools.partial(_sample_kernel, t_new=t_new),
        grid=(nb,),
        in_specs=[whole(qd), whole(qt), whole(kn), whole(vn),
                  pl.BlockSpec(slopes16.shape, lambda b: (0, 0, 0))]
                 + [head_rows(s) for s in states] + [whole(mem_k), whole(mem_v)]
                 + [pl.BlockSpec(memory_space=pl.ANY)] * len(shifted),
        out_specs=[o_spec, o_spec] + [last_rows(s) for s in shifted],
        out_shape=[o_shape, o_shape] + [jax.ShapeDtypeStruct(s.shape, F32) for s in shifted],
        input_output_aliases={n_in + i: 2 + i for i in range(len(shifted))},
        compiler_params=_params(1),
        name="sample_attn",
    )(qd, qt, kn, vn, slopes16, *states, mem_k, mem_v, *shifted)


def kernel(x_prompt, x_sample, mem_prompt, state_conv, state_k_w128, state_v_w128, state_k_w512, state_v_w512, state_k_w2048, state_v_w2048, cache_mem_k, cache_mem_v, ln1_g, ln1_b, w_ffn1_gu, w_ffn1_down, w_in, conv_w, w_mem_kv, w_br_conv, w_br_attn, w_br_mem, w_o, ln2_g, ln2_b, w_ffn2_gu, w_ffn2_down, ln3_g, ln3_b):
    depth = w_in.shape[0]
    seq = x_prompt.shape[1]
    nb, t_new = x_sample.shape[0], x_sample.shape[1]
    assert depth == 1 and x_prompt.shape[0] == 1 and seq % CHUNK == 0
    assert t_new * HEADS == ROWS16 and t_new <= DIL_GROUPS[1][1]

    wb = lambda w: w[0].astype(BF16)
    wc_b, wa_b, wm_b, wo_b = wb(w_br_conv), wb(w_br_attn), wb(w_br_mem), wb(w_o)
    n_ff_blocks = D_FF // 512
    cw = conv_w[0]
    slopes_lane = np.broadcast_to(SLOPES[:, :, None], (N_GROUPS, HEADS, HEAD_DIM))
    slopes_band = jnp.asarray(np.pad(slopes_lane.transpose(1, 0, 2), ((0, 0), (0, 8 - N_GROUPS), (0, 0))))
    slopes16 = jnp.asarray(np.tile(slopes_lane, (1, ROWS16 // HEADS, 1)))

    rows_s = nb * SAMPLE_PAD
    xs = jnp.pad(x_sample, ((0, 0), (0, SAMPLE_PAD - t_new), (0, 0))).reshape(rows_s, D_MODEL)
    s1, s1b, w1g, w1u, w1d = _ffn(xs, w_ffn1_gu[0], w_ffn1_gu[0], w_ffn1_down[0], ln1_g, ln1_b,
                                  tm=rows_s, tf=512, up_blk0=n_ff_blocks, emit_bf16=True)
    ps_all, win_b = _matmul(s1b, w_in[0], tm=rows_s, tn=1024, out_dtype=F32, name="in_proj_s", emit_bf16=True)

    def tiles(state):
        win = state.shape[2]
        per_tile = 4 if win <= DIL_GROUPS[1][0] else DIL_GROUPS[2][1]
        return state.reshape(nb, win // per_tile, per_tile * HEADS, HEAD_DIM)

    k_states = [tiles(s) for s in (state_k_w128, state_k_w512, state_k_w2048)]
    v_states = [tiles(s) for s in (state_v_w128, state_v_w512, state_v_w2048)]
    h1, h1b, *shifted = _ffn(x_prompt[0], w1g, w1u, w1d, ln1_g, ln1_b, tm=1024, tf=512,
                             windows=k_states + v_states)
    shifted = [shifted[g + part * N_GROUPS] for g in range(N_GROUPS) for part in (0, 1)]
    p_all, qkv = _in_proj(h1b, win_b, tm=1024, tn=1024)
    mem_kv = _matmul(mem_prompt[0].astype(BF16), w_mem_kv[0], tm=MEM_TOKENS, tn=512, out_dtype=F32, name="mem_kv")
    o_attn = _band_attention(qkv, slopes_band)
    o_mem = _mem_attention(p_all, mem_kv, tq=512)
    o_conv, conv_p = _conv(p_all[None], jnp.zeros((1, 2, CONV_W), F32), cw, tm=512, t_last=512)
    h2, _ = _merge(o_conv[0], o_attn, o_mem, p_all, h1, wc_b, wa_b, wm_b, wo_b, ln2_g, ln2_b, tm=256)

    prompt_states = []
    for g, (win, _) in enumerate(DIL_GROUPS):
        for tensor in (1, 2):
            c0 = tensor * N_GROUPS * GROUP_W + g * GROUP_W
            prompt_states.append(qkv[seq - win:, c0:c0 + GROUP_W].reshape(1, 1, win, HEADS, HEAD_DIM))
    mem_k_p = mem_kv[:, :GROUP_W].reshape(1, 1, MEM_TOKENS, HEADS, HEAD_DIM)
    mem_v_p = mem_kv[:, GROUP_W:].reshape(1, 1, MEM_TOKENS, HEADS, HEAD_DIM)

    ps3 = ps_all.reshape(nb, SAMPLE_PAD, IN_W)

    real = ps3[:, :t_new]
    heads = lambda a, n: a.reshape(nb, t_new, n, HEADS, HEAD_DIM)
    q, k, v = (heads(real[..., c:c + N_GROUPS * GROUP_W], N_GROUPS) for c in (COL_Q, COL_K, COL_VA))
    rows_th = lambda a: a.transpose(0, 2, 1, 3, 4).reshape(nb, a.shape[2], ROWS16, HEAD_DIM)
    over_quad = lambda a: jnp.broadcast_to(a[:, :, None], (nb, t_new, ROWS16 // HEADS, HEADS, HEAD_DIM)
                                           ).reshape(nb, t_new, ROWS16, HEAD_DIM)
    qd = rows_th(q[:, :, 1:])
    qm = real[..., COL_QM:COL_QM + GROUP_W].reshape(nb, t_new, HEADS, HEAD_DIM)
    qt = jnp.stack([over_quad(q[:, :, 0]), over_quad(qm)], axis=1)
    kn, vn = rows_th(k), rows_th(v)

    outs = _sample_attention(qd, qt, kn, vn, slopes16, k_states, v_states, shifted,
                             tiles(cache_mem_k), tiles(cache_mem_v), t_new=t_new)
    pad_rows = lambda a, w: jnp.pad(a.reshape(nb, t_new, w), ((0, 0), (0, SAMPLE_PAD - t_new), (0, 0))
                                    ).reshape(rows_s, w).astype(BF16)
    o_attn_s, o_mem_s = pad_rows(outs[0], GROUP_W), pad_rows(outs[1], GROUP_W)
    sample_states = [s.reshape(1, nb, -1, HEADS, HEAD_DIM) for s in outs[2:]]

    o_conv_s, conv_s = _conv(ps3, state_conv[0], cw, tm=SAMPLE_PAD, t_last=t_new)
    s2, _ = _merge(o_conv_s.reshape(rows_s, CONV_W), o_attn_s, o_mem_s, ps_all, s1,
                   wc_b, wa_b, wm_b, wo_b, ln2_g, ln2_b, tm=rows_s)
    y_s, w2g, w2u, w2d = _ffn(s2, w_ffn2_gu[0], w_ffn2_gu[0], w_ffn2_down[0], ln3_g, ln3_b,
                              tm=rows_s, tf=512, up_blk0=n_ff_blocks, emit_bf16=True, bf16_out=False)
    y_sample = y_s.reshape(nb, SAMPLE_PAD, D_MODEL)[:, :t_new]
    (y_prompt,) = _ffn(h2, w2g, w2u, w2d, ln3_g, ln3_b, tm=1024, tf=512, bf16_out=False, out_buffers=2)

    return (y_prompt[None], y_sample, conv_p[None], *prompt_states, mem_k_p, mem_v_p,
            conv_s[None], *sample_states)
```

```python
import functools

import jax
import jax.numpy as jnp
import numpy as np
from jax import lax
from jax.experimental import pallas as pl
from jax.experimental.pallas import tpu as pltpu

F32 = jnp.float32
BF16 = jnp.bfloat16

D_MODEL = 2048
D_FF = 5632
HEAD_DIM = 128
HEADS = 4
GROUP_W = HEADS * HEAD_DIM
DIL_GROUPS = ((128, 1), (512, 4), (2048, 16))
N_GROUPS = len(DIL_GROUPS)
BAND = 128
CHUNK = BAND * DIL_GROUPS[-1][1]
CONV_W = 1024
MEM_TOKENS = 256
IN_W = 14336
COL_B, COL_C, COL_V = 0, 1024, 2048
COL_Q, COL_K, COL_VA = 3072, 4608, 6144
COL_QM = 7680
COL_GC, COL_GA, COL_GM = 8192, 10240, 12288
QKV_W = COL_QM - COL_Q
ALPHA = 2.0 ** 0.25
LN_EPS = 1e-5
NEG_INF = -1e30
SCALE = HEAD_DIM ** -0.5
SLOPES = np.exp2(np.float32(-8.0) * np.arange(1, 13, dtype=np.float32) / np.float32(12)).reshape(3, 4)

SAMPLE_PAD = 8
ROWS16 = 16
VMEM_LIMIT = 56 * 1024 * 1024


def _params(n_axes, vmem=VMEM_LIMIT):
    return pltpu.CompilerParams(dimension_semantics=("arbitrary",) * n_axes, vmem_limit_bytes=vmem)


def _layer_norm_rows(z, g, b):
    mu = jnp.mean(z, axis=-1, keepdims=True)
    zc = z - mu
    var = jnp.mean(zc * zc, axis=-1, keepdims=True)
    return zc * lax.rsqrt(var + LN_EPS) * g + b


def _window_loads(b, src_refs, stage_refs, slot, sems):
    return [pltpu.make_async_copy(src.at[b], stage.at[slot], sems.at[slot, n])
            for n, (src, stage) in enumerate(zip(src_refs, stage_refs))]


def _window_stores(b, stage_refs, dst_refs, slot, sems):
    copies = []
    for stage, dst in zip(stage_refs, dst_refs):
        tiles, rows = dst.shape[1], dst.shape[2]
        keep = rows - ROWS16
        if keep:
            copies.append((stage.at[slot, :, pl.ds(ROWS16, keep), :], dst.at[b, :, pl.ds(0, keep), :]))
        copies.append((stage.at[slot, pl.ds(1, tiles - 1), pl.ds(0, ROWS16), :],
                       dst.at[b, pl.ds(0, tiles - 1), pl.ds(keep, ROWS16), :]))
        copies.append((stage.at[slot, pl.ds(tiles - 1, 1), pl.ds(keep, ROWS16), :],
                       dst.at[b, pl.ds(tiles - 1, 1), pl.ds(keep, ROWS16), :]))
    return [pltpu.make_async_copy(s, d, sems.at[slot, n]) for n, (s, d) in enumerate(copies)]


def _ffn_kernel(*refs, n_ff, row_chunk, emit_bf16, bf16_out, n_windows, n_requests):
    x_ref, wg_ref, wu_ref, wd_ref, g_ref, b_ref = refs[:6]
    win_src = refs[6:6 + n_windows]
    y_ref = refs[6 + n_windows]
    yb_ref = refs[7 + n_windows] if bf16_out else None
    rest = refs[7 + bf16_out + n_windows:]
    if emit_bf16:
        wb_refs, rest = rest[:3], rest[3:]
    win_dst, rest = rest[:n_windows], rest[n_windows:]
    xb_ref = rest[0]
    j = pl.program_id(1)

    if n_windows:
        half = n_windows // 2
        stage_refs, load_sems, store_sems = rest[1:1 + half], rest[1 + half], rest[2 + half]
        step = pl.program_id(0) * n_ff + j
        n_items = 2 * n_requests
        for part in (0, 1):
            srcs, dsts = win_src[part * half:(part + 1) * half], win_dst[part * half:(part + 1) * half]
            mine = step % 2 == part

            @pl.when(mine & (step >= 2) & (step < n_items + 2))
            def _():
                for copy in _window_stores((step - 2) // 2, stage_refs, dsts, part, store_sems):
                    copy.wait()

            @pl.when(mine & (step < n_items))
            def _():
                for copy in _window_loads(step // 2, srcs, stage_refs, part, load_sems):
                    copy.start()

            @pl.when(jnp.logical_not(mine) & (step >= 1) & (step < n_items + 1))
            def _():
                for copy in _window_loads((step - 1) // 2, srcs, stage_refs, part, load_sems):
                    copy.wait()
                for copy in _window_stores((step - 1) // 2, stage_refs, dsts, part, store_sems):
                    copy.start()

    @pl.when(j == 0)
    def _():
        xb_ref[...] = x_ref[...].astype(BF16)
        y_ref[...] = jnp.zeros(y_ref.shape, F32)

    wg, wu, wd = wg_ref[...], wu_ref[...], wd_ref[...]
    if emit_bf16:
        wg, wu, wd = wg.astype(BF16), wu.astype(BF16), wd.astype(BF16)
        wb_refs[0][...], wb_refs[1][...], wb_refs[2][...] = wg, wu, wd
    xb = xb_ref[...]
    gate = jnp.dot(xb, wg, preferred_element_type=F32)
    up = jnp.dot(xb, wu, preferred_element_type=F32)
    act = (gate * jax.nn.sigmoid(gate) * up).astype(BF16)
    y_ref[...] += jnp.dot(act, wd, preferred_element_type=F32)

    @pl.when(j == n_ff - 1)
    def _():
        def chunk(c, carry):
            rows = pl.ds(pl.multiple_of(c * row_chunk, row_chunk), row_chunk)
            z = ALPHA * x_ref[rows, :] + 0.5 * y_ref[rows, :]
            y = _layer_norm_rows(z, g_ref[...], b_ref[...])
            y_ref[rows, :] = y
            if bf16_out:
                yb_ref[rows, :] = y.astype(BF16)
            return carry
        lax.fori_loop(0, x_ref.shape[0] // row_chunk, chunk, 0)


def _ffn(x, w_gate, w_up, w_down, g, b, *, tm, tf, up_blk0=0, emit_bf16=False, bf16_out=True,
         row_buffers=1, windows=()):
    rows = x.shape[0]
    n_ff = D_FF // tf
    n_steps = (rows // tm) * n_ff
    n_requests = windows[0].shape[0] if windows else 0
    assert not emit_bf16 or rows == tm
    assert n_steps >= 2 * n_requests + 2
    row_mode = pl.Buffered(1) if (rows == tm or row_buffers == 1) else None
    hbm = pl.BlockSpec(memory_space=pl.ANY)
    w_col = pl.BlockSpec((D_MODEL, tf), lambda i, j: (0, j))
    w_col_tile = pl.BlockSpec((None, D_MODEL, tf), lambda i, j: (j, 0, 0))
    tiled = w_gate.ndim == 3
    w_row = pl.BlockSpec((tf, D_MODEL), lambda i, j: (j, 0))
    row_tile = pl.BlockSpec((tm, D_MODEL), lambda i, j: (i, 0), pipeline_mode=row_mode)
    out_specs = [row_tile] * (1 + bf16_out)
    out_shape = [jax.ShapeDtypeStruct((rows, D_MODEL), F32), jax.ShapeDtypeStruct((rows, D_MODEL), BF16)][:1 + bf16_out]
    scratch = [pltpu.VMEM((tm, D_MODEL), BF16)]
    if emit_bf16:
        out_specs += [w_col_tile, w_col_tile, w_row]
        out_shape += [jax.ShapeDtypeStruct((n_ff, D_MODEL, tf), BF16)] * 2 + [jax.ShapeDtypeStruct((D_FF, D_MODEL), BF16)]
    if windows:
        out_specs += [hbm] * len(windows)
        out_shape += [jax.ShapeDtypeStruct(w.shape, w.dtype) for w in windows]
        half = windows[:len(windows) // 2]
        scratch += [pltpu.VMEM((2,) + w.shape[1:], w.dtype) for w in half]
        scratch += [pltpu.SemaphoreType.DMA((2, len(half))),
                    pltpu.SemaphoreType.DMA((2, sum(2 + (w.shape[2] > ROWS16) for w in half)))]
    return pl.pallas_call(
        functools.partial(_ffn_kernel, n_ff=n_ff, row_chunk=min(tm, 128), emit_bf16=emit_bf16, bf16_out=bf16_out,
                          n_windows=len(windows), n_requests=n_requests),
        grid=(rows // tm, n_ff),
        in_specs=[
            row_tile,
            w_col_tile if tiled else w_col,
            w_col_tile if tiled else pl.BlockSpec((D_MODEL, tf), lambda i, j: (0, j + up_blk0)),
            w_row,
            pl.BlockSpec((1, D_MODEL), lambda i, j: (0, 0)),
            pl.BlockSpec((1, D_MODEL), lambda i, j: (0, 0)),
        ] + [hbm] * len(windows),
        out_specs=out_specs,
        out_shape=out_shape,
        scratch_shapes=scratch,
        compiler_params=_params(2),
        name="ffn_cast" if emit_bf16 else "ffn",
    )(x, w_gate, w_up, w_down, g, b, *windows)


def _mm_kernel(x_ref, w_ref, o_ref, *wb_ref):
    w = w_ref[...].astype(BF16)
    if wb_ref:
        wb_ref[0][...] = w
    o_ref[...] = jnp.dot(x_ref[...], w, preferred_element_type=F32).astype(o_ref.dtype)


def _matmul(x, w, *, tm, tn, out_dtype, name, emit_bf16=False):
    rows, k = x.shape
    n_cols = w.shape[1]
    assert not emit_bf16 or rows == tm
    w_spec = pl.BlockSpec((k, tn), lambda i, j: (0, j))
    out_specs = [pl.BlockSpec((tm, tn), lambda i, j: (i, j))]
    out_shape = [jax.ShapeDtypeStruct((rows, n_cols), out_dtype)]
    if emit_bf16:
        out_specs.append(pl.BlockSpec((None, k, tn), lambda i, j: (j, 0, 0)))
        out_shape.append(jax.ShapeDtypeStruct((n_cols // tn, k, tn), BF16))
    out = pl.pallas_call(
        _mm_kernel,
        grid=(rows // tm, n_cols // tn),
        in_specs=[pl.BlockSpec((tm, k), lambda i, j: (i, 0)), w_spec],
        out_specs=out_specs,
        out_shape=out_shape,
        compiler_params=_params(2),
        name=name,
    )(x, w)
    return out if emit_bf16 else out[0]


def _in_proj_kernel(x_ref, w_ref, p_ref, qkv_ref, *, n_own):
    j = pl.program_id(1)
    acc = jnp.dot(x_ref[...], w_ref[...], preferred_element_type=F32)
    p_ref[...] = acc.astype(BF16)

    @pl.when(j < n_own)
    def _():
        qkv_ref[...] = acc


def _in_proj(x, w, *, tm, tn):
    rows, k = x.shape
    assert w.shape == (IN_W // tn, k, tn) and COL_Q % tn == 0 and COL_GC % tn == 0
    c_lo, n_own = COL_Q // tn, (COL_GC - COL_Q) // tn

    def col_block(j):
        return jnp.where(j < n_own, j + c_lo, jnp.where(j < n_own + c_lo, j - n_own, j))

    return pl.pallas_call(
        functools.partial(_in_proj_kernel, n_own=n_own),
        grid=(rows // tm, IN_W // tn),
        in_specs=[pl.BlockSpec((tm, k), lambda i, j: (i, 0)),
                  pl.BlockSpec((None, k, tn), lambda i, j: (col_block(j), 0, 0))],
        out_specs=[pl.BlockSpec((tm, tn), lambda i, j: (i, col_block(j))),
                   pl.BlockSpec((tm, tn), lambda i, j: (i, jnp.minimum(j, n_own - 1)))],
        out_shape=[jax.ShapeDtypeStruct((rows, IN_W), BF16),
                   jax.ShapeDtypeStruct((rows, COL_GC - COL_Q), F32)],
        compiler_params=_params(2),
        name="in_proj",
    )(x, w)


def _rows(ref, start, dil):
    if dil == 1:
        return ref[start:start + BAND, :]
    return ref[pl.ds(start, BAND, stride=dil), :]


def _band_kernel(sl_ref, q0_ref, k0_ref, v0_ref, q1_ref, k1_ref, v1_ref, q2_ref, k2_ref, v2_ref,
                 o_ref, kt0_ref, vt0_ref, kt1_ref, vt1_ref, kt2_ref, vt2_ref, og_ref, lg_ref):
    c = pl.program_id(1)
    q_refs = (q0_ref, q1_ref, q2_ref)
    k_refs = (k0_ref, k1_ref, k2_ref)
    v_refs = (v0_ref, v1_ref, v2_ref)
    kt_refs = (kt0_ref, kt1_ref, kt2_ref)
    vt_refs = (vt0_ref, vt1_ref, vt2_ref)

    @pl.when(c == 0)
    def _():
        for ref in kt_refs + vt_refs:
            ref[...] = jnp.zeros(ref.shape, F32)

    qi = lax.broadcasted_iota(jnp.int32, (BAND, 2 * BAND), 0)
    ki = lax.broadcasted_iota(jnp.int32, (BAND, 2 * BAND), 1)
    rel = BAND + qi - ki
    valid = (rel >= 0) & (rel <= BAND)
    no_prev = jnp.where(ki < BAND, jnp.where(c == 0, NEG_INF, 0.0), 0.0)

    for g, (_, dil) in enumerate(DIL_GROUPS):
        slope = sl_ref[0, g:g + 1, 0:1]
        bias = jnp.where(valid, -(slope * (rel * dil).astype(F32)), NEG_INF)
        bias_first = bias + no_prev
        span = BAND * dil
        for r in range(dil):
            for n in range(CHUNK // span):
                start = r + n * span
                q = _rows(q_refs[g], start, dil).astype(BF16)
                if n == 0:
                    k_prev, v_prev = _rows(kt_refs[g], r, dil), _rows(vt_refs[g], r, dil)
                else:
                    k_prev, v_prev = _rows(k_refs[g], start - span, dil), _rows(v_refs[g], start - span, dil)
                k = jnp.concatenate([k_prev, _rows(k_refs[g], start, dil)], axis=0).astype(BF16)
                v = jnp.concatenate([v_prev, _rows(v_refs[g], start, dil)], axis=0).astype(BF16)
                s = lax.dot_general(q, k, (((1,), (1,)), ((), ())), preferred_element_type=F32) * SCALE
                s = s + (bias_first if n == 0 else bias)
                m = jnp.max(s, axis=-1, keepdims=True)
                p = jnp.exp(s - m)
                l = jnp.sum(p, axis=-1, keepdims=True)
                o = jnp.dot(p.astype(BF16), v, preferred_element_type=F32) / l
                lse = jnp.broadcast_to(m + jnp.log(l), (BAND, HEAD_DIM))
                if dil == 1:
                    og_ref[g, start:start + BAND, :] = o
                    lg_ref[g, start:start + BAND, :] = lse
                else:
                    og_ref[g, pl.ds(start, BAND, stride=dil), :] = o
                    lg_ref[g, pl.ds(start, BAND, stride=dil), :] = lse
        kt_refs[g][...] = k_refs[g][CHUNK - span:CHUNK, :]
        vt_refs[g][...] = v_refs[g][CHUNK - span:CHUNK, :]

    rows_per_step = 256

    def combine(i, carry):
        rows = pl.ds(pl.multiple_of(i * rows_per_step, rows_per_step), rows_per_step)
        l0, l1, l2 = lg_ref[0, rows, :], lg_ref[1, rows, :], lg_ref[2, rows, :]
        m = jnp.maximum(jnp.maximum(l0, l1), l2)
        e0, e1, e2 = jnp.exp(l0 - m), jnp.exp(l1 - m), jnp.exp(l2 - m)
        o = (e0 * og_ref[0, rows, :] + e1 * og_ref[1, rows, :] + e2 * og_ref[2, rows, :]) / (e0 + e1 + e2)
        o_ref[rows, :] = o.astype(o_ref.dtype)
        return carry

    lax.fori_loop(0, CHUNK // rows_per_step, combine, 0)


def _band_attention(qkv, slopes):
    seq = qkv.shape[0]
    blk = (CHUNK, HEAD_DIM)
    per_tensor = N_GROUPS * HEADS
    in_specs = [pl.BlockSpec((1, 8, HEAD_DIM), lambda h, c: (h, 0, 0))]
    for g in range(N_GROUPS):
        for tensor in range(3):
            in_specs.append(pl.BlockSpec(
                blk, lambda h, c, g=g, tensor=tensor: (c, tensor * per_tensor + g * HEADS + h)))
    tails = []
    for _, dil in DIL_GROUPS:
        tails += [pltpu.VMEM((BAND * dil, HEAD_DIM), F32)] * 2
    return pl.pallas_call(
        _band_kernel,
        grid=(HEADS, seq // CHUNK),
        in_specs=in_specs,
        out_specs=pl.BlockSpec(blk, lambda h, c: (c, h)),
        out_shape=jax.ShapeDtypeStruct((seq, GROUP_W), BF16),
        scratch_shapes=tails + [pltpu.VMEM((N_GROUPS, CHUNK, HEAD_DIM), F32),
                                pltpu.VMEM((N_GROUPS, CHUNK, HEAD_DIM), F32)],
        compiler_params=_params(2),
        name="band_attn",
    )(slopes, *([qkv] * 9))


def _mem_attn_kernel(q_ref, mk_ref, mv_ref, o_ref):
    for h in range(HEADS):
        hs = slice(h * HEAD_DIM, (h + 1) * HEAD_DIM)
        q = q_ref[:, hs]
        k = mk_ref[:, hs].astype(BF16)
        v = mv_ref[:, hs].astype(BF16)
        s = lax.dot_general(q, k, (((1,), (1,)), ((), ())), preferred_element_type=F32) * SCALE
        m = jnp.max(s, axis=-1, keepdims=True)
        p = jnp.exp(s - m)
        l = jnp.sum(p, axis=-1, keepdims=True)
        o = jnp.dot(p.astype(BF16), v, preferred_element_type=F32) / l
        o_ref[:, hs] = o.astype(o_ref.dtype)


def _mem_attention(p_all, mem_kv, *, tq):
    seq = p_all.shape[0]
    return pl.pallas_call(
        _mem_attn_kernel,
        grid=(seq // tq,),
        in_specs=[pl.BlockSpec((tq, GROUP_W), lambda i: (i, COL_QM // GROUP_W)),
                  pl.BlockSpec((MEM_TOKENS, GROUP_W), lambda i: (0, 0)),
                  pl.BlockSpec((MEM_TOKENS, GROUP_W), lambda i: (0, 1))],
        out_specs=pl.BlockSpec((tq, GROUP_W), lambda i: (i, 0)),
        out_shape=jax.ShapeDtypeStruct((seq, GROUP_W), BF16),
        compiler_params=_params(1),
        name="mem_attn",
    )(p_all, mem_kv, mem_kv)


def _conv_rows(b, c, v, w_ref, carry_ref):
    u = c.astype(F32) * v.astype(F32)
    c2 = carry_ref[0:1, :]
    c1 = carry_ref[1:2, :]
    row = lax.broadcasted_iota(jnp.int32, u.shape, 0)
    u1 = jnp.where(row == 0, c1, pltpu.roll(u, 1, 0))
    u2 = jnp.where(row == 0, c2, jnp.where(row == 1, c1, pltpu.roll(u, 2, 0)))
    return b.astype(F32) * (w_ref[0:1, :] * u2 + w_ref[1:2, :] * u1 + w_ref[2:3, :] * u), u


def _conv_kernel(b_ref, c_ref, v_ref, st_ref, w_ref, o_ref, ns_ref, carry_ref, *, t_last):
    i = pl.program_id(1)

    @pl.when(i == 0)
    def _():
        carry_ref[0:2, :] = st_ref[0]

    o, u = _conv_rows(b_ref[0], c_ref[0], v_ref[0], w_ref, carry_ref)
    o_ref[0] = o.astype(o_ref.dtype)
    last2 = u[t_last - 2:t_last, :]
    carry_ref[0:2, :] = last2
    ns_ref[0] = last2


def _conv(p3, state, conv_w, *, tm, t_last):
    nb, t = p3.shape[0], p3.shape[1]
    blk = (1, tm, CONV_W)
    return pl.pallas_call(
        functools.partial(_conv_kernel, t_last=t_last),
        grid=(nb, t // tm),
        in_specs=[
            pl.BlockSpec(blk, lambda b, i: (b, i, COL_B // CONV_W)),
            pl.BlockSpec(blk, lambda b, i: (b, i, COL_C // CONV_W)),
            pl.BlockSpec(blk, lambda b, i: (b, i, COL_V // CONV_W)),
            pl.BlockSpec((1, 2, CONV_W), lambda b, i: (b, 0, 0)),
            pl.BlockSpec((3, CONV_W), lambda b, i: (0, 0)),
        ],
        out_specs=[pl.BlockSpec(blk, lambda b, i: (b, i, 0)),
                   pl.BlockSpec((1, 2, CONV_W), lambda b, i: (b, 0, 0))],
        out_shape=[jax.ShapeDtypeStruct((nb, t, CONV_W), BF16),
                   jax.ShapeDtypeStruct((nb, 2, CONV_W), F32)],
        scratch_shapes=[pltpu.VMEM((8, CONV_W), F32)],
        compiler_params=_params(2),
        name="conv",
    )(p3, p3, p3, state, conv_w)


def _merge_kernel(*refs, sub_rows, fuse_conv):
    if fuse_conv:
        b_ref, c_ref, v_ref, cw_ref = refs[:4]
        refs = refs[4:]
    else:
        oc_ref = refs[0]
        refs = refs[1:]
    oa_ref, om_ref, gc_ref, ga_ref, gm_ref, h_ref, wc_ref, wa_ref, wm_ref, wo_ref, g_ref, beta_ref, y_ref = refs[:13]
    if fuse_conv:
        ns_ref, carry_ref = refs[13:]

        @pl.when(pl.program_id(0) == 0)
        def _():
            carry_ref[...] = jnp.zeros(carry_ref.shape, F32)

    for r0 in range(0, h_ref.shape[0], sub_rows):
        rows = slice(r0, r0 + sub_rows)
        if fuse_conv:
            oc, u = _conv_rows(b_ref[rows, :], c_ref[rows, :], v_ref[rows, :], cw_ref, carry_ref)
            oc = oc.astype(BF16)
            carry_ref[0:2, :] = u[sub_rows - 2:sub_rows, :]
        else:
            oc = oc_ref[rows, :]
        mc = jnp.dot(oc, wc_ref[...], preferred_element_type=F32)
        ma = jnp.dot(oa_ref[rows, :], wa_ref[...], preferred_element_type=F32)
        mm = jnp.dot(om_ref[rows, :], wm_ref[...], preferred_element_type=F32)
        mix = (jax.nn.sigmoid(gc_ref[rows, :].astype(F32)) * mc
               + jax.nn.sigmoid(ga_ref[rows, :].astype(F32)) * ma
               + jax.nn.sigmoid(gm_ref[rows, :].astype(F32)) * mm)
        z = ALPHA * h_ref[rows, :] + jnp.dot(mix.astype(BF16), wo_ref[...], preferred_element_type=F32)
        y_ref[rows, :] = _layer_norm_rows(z, g_ref[...], beta_ref[...])
    if fuse_conv:
        ns_ref[...] = carry_ref[0:2, :]


def _merge(o_conv, o_attn, o_mem, p_all, h, w_c, w_a, w_m, w_o, g, b, *, tm, conv_w=None):
    rows = h.shape[0]
    fuse_conv = o_conv is None
    single = pl.Buffered(1)
    row_blk = lambda w: pl.BlockSpec((tm, w), lambda i: (i, 0))
    col_blk = lambda w, col: pl.BlockSpec((tm, w), lambda i: (i, col // w))
    whole = lambda a: pl.BlockSpec(a.shape, lambda i: (0, 0), pipeline_mode=single)
    if fuse_conv:
        conv_specs = [col_blk(CONV_W, COL_B), col_blk(CONV_W, COL_C), col_blk(CONV_W, COL_V), whole(conv_w)]
        conv_args = [p_all, p_all, p_all, conv_w]
    else:
        conv_specs, conv_args = [row_blk(CONV_W)], [o_conv]
    out = pl.pallas_call(
        functools.partial(_merge_kernel, sub_rows=min(tm, 256), fuse_conv=fuse_conv),
        grid=(rows // tm,),
        in_specs=conv_specs + [row_blk(GROUP_W), row_blk(GROUP_W),
                               col_blk(D_MODEL, COL_GC), col_blk(D_MODEL, COL_GA), col_blk(D_MODEL, COL_GM),
                               row_blk(D_MODEL),
                               whole(w_c), whole(w_a), whole(w_m), whole(w_o), whole(g), whole(b)],
        out_specs=[row_blk(D_MODEL)] + ([pl.BlockSpec((2, CONV_W), lambda i: (0, 0))] if fuse_conv else []),
        out_shape=[jax.ShapeDtypeStruct((rows, D_MODEL), F32)]
                  + ([jax.ShapeDtypeStruct((2, CONV_W), F32)] if fuse_conv else []),
        scratch_shapes=[pltpu.VMEM((8, CONV_W), F32)] if fuse_conv else [],
        compiler_params=_params(1),
        name="merge",
    )(*conv_args, o_attn, o_mem, p_all, p_all, p_all, h, w_c, w_a, w_m, w_o, g, b)
    return out if fuse_conv else out[0]


def _over_quad(x, op):
    x = op(x, pltpu.roll(x, HEADS, 0))
    return op(x, pltpu.roll(x, 2 * HEADS, 0))


def _lane_dot(k3, q):
    return jnp.sum(k3 * q, axis=-1, keepdims=True) * SCALE


def _quad_attention(k3, v3, qt, bias3, k_new, v_new, bias_new):
    s = _lane_dot(k3, qt[None]) + bias3
    m = jnp.max(s, axis=0)
    if k_new is not None:
        s_new = _lane_dot(k_new, qt) + bias_new
        m = jnp.maximum(m, s_new)
    m = _over_quad(m, jnp.maximum)
    e = jnp.exp(s - m[None])
    l = jnp.sum(e, axis=0)
    acc = jnp.sum(e * v3, axis=0)
    if k_new is not None:
        e_new = jnp.exp(s_new - m)
        l = l + e_new
        acc = acc + e_new * v_new
    l = _over_quad(l, jnp.add)
    acc = _over_quad(acc, jnp.add)
    return acc / l, m + jnp.log(l)


def _sample_kernel(qd_ref, qt_ref, kn_ref, vn_ref, sl_ref,
                   k0_ref, v0_ref, k1_ref, v1_ref, k2_ref, v2_ref, mk_ref, mv_ref, *new_refs, t_new):
    k_refs = (k0_ref, k1_ref, k2_ref)
    v_refs = (v0_ref, v1_ref, v2_ref)
    oa_ref, om_ref = new_refs[N_GROUPS * 2], new_refs[N_GROUPS * 2 + 1]
    quad16 = lax.broadcasted_iota(jnp.int32, (ROWS16, HEAD_DIM), 0) >> 2

    outs, lses = [], []

    win0 = DIL_GROUPS[0][0]
    a_idx = lax.broadcasted_iota(jnp.int32, (win0 // 4, ROWS16, HEAD_DIM), 0)
    b_idx = lax.broadcasted_iota(jnp.int32, (win0 // 4, ROWS16, HEAD_DIM), 1) >> 2
    tok = a_idx * 4 + b_idx
    slope0 = sl_ref[0]
    k3, v3 = k0_ref[0], v0_ref[0]
    o0 = jnp.zeros((ROWS16, HEAD_DIM), F32)
    l0 = jnp.zeros((ROWS16, HEAD_DIM), F32)
    for t in range(t_new):
        bias3 = jnp.where(tok >= t, -(slope0[None] * (win0 + t - tok).astype(F32)), NEG_INF)
        bias_new = jnp.where(quad16 <= t, -(slope0 * (t - quad16).astype(F32)), NEG_INF)
        o_t, l_t = _quad_attention(k3, v3, qt_ref[0, 0, t], bias3, kn_ref[0, 0], vn_ref[0, 0], bias_new)
        o0 = jnp.where(quad16 == t, o_t, o0)
        l0 = jnp.where(quad16 == t, l_t, l0)
    outs.append(o0)
    lses.append(l0)

    for g in (1, 2):
        win, dil = DIL_GROUPS[g]
        q16 = qd_ref[0, g - 1]
        k3 = k_refs[g][0, :, 0:ROWS16, :]
        v3 = v_refs[g][0, :, 0:ROWS16, :]
        i_idx = lax.broadcasted_iota(jnp.int32, (BAND, ROWS16, HEAD_DIM), 0)
        bias3 = -(sl_ref[g][None] * (win - i_idx * dil).astype(F32))
        s = _lane_dot(k3, q16[None]) + bias3
        s_new = _lane_dot(kn_ref[0, g], q16)
        m = jnp.maximum(jnp.max(s, axis=0), s_new)
        e = jnp.exp(s - m[None])
        e_new = jnp.exp(s_new - m)
        l = jnp.sum(e, axis=0) + e_new
        acc = jnp.sum(e * v3, axis=0) + e_new * vn_ref[0, g]
        outs.append(acc / l)
        lses.append(m + jnp.log(l))

    m = jnp.maximum(jnp.maximum(lses[0], lses[1]), lses[2])
    w = [jnp.exp(x - m) for x in lses]
    oa_ref[0] = (w[0] * outs[0] + w[1] * outs[1] + w[2] * outs[2]) / (w[0] + w[1] + w[2])

    k3, v3 = mk_ref[0], mv_ref[0]
    om = jnp.zeros((ROWS16, HEAD_DIM), F32)
    for t in range(t_new):
        o_t, _ = _quad_attention(k3, v3, qt_ref[0, 1, t], jnp.zeros((1, ROWS16, HEAD_DIM), F32), None, None, None)
        om = jnp.where(quad16 == t, o_t, om)
    om_ref[0] = om

    for i, out_ref in enumerate(new_refs[N_GROUPS * 2 + 2:]):
        out_ref[0, 0] = (kn_ref, vn_ref)[i % 2][0, i // 2]


def _sample_attention(qd, qt, kn, vn, slopes16, k_states, v_states, shifted, mem_k, mem_v, *, t_new):
    nb = qd.shape[0]
    states = []
    for g in range(N_GROUPS):
        states += [k_states[g], v_states[g]]
    whole = lambda a: pl.BlockSpec((1,) + a.shape[1:], lambda b: (b,) + (0,) * (a.ndim - 1))
    head_rows = lambda a: pl.BlockSpec((1, a.shape[1], ROWS16, HEAD_DIM), lambda b: (b, 0, 0, 0))
    last_rows = lambda a: pl.BlockSpec((1, 1, ROWS16, HEAD_DIM),
                                       lambda b, a=a: (b, a.shape[1] - 1, a.shape[2] // ROWS16 - 1, 0))
    o_spec = pl.BlockSpec((1, ROWS16, HEAD_DIM), lambda b: (b, 0, 0))
    o_shape = jax.ShapeDtypeStruct((nb, ROWS16, HEAD_DIM), F32)
    n_in = 5 + len(states) + 2
    return pl.pallas_call(
        functools.partial(_sample_kernel, t_new=t_new),
        grid=(nb,),
        in_specs=[whole(qd), whole(qt), whole(kn), whole(vn),
                  pl.BlockSpec(slopes16.shape, lambda b: (0, 0, 0))]
                 + [head_rows(s) for s in states] + [whole(mem_k), whole(mem_v)]
                 + [pl.BlockSpec(memory_space=pl.ANY)] * len(shifted),
        out_specs=[o_spec, o_spec] + [last_rows(s) for s in shifted],
        out_shape=[o_shape, o_shape] + [jax.ShapeDtypeStruct(s.shape, F32) for s in shifted],
        input_output_aliases={n_in + i: 2 + i for i in range(len(shifted))},
        compiler_params=_params(1),
        name="sample_attn",
    )(qd, qt, kn, vn, slopes16, *states, mem_k, mem_v, *shifted)


def kernel(x_prompt, x_sample, mem_prompt, state_conv, state_k_w128, state_v_w128, state_k_w512, state_v_w512, state_k_w2048, state_v_w2048, cache_mem_k, cache_mem_v, ln1_g, ln1_b, w_ffn1_gu, w_ffn1_down, w_in, conv_w, w_mem_kv, w_br_conv, w_br_attn, w_br_mem, w_o, ln2_g, ln2_b, w_ffn2_gu, w_ffn2_down, ln3_g, ln3_b):
    depth = w_in.shape[0]
    seq = x_prompt.shape[1]
    nb, t_new = x_sample.shape[0], x_sample.shape[1]
    assert depth == 1 and x_prompt.shape[0] == 1 and seq % CHUNK == 0
    assert t_new * HEADS == ROWS16 and t_new <= DIL_GROUPS[1][1]

    wb = lambda w: w[0].astype(BF16)
    wc_b, wa_b, wm_b, wo_b = wb(w_br_conv), wb(w_br_attn), wb(w_br_mem), wb(w_o)
    n_ff_blocks = D_FF // 512
    cw = conv_w[0]
    slopes_lane = np.broadcast_to(SLOPES[:, :, None], (N_GROUPS, HEADS, HEAD_DIM))
    slopes_band = jnp.asarray(np.pad(slopes_lane.transpose(1, 0, 2), ((0, 0), (0, 8 - N_GROUPS), (0, 0))))
    slopes16 = jnp.asarray(np.tile(slopes_lane, (1, ROWS16 // HEADS, 1)))

    rows_s = nb * SAMPLE_PAD
    xs = jnp.pad(x_sample, ((0, 0), (0, SAMPLE_PAD - t_new), (0, 0))).reshape(rows_s, D_MODEL)
    s1, s1b, w1g, w1u, w1d = _ffn(xs, w_ffn1_gu[0], w_ffn1_gu[0], w_ffn1_down[0], ln1_g, ln1_b,
                                  tm=rows_s, tf=512, up_blk0=n_ff_blocks, emit_bf16=True)
    ps_all, win_b = _matmul(s1b, w_in[0], tm=rows_s, tn=1024, out_dtype=F32, name="in_proj_s", emit_bf16=True)

    def tiles(state):
        win = state.shape[2]
        per_tile = 4 if win <= DIL_GROUPS[1][0] else DIL_GROUPS[2][1]
        return state.reshape(nb, win // per_tile, per_tile * HEADS, HEAD_DIM)

    k_states = [tiles(s) for s in (state_k_w128, state_k_w512, state_k_w2048)]
    v_states = [tiles(s) for s in (state_v_w128, state_v_w512, state_v_w2048)]
    h1, h1b, *shifted = _ffn(x_prompt[0], w1g, w1u, w1d, ln1_g, ln1_b, tm=1024, tf=512,
                             windows=k_states + v_states)
    shifted = [shifted[g + part * N_GROUPS] for g in range(N_GROUPS) for part in (0, 1)]
    p_all, qkv = _in_proj(h1b, win_b, tm=1024, tn=1024)
    mem_kv = _matmul(mem_prompt[0].astype(BF16), w_mem_kv[0], tm=MEM_TOKENS, tn=512, out_dtype=F32, name="mem_kv")
    o_attn = _band_attention(qkv, slopes_band)
    o_mem = _mem_attention(p_all, mem_kv, tq=512)
    h2, conv_p = _merge(None, o_attn, o_mem, p_all, h1, wc_b, wa_b, wm_b, wo_b, ln2_g, ln2_b, tm=256, conv_w=cw)

    prompt_states = []
    for g, (win, _) in enumerate(DIL_GROUPS):
        for tensor in (1, 2):
            c0 = tensor * N_GROUPS * GROUP_W + g * GROUP_W
            prompt_states.append(qkv[seq - win:, c0:c0 + GROUP_W].reshape(1, 1, win, HEADS, HEAD_DIM))
    mem_k_p = mem_kv[:, :GROUP_W].reshape(1, 1, MEM_TOKENS, HEADS, HEAD_DIM)
    mem_v_p = mem_kv[:, GROUP_W:].reshape(1, 1, MEM_TOKENS, HEADS, HEAD_DIM)

    ps3 = ps_all.reshape(nb, SAMPLE_PAD, IN_W)

    real = ps3[:, :t_new]
    heads = lambda a, n: a.reshape(nb, t_new, n, HEADS, HEAD_DIM)
    q, k, v = (heads(real[..., c:c + N_GROUPS * GROUP_W], N_GROUPS) for c in (COL_Q, COL_K, COL_VA))
    rows_th = lambda a: a.transpose(0, 2, 1, 3, 4).reshape(nb, a.shape[2], ROWS16, HEAD_DIM)
    over_quad = lambda a: jnp.broadcast_to(a[:, :, None], (nb, t_new, ROWS16 // HEADS, HEADS, HEAD_DIM)
                                           ).reshape(nb, t_new, ROWS16, HEAD_DIM)
    qd = rows_th(q[:, :, 1:])
    qm = real[..., COL_QM:COL_QM + GROUP_W].reshape(nb, t_new, HEADS, HEAD_DIM)
    qt = jnp.stack([over_quad(q[:, :, 0]), over_quad(qm)], axis=1)
    kn, vn = rows_th(k), rows_th(v)

    outs = _sample_attention(qd, qt, kn, vn, slopes16, k_states, v_states, shifted,
                             tiles(cache_mem_k), tiles(cache_mem_v), t_new=t_new)
    pad_rows = lambda a, w: jnp.pad(a.reshape(nb, t_new, w), ((0, 0), (0, SAMPLE_PAD - t_new), (0, 0))
                                    ).reshape(rows_s, w).astype(BF16)
    o_attn_s, o_mem_s = pad_rows(outs[0], GROUP_W), pad_rows(outs[1], GROUP_W)
    sample_states = [s.reshape(1, nb, -1, HEADS, HEAD_DIM) for s in outs[2:]]

    o_conv_s, conv_s = _conv(ps3, state_conv[0], cw, tm=SAMPLE_PAD, t_last=t_new)
    s2 = _merge(o_conv_s.reshape(rows_s, CONV_W), o_attn_s, o_mem_s, ps_all, s1,
                wc_b, wa_b, wm_b, wo_b, ln2_g, ln2_b, tm=rows_s)
    y_s, w2g, w2u, w2d = _ffn(s2, w_ffn2_gu[0], w_ffn2_gu[0], w_ffn2_down[0], ln3_g, ln3_b,
                              tm=rows_s, tf=512, up_blk0=n_ff_blocks, emit_bf16=True, bf16_out=False)
    y_sample = y_s.reshape(nb, SAMPLE_PAD, D_MODEL)[:, :t_new]
    (y_prompt,) = _ffn(h2, w2g, w2u, w2d, ln3_g, ln3_b, tm=1024, tf=512, bf16_out=False, row_buffers=2)

    return (y_prompt[None], y_sample, conv_p[None, None], *prompt_states, mem_k_p, mem_v_p,
            conv_s[None], *sample_states)
```

```python
import functools

import jax
import jax.numpy as jnp
import numpy as np
from jax import lax
from jax.experimental import pallas as pl
from jax.experimental.pallas import tpu as pltpu

F32 = jnp.float32
BF16 = jnp.bfloat16

D_MODEL = 2048
D_FF = 5632
HEAD_DIM = 128
HEADS = 4
GROUP_W = HEADS * HEAD_DIM
DIL_GROUPS = ((128, 1), (512, 4), (2048, 16))
N_GROUPS = len(DIL_GROUPS)
BAND = 128
CHUNK = BAND * DIL_GROUPS[-1][1]
CONV_W = 1024
MEM_TOKENS = 256
IN_W = 14336
COL_B, COL_C, COL_V = 0, 1024, 2048
COL_Q, COL_K, COL_VA = 3072, 4608, 6144
COL_QM = 7680
COL_GC, COL_GA, COL_GM = 8192, 10240, 12288
QKV_W = COL_QM - COL_Q
ALPHA = 2.0 ** 0.25
LN_EPS = 1e-5
NEG_INF = -1e30
SCALE = HEAD_DIM ** -0.5
SLOPES = np.exp2(np.float32(-8.0) * np.arange(1, 13, dtype=np.float32) / np.float32(12)).reshape(3, 4)

SAMPLE_PAD = 8
ROWS16 = 16
VMEM_LIMIT = 56 * 1024 * 1024


def _params(n_axes, vmem=VMEM_LIMIT):
    return pltpu.CompilerParams(dimension_semantics=("arbitrary",) * n_axes, vmem_limit_bytes=vmem)


def _layer_norm_rows(z, g, b):
    mu = jnp.mean(z, axis=-1, keepdims=True)
    zc = z - mu
    var = jnp.mean(zc * zc, axis=-1, keepdims=True)
    return zc * lax.rsqrt(var + LN_EPS) * g + b


def _window_loads(b, src_refs, stage_refs, slot, sems):
    return [pltpu.make_async_copy(src.at[b], stage.at[slot], sems.at[slot, n])
            for n, (src, stage) in enumerate(zip(src_refs, stage_refs))]


def _window_stores(b, stage_refs, dst_refs, slot, sems):
    copies = []
    for stage, dst in zip(stage_refs, dst_refs):
        tiles, rows = dst.shape[1], dst.shape[2]
        keep = rows - ROWS16
        if keep:
            copies.append((stage.at[slot, :, pl.ds(ROWS16, keep), :], dst.at[b, :, pl.ds(0, keep), :]))
        copies.append((stage.at[slot, pl.ds(1, tiles - 1), pl.ds(0, ROWS16), :],
                       dst.at[b, pl.ds(0, tiles - 1), pl.ds(keep, ROWS16), :]))
        copies.append((stage.at[slot, pl.ds(tiles - 1, 1), pl.ds(keep, ROWS16), :],
                       dst.at[b, pl.ds(tiles - 1, 1), pl.ds(keep, ROWS16), :]))
    return [pltpu.make_async_copy(s, d, sems.at[slot, n]) for n, (s, d) in enumerate(copies)]


def _ffn_kernel(*refs, n_ff, row_chunk, emit_bf16, bf16_out, prefetch_x, n_windows, n_requests):
    x_ref, wg_ref, wu_ref, wd_ref, g_ref, b_ref = refs[:6]
    win_src = refs[6:6 + n_windows]
    y_ref = refs[6 + n_windows]
    yb_ref = refs[7 + n_windows] if bf16_out else None
    rest = refs[7 + bf16_out + n_windows:]
    if emit_bf16:
        wb_refs, rest = rest[:3], rest[3:]
    win_dst, rest = rest[:n_windows], rest[n_windows:]
    xb_ref, rest = rest[0], rest[1:]
    i, j = pl.program_id(0), pl.program_id(1)
    tm = y_ref.shape[0]

    if prefetch_x:
        xbuf_ref, xsem = rest[:2]
        rest = rest[2:]

        def x_copy(tile):
            return pltpu.make_async_copy(x_ref.at[pl.ds(pl.multiple_of(tile * tm, tm), tm), :], xbuf_ref, xsem.at[0])

        @pl.when((i == 0) & (j == 0))
        def _():
            x_copy(0).start()

    if n_windows:
        half = n_windows // 2
        stage_refs, load_sems, store_sems = rest[:half], rest[half], rest[half + 1]
        step = i * n_ff + j
        n_items = 2 * n_requests
        for part in (0, 1):
            srcs, dsts = win_src[part * half:(part + 1) * half], win_dst[part * half:(part + 1) * half]
            mine = step % 2 == part

            @pl.when(mine & (step >= 2) & (step < n_items + 2))
            def _():
                for copy in _window_stores((step - 2) // 2, stage_refs, dsts, part, store_sems):
                    copy.wait()

            @pl.when(mine & (step < n_items))
            def _():
                for copy in _window_loads(step // 2, srcs, stage_refs, part, load_sems):
                    copy.start()

            @pl.when(jnp.logical_not(mine) & (step >= 1) & (step < n_items + 1))
            def _():
                for copy in _window_loads((step - 1) // 2, srcs, stage_refs, part, load_sems):
                    copy.wait()
                for copy in _window_stores((step - 1) // 2, stage_refs, dsts, part, store_sems):
                    copy.start()

    @pl.when(j == 0)
    def _():
        if prefetch_x:
            x_copy(i).wait()
        x = xbuf_ref[...] if prefetch_x else x_ref[...]
        xb_ref[...] = x.astype(BF16)
        y_ref[...] = (2.0 * ALPHA) * x

    if prefetch_x:
        @pl.when((j == 1) & (i + 1 < pl.num_programs(0)))
        def _():
            x_copy(i + 1).start()

    wg, wu, wd = wg_ref[...], wu_ref[...], wd_ref[...]
    if emit_bf16:
        wg, wu, wd = wg.astype(BF16), wu.astype(BF16), wd.astype(BF16)
        wb_refs[0][...], wb_refs[1][...], wb_refs[2][...] = wg, wu, wd
    xb = xb_ref[...]
    gate = jnp.dot(xb, wg, preferred_element_type=F32)
    up = jnp.dot(xb, wu, preferred_element_type=F32)
    act = (gate * jax.nn.sigmoid(gate) * up).astype(BF16)
    y_ref[...] += jnp.dot(act, wd, preferred_element_type=F32)

    @pl.when(j == n_ff - 1)
    def _():
        def chunk(c, carry):
            rows = pl.ds(pl.multiple_of(c * row_chunk, row_chunk), row_chunk)
            y = _layer_norm_rows(0.5 * y_ref[rows, :], g_ref[...], b_ref[...])
            y_ref[rows, :] = y
            if bf16_out:
                yb_ref[rows, :] = y.astype(BF16)
            return carry
        lax.fori_loop(0, tm // row_chunk, chunk, 0)


def _ffn(x, w_gate, w_up, w_down, g, b, *, tm, tf, up_blk0=0, emit_bf16=False, bf16_out=True,
         row_buffers=1, prefetch_x=False, windows=()):
    rows = x.shape[0]
    n_ff = D_FF // tf
    n_steps = (rows // tm) * n_ff
    n_requests = windows[0].shape[0] if windows else 0
    assert not emit_bf16 or rows == tm
    assert n_steps >= 2 * n_requests + 2
    row_mode = pl.Buffered(1) if (rows == tm or row_buffers == 1) else None
    hbm = pl.BlockSpec(memory_space=pl.ANY)
    w_col = pl.BlockSpec((D_MODEL, tf), lambda i, j: (0, j))
    w_col_tile = pl.BlockSpec((None, D_MODEL, tf), lambda i, j: (j, 0, 0))
    tiled = w_gate.ndim == 3
    w_row = pl.BlockSpec((tf, D_MODEL), lambda i, j: (j, 0))
    row_tile = pl.BlockSpec((tm, D_MODEL), lambda i, j: (i, 0), pipeline_mode=row_mode)
    out_specs = [row_tile] * (1 + bf16_out)
    out_shape = [jax.ShapeDtypeStruct((rows, D_MODEL), F32), jax.ShapeDtypeStruct((rows, D_MODEL), BF16)][:1 + bf16_out]
    scratch = [pltpu.VMEM((tm, D_MODEL), BF16)]
    if prefetch_x:
        assert n_ff >= 2
        scratch += [pltpu.VMEM((tm, D_MODEL), F32), pltpu.SemaphoreType.DMA((1,))]
    if emit_bf16:
        out_specs += [w_col_tile, w_col_tile, w_row]
        out_shape += [jax.ShapeDtypeStruct((n_ff, D_MODEL, tf), BF16)] * 2 + [jax.ShapeDtypeStruct((D_FF, D_MODEL), BF16)]
    if windows:
        out_specs += [hbm] * len(windows)
        out_shape += [jax.ShapeDtypeStruct(w.shape, w.dtype) for w in windows]
        half = windows[:len(windows) // 2]
        scratch += [pltpu.VMEM((2,) + w.shape[1:], w.dtype) for w in half]
        scratch += [pltpu.SemaphoreType.DMA((2, len(half))),
                    pltpu.SemaphoreType.DMA((2, sum(2 + (w.shape[2] > ROWS16) for w in half)))]
    return pl.pallas_call(
        functools.partial(_ffn_kernel, n_ff=n_ff, row_chunk=min(tm, 128), emit_bf16=emit_bf16, bf16_out=bf16_out,
                          prefetch_x=prefetch_x, n_windows=len(windows), n_requests=n_requests),
        grid=(rows // tm, n_ff),
        in_specs=[
            hbm if prefetch_x else row_tile,
            w_col_tile if tiled else w_col,
            w_col_tile if tiled else pl.BlockSpec((D_MODEL, tf), lambda i, j: (0, j + up_blk0)),
            w_row,
            pl.BlockSpec((1, D_MODEL), lambda i, j: (0, 0)),
            pl.BlockSpec((1, D_MODEL), lambda i, j: (0, 0)),
        ] + [hbm] * len(windows),
        out_specs=out_specs,
        out_shape=out_shape,
        scratch_shapes=scratch,
        compiler_params=_params(2),
        name="ffn_cast" if emit_bf16 else "ffn",
    )(x, w_gate, w_up, w_down, g, b, *windows)


def _mm_kernel(x_ref, w_ref, o_ref, *wb_ref):
    w = w_ref[...].astype(BF16)
    if wb_ref:
        wb_ref[0][...] = w
    o_ref[...] = jnp.dot(x_ref[...], w, preferred_element_type=F32).astype(o_ref.dtype)


def _matmul(x, w, *, tm, tn, out_dtype, name, emit_bf16=False):
    rows, k = x.shape
    n_cols = w.shape[1]
    assert not emit_bf16 or rows == tm
    w_spec = pl.BlockSpec((k, tn), lambda i, j: (0, j))
    out_specs = [pl.BlockSpec((tm, tn), lambda i, j: (i, j))]
    out_shape = [jax.ShapeDtypeStruct((rows, n_cols), out_dtype)]
    if emit_bf16:
        out_specs.append(pl.BlockSpec((None, k, tn), lambda i, j: (j, 0, 0)))
        out_shape.append(jax.ShapeDtypeStruct((n_cols // tn, k, tn), BF16))
    out = pl.pallas_call(
        _mm_kernel,
        grid=(rows // tm, n_cols // tn),
        in_specs=[pl.BlockSpec((tm, k), lambda i, j: (i, 0)), w_spec],
        out_specs=out_specs,
        out_shape=out_shape,
        compiler_params=_params(2),
        name=name,
    )(x, w)
    return out if emit_bf16 else out[0]


def _in_proj_kernel(x_ref, w_ref, p_ref, qkv_ref, *, n_own):
    j = pl.program_id(1)
    acc = jnp.dot(x_ref[...], w_ref[...], preferred_element_type=F32)
    p_ref[...] = acc.astype(BF16)

    @pl.when(j < n_own)
    def _():
        qkv_ref[...] = acc


def _in_proj(x, w, *, tm, tn):
    rows, k = x.shape
    assert w.shape == (IN_W // tn, k, tn) and COL_Q % tn == 0 and COL_GC % tn == 0
    c_lo, n_own = COL_Q // tn, (COL_GC - COL_Q) // tn

    def col_block(j):
        return jnp.where(j < n_own, j + c_lo, jnp.where(j < n_own + c_lo, j - n_own, j))

    return pl.pallas_call(
        functools.partial(_in_proj_kernel, n_own=n_own),
        grid=(rows // tm, IN_W // tn),
        in_specs=[pl.BlockSpec((tm, k), lambda i, j: (i, 0)),
                  pl.BlockSpec((None, k, tn), lambda i, j: (col_block(j), 0, 0))],
        out_specs=[pl.BlockSpec((tm, tn), lambda i, j: (i, col_block(j))),
                   pl.BlockSpec((tm, tn), lambda i, j: (i, jnp.minimum(j, n_own - 1)))],
        out_shape=[jax.ShapeDtypeStruct((rows, IN_W), BF16),
                   jax.ShapeDtypeStruct((rows, COL_GC - COL_Q), F32)],
        compiler_params=_params(2),
        name="in_proj",
    )(x, w)


def _rows(ref, start, dil):
    if dil == 1:
        return ref[start:start + BAND, :]
    return ref[pl.ds(start, BAND, stride=dil), :]


def _band_kernel(sl_ref, q0_ref, k0_ref, v0_ref, q1_ref, k1_ref, v1_ref, q2_ref, k2_ref, v2_ref,
                 o_ref, kt0_ref, vt0_ref, kt1_ref, vt1_ref, kt2_ref, vt2_ref, og_ref, lg_ref):
    c = pl.program_id(1)
    q_refs = (q0_ref, q1_ref, q2_ref)
    k_refs = (k0_ref, k1_ref, k2_ref)
    v_refs = (v0_ref, v1_ref, v2_ref)
    kt_refs = (kt0_ref, kt1_ref, kt2_ref)
    vt_refs = (vt0_ref, vt1_ref, vt2_ref)

    @pl.when(c == 0)
    def _():
        for ref in kt_refs + vt_refs:
            ref[...] = jnp.zeros(ref.shape, F32)

    qi = lax.broadcasted_iota(jnp.int32, (BAND, 2 * BAND), 0)
    ki = lax.broadcasted_iota(jnp.int32, (BAND, 2 * BAND), 1)
    rel = BAND + qi - ki
    valid = (rel >= 0) & (rel <= BAND)
    no_prev = jnp.where(ki < BAND, jnp.where(c == 0, NEG_INF, 0.0), 0.0)

    for g, (_, dil) in enumerate(DIL_GROUPS):
        slope = sl_ref[0, g:g + 1, 0:1]
        bias = jnp.where(valid, -(slope * (rel * dil).astype(F32)), NEG_INF)
        bias_first = bias + no_prev
        span = BAND * dil
        for r in range(dil):
            for n in range(CHUNK // span):
                start = r + n * span
                q = _rows(q_refs[g], start, dil).astype(BF16)
                if n == 0:
                    k_prev, v_prev = _rows(kt_refs[g], r, dil), _rows(vt_refs[g], r, dil)
                else:
                    k_prev, v_prev = _rows(k_refs[g], start - span, dil), _rows(v_refs[g], start - span, dil)
                k = jnp.concatenate([k_prev, _rows(k_refs[g], start, dil)], axis=0).astype(BF16)
                v = jnp.concatenate([v_prev, _rows(v_refs[g], start, dil)], axis=0).astype(BF16)
                s = lax.dot_general(q, k, (((1,), (1,)), ((), ())), preferred_element_type=F32) * SCALE
                s = s + (bias_first if n == 0 else bias)
                m = jnp.max(s, axis=-1, keepdims=True)
                p = jnp.exp(s - m)
                l = jnp.sum(p, axis=-1, keepdims=True)
                o = jnp.dot(p.astype(BF16), v, preferred_element_type=F32) / l
                lse = jnp.broadcast_to(m + jnp.log(l), (BAND, HEAD_DIM))
                if dil == 1:
                    og_ref[g, start:start + BAND, :] = o
                    lg_ref[g, start:start + BAND, :] = lse
                else:
                    og_ref[g, pl.ds(start, BAND, stride=dil), :] = o
                    lg_ref[g, pl.ds(start, BAND, stride=dil), :] = lse
        kt_refs[g][...] = k_refs[g][CHUNK - span:CHUNK, :]
        vt_refs[g][...] = v_refs[g][CHUNK - span:CHUNK, :]

    rows_per_step = 256

    def combine(i, carry):
        rows = pl.ds(pl.multiple_of(i * rows_per_step, rows_per_step), rows_per_step)
        l0, l1, l2 = lg_ref[0, rows, :], lg_ref[1, rows, :], lg_ref[2, rows, :]
        m = jnp.maximum(jnp.maximum(l0, l1), l2)
        e0, e1, e2 = jnp.exp(l0 - m), jnp.exp(l1 - m), jnp.exp(l2 - m)
        o = (e0 * og_ref[0, rows, :] + e1 * og_ref[1, rows, :] + e2 * og_ref[2, rows, :]) / (e0 + e1 + e2)
        o_ref[rows, :] = o.astype(o_ref.dtype)
        return carry

    lax.fori_loop(0, CHUNK // rows_per_step, combine, 0)


def _band_attention(qkv, slopes):
    seq = qkv.shape[0]
    blk = (CHUNK, HEAD_DIM)
    per_tensor = N_GROUPS * HEADS
    in_specs = [pl.BlockSpec((1, 8, HEAD_DIM), lambda h, c: (h, 0, 0))]
    for g in range(N_GROUPS):
        for tensor in range(3):
            in_specs.append(pl.BlockSpec(
                blk, lambda h, c, g=g, tensor=tensor: (c, tensor * per_tensor + g * HEADS + h)))
    tails = []
    for _, dil in DIL_GROUPS:
        tails += [pltpu.VMEM((BAND * dil, HEAD_DIM), F32)] * 2
    return pl.pallas_call(
        _band_kernel,
        grid=(HEADS, seq // CHUNK),
        in_specs=in_specs,
        out_specs=pl.BlockSpec(blk, lambda h, c: (c, h)),
        out_shape=jax.ShapeDtypeStruct((seq, GROUP_W), BF16),
        scratch_shapes=tails + [pltpu.VMEM((N_GROUPS, CHUNK, HEAD_DIM), F32),
                                pltpu.VMEM((N_GROUPS, CHUNK, HEAD_DIM), F32)],
        compiler_params=_params(2),
        name="band_attn",
    )(slopes, *([qkv] * 9))


def _mem_attn_kernel(q_ref, mk_ref, mv_ref, o_ref):
    for h in range(HEADS):
        hs = slice(h * HEAD_DIM, (h + 1) * HEAD_DIM)
        q = q_ref[:, hs]
        k = mk_ref[:, hs].astype(BF16)
        v = mv_ref[:, hs].astype(BF16)
        s = lax.dot_general(q, k, (((1,), (1,)), ((), ())), preferred_element_type=F32) * SCALE
        m = jnp.max(s, axis=-1, keepdims=True)
        p = jnp.exp(s - m)
        l = jnp.sum(p, axis=-1, keepdims=True)
        o = jnp.dot(p.astype(BF16), v, preferred_element_type=F32) / l
        o_ref[:, hs] = o.astype(o_ref.dtype)


def _mem_attention(p_all, mem_kv, *, tq):
    seq = p_all.shape[0]
    return pl.pallas_call(
        _mem_attn_kernel,
        grid=(seq // tq,),
        in_specs=[pl.BlockSpec((tq, GROUP_W), lambda i: (i, COL_QM // GROUP_W)),
                  pl.BlockSpec((MEM_TOKENS, GROUP_W), lambda i: (0, 0)),
                  pl.BlockSpec((MEM_TOKENS, GROUP_W), lambda i: (0, 1))],
        out_specs=pl.BlockSpec((tq, GROUP_W), lambda i: (i, 0)),
        out_shape=jax.ShapeDtypeStruct((seq, GROUP_W), BF16),
        compiler_params=_params(1),
        name="mem_attn",
    )(p_all, mem_kv, mem_kv)


def _conv_rows(b, c, v, w_ref, carry_ref):
    u = c.astype(F32) * v.astype(F32)
    c2 = carry_ref[0:1, :]
    c1 = carry_ref[1:2, :]
    row = lax.broadcasted_iota(jnp.int32, u.shape, 0)
    u1 = jnp.where(row == 0, c1, pltpu.roll(u, 1, 0))
    u2 = jnp.where(row == 0, c2, jnp.where(row == 1, c1, pltpu.roll(u, 2, 0)))
    return b.astype(F32) * (w_ref[0:1, :] * u2 + w_ref[1:2, :] * u1 + w_ref[2:3, :] * u), u


def _conv_kernel(b_ref, c_ref, v_ref, st_ref, w_ref, o_ref, ns_ref, carry_ref, *, t_last):
    i = pl.program_id(1)

    @pl.when(i == 0)
    def _():
        carry_ref[0:2, :] = st_ref[0]

    o, u = _conv_rows(b_ref[0], c_ref[0], v_ref[0], w_ref, carry_ref)
    o_ref[0] = o.astype(o_ref.dtype)
    last2 = u[t_last - 2:t_last, :]
    carry_ref[0:2, :] = last2
    ns_ref[0] = last2


def _conv(p3, state, conv_w, *, tm, t_last):
    nb, t = p3.shape[0], p3.shape[1]
    blk = (1, tm, CONV_W)
    return pl.pallas_call(
        functools.partial(_conv_kernel, t_last=t_last),
        grid=(nb, t // tm),
        in_specs=[
            pl.BlockSpec(blk, lambda b, i: (b, i, COL_B // CONV_W)),
            pl.BlockSpec(blk, lambda b, i: (b, i, COL_C // CONV_W)),
            pl.BlockSpec(blk, lambda b, i: (b, i, COL_V // CONV_W)),
            pl.BlockSpec((1, 2, CONV_W), lambda b, i: (b, 0, 0)),
            pl.BlockSpec((3, CONV_W), lambda b, i: (0, 0)),
        ],
        out_specs=[pl.BlockSpec(blk, lambda b, i: (b, i, 0)),
                   pl.BlockSpec((1, 2, CONV_W), lambda b, i: (b, 0, 0))],
        out_shape=[jax.ShapeDtypeStruct((nb, t, CONV_W), BF16),
                   jax.ShapeDtypeStruct((nb, 2, CONV_W), F32)],
        scratch_shapes=[pltpu.VMEM((8, CONV_W), F32)],
        compiler_params=_params(2),
        name="conv",
    )(p3, p3, p3, state, conv_w)


def _merge_kernel(*refs, sub_rows, fuse_conv):
    if fuse_conv:
        b_ref, c_ref, v_ref, cw_ref = refs[:4]
        refs = refs[4:]
    else:
        oc_ref = refs[0]
        refs = refs[1:]
    oa_ref, om_ref, gc_ref, ga_ref, gm_ref, h_ref, wc_ref, wa_ref, wm_ref, wo_ref, g_ref, beta_ref, y_ref = refs[:13]
    if fuse_conv:
        ns_ref, carry_ref = refs[13:]

        @pl.when(pl.program_id(0) == 0)
        def _():
            carry_ref[...] = jnp.zeros(carry_ref.shape, F32)

    for r0 in range(0, h_ref.shape[0], sub_rows):
        rows = slice(r0, r0 + sub_rows)
        if fuse_conv:
            oc, u = _conv_rows(b_ref[rows, :], c_ref[rows, :], v_ref[rows, :], cw_ref, carry_ref)
            oc = oc.astype(BF16)
            carry_ref[0:2, :] = u[sub_rows - 2:sub_rows, :]
        else:
            oc = oc_ref[rows, :]
        mc = jnp.dot(oc, wc_ref[...], preferred_element_type=F32)
        ma = jnp.dot(oa_ref[rows, :], wa_ref[...], preferred_element_type=F32)
        mm = jnp.dot(om_ref[rows, :], wm_ref[...], preferred_element_type=F32)
        mix = (jax.nn.sigmoid(gc_ref[rows, :].astype(F32)) * mc
               + jax.nn.sigmoid(ga_ref[rows, :].astype(F32)) * ma
               + jax.nn.sigmoid(gm_ref[rows, :].astype(F32)) * mm)
        z = ALPHA * h_ref[rows, :] + jnp.dot(mix.astype(BF16), wo_ref[...], preferred_element_type=F32)
        y_ref[rows, :] = _layer_norm_rows(z, g_ref[...], beta_ref[...])
    if fuse_conv:
        ns_ref[...] = carry_ref[0:2, :]


def _merge(o_conv, o_attn, o_mem, p_all, h, w_c, w_a, w_m, w_o, g, b, *, tm, conv_w=None):
    rows = h.shape[0]
    fuse_conv = o_conv is None
    single = pl.Buffered(1)
    row_blk = lambda w: pl.BlockSpec((tm, w), lambda i: (i, 0))
    col_blk = lambda w, col: pl.BlockSpec((tm, w), lambda i: (i, col // w))
    whole = lambda a: pl.BlockSpec(a.shape, lambda i: (0, 0), pipeline_mode=single)
    if fuse_conv:
        conv_specs = [col_blk(CONV_W, COL_B), col_blk(CONV_W, COL_C), col_blk(CONV_W, COL_V), whole(conv_w)]
        conv_args = [p_all, p_all, p_all, conv_w]
    else:
        conv_specs, conv_args = [row_blk(CONV_W)], [o_conv]
    out = pl.pallas_call(
        functools.partial(_merge_kernel, sub_rows=min(tm, 256), fuse_conv=fuse_conv),
        grid=(rows // tm,),
        in_specs=conv_specs + [row_blk(GROUP_W), row_blk(GROUP_W),
                               col_blk(D_MODEL, COL_GC), col_blk(D_MODEL, COL_GA), col_blk(D_MODEL, COL_GM),
                               row_blk(D_MODEL),
                               whole(w_c), whole(w_a), whole(w_m), whole(w_o), whole(g), whole(b)],
        out_specs=[row_blk(D_MODEL)] + ([pl.BlockSpec((2, CONV_W), lambda i: (0, 0))] if fuse_conv else []),
        out_shape=[jax.ShapeDtypeStruct((rows, D_MODEL), F32)]
                  + ([jax.ShapeDtypeStruct((2, CONV_W), F32)] if fuse_conv else []),
        scratch_shapes=[pltpu.VMEM((8, CONV_W), F32)] if fuse_conv else [],
        compiler_params=_params(1),
        name="merge",
    )(*conv_args, o_attn, o_mem, p_all, p_all, p_all, h, w_c, w_a, w_m, w_o, g, b)
    return out if fuse_conv else out[0]


def _over_quad(x, op):
    x = op(x, pltpu.roll(x, HEADS, 0))
    return op(x, pltpu.roll(x, 2 * HEADS, 0))


def _lane_dot(k3, q):
    return jnp.sum(k3 * q, axis=-1, keepdims=True) * SCALE


def _quad_attention(k3, v3, qt, bias3, k_new, v_new, bias_new):
    s = _lane_dot(k3, qt[None]) + bias3
    m = jnp.max(s, axis=0)
    if k_new is not None:
        s_new = _lane_dot(k_new, qt) + bias_new
        m = jnp.maximum(m, s_new)
    m = _over_quad(m, jnp.maximum)
    e = jnp.exp(s - m[None])
    l = jnp.sum(e, axis=0)
    acc = jnp.sum(e * v3, axis=0)
    if k_new is not None:
        e_new = jnp.exp(s_new - m)
        l = l + e_new
        acc = acc + e_new * v_new
    l = _over_quad(l, jnp.add)
    acc = _over_quad(acc, jnp.add)
    return acc / l, m + jnp.log(l)


def _sample_kernel(qd_ref, qt_ref, kn_ref, vn_ref, sl_ref,
                   k0_ref, v0_ref, k1_ref, v1_ref, k2_ref, v2_ref, mk_ref, mv_ref, *new_refs, t_new):
    k_refs = (k0_ref, k1_ref, k2_ref)
    v_refs = (v0_ref, v1_ref, v2_ref)
    oa_ref, om_ref = new_refs[N_GROUPS * 2], new_refs[N_GROUPS * 2 + 1]
    quad16 = lax.broadcasted_iota(jnp.int32, (ROWS16, HEAD_DIM), 0) >> 2

    outs, lses = [], []

    win0 = DIL_GROUPS[0][0]
    a_idx = lax.broadcasted_iota(jnp.int32, (win0 // 4, ROWS16, HEAD_DIM), 0)
    b_idx = lax.broadcasted_iota(jnp.int32, (win0 // 4, ROWS16, HEAD_DIM), 1) >> 2
    tok = a_idx * 4 + b_idx
    slope0 = sl_ref[0]
    k3, v3 = k0_ref[0], v0_ref[0]
    o0 = jnp.zeros((ROWS16, HEAD_DIM), F32)
    l0 = jnp.zeros((ROWS16, HEAD_DIM), F32)
    for t in range(t_new):
        bias3 = jnp.where(tok >= t, -(slope0[None] * (win0 + t - tok).astype(F32)), NEG_INF)
        bias_new = jnp.where(quad16 <= t, -(slope0 * (t - quad16).astype(F32)), NEG_INF)
        o_t, l_t = _quad_attention(k3, v3, qt_ref[0, 0, t], bias3, kn_ref[0, 0], vn_ref[0, 0], bias_new)
        o0 = jnp.where(quad16 == t, o_t, o0)
        l0 = jnp.where(quad16 == t, l_t, l0)
    outs.append(o0)
    lses.append(l0)

    for g in (1, 2):
        win, dil = DIL_GROUPS[g]
        q16 = qd_ref[0, g - 1]
        k3 = k_refs[g][0, :, 0:ROWS16, :]
        v3 = v_refs[g][0, :, 0:ROWS16, :]
        i_idx = lax.broadcasted_iota(jnp.int32, (BAND, ROWS16, HEAD_DIM), 0)
        bias3 = -(sl_ref[g][None] * (win - i_idx * dil).astype(F32))
        s = _lane_dot(k3, q16[None]) + bias3
        s_new = _lane_dot(kn_ref[0, g], q16)
        m = jnp.maximum(jnp.max(s, axis=0), s_new)
        e = jnp.exp(s - m[None])
        e_new = jnp.exp(s_new - m)
        l = jnp.sum(e, axis=0) + e_new
        acc = jnp.sum(e * v3, axis=0) + e_new * vn_ref[0, g]
        outs.append(acc / l)
        lses.append(m + jnp.log(l))

    m = jnp.maximum(jnp.maximum(lses[0], lses[1]), lses[2])
    w = [jnp.exp(x - m) for x in lses]
    oa_ref[0] = (w[0] * outs[0] + w[1] * outs[1] + w[2] * outs[2]) / (w[0] + w[1] + w[2])

    k3, v3 = mk_ref[0], mv_ref[0]
    om = jnp.zeros((ROWS16, HEAD_DIM), F32)
    for t in range(t_new):
        o_t, _ = _quad_attention(k3, v3, qt_ref[0, 1, t], jnp.zeros((1, ROWS16, HEAD_DIM), F32), None, None, None)
        om = jnp.where(quad16 == t, o_t, om)
    om_ref[0] = om

    for i, out_ref in enumerate(new_refs[N_GROUPS * 2 + 2:]):
        out_ref[0, 0] = (kn_ref, vn_ref)[i % 2][0, i // 2]


def _sample_attention(qd, qt, kn, vn, slopes16, k_states, v_states, shifted, mem_k, mem_v, *, t_new):
    nb = qd.shape[0]
    states = []
    for g in range(N_GROUPS):
        states += [k_states[g], v_states[g]]
    whole = lambda a: pl.BlockSpec((1,) + a.shape[1:], lambda b: (b,) + (0,) * (a.ndim - 1))
    head_rows = lambda a: pl.BlockSpec((1, a.shape[1], ROWS16, HEAD_DIM), lambda b: (b, 0, 0, 0))
    last_rows = lambda a: pl.BlockSpec((1, 1, ROWS16, HEAD_DIM),
                                       lambda b, a=a: (b, a.shape[1] - 1, a.shape[2] // ROWS16 - 1, 0))
    o_spec = pl.BlockSpec((1, ROWS16, HEAD_DIM), lambda b: (b, 0, 0))
    o_shape = jax.ShapeDtypeStruct((nb, ROWS16, HEAD_DIM), F32)
    n_in = 5 + len(states) + 2
    return pl.pallas_call(
        functools.partial(_sample_kernel, t_new=t_new),
        grid=(nb,),
        in_specs=[whole(qd), whole(qt), whole(kn), whole(vn),
                  pl.BlockSpec(slopes16.shape, lambda b: (0, 0, 0))]
                 + [head_rows(s) for s in states] + [whole(mem_k), whole(mem_v)]
                 + [pl.BlockSpec(memory_space=pl.ANY)] * len(shifted),
        out_specs=[o_spec, o_spec] + [last_rows(s) for s in shifted],
        out_shape=[o_shape, o_shape] + [jax.ShapeDtypeStruct(s.shape, F32) for s in shifted],
        input_output_aliases={n_in + i: 2 + i for i in range(len(shifted))},
        compiler_params=_params(1),
        name="sample_attn",
    )(qd, qt, kn, vn, slopes16, *states, mem_k, mem_v, *shifted)


def kernel(x_prompt, x_sample, mem_prompt, state_conv, state_k_w128, state_v_w128, state_k_w512, state_v_w512, state_k_w2048, state_v_w2048, cache_mem_k, cache_mem_v, ln1_g, ln1_b, w_ffn1_gu, w_ffn1_down, w_in, conv_w, w_mem_kv, w_br_conv, w_br_attn, w_br_mem, w_o, ln2_g, ln2_b, w_ffn2_gu, w_ffn2_down, ln3_g, ln3_b):
    depth = w_in.shape[0]
    seq = x_prompt.shape[1]
    nb, t_new = x_sample.shape[0], x_sample.shape[1]
    assert depth == 1 and x_prompt.shape[0] == 1 and seq % CHUNK == 0
    assert t_new * HEADS == ROWS16 and t_new <= DIL_GROUPS[1][1]

    wb = lambda w: w[0].astype(BF16)
    wc_b, wa_b, wm_b, wo_b = wb(w_br_conv), wb(w_br_attn), wb(w_br_mem), wb(w_o)
    n_ff_blocks = D_FF // 512
    cw = conv_w[0]
    slopes_lane = np.broadcast_to(SLOPES[:, :, None], (N_GROUPS, HEADS, HEAD_DIM))
    slopes_band = jnp.asarray(np.pad(slopes_lane.transpose(1, 0, 2), ((0, 0), (0, 8 - N_GROUPS), (0, 0))))
    slopes16 = jnp.asarray(np.tile(slopes_lane, (1, ROWS16 // HEADS, 1)))

    rows_s = nb * SAMPLE_PAD
    xs = jnp.pad(x_sample, ((0, 0), (0, SAMPLE_PAD - t_new), (0, 0))).reshape(rows_s, D_MODEL)
    s1, s1b, w1g, w1u, w1d = _ffn(xs, w_ffn1_gu[0], w_ffn1_gu[0], w_ffn1_down[0], ln1_g, ln1_b,
                                  tm=rows_s, tf=512, up_blk0=n_ff_blocks, emit_bf16=True)
    ps_all, win_b = _matmul(s1b, w_in[0], tm=rows_s, tn=1024, out_dtype=F32, name="in_proj_s", emit_bf16=True)

    def tiles(state):
        win = state.shape[2]
        per_tile = 4 if win <= DIL_GROUPS[1][0] else DIL_GROUPS[2][1]
        return state.reshape(nb, win // per_tile, per_tile * HEADS, HEAD_DIM)

    k_states = [tiles(s) for s in (state_k_w128, state_k_w512, state_k_w2048)]
    v_states = [tiles(s) for s in (state_v_w128, state_v_w512, state_v_w2048)]
    h1, h1b, *shifted = _ffn(x_prompt[0], w1g, w1u, w1d, ln1_g, ln1_b, tm=1024, tf=512, prefetch_x=True,
                             windows=k_states + v_states)
    shifted = [shifted[g + part * N_GROUPS] for g in range(N_GROUPS) for part in (0, 1)]
    p_all, qkv = _in_proj(h1b, win_b, tm=1024, tn=1024)
    mem_kv = _matmul(mem_prompt[0].astype(BF16), w_mem_kv[0], tm=MEM_TOKENS, tn=512, out_dtype=F32, name="mem_kv")
    o_attn = _band_attention(qkv, slopes_band)
    o_mem = _mem_attention(p_all, mem_kv, tq=512)
    h2, conv_p = _merge(None, o_attn, o_mem, p_all, h1, wc_b, wa_b, wm_b, wo_b, ln2_g, ln2_b, tm=256, conv_w=cw)

    prompt_states = []
    for g, (win, _) in enumerate(DIL_GROUPS):
        for tensor in (1, 2):
            c0 = tensor * N_GROUPS * GROUP_W + g * GROUP_W
            prompt_states.append(qkv[seq - win:, c0:c0 + GROUP_W].reshape(1, 1, win, HEADS, HEAD_DIM))
    mem_k_p = mem_kv[:, :GROUP_W].reshape(1, 1, MEM_TOKENS, HEADS, HEAD_DIM)
    mem_v_p = mem_kv[:, GROUP_W:].reshape(1, 1, MEM_TOKENS, HEADS, HEAD_DIM)

    ps3 = ps_all.reshape(nb, SAMPLE_PAD, IN_W)

    real = ps3[:, :t_new]
    heads = lambda a, n: a.reshape(nb, t_new, n, HEADS, HEAD_DIM)
    q, k, v = (heads(real[..., c:c + N_GROUPS * GROUP_W], N_GROUPS) for c in (COL_Q, COL_K, COL_VA))
    rows_th = lambda a: a.transpose(0, 2, 1, 3, 4).reshape(nb, a.shape[2], ROWS16, HEAD_DIM)
    over_quad = lambda a: jnp.broadcast_to(a[:, :, None], (nb, t_new, ROWS16 // HEADS, HEADS, HEAD_DIM)
                                           ).reshape(nb, t_new, ROWS16, HEAD_DIM)
    qd = rows_th(q[:, :, 1:])
    qm = real[..., COL_QM:COL_QM + GROUP_W].reshape(nb, t_new, HEADS, HEAD_DIM)
    qt = jnp.stack([over_quad(q[:, :, 0]), over_quad(qm)], axis=1)
    kn, vn = rows_th(k), rows_th(v)

    outs = _sample_attention(qd, qt, kn, vn, slopes16, k_states, v_states, shifted,
                             tiles(cache_mem_k), tiles(cache_mem_v), t_new=t_new)
    pad_rows = lambda a, w: jnp.pad(a.reshape(nb, t_new, w), ((0, 0), (0, SAMPLE_PAD - t_new), (0, 0))
                                    ).reshape(rows_s, w).astype(BF16)
    o_attn_s, o_mem_s = pad_rows(outs[0], GROUP_W), pad_rows(outs[1], GROUP_W)
    sample_states = [s.reshape(1, nb, -1, HEADS, HEAD_DIM) for s in outs[2:]]

    o_conv_s, conv_s = _conv(ps3, state_conv[0], cw, tm=SAMPLE_PAD, t_last=t_new)
    s2 = _merge(o_conv_s.reshape(rows_s, CONV_W), o_attn_s, o_mem_s, ps_all, s1,
                wc_b, wa_b, wm_b, wo_b, ln2_g, ln2_b, tm=rows_s)
    y_s, w2g, w2u, w2d = _ffn(s2, w_ffn2_gu[0], w_ffn2_gu[0], w_ffn2_down[0], ln3_g, ln3_b,
                              tm=rows_s, tf=512, up_blk0=n_ff_blocks, emit_bf16=True, bf16_out=False)
    y_sample = y_s.reshape(nb, SAMPLE_PAD, D_MODEL)[:, :t_new]
    (y_prompt,) = _ffn(h2, w2g, w2u, w2d, ln3_g, ln3_b, tm=1024, tf=512, bf16_out=False, row_buffers=2)

    return (y_prompt[None], y_sample, conv_p[None, None], *prompt_states, mem_k_p, mem_v_p,
            conv_s[None], *sample_states)
```

```python
import functools

import jax
import jax.numpy as jnp
import numpy as np
from jax import lax
from jax.experimental import pallas as pl
from jax.experimental.pallas import tpu as pltpu

F32 = jnp.float32
BF16 = jnp.bfloat16

D_MODEL = 2048
D_FF = 5632
HEAD_DIM = 128
HEADS = 4
GROUP_W = HEADS * HEAD_DIM
DIL_GROUPS = ((128, 1), (512, 4), (2048, 16))
N_GROUPS = len(DIL_GROUPS)
BAND = 128
CHUNK = BAND * DIL_GROUPS[-1][1]
CONV_W = 1024
MEM_TOKENS = 256
IN_W = 14336
COL_B, COL_C, COL_V = 0, 1024, 2048
COL_Q, COL_K, COL_VA = 3072, 4608, 6144
COL_QM = 7680
COL_GC, COL_GA, COL_GM = 8192, 10240, 12288
QKV_W = COL_QM - COL_Q
ALPHA = 2.0 ** 0.25
LN_EPS = 1e-5
NEG_INF = -1e30
SCALE = HEAD_DIM ** -0.5
SLOPES = np.exp2(np.float32(-8.0) * np.arange(1, 13, dtype=np.float32) / np.float32(12)).reshape(3, 4)

SAMPLE_PAD = 8
ROWS16 = 16
VMEM_LIMIT = 56 * 1024 * 1024


def _params(n_axes, vmem=VMEM_LIMIT):
    return pltpu.CompilerParams(dimension_semantics=("arbitrary",) * n_axes, vmem_limit_bytes=vmem)


def _layer_norm_rows(z, g, b):
    mu = jnp.mean(z, axis=-1, keepdims=True)
    zc = z - mu
    var = jnp.mean(zc * zc, axis=-1, keepdims=True)
    return zc * lax.rsqrt(var + LN_EPS) * g + b


def _window_loads(b, src_refs, stage_refs, slot, sems):
    return [pltpu.make_async_copy(src.at[b], stage.at[slot], sems.at[slot, n])
            for n, (src, stage) in enumerate(zip(src_refs, stage_refs))]


def _window_stores(b, stage_refs, dst_refs, slot, sems):
    copies = []
    for stage, dst in zip(stage_refs, dst_refs):
        tiles, rows = dst.shape[1], dst.shape[2]
        keep = rows - ROWS16
        if keep:
            copies.append((stage.at[slot, :, pl.ds(ROWS16, keep), :], dst.at[b, :, pl.ds(0, keep), :]))
        copies.append((stage.at[slot, pl.ds(1, tiles - 1), pl.ds(0, ROWS16), :],
                       dst.at[b, pl.ds(0, tiles - 1), pl.ds(keep, ROWS16), :]))
        copies.append((stage.at[slot, pl.ds(tiles - 1, 1), pl.ds(keep, ROWS16), :],
                       dst.at[b, pl.ds(tiles - 1, 1), pl.ds(keep, ROWS16), :]))
    return [pltpu.make_async_copy(s, d, sems.at[slot, n]) for n, (s, d) in enumerate(copies)]


def _ffn_kernel(*refs, n_ff, row_chunk, emit_bf16, bf16_out, manual_rows, n_windows, n_requests):
    x_ref, wg_ref, wu_ref, wd_ref, g_ref, b_ref = refs[:6]
    win_src = refs[6:6 + n_windows]
    y_ref = refs[6 + n_windows]
    yb_ref = refs[7 + n_windows] if bf16_out else None
    rest = refs[7 + bf16_out + n_windows:]
    if emit_bf16:
        wb_refs, rest = rest[:3], rest[3:]
    win_dst, rest = rest[:n_windows], rest[n_windows:]
    xb_ref, rest = rest[0], rest[1:]
    i, j = pl.program_id(0), pl.program_id(1)
    n_tiles = pl.num_programs(0)
    tm = xb_ref.shape[0]

    if manual_rows:
        y_hbm, yb_hbm = y_ref, yb_ref
        n_ring = 1 + bf16_out
        xbuf_ref, xsem, y_ref = rest[:3]
        ring_refs, out_sems = rest[3:3 + n_ring], rest[3 + n_ring]
        rest = rest[4 + n_ring:]

        def x_copy(tile):
            return pltpu.make_async_copy(x_ref.at[pl.ds(pl.multiple_of(tile * tm, tm), tm), :], xbuf_ref, xsem.at[0])

        @pl.when((i == 0) & (j == 0))
        def _():
            x_copy(0).start()

    if n_windows:
        half = n_windows // 2
        stage_refs, load_sems, store_sems = rest[:half], rest[half], rest[half + 1]
        step = i * n_ff + j
        n_items = 2 * n_requests
        for part in (0, 1):
            srcs, dsts = win_src[part * half:(part + 1) * half], win_dst[part * half:(part + 1) * half]
            mine = step % 2 == part

            @pl.when(mine & (step >= 2) & (step < n_items + 2))
            def _():
                for copy in _window_stores((step - 2) // 2, stage_refs, dsts, part, store_sems):
                    copy.wait()

            @pl.when(mine & (step < n_items))
            def _():
                for copy in _window_loads(step // 2, srcs, stage_refs, part, load_sems):
                    copy.start()

            @pl.when(jnp.logical_not(mine) & (step >= 1) & (step < n_items + 1))
            def _():
                for copy in _window_loads((step - 1) // 2, srcs, stage_refs, part, load_sems):
                    copy.wait()
                for copy in _window_stores((step - 1) // 2, stage_refs, dsts, part, store_sems):
                    copy.start()

    @pl.when(j == 0)
    def _():
        if manual_rows:
            x_copy(i).wait()
        x = xbuf_ref[...] if manual_rows else x_ref[...]
        xb_ref[...] = x.astype(BF16)
        y_ref[...] = (2.0 * ALPHA) * x

    if manual_rows:
        @pl.when((j == 1) & (i + 1 < n_tiles))
        def _():
            x_copy(i + 1).start()

    wg, wu, wd = wg_ref[...], wu_ref[...], wd_ref[...]
    if emit_bf16:
        wg, wu, wd = wg.astype(BF16), wu.astype(BF16), wd.astype(BF16)
        wb_refs[0][...], wb_refs[1][...], wb_refs[2][...] = wg, wu, wd
    xb = xb_ref[...]
    gate = jnp.dot(xb, wg, preferred_element_type=F32)
    up = jnp.dot(xb, wu, preferred_element_type=F32)
    act = (gate * jax.nn.sigmoid(gate) * up).astype(BF16)
    y_ref[...] += jnp.dot(act, wd, preferred_element_type=F32)

    if not manual_rows:
        @pl.when(j == n_ff - 1)
        def _():
            def chunk(c, carry):
                rows = pl.ds(pl.multiple_of(c * row_chunk, row_chunk), row_chunk)
                y = _layer_norm_rows(0.5 * y_ref[rows, :], g_ref[...], b_ref[...])
                y_ref[rows, :] = y
                if bf16_out:
                    yb_ref[rows, :] = y.astype(BF16)
                return carry
            lax.fori_loop(0, tm // row_chunk, chunk, 0)
    else:
        def out_copies(slot, row0):
            dsts = (y_hbm, yb_hbm)[:n_ring]
            return [pltpu.make_async_copy(ring.at[slot], dst.at[pl.ds(row0, row_chunk), :], out_sems.at[slot, n])
                    for n, (ring, dst) in enumerate(zip(ring_refs, dsts))]

        def wait_slot(slot):
            for copy in out_copies(slot, 0):
                copy.wait()

        @pl.when(j == n_ff - 1)
        def _():
            n_chunks = tm // row_chunk
            for c in range(n_chunks):
                slot = c % 2
                if c >= 2:
                    wait_slot(slot)
                else:
                    pl.when(i > 0)(functools.partial(wait_slot, slot))
                y = _layer_norm_rows(0.5 * y_ref[c * row_chunk:(c + 1) * row_chunk, :], g_ref[...], b_ref[...])
                ring_refs[0][slot] = y
                if bf16_out:
                    ring_refs[1][slot] = y.astype(BF16)
                for copy in out_copies(slot, pl.multiple_of(i * tm, tm) + c * row_chunk):
                    copy.start()

            @pl.when(i == n_tiles - 1)
            def _():
                wait_slot(0)
                wait_slot(1)


def _ffn(x, w_gate, w_up, w_down, g, b, *, tm, tf, up_blk0=0, emit_bf16=False, bf16_out=True,
         row_buffers=1, manual_rows=False, windows=()):
    rows = x.shape[0]
    n_ff = D_FF // tf
    n_steps = (rows // tm) * n_ff
    n_requests = windows[0].shape[0] if windows else 0
    assert not emit_bf16 or rows == tm
    assert n_steps >= 2 * n_requests + 2
    row_mode = pl.Buffered(1) if (rows == tm or row_buffers == 1) else None
    hbm = pl.BlockSpec(memory_space=pl.ANY)
    w_col = pl.BlockSpec((D_MODEL, tf), lambda i, j: (0, j))
    w_col_tile = pl.BlockSpec((None, D_MODEL, tf), lambda i, j: (j, 0, 0))
    tiled = w_gate.ndim == 3
    w_row = pl.BlockSpec((tf, D_MODEL), lambda i, j: (j, 0))
    row_tile = pl.BlockSpec((tm, D_MODEL), lambda i, j: (i, 0), pipeline_mode=row_mode)
    out_specs = [hbm if manual_rows else row_tile] * (1 + bf16_out)
    out_shape = [jax.ShapeDtypeStruct((rows, D_MODEL), F32), jax.ShapeDtypeStruct((rows, D_MODEL), BF16)][:1 + bf16_out]
    scratch = [pltpu.VMEM((tm, D_MODEL), BF16)]
    if manual_rows:
        row_chunk = min(tm, 128)
        assert n_ff >= 2 and (tm // row_chunk) % 2 == 0
        scratch += [pltpu.VMEM((tm, D_MODEL), F32), pltpu.SemaphoreType.DMA((1,)), pltpu.VMEM((tm, D_MODEL), F32)]
        scratch += [pltpu.VMEM((2, row_chunk, D_MODEL), dt) for dt in (F32, BF16)[:1 + bf16_out]]
        scratch += [pltpu.SemaphoreType.DMA((2, 1 + bf16_out))]
    if emit_bf16:
        out_specs += [w_col_tile, w_col_tile, w_row]
        out_shape += [jax.ShapeDtypeStruct((n_ff, D_MODEL, tf), BF16)] * 2 + [jax.ShapeDtypeStruct((D_FF, D_MODEL), BF16)]
    if windows:
        out_specs += [hbm] * len(windows)
        out_shape += [jax.ShapeDtypeStruct(w.shape, w.dtype) for w in windows]
        half = windows[:len(windows) // 2]
        scratch += [pltpu.VMEM((2,) + w.shape[1:], w.dtype) for w in half]
        scratch += [pltpu.SemaphoreType.DMA((2, len(half))),
                    pltpu.SemaphoreType.DMA((2, sum(2 + (w.shape[2] > ROWS16) for w in half)))]
    return pl.pallas_call(
        functools.partial(_ffn_kernel, n_ff=n_ff, row_chunk=min(tm, 128), emit_bf16=emit_bf16, bf16_out=bf16_out,
                          manual_rows=manual_rows, n_windows=len(windows), n_requests=n_requests),
        grid=(rows // tm, n_ff),
        in_specs=[
            hbm if manual_rows else row_tile,
            w_col_tile if tiled else w_col,
            w_col_tile if tiled else pl.BlockSpec((D_MODEL, tf), lambda i, j: (0, j + up_blk0)),
            w_row,
            pl.BlockSpec((1, D_MODEL), lambda i, j: (0, 0)),
            pl.BlockSpec((1, D_MODEL), lambda i, j: (0, 0)),
        ] + [hbm] * len(windows),
        out_specs=out_specs,
        out_shape=out_shape,
        scratch_shapes=scratch,
        compiler_params=_params(2),
        name="ffn_cast" if emit_bf16 else "ffn",
    )(x, w_gate, w_up, w_down, g, b, *windows)


def _mm_kernel(x_ref, w_ref, o_ref, *wb_ref):
    w = w_ref[...].astype(BF16)
    if wb_ref:
        wb_ref[0][...] = w
    o_ref[...] = jnp.dot(x_ref[...], w, preferred_element_type=F32).astype(o_ref.dtype)


def _matmul(x, w, *, tm, tn, out_dtype, name, emit_bf16=False):
    rows, k = x.shape
    n_cols = w.shape[1]
    assert not emit_bf16 or rows == tm
    w_spec = pl.BlockSpec((k, tn), lambda i, j: (0, j))
    out_specs = [pl.BlockSpec((tm, tn), lambda i, j: (i, j))]
    out_shape = [jax.ShapeDtypeStruct((rows, n_cols), out_dtype)]
    if emit_bf16:
        out_specs.append(pl.BlockSpec((None, k, tn), lambda i, j: (j, 0, 0)))
        out_shape.append(jax.ShapeDtypeStruct((n_cols // tn, k, tn), BF16))
    out = pl.pallas_call(
        _mm_kernel,
        grid=(rows // tm, n_cols // tn),
        in_specs=[pl.BlockSpec((tm, k), lambda i, j: (i, 0)), w_spec],
        out_specs=out_specs,
        out_shape=out_shape,
        compiler_params=_params(2),
        name=name,
    )(x, w)
    return out if emit_bf16 else out[0]


def _in_proj_kernel(x_ref, w_ref, p_ref, qkv_ref, *, n_own):
    j = pl.program_id(1)
    acc = jnp.dot(x_ref[...], w_ref[...], preferred_element_type=F32)
    p_ref[...] = acc.astype(BF16)

    @pl.when(j < n_own)
    def _():
        qkv_ref[...] = acc


def _in_proj(x, w, *, tm, tn):
    rows, k = x.shape
    assert w.shape == (IN_W // tn, k, tn) and COL_Q % tn == 0 and COL_GC % tn == 0
    c_lo, n_own = COL_Q // tn, (COL_GC - COL_Q) // tn

    def col_block(j):
        return jnp.where(j < n_own, j + c_lo, jnp.where(j < n_own + c_lo, j - n_own, j))

    return pl.pallas_call(
        functools.partial(_in_proj_kernel, n_own=n_own),
        grid=(rows // tm, IN_W // tn),
        in_specs=[pl.BlockSpec((tm, k), lambda i, j: (i, 0)),
                  pl.BlockSpec((None, k, tn), lambda i, j: (col_block(j), 0, 0))],
        out_specs=[pl.BlockSpec((tm, tn), lambda i, j: (i, col_block(j))),
                   pl.BlockSpec((tm, tn), lambda i, j: (i, jnp.minimum(j, n_own - 1)))],
        out_shape=[jax.ShapeDtypeStruct((rows, IN_W), BF16),
                   jax.ShapeDtypeStruct((rows, COL_GC - COL_Q), F32)],
        compiler_params=_params(2),
        name="in_proj",
    )(x, w)


def _rows(ref, start, dil):
    if dil == 1:
        return ref[start:start + BAND, :]
    return ref[pl.ds(start, BAND, stride=dil), :]


def _band_kernel(sl_ref, q0_ref, k0_ref, v0_ref, q1_ref, k1_ref, v1_ref, q2_ref, k2_ref, v2_ref,
                 o_ref, kt0_ref, vt0_ref, kt1_ref, vt1_ref, kt2_ref, vt2_ref, og_ref, lg_ref):
    c = pl.program_id(1)
    q_refs = (q0_ref, q1_ref, q2_ref)
    k_refs = (k0_ref, k1_ref, k2_ref)
    v_refs = (v0_ref, v1_ref, v2_ref)
    kt_refs = (kt0_ref, kt1_ref, kt2_ref)
    vt_refs = (vt0_ref, vt1_ref, vt2_ref)

    @pl.when(c == 0)
    def _():
        for ref in kt_refs + vt_refs:
            ref[...] = jnp.zeros(ref.shape, F32)

    qi = lax.broadcasted_iota(jnp.int32, (BAND, 2 * BAND), 0)
    ki = lax.broadcasted_iota(jnp.int32, (BAND, 2 * BAND), 1)
    rel = BAND + qi - ki
    valid = (rel >= 0) & (rel <= BAND)
    no_prev = jnp.where(ki < BAND, jnp.where(c == 0, NEG_INF, 0.0), 0.0)

    for g, (_, dil) in enumerate(DIL_GROUPS):
        slope = sl_ref[0, g:g + 1, 0:1]
        bias = jnp.where(valid, -(slope * (rel * dil).astype(F32)), NEG_INF)
        bias_first = bias + no_prev
        span = BAND * dil
        for r in range(dil):
            for n in range(CHUNK // span):
                start = r + n * span
                q = _rows(q_refs[g], start, dil).astype(BF16)
                if n == 0:
                    k_prev, v_prev = _rows(kt_refs[g], r, dil), _rows(vt_refs[g], r, dil)
                else:
                    k_prev, v_prev = _rows(k_refs[g], start - span, dil), _rows(v_refs[g], start - span, dil)
                k = jnp.concatenate([k_prev, _rows(k_refs[g], start, dil)], axis=0).astype(BF16)
                v = jnp.concatenate([v_prev, _rows(v_refs[g], start, dil)], axis=0).astype(BF16)
                s = lax.dot_general(q, k, (((1,), (1,)), ((), ())), preferred_element_type=F32) * SCALE
                s = s + (bias_first if n == 0 else bias)
                m = jnp.max(s, axis=-1, keepdims=True)
                p = jnp.exp(s - m)
                l = jnp.sum(p, axis=-1, keepdims=True)
                o = jnp.dot(p.astype(BF16), v, preferred_element_type=F32) / l
                lse = jnp.broadcast_to(m + jnp.log(l), (BAND, HEAD_DIM))
                if dil == 1:
                    og_ref[g, start:start + BAND, :] = o
                    lg_ref[g, start:start + BAND, :] = lse
                else:
                    og_ref[g, pl.ds(start, BAND, stride=dil), :] = o
                    lg_ref[g, pl.ds(start, BAND, stride=dil), :] = lse
        kt_refs[g][...] = k_refs[g][CHUNK - span:CHUNK, :]
        vt_refs[g][...] = v_refs[g][CHUNK - span:CHUNK, :]

    rows_per_step = 256

    def combine(i, carry):
        rows = pl.ds(pl.multiple_of(i * rows_per_step, rows_per_step), rows_per_step)
        l0, l1, l2 = lg_ref[0, rows, :], lg_ref[1, rows, :], lg_ref[2, rows, :]
        m = jnp.maximum(jnp.maximum(l0, l1), l2)
        e0, e1, e2 = jnp.exp(l0 - m), jnp.exp(l1 - m), jnp.exp(l2 - m)
        o = (e0 * og_ref[0, rows, :] + e1 * og_ref[1, rows, :] + e2 * og_ref[2, rows, :]) / (e0 + e1 + e2)
        o_ref[rows, :] = o.astype(o_ref.dtype)
        return carry

    lax.fori_loop(0, CHUNK // rows_per_step, combine, 0)


def _band_attention(qkv, slopes):
    seq = qkv.shape[0]
    blk = (CHUNK, HEAD_DIM)
    per_tensor = N_GROUPS * HEADS
    in_specs = [pl.BlockSpec((1, 8, HEAD_DIM), lambda h, c: (h, 0, 0))]
    for g in range(N_GROUPS):
        for tensor in range(3):
            in_specs.append(pl.BlockSpec(
                blk, lambda h, c, g=g, tensor=tensor: (c, tensor * per_tensor + g * HEADS + h)))
    tails = []
    for _, dil in DIL_GROUPS:
        tails += [pltpu.VMEM((BAND * dil, HEAD_DIM), F32)] * 2
    return pl.pallas_call(
        _band_kernel,
        grid=(HEADS, seq // CHUNK),
        in_specs=in_specs,
        out_specs=pl.BlockSpec(blk, lambda h, c: (c, h)),
        out_shape=jax.ShapeDtypeStruct((seq, GROUP_W), BF16),
        scratch_shapes=tails + [pltpu.VMEM((N_GROUPS, CHUNK, HEAD_DIM), F32),
                                pltpu.VMEM((N_GROUPS, CHUNK, HEAD_DIM), F32)],
        compiler_params=_params(2),
        name="band_attn",
    )(slopes, *([qkv] * 9))


def _mem_attn_kernel(q_ref, mk_ref, mv_ref, o_ref):
    for h in range(HEADS):
        hs = slice(h * HEAD_DIM, (h + 1) * HEAD_DIM)
        q = q_ref[:, hs]
        k = mk_ref[:, hs].astype(BF16)
        v = mv_ref[:, hs].astype(BF16)
        s = lax.dot_general(q, k, (((1,), (1,)), ((), ())), preferred_element_type=F32) * SCALE
        m = jnp.max(s, axis=-1, keepdims=True)
        p = jnp.exp(s - m)
        l = jnp.sum(p, axis=-1, keepdims=True)
        o = jnp.dot(p.astype(BF16), v, preferred_element_type=F32) / l
        o_ref[:, hs] = o.astype(o_ref.dtype)


def _mem_attention(p_all, mem_kv, *, tq):
    seq = p_all.shape[0]
    return pl.pallas_call(
        _mem_attn_kernel,
        grid=(seq // tq,),
        in_specs=[pl.BlockSpec((tq, GROUP_W), lambda i: (i, COL_QM // GROUP_W)),
                  pl.BlockSpec((MEM_TOKENS, GROUP_W), lambda i: (0, 0)),
                  pl.BlockSpec((MEM_TOKENS, GROUP_W), lambda i: (0, 1))],
        out_specs=pl.BlockSpec((tq, GROUP_W), lambda i: (i, 0)),
        out_shape=jax.ShapeDtypeStruct((seq, GROUP_W), BF16),
        compiler_params=_params(1),
        name="mem_attn",
    )(p_all, mem_kv, mem_kv)


def _conv_rows(b, c, v, w_ref, carry_ref):
    u = c.astype(F32) * v.astype(F32)
    c2 = carry_ref[0:1, :]
    c1 = carry_ref[1:2, :]
    row = lax.broadcasted_iota(jnp.int32, u.shape, 0)
    u1 = jnp.where(row == 0, c1, pltpu.roll(u, 1, 0))
    u2 = jnp.where(row == 0, c2, jnp.where(row == 1, c1, pltpu.roll(u, 2, 0)))
    return b.astype(F32) * (w_ref[0:1, :] * u2 + w_ref[1:2, :] * u1 + w_ref[2:3, :] * u), u


def _conv_kernel(b_ref, c_ref, v_ref, st_ref, w_ref, o_ref, ns_ref, carry_ref, *, t_last):
    i = pl.program_id(1)

    @pl.when(i == 0)
    def _():
        carry_ref[0:2, :] = st_ref[0]

    o, u = _conv_rows(b_ref[0], c_ref[0], v_ref[0], w_ref, carry_ref)
    o_ref[0] = o.astype(o_ref.dtype)
    last2 = u[t_last - 2:t_last, :]
    carry_ref[0:2, :] = last2
    ns_ref[0] = last2


def _conv(p3, state, conv_w, *, tm, t_last):
    nb, t = p3.shape[0], p3.shape[1]
    blk = (1, tm, CONV_W)
    return pl.pallas_call(
        functools.partial(_conv_kernel, t_last=t_last),
        grid=(nb, t // tm),
        in_specs=[
            pl.BlockSpec(blk, lambda b, i: (b, i, COL_B // CONV_W)),
            pl.BlockSpec(blk, lambda b, i: (b, i, COL_C // CONV_W)),
            pl.BlockSpec(blk, lambda b, i: (b, i, COL_V // CONV_W)),
            pl.BlockSpec((1, 2, CONV_W), lambda b, i: (b, 0, 0)),
            pl.BlockSpec((3, CONV_W), lambda b, i: (0, 0)),
        ],
        out_specs=[pl.BlockSpec(blk, lambda b, i: (b, i, 0)),
                   pl.BlockSpec((1, 2, CONV_W), lambda b, i: (b, 0, 0))],
        out_shape=[jax.ShapeDtypeStruct((nb, t, CONV_W), BF16),
                   jax.ShapeDtypeStruct((nb, 2, CONV_W), F32)],
        scratch_shapes=[pltpu.VMEM((8, CONV_W), F32)],
        compiler_params=_params(2),
        name="conv",
    )(p3, p3, p3, state, conv_w)


def _merge_kernel(*refs, sub_rows, fuse_conv):
    if fuse_conv:
        b_ref, c_ref, v_ref, cw_ref = refs[:4]
        refs = refs[4:]
    else:
        oc_ref = refs[0]
        refs = refs[1:]
    oa_ref, om_ref, gc_ref, ga_ref, gm_ref, h_ref, wc_ref, wa_ref, wm_ref, wo_ref, g_ref, beta_ref, y_ref = refs[:13]
    if fuse_conv:
        ns_ref, carry_ref = refs[13:]

        @pl.when(pl.program_id(0) == 0)
        def _():
            carry_ref[...] = jnp.zeros(carry_ref.shape, F32)

    for r0 in range(0, h_ref.shape[0], sub_rows):
        rows = slice(r0, r0 + sub_rows)
        if fuse_conv:
            oc, u = _conv_rows(b_ref[rows, :], c_ref[rows, :], v_ref[rows, :], cw_ref, carry_ref)
            oc = oc.astype(BF16)
            carry_ref[0:2, :] = u[sub_rows - 2:sub_rows, :]
        else:
            oc = oc_ref[rows, :]
        mc = jnp.dot(oc, wc_ref[...], preferred_element_type=F32)
        ma = jnp.dot(oa_ref[rows, :], wa_ref[...], preferred_element_type=F32)
        mm = jnp.dot(om_ref[rows, :], wm_ref[...], preferred_element_type=F32)
        mix = (jax.nn.sigmoid(gc_ref[rows, :].astype(F32)) * mc
               + jax.nn.sigmoid(ga_ref[rows, :].astype(F32)) * ma
               + jax.nn.sigmoid(gm_ref[rows, :].astype(F32)) * mm)
        z = ALPHA * h_ref[rows, :] + jnp.dot(mix.astype(BF16), wo_ref[...], preferred_element_type=F32)
        y_ref[rows, :] = _layer_norm_rows(z, g_ref[...], beta_ref[...])
    if fuse_conv:
        ns_ref[...] = carry_ref[0:2, :]


def _merge(o_conv, o_attn, o_mem, p_all, h, w_c, w_a, w_m, w_o, g, b, *, tm, conv_w=None):
    rows = h.shape[0]
    fuse_conv = o_conv is None
    single = pl.Buffered(1)
    row_blk = lambda w: pl.BlockSpec((tm, w), lambda i: (i, 0))
    col_blk = lambda w, col: pl.BlockSpec((tm, w), lambda i: (i, col // w))
    whole = lambda a: pl.BlockSpec(a.shape, lambda i: (0, 0), pipeline_mode=single)
    if fuse_conv:
        conv_specs = [col_blk(CONV_W, COL_B), col_blk(CONV_W, COL_C), col_blk(CONV_W, COL_V), whole(conv_w)]
        conv_args = [p_all, p_all, p_all, conv_w]
    else:
        conv_specs, conv_args = [row_blk(CONV_W)], [o_conv]
    out = pl.pallas_call(
        functools.partial(_merge_kernel, sub_rows=min(tm, 256), fuse_conv=fuse_conv),
        grid=(rows // tm,),
        in_specs=conv_specs + [row_blk(GROUP_W), row_blk(GROUP_W),
                               col_blk(D_MODEL, COL_GC), col_blk(D_MODEL, COL_GA), col_blk(D_MODEL, COL_GM),
                               row_blk(D_MODEL),
                               whole(w_c), whole(w_a), whole(w_m), whole(w_o), whole(g), whole(b)],
        out_specs=[row_blk(D_MODEL)] + ([pl.BlockSpec((2, CONV_W), lambda i: (0, 0))] if fuse_conv else []),
        out_shape=[jax.ShapeDtypeStruct((rows, D_MODEL), F32)]
                  + ([jax.ShapeDtypeStruct((2, CONV_W), F32)] if fuse_conv else []),
        scratch_shapes=[pltpu.VMEM((8, CONV_W), F32)] if fuse_conv else [],
        compiler_params=_params(1),
        name="merge",
    )(*conv_args, o_attn, o_mem, p_all, p_all, p_all, h, w_c, w_a, w_m, w_o, g, b)
    return out if fuse_conv else out[0]


def _over_quad(x, op):
    x = op(x, pltpu.roll(x, HEADS, 0))
    return op(x, pltpu.roll(x, 2 * HEADS, 0))


def _lane_dot(k3, q):
    return jnp.sum(k3 * q, axis=-1, keepdims=True) * SCALE


def _quad_attention(k3, v3, qt, bias3, k_new, v_new, bias_new):
    s = _lane_dot(k3, qt[None]) + bias3
    m = jnp.max(s, axis=0)
    if k_new is not None:
        s_new = _lane_dot(k_new, qt) + bias_new
        m = jnp.maximum(m, s_new)
    m = _over_quad(m, jnp.maximum)
    e = jnp.exp(s - m[None])
    l = jnp.sum(e, axis=0)
    acc = jnp.sum(e * v3, axis=0)
    if k_new is not None:
        e_new = jnp.exp(s_new - m)
        l = l + e_new
        acc = acc + e_new * v_new
    l = _over_quad(l, jnp.add)
    acc = _over_quad(acc, jnp.add)
    return acc / l, m + jnp.log(l)


def _sample_kernel(qd_ref, qt_ref, kn_ref, vn_ref, sl_ref,
                   k0_ref, v0_ref, k1_ref, v1_ref, k2_ref, v2_ref, mk_ref, mv_ref, *new_refs, t_new):
    k_refs = (k0_ref, k1_ref, k2_ref)
    v_refs = (v0_ref, v1_ref, v2_ref)
    oa_ref, om_ref = new_refs[N_GROUPS * 2], new_refs[N_GROUPS * 2 + 1]
    quad16 = lax.broadcasted_iota(jnp.int32, (ROWS16, HEAD_DIM), 0) >> 2

    outs, lses = [], []

    win0 = DIL_GROUPS[0][0]
    a_idx = lax.broadcasted_iota(jnp.int32, (win0 // 4, ROWS16, HEAD_DIM), 0)
    b_idx = lax.broadcasted_iota(jnp.int32, (win0 // 4, ROWS16, HEAD_DIM), 1) >> 2
    tok = a_idx * 4 + b_idx
    slope0 = sl_ref[0]
    k3, v3 = k0_ref[0], v0_ref[0]
    o0 = jnp.zeros((ROWS16, HEAD_DIM), F32)
    l0 = jnp.zeros((ROWS16, HEAD_DIM), F32)
    for t in range(t_new):
        bias3 = jnp.where(tok >= t, -(slope0[None] * (win0 + t - tok).astype(F32)), NEG_INF)
        bias_new = jnp.where(quad16 <= t, -(slope0 * (t - quad16).astype(F32)), NEG_INF)
        o_t, l_t = _quad_attention(k3, v3, qt_ref[0, 0, t], bias3, kn_ref[0, 0], vn_ref[0, 0], bias_new)
        o0 = jnp.where(quad16 == t, o_t, o0)
        l0 = jnp.where(quad16 == t, l_t, l0)
    outs.append(o0)
    lses.append(l0)

    for g in (1, 2):
        win, dil = DIL_GROUPS[g]
        q16 = qd_ref[0, g - 1]
        k3 = k_refs[g][0, :, 0:ROWS16, :]
        v3 = v_refs[g][0, :, 0:ROWS16, :]
        i_idx = lax.broadcasted_iota(jnp.int32, (BAND, ROWS16, HEAD_DIM), 0)
        bias3 = -(sl_ref[g][None] * (win - i_idx * dil).astype(F32))
        s = _lane_dot(k3, q16[None]) + bias3
        s_new = _lane_dot(kn_ref[0, g], q16)
        m = jnp.maximum(jnp.max(s, axis=0), s_new)
        e = jnp.exp(s - m[None])
        e_new = jnp.exp(s_new - m)
        l = jnp.sum(e, axis=0) + e_new
        acc = jnp.sum(e * v3, axis=0) + e_new * vn_ref[0, g]
        outs.append(acc / l)
        lses.append(m + jnp.log(l))

    m = jnp.maximum(jnp.maximum(lses[0], lses[1]), lses[2])
    w = [jnp.exp(x - m) for x in lses]
    oa_ref[0] = (w[0] * outs[0] + w[1] * outs[1] + w[2] * outs[2]) / (w[0] + w[1] + w[2])

    k3, v3 = mk_ref[0], mv_ref[0]
    om = jnp.zeros((ROWS16, HEAD_DIM), F32)
    for t in range(t_new):
        o_t, _ = _quad_attention(k3, v3, qt_ref[0, 1, t], jnp.zeros((1, ROWS16, HEAD_DIM), F32), None, None, None)
        om = jnp.where(quad16 == t, o_t, om)
    om_ref[0] = om

    for i, out_ref in enumerate(new_refs[N_GROUPS * 2 + 2:]):
        out_ref[0, 0] = (kn_ref, vn_ref)[i % 2][0, i // 2]


def _sample_attention(qd, qt, kn, vn, slopes16, k_states, v_states, shifted, mem_k, mem_v, *, t_new):
    nb = qd.shape[0]
    states = []
    for g in range(N_GROUPS):
        states += [k_states[g], v_states[g]]
    whole = lambda a: pl.BlockSpec((1,) + a.shape[1:], lambda b: (b,) + (0,) * (a.ndim - 1))
    head_rows = lambda a: pl.BlockSpec((1, a.shape[1], ROWS16, HEAD_DIM), lambda b: (b, 0, 0, 0))
    last_rows = lambda a: pl.BlockSpec((1, 1, ROWS16, HEAD_DIM),
                                       lambda b, a=a: (b, a.shape[1] - 1, a.shape[2] // ROWS16 - 1, 0))
    o_spec = pl.BlockSpec((1, ROWS16, HEAD_DIM), lambda b: (b, 0, 0))
    o_shape = jax.ShapeDtypeStruct((nb, ROWS16, HEAD_DIM), F32)
    n_in = 5 + len(states) + 2
    return pl.pallas_call(
        functools.partial(_sample_kernel, t_new=t_new),
        grid=(nb,),
        in_specs=[whole(qd), whole(qt), whole(kn), whole(vn),
                  pl.BlockSpec(slopes16.shape, lambda b: (0, 0, 0))]
                 + [head_rows(s) for s in states] + [whole(mem_k), whole(mem_v)]
                 + [pl.BlockSpec(memory_space=pl.ANY)] * len(shifted),
        out_specs=[o_spec, o_spec] + [last_rows(s) for s in shifted],
        out_shape=[o_shape, o_shape] + [jax.ShapeDtypeStruct(s.shape, F32) for s in shifted],
        input_output_aliases={n_in + i: 2 + i for i in range(len(shifted))},
        compiler_params=_params(1),
        name="sample_attn",
    )(qd, qt, kn, vn, slopes16, *states, mem_k, mem_v, *shifted)


def kernel(x_prompt, x_sample, mem_prompt, state_conv, state_k_w128, state_v_w128, state_k_w512, state_v_w512, state_k_w2048, state_v_w2048, cache_mem_k, cache_mem_v, ln1_g, ln1_b, w_ffn1_gu, w_ffn1_down, w_in, conv_w, w_mem_kv, w_br_conv, w_br_attn, w_br_mem, w_o, ln2_g, ln2_b, w_ffn2_gu, w_ffn2_down, ln3_g, ln3_b):
    depth = w_in.shape[0]
    seq = x_prompt.shape[1]
    nb, t_new = x_sample.shape[0], x_sample.shape[1]
    assert depth == 1 and x_prompt.shape[0] == 1 and seq % CHUNK == 0
    assert t_new * HEADS == ROWS16 and t_new <= DIL_GROUPS[1][1]

    wb = lambda w: w[0].astype(BF16)
    wc_b, wa_b, wm_b, wo_b = wb(w_br_conv), wb(w_br_attn), wb(w_br_mem), wb(w_o)
    n_ff_blocks = D_FF // 512
    cw = conv_w[0]
    slopes_lane = np.broadcast_to(SLOPES[:, :, None], (N_GROUPS, HEADS, HEAD_DIM))
    slopes_band = jnp.asarray(np.pad(slopes_lane.transpose(1, 0, 2), ((0, 0), (0, 8 - N_GROUPS), (0, 0))))
    slopes16 = jnp.asarray(np.tile(slopes_lane, (1, ROWS16 // HEADS, 1)))

    rows_s = nb * SAMPLE_PAD
    xs = jnp.pad(x_sample, ((0, 0), (0, SAMPLE_PAD - t_new), (0, 0))).reshape(rows_s, D_MODEL)
    s1, s1b, w1g, w1u, w1d = _ffn(xs, w_ffn1_gu[0], w_ffn1_gu[0], w_ffn1_down[0], ln1_g, ln1_b,
                                  tm=rows_s, tf=512, up_blk0=n_ff_blocks, emit_bf16=True)
    ps_all, win_b = _matmul(s1b, w_in[0], tm=rows_s, tn=1024, out_dtype=F32, name="in_proj_s", emit_bf16=True)

    def tiles(state):
        win = state.shape[2]
        per_tile = 4 if win <= DIL_GROUPS[1][0] else DIL_GROUPS[2][1]
        return state.reshape(nb, win // per_tile, per_tile * HEADS, HEAD_DIM)

    k_states = [tiles(s) for s in (state_k_w128, state_k_w512, state_k_w2048)]
    v_states = [tiles(s) for s in (state_v_w128, state_v_w512, state_v_w2048)]
    h1, h1b, *shifted = _ffn(x_prompt[0], w1g, w1u, w1d, ln1_g, ln1_b, tm=1024, tf=512, manual_rows=True,
                             windows=k_states + v_states)
    shifted = [shifted[g + part * N_GROUPS] for g in range(N_GROUPS) for part in (0, 1)]
    p_all, qkv = _in_proj(h1b, win_b, tm=1024, tn=1024)
    mem_kv = _matmul(mem_prompt[0].astype(BF16), w_mem_kv[0], tm=MEM_TOKENS, tn=512, out_dtype=F32, name="mem_kv")
    o_attn = _band_attention(qkv, slopes_band)
    o_mem = _mem_attention(p_all, mem_kv, tq=512)
    h2, conv_p = _merge(None, o_attn, o_mem, p_all, h1, wc_b, wa_b, wm_b, wo_b, ln2_g, ln2_b, tm=256, conv_w=cw)

    prompt_states = []
    for g, (win, _) in enumerate(DIL_GROUPS):
        for tensor in (1, 2):
            c0 = tensor * N_GROUPS * GROUP_W + g * GROUP_W
            prompt_states.append(qkv[seq - win:, c0:c0 + GROUP_W].reshape(1, 1, win, HEADS, HEAD_DIM))
    mem_k_p = mem_kv[:, :GROUP_W].reshape(1, 1, MEM_TOKENS, HEADS, HEAD_DIM)
    mem_v_p = mem_kv[:, GROUP_W:].reshape(1, 1, MEM_TOKENS, HEADS, HEAD_DIM)

    ps3 = ps_all.reshape(nb, SAMPLE_PAD, IN_W)

    real = ps3[:, :t_new]
    heads = lambda a, n: a.reshape(nb, t_new, n, HEADS, HEAD_DIM)
    q, k, v = (heads(real[..., c:c + N_GROUPS * GROUP_W], N_GROUPS) for c in (COL_Q, COL_K, COL_VA))
    rows_th = lambda a: a.transpose(0, 2, 1, 3, 4).reshape(nb, a.shape[2], ROWS16, HEAD_DIM)
    over_quad = lambda a: jnp.broadcast_to(a[:, :, None], (nb, t_new, ROWS16 // HEADS, HEADS, HEAD_DIM)
                                           ).reshape(nb, t_new, ROWS16, HEAD_DIM)
    qd = rows_th(q[:, :, 1:])
    qm = real[..., COL_QM:COL_QM + GROUP_W].reshape(nb, t_new, HEADS, HEAD_DIM)
    qt = jnp.stack([over_quad(q[:, :, 0]), over_quad(qm)], axis=1)
    kn, vn = rows_th(k), rows_th(v)

    outs = _sample_attention(qd, qt, kn, vn, slopes16, k_states, v_states, shifted,
                             tiles(cache_mem_k), tiles(cache_mem_v), t_new=t_new)
    pad_rows = lambda a, w: jnp.pad(a.reshape(nb, t_new, w), ((0, 0), (0, SAMPLE_PAD - t_new), (0, 0))
                                    ).reshape(rows_s, w).astype(BF16)
    o_attn_s, o_mem_s = pad_rows(outs[0], GROUP_W), pad_rows(outs[1], GROUP_W)
    sample_states = [s.reshape(1, nb, -1, HEADS, HEAD_DIM) for s in outs[2:]]

    o_conv_s, conv_s = _conv(ps3, state_conv[0], cw, tm=SAMPLE_PAD, t_last=t_new)
    s2 = _merge(o_conv_s.reshape(rows_s, CONV_W), o_attn_s, o_mem_s, ps_all, s1,
                wc_b, wa_b, wm_b, wo_b, ln2_g, ln2_b, tm=rows_s)
    y_s, w2g, w2u, w2d = _ffn(s2, w_ffn2_gu[0], w_ffn2_gu[0], w_ffn2_down[0], ln3_g, ln3_b,
                              tm=rows_s, tf=512, up_blk0=n_ff_blocks, emit_bf16=True, bf16_out=False)
    y_sample = y_s.reshape(nb, SAMPLE_PAD, D_MODEL)[:, :t_new]
    (y_prompt,) = _ffn(h2, w2g, w2u, w2d, ln3_g, ln3_b, tm=1024, tf=512, bf16_out=False, row_buffers=2)

    return (y_prompt[None], y_sample, conv_p[None, None], *prompt_states, mem_k_p, mem_v_p,
            conv_s[None], *sample_states)
```

```python
import functools

import jax
import jax.numpy as jnp
import numpy as np
from jax import lax
from jax.experimental import pallas as pl
from jax.experimental.pallas import tpu as pltpu

F32 = jnp.float32
BF16 = jnp.bfloat16

D_MODEL = 2048
D_FF = 5632
HEAD_DIM = 128
HEADS = 4
GROUP_W = HEADS * HEAD_DIM
DIL_GROUPS = ((128, 1), (512, 4), (2048, 16))
N_GROUPS = len(DIL_GROUPS)
BAND = 128
CHUNK = BAND * DIL_GROUPS[-1][1]
CONV_W = 1024
MEM_TOKENS = 256
IN_W = 14336
COL_B, COL_C, COL_V = 0, 1024, 2048
COL_Q, COL_K, COL_VA = 3072, 4608, 6144
COL_QM = 7680
COL_GC, COL_GA, COL_GM = 8192, 10240, 12288
QKV_W = COL_QM - COL_Q
ALPHA = 2.0 ** 0.25
LN_EPS = 1e-5
NEG_INF = -1e30
SCALE = HEAD_DIM ** -0.5
LOG2E = 1.4426950408889634
LN2 = 0.6931471805599453
SLOPES = np.exp2(np.float32(-8.0) * np.arange(1, 13, dtype=np.float32) / np.float32(12)).reshape(3, 4)

SAMPLE_PAD = 8
ROWS16 = 16
VMEM_LIMIT = 56 * 1024 * 1024


def _params(n_axes, vmem=VMEM_LIMIT):
    return pltpu.CompilerParams(dimension_semantics=("arbitrary",) * n_axes, vmem_limit_bytes=vmem)


def _layer_norm_rows(z, g, b):
    mu = jnp.mean(z, axis=-1, keepdims=True)
    zc = z - mu
    var = jnp.mean(zc * zc, axis=-1, keepdims=True)
    return zc * lax.rsqrt(var + LN_EPS) * g + b


def _window_loads(b, src_refs, stage_refs, slot, sems):
    return [pltpu.make_async_copy(src.at[b], stage.at[slot], sems.at[slot, n])
            for n, (src, stage) in enumerate(zip(src_refs, stage_refs))]


def _window_stores(b, stage_refs, dst_refs, slot, sems):
    copies = []
    for stage, dst in zip(stage_refs, dst_refs):
        tiles, rows = dst.shape[1], dst.shape[2]
        keep = rows - ROWS16
        if keep:
            copies.append((stage.at[slot, :, pl.ds(ROWS16, keep), :], dst.at[b, :, pl.ds(0, keep), :]))
        copies.append((stage.at[slot, pl.ds(1, tiles - 1), pl.ds(0, ROWS16), :],
                       dst.at[b, pl.ds(0, tiles - 1), pl.ds(keep, ROWS16), :]))
        copies.append((stage.at[slot, pl.ds(tiles - 1, 1), pl.ds(keep, ROWS16), :],
                       dst.at[b, pl.ds(tiles - 1, 1), pl.ds(keep, ROWS16), :]))
    return [pltpu.make_async_copy(s, d, sems.at[slot, n]) for n, (s, d) in enumerate(copies)]


def _ffn_kernel(*refs, n_ff, row_chunk, emit_bf16, bf16_out, manual_rows, n_windows, n_requests):
    x_ref, wg_ref, wu_ref, wd_ref, g_ref, b_ref = refs[:6]
    win_src = refs[6:6 + n_windows]
    y_ref = refs[6 + n_windows]
    yb_ref = refs[7 + n_windows] if bf16_out else None
    rest = refs[7 + bf16_out + n_windows:]
    if emit_bf16:
        wb_refs, rest = rest[:3], rest[3:]
    win_dst, rest = rest[:n_windows], rest[n_windows:]
    xb_ref, rest = rest[0], rest[1:]
    i, j = pl.program_id(0), pl.program_id(1)
    n_tiles = pl.num_programs(0)
    tm = xb_ref.shape[0]

    if manual_rows:
        y_hbm, yb_hbm = y_ref, yb_ref
        n_ring = 1 + bf16_out
        xbuf_ref, xsem, y_ref = rest[:3]
        ring_refs, out_sems = rest[3:3 + n_ring], rest[3 + n_ring]
        rest = rest[4 + n_ring:]

        def x_copy(tile):
            return pltpu.make_async_copy(x_ref.at[pl.ds(pl.multiple_of(tile * tm, tm), tm), :], xbuf_ref, xsem.at[0])

        @pl.when((i == 0) & (j == 0))
        def _():
            x_copy(0).start()

    if n_windows:
        half = n_windows // 2
        stage_refs, load_sems, store_sems = rest[:half], rest[half], rest[half + 1]
        step = i * n_ff + j
        n_items = 2 * n_requests
        for part in (0, 1):
            srcs, dsts = win_src[part * half:(part + 1) * half], win_dst[part * half:(part + 1) * half]
            mine = step % 2 == part

            @pl.when(mine & (step >= 2) & (step < n_items + 2))
            def _():
                for copy in _window_stores((step - 2) // 2, stage_refs, dsts, part, store_sems):
                    copy.wait()

            @pl.when(mine & (step < n_items))
            def _():
                for copy in _window_loads(step // 2, srcs, stage_refs, part, load_sems):
                    copy.start()

            @pl.when(jnp.logical_not(mine) & (step >= 1) & (step < n_items + 1))
            def _():
                for copy in _window_loads((step - 1) // 2, srcs, stage_refs, part, load_sems):
                    copy.wait()
                for copy in _window_stores((step - 1) // 2, stage_refs, dsts, part, store_sems):
                    copy.start()

    @pl.when(j == 0)
    def _():
        if manual_rows:
            x_copy(i).wait()
        x = xbuf_ref[...] if manual_rows else x_ref[...]
        xb_ref[...] = x.astype(BF16)
        y_ref[...] = (2.0 * ALPHA) * x

    if manual_rows:
        @pl.when((j == 1) & (i + 1 < n_tiles))
        def _():
            x_copy(i + 1).start()

    wg, wu, wd = wg_ref[...], wu_ref[...], wd_ref[...]
    if emit_bf16:
        wg, wu, wd = wg.astype(BF16), wu.astype(BF16), wd.astype(BF16)
        wb_refs[0][...], wb_refs[1][...], wb_refs[2][...] = wg, wu, wd
    xb = xb_ref[...]
    gate = jnp.dot(xb, wg, preferred_element_type=F32)
    up = jnp.dot(xb, wu, preferred_element_type=F32)
    act = (gate * jax.nn.sigmoid(gate) * up).astype(BF16)
    y_ref[...] += jnp.dot(act, wd, preferred_element_type=F32)

    if not manual_rows:
        @pl.when(j == n_ff - 1)
        def _():
            def chunk(c, carry):
                rows = pl.ds(pl.multiple_of(c * row_chunk, row_chunk), row_chunk)
                y = _layer_norm_rows(0.5 * y_ref[rows, :], g_ref[...], b_ref[...])
                y_ref[rows, :] = y
                if bf16_out:
                    yb_ref[rows, :] = y.astype(BF16)
                return carry
            lax.fori_loop(0, tm // row_chunk, chunk, 0)
    else:
        def out_copies(slot, row0):
            dsts = (y_hbm, yb_hbm)[:n_ring]
            return [pltpu.make_async_copy(ring.at[slot], dst.at[pl.ds(row0, row_chunk), :], out_sems.at[slot, n])
                    for n, (ring, dst) in enumerate(zip(ring_refs, dsts))]

        def wait_slot(slot):
            for copy in out_copies(slot, 0):
                copy.wait()

        @pl.when(j == n_ff - 1)
        def _():
            n_chunks = tm // row_chunk
            for c in range(n_chunks):
                slot = c % 2
                if c >= 2:
                    wait_slot(slot)
                else:
                    pl.when(i > 0)(functools.partial(wait_slot, slot))
                y = _layer_norm_rows(0.5 * y_ref[c * row_chunk:(c + 1) * row_chunk, :], g_ref[...], b_ref[...])
                ring_refs[0][slot] = y
                if bf16_out:
                    ring_refs[1][slot] = y.astype(BF16)
                for copy in out_copies(slot, pl.multiple_of(i * tm, tm) + c * row_chunk):
                    copy.start()

            @pl.when(i == n_tiles - 1)
            def _():
                wait_slot(0)
                wait_slot(1)


def _ffn(x, w_gate, w_up, w_down, g, b, *, tm, tf, up_blk0=0, emit_bf16=False, bf16_out=True,
         row_buffers=1, manual_rows=False, windows=()):
    rows = x.shape[0]
    n_ff = D_FF // tf
    n_steps = (rows // tm) * n_ff
    n_requests = windows[0].shape[0] if windows else 0
    assert not emit_bf16 or rows == tm
    assert n_steps >= 2 * n_requests + 2
    row_mode = pl.Buffered(1) if (rows == tm or row_buffers == 1) else None
    hbm = pl.BlockSpec(memory_space=pl.ANY)
    w_col = pl.BlockSpec((D_MODEL, tf), lambda i, j: (0, j))
    w_col_tile = pl.BlockSpec((None, D_MODEL, tf), lambda i, j: (j, 0, 0))
    tiled = w_gate.ndim == 3
    w_row = pl.BlockSpec((tf, D_MODEL), lambda i, j: (j, 0))
    row_tile = pl.BlockSpec((tm, D_MODEL), lambda i, j: (i, 0), pipeline_mode=row_mode)
    out_specs = [hbm if manual_rows else row_tile] * (1 + bf16_out)
    out_shape = [jax.ShapeDtypeStruct((rows, D_MODEL), F32), jax.ShapeDtypeStruct((rows, D_MODEL), BF16)][:1 + bf16_out]
    scratch = [pltpu.VMEM((tm, D_MODEL), BF16)]
    if manual_rows:
        row_chunk = min(tm, 128)
        assert n_ff >= 2 and (tm // row_chunk) % 2 == 0
        scratch += [pltpu.VMEM((tm, D_MODEL), F32), pltpu.SemaphoreType.DMA((1,)), pltpu.VMEM((tm, D_MODEL), F32)]
        scratch += [pltpu.VMEM((2, row_chunk, D_MODEL), dt) for dt in (F32, BF16)[:1 + bf16_out]]
        scratch += [pltpu.SemaphoreType.DMA((2, 1 + bf16_out))]
    if emit_bf16:
        out_specs += [w_col_tile, w_col_tile, w_row]
        out_shape += [jax.ShapeDtypeStruct((n_ff, D_MODEL, tf), BF16)] * 2 + [jax.ShapeDtypeStruct((D_FF, D_MODEL), BF16)]
    if windows:
        out_specs += [hbm] * len(windows)
        out_shape += [jax.ShapeDtypeStruct(w.shape, w.dtype) for w in windows]
        half = windows[:len(windows) // 2]
        scratch += [pltpu.VMEM((2,) + w.shape[1:], w.dtype) for w in half]
        scratch += [pltpu.SemaphoreType.DMA((2, len(half))),
                    pltpu.SemaphoreType.DMA((2, sum(2 + (w.shape[2] > ROWS16) for w in half)))]
    return pl.pallas_call(
        functools.partial(_ffn_kernel, n_ff=n_ff, row_chunk=min(tm, 128), emit_bf16=emit_bf16, bf16_out=bf16_out,
                          manual_rows=manual_rows, n_windows=len(windows), n_requests=n_requests),
        grid=(rows // tm, n_ff),
        in_specs=[
            hbm if manual_rows else row_tile,
            w_col_tile if tiled else w_col,
            w_col_tile if tiled else pl.BlockSpec((D_MODEL, tf), lambda i, j: (0, j + up_blk0)),
            w_row,
            pl.BlockSpec((1, D_MODEL), lambda i, j: (0, 0)),
            pl.BlockSpec((1, D_MODEL), lambda i, j: (0, 0)),
        ] + [hbm] * len(windows),
        out_specs=out_specs,
        out_shape=out_shape,
        scratch_shapes=scratch,
        compiler_params=_params(2),
        name="ffn_cast" if emit_bf16 else "ffn",
    )(x, w_gate, w_up, w_down, g, b, *windows)


def _mm_kernel(x_ref, w_ref, o_ref, *wb_ref):
    w = w_ref[...].astype(BF16)
    if wb_ref:
        wb_ref[0][...] = w
    o_ref[...] = jnp.dot(x_ref[...], w, preferred_element_type=F32).astype(o_ref.dtype)


def _matmul(x, w, *, tm, tn, out_dtype, name, emit_bf16=False):
    rows, k = x.shape
    n_cols = w.shape[1]
    assert not emit_bf16 or rows == tm
    w_spec = pl.BlockSpec((k, tn), lambda i, j: (0, j))
    out_specs = [pl.BlockSpec((tm, tn), lambda i, j: (i, j))]
    out_shape = [jax.ShapeDtypeStruct((rows, n_cols), out_dtype)]
    if emit_bf16:
        out_specs.append(pl.BlockSpec((None, k, tn), lambda i, j: (j, 0, 0)))
        out_shape.append(jax.ShapeDtypeStruct((n_cols // tn, k, tn), BF16))
    out = pl.pallas_call(
        _mm_kernel,
        grid=(rows // tm, n_cols // tn),
        in_specs=[pl.BlockSpec((tm, k), lambda i, j: (i, 0)), w_spec],
        out_specs=out_specs,
        out_shape=out_shape,
        compiler_params=_params(2),
        name=name,
    )(x, w)
    return out if emit_bf16 else out[0]


def _in_proj_kernel(x_ref, w_ref, p_ref, qkv_ref, *, n_own):
    j = pl.program_id(1)
    acc = jnp.dot(x_ref[...], w_ref[...], preferred_element_type=F32)
    p_ref[...] = acc.astype(BF16)

    @pl.when(j < n_own)
    def _():
        qkv_ref[...] = acc


def _in_proj(x, w, *, tm, tn):
    rows, k = x.shape
    assert w.shape == (IN_W // tn, k, tn) and COL_Q % tn == 0 and COL_GC % tn == 0
    c_lo, n_own = COL_Q // tn, (COL_GC - COL_Q) // tn

    def col_block(j):
        return jnp.where(j < n_own, j + c_lo, jnp.where(j < n_own + c_lo, j - n_own, j))

    return pl.pallas_call(
        functools.partial(_in_proj_kernel, n_own=n_own),
        grid=(rows // tm, IN_W // tn),
        in_specs=[pl.BlockSpec((tm, k), lambda i, j: (i, 0)),
                  pl.BlockSpec((None, k, tn), lambda i, j: (col_block(j), 0, 0))],
        out_specs=[pl.BlockSpec((tm, tn), lambda i, j: (i, col_block(j))),
                   pl.BlockSpec((tm, tn), lambda i, j: (i, jnp.minimum(j, n_own - 1)))],
        out_shape=[jax.ShapeDtypeStruct((rows, IN_W), BF16),
                   jax.ShapeDtypeStruct((rows, COL_GC - COL_Q), F32)],
        compiler_params=_params(2),
        name="in_proj",
    )(x, w)


def _rows(ref, start, dil):
    if dil == 1:
        return ref[start:start + BAND, :]
    return ref[pl.ds(start, BAND, stride=dil), :]


def _band_kernel(sl_ref, q0_ref, k0_ref, v0_ref, q1_ref, k1_ref, v1_ref, q2_ref, k2_ref, v2_ref,
                 o_ref, kt0_ref, vt0_ref, kt1_ref, vt1_ref, kt2_ref, vt2_ref, og_ref, lg_ref):
    c = pl.program_id(1)
    q_refs = (q0_ref, q1_ref, q2_ref)
    k_refs = (k0_ref, k1_ref, k2_ref)
    v_refs = (v0_ref, v1_ref, v2_ref)
    kt_refs = (kt0_ref, kt1_ref, kt2_ref)
    vt_refs = (vt0_ref, vt1_ref, vt2_ref)

    @pl.when(c == 0)
    def _():
        for ref in kt_refs + vt_refs:
            ref[...] = jnp.zeros(ref.shape, F32)

    qi = lax.broadcasted_iota(jnp.int32, (BAND, 2 * BAND), 0)
    ki = lax.broadcasted_iota(jnp.int32, (BAND, 2 * BAND), 1)
    rel = BAND + qi - ki
    valid = (rel >= 0) & (rel <= BAND)
    no_prev = jnp.where(ki < BAND, jnp.where(c == 0, NEG_INF, 0.0), 0.0)

    for g, (_, dil) in enumerate(DIL_GROUPS):
        slope = sl_ref[0, g:g + 1, 0:1] * LOG2E
        bias = jnp.where(valid, -(slope * (rel * dil).astype(F32)), NEG_INF)
        bias_first = bias + no_prev
        span = BAND * dil
        for r in range(dil):
            for n in range(CHUNK // span):
                start = r + n * span
                q = _rows(q_refs[g], start, dil).astype(BF16)
                if n == 0:
                    k_prev, v_prev = _rows(kt_refs[g], r, dil), _rows(vt_refs[g], r, dil)
                else:
                    k_prev, v_prev = _rows(k_refs[g], start - span, dil), _rows(v_refs[g], start - span, dil)
                k = jnp.concatenate([k_prev, _rows(k_refs[g], start, dil)], axis=0).astype(BF16)
                v = jnp.concatenate([v_prev, _rows(v_refs[g], start, dil)], axis=0).astype(BF16)
                s = lax.dot_general(q, k, (((1,), (1,)), ((), ())), preferred_element_type=F32) * (SCALE * LOG2E)
                s = s + (bias_first if n == 0 else bias)
                m = jnp.max(s, axis=-1, keepdims=True)
                p = jnp.exp2(s - m)
                l = jnp.sum(p, axis=-1, keepdims=True)
                o = jnp.dot(p.astype(BF16), v, preferred_element_type=F32) / l
                lse = jnp.broadcast_to(m * LN2 + jnp.log(l), (BAND, HEAD_DIM))
                if dil == 1:
                    og_ref[g, start:start + BAND, :] = o
                    lg_ref[g, start:start + BAND, :] = lse
                else:
                    og_ref[g, pl.ds(start, BAND, stride=dil), :] = o
                    lg_ref[g, pl.ds(start, BAND, stride=dil), :] = lse
        kt_refs[g][...] = k_refs[g][CHUNK - span:CHUNK, :]
        vt_refs[g][...] = v_refs[g][CHUNK - span:CHUNK, :]

    rows_per_step = 256

    def combine(i, carry):
        rows = pl.ds(pl.multiple_of(i * rows_per_step, rows_per_step), rows_per_step)
        l0, l1, l2 = lg_ref[0, rows, :], lg_ref[1, rows, :], lg_ref[2, rows, :]
        m = jnp.maximum(jnp.maximum(l0, l1), l2)
        e0, e1, e2 = jnp.exp(l0 - m), jnp.exp(l1 - m), jnp.exp(l2 - m)
        o = (e0 * og_ref[0, rows, :] + e1 * og_ref[1, rows, :] + e2 * og_ref[2, rows, :]) / (e0 + e1 + e2)
        o_ref[rows, :] = o.astype(o_ref.dtype)
        return carry

    lax.fori_loop(0, CHUNK // rows_per_step, combine, 0)


def _band_attention(qkv, slopes):
    seq = qkv.shape[0]
    blk = (CHUNK, HEAD_DIM)
    per_tensor = N_GROUPS * HEADS
    in_specs = [pl.BlockSpec((1, 8, HEAD_DIM), lambda h, c: (h, 0, 0))]
    for g in range(N_GROUPS):
        for tensor in range(3):
            in_specs.append(pl.BlockSpec(
                blk, lambda h, c, g=g, tensor=tensor: (c, tensor * per_tensor + g * HEADS + h)))
    tails = []
    for _, dil in DIL_GROUPS:
        tails += [pltpu.VMEM((BAND * dil, HEAD_DIM), F32)] * 2
    return pl.pallas_call(
        _band_kernel,
        grid=(HEADS, seq // CHUNK),
        in_specs=in_specs,
        out_specs=pl.BlockSpec(blk, lambda h, c: (c, h)),
        out_shape=jax.ShapeDtypeStruct((seq, GROUP_W), BF16),
        scratch_shapes=tails + [pltpu.VMEM((N_GROUPS, CHUNK, HEAD_DIM), F32),
                                pltpu.VMEM((N_GROUPS, CHUNK, HEAD_DIM), F32)],
        compiler_params=_params(2),
        name="band_attn",
    )(slopes, *([qkv] * 9))


def _mem_attn_kernel(q_ref, mk_ref, mv_ref, o_ref):
    for h in range(HEADS):
        hs = slice(h * HEAD_DIM, (h + 1) * HEAD_DIM)
        q = q_ref[:, hs]
        k = mk_ref[:, hs].astype(BF16)
        v = mv_ref[:, hs].astype(BF16)
        s = lax.dot_general(q, k, (((1,), (1,)), ((), ())), preferred_element_type=F32) * SCALE
        m = jnp.max(s, axis=-1, keepdims=True)
        p = jnp.exp(s - m)
        l = jnp.sum(p, axis=-1, keepdims=True)
        o = jnp.dot(p.astype(BF16), v, preferred_element_type=F32) / l
        o_ref[:, hs] = o.astype(o_ref.dtype)


def _mem_attention(p_all, mem_kv, *, tq):
    seq = p_all.shape[0]
    return pl.pallas_call(
        _mem_attn_kernel,
        grid=(seq // tq,),
        in_specs=[pl.BlockSpec((tq, GROUP_W), lambda i: (i, COL_QM // GROUP_W)),
                  pl.BlockSpec((MEM_TOKENS, GROUP_W), lambda i: (0, 0)),
                  pl.BlockSpec((MEM_TOKENS, GROUP_W), lambda i: (0, 1))],
        out_specs=pl.BlockSpec((tq, GROUP_W), lambda i: (i, 0)),
        out_shape=jax.ShapeDtypeStruct((seq, GROUP_W), BF16),
        compiler_params=_params(1),
        name="mem_attn",
    )(p_all, mem_kv, mem_kv)


def _conv_rows(b, c, v, w_ref, carry_ref):
    u = c.astype(F32) * v.astype(F32)
    c2 = carry_ref[0:1, :]
    c1 = carry_ref[1:2, :]
    row = lax.broadcasted_iota(jnp.int32, u.shape, 0)
    u1 = jnp.where(row == 0, c1, pltpu.roll(u, 1, 0))
    u2 = jnp.where(row == 0, c2, jnp.where(row == 1, c1, pltpu.roll(u, 2, 0)))
    return b.astype(F32) * (w_ref[0:1, :] * u2 + w_ref[1:2, :] * u1 + w_ref[2:3, :] * u), u


def _conv_kernel(b_ref, c_ref, v_ref, st_ref, w_ref, o_ref, ns_ref, carry_ref, *, t_last):
    i = pl.program_id(1)

    @pl.when(i == 0)
    def _():
        carry_ref[0:2, :] = st_ref[0]

    o, u = _conv_rows(b_ref[0], c_ref[0], v_ref[0], w_ref, carry_ref)
    o_ref[0] = o.astype(o_ref.dtype)
    last2 = u[t_last - 2:t_last, :]
    carry_ref[0:2, :] = last2
    ns_ref[0] = last2


def _conv(p3, state, conv_w, *, tm, t_last):
    nb, t = p3.shape[0], p3.shape[1]
    blk = (1, tm, CONV_W)
    return pl.pallas_call(
        functools.partial(_conv_kernel, t_last=t_last),
        grid=(nb, t // tm),
        in_specs=[
            pl.BlockSpec(blk, lambda b, i: (b, i, COL_B // CONV_W)),
            pl.BlockSpec(blk, lambda b, i: (b, i, COL_C // CONV_W)),
            pl.BlockSpec(blk, lambda b, i: (b, i, COL_V // CONV_W)),
            pl.BlockSpec((1, 2, CONV_W), lambda b, i: (b, 0, 0)),
            pl.BlockSpec((3, CONV_W), lambda b, i: (0, 0)),
        ],
        out_specs=[pl.BlockSpec(blk, lambda b, i: (b, i, 0)),
                   pl.BlockSpec((1, 2, CONV_W), lambda b, i: (b, 0, 0))],
        out_shape=[jax.ShapeDtypeStruct((nb, t, CONV_W), BF16),
                   jax.ShapeDtypeStruct((nb, 2, CONV_W), F32)],
        scratch_shapes=[pltpu.VMEM((8, CONV_W), F32)],
        compiler_params=_params(2),
        name="conv",
    )(p3, p3, p3, state, conv_w)


def _merge_kernel(*refs, sub_rows, fuse_conv):
    if fuse_conv:
        b_ref, c_ref, v_ref, cw_ref = refs[:4]
        refs = refs[4:]
    else:
        oc_ref = refs[0]
        refs = refs[1:]
    oa_ref, om_ref, gc_ref, ga_ref, gm_ref, h_ref, wc_ref, wa_ref, wm_ref, wo_ref, g_ref, beta_ref, y_ref = refs[:13]
    if fuse_conv:
        ns_ref, carry_ref = refs[13:]

        @pl.when(pl.program_id(0) == 0)
        def _():
            carry_ref[...] = jnp.zeros(carry_ref.shape, F32)

    for r0 in range(0, h_ref.shape[0], sub_rows):
        rows = slice(r0, r0 + sub_rows)
        if fuse_conv:
            oc, u = _conv_rows(b_ref[rows, :], c_ref[rows, :], v_ref[rows, :], cw_ref, carry_ref)
            oc = oc.astype(BF16)
            carry_ref[0:2, :] = u[sub_rows - 2:sub_rows, :]
        else:
            oc = oc_ref[rows, :]
        mc = jnp.dot(oc, wc_ref[...], preferred_element_type=F32)
        ma = jnp.dot(oa_ref[rows, :], wa_ref[...], preferred_element_type=F32)
        mm = jnp.dot(om_ref[rows, :], wm_ref[...], preferred_element_type=F32)
        mix = (jax.nn.sigmoid(gc_ref[rows, :].astype(F32)) * mc
               + jax.nn.sigmoid(ga_ref[rows, :].astype(F32)) * ma
               + jax.nn.sigmoid(gm_ref[rows, :].astype(F32)) * mm)
        z = ALPHA * h_ref[rows, :] + jnp.dot(mix.astype(BF16), wo_ref[...], preferred_element_type=F32)
        y_ref[rows, :] = _layer_norm_rows(z, g_ref[...], beta_ref[...])
    if fuse_conv:
        ns_ref[...] = carry_ref[0:2, :]


def _merge(o_conv, o_attn, o_mem, p_all, h, w_c, w_a, w_m, w_o, g, b, *, tm, conv_w=None):
    rows = h.shape[0]
    fuse_conv = o_conv is None
    single = pl.Buffered(1)
    row_blk = lambda w: pl.BlockSpec((tm, w), lambda i: (i, 0))
    col_blk = lambda w, col: pl.BlockSpec((tm, w), lambda i: (i, col // w))
    whole = lambda a: pl.BlockSpec(a.shape, lambda i: (0, 0), pipeline_mode=single)
    if fuse_conv:
        conv_specs = [col_blk(CONV_W, COL_B), col_blk(CONV_W, COL_C), col_blk(CONV_W, COL_V), whole(conv_w)]
        conv_args = [p_all, p_all, p_all, conv_w]
    else:
        conv_specs, conv_args = [row_blk(CONV_W)], [o_conv]
    out = pl.pallas_call(
        functools.partial(_merge_kernel, sub_rows=min(tm, 256), fuse_conv=fuse_conv),
        grid=(rows // tm,),
        in_specs=conv_specs + [row_blk(GROUP_W), row_blk(GROUP_W),
                               col_blk(D_MODEL, COL_GC), col_blk(D_MODEL, COL_GA), col_blk(D_MODEL, COL_GM),
                               row_blk(D_MODEL),
                               whole(w_c), whole(w_a), whole(w_m), whole(w_o), whole(g), whole(b)],
        out_specs=[row_blk(D_MODEL)] + ([pl.BlockSpec((2, CONV_W), lambda i: (0, 0))] if fuse_conv else []),
        out_shape=[jax.ShapeDtypeStruct((rows, D_MODEL), F32)]
                  + ([jax.ShapeDtypeStruct((2, CONV_W), F32)] if fuse_conv else []),
        scratch_shapes=[pltpu.VMEM((8, CONV_W), F32)] if fuse_conv else [],
        compiler_params=_params(1),
        name="merge",
    )(*conv_args, o_attn, o_mem, p_all, p_all, p_all, h, w_c, w_a, w_m, w_o, g, b)
    return out if fuse_conv else out[0]


def _over_quad(x, op):
    x = op(x, pltpu.roll(x, HEADS, 0))
    return op(x, pltpu.roll(x, 2 * HEADS, 0))


def _lane_dot(k3, q):
    return jnp.sum(k3 * q, axis=-1, keepdims=True)


def _quad_attention(k3, v3, qt, bias3, k_new, v_new, bias_new):
    s = _lane_dot(k3, qt[None]) + bias3
    m = jnp.max(s, axis=0)
    if k_new is not None:
        s_new = _lane_dot(k_new, qt) + bias_new
        m = jnp.maximum(m, s_new)
    m = _over_quad(m, jnp.maximum)
    e = jnp.exp2(s - m[None])
    l = jnp.sum(e, axis=0)
    acc = jnp.sum(e * v3, axis=0)
    if k_new is not None:
        e_new = jnp.exp2(s_new - m)
        l = l + e_new
        acc = acc + e_new * v_new
    l = _over_quad(l, jnp.add)
    acc = _over_quad(acc, jnp.add)
    return acc / l, m * LN2 + jnp.log(l)


def _sample_kernel(qd_ref, qt_ref, kn_ref, vn_ref, sl_ref,
                   k0_ref, v0_ref, k1_ref, v1_ref, k2_ref, v2_ref, mk_ref, mv_ref, *new_refs, t_new):
    k_refs = (k0_ref, k1_ref, k2_ref)
    v_refs = (v0_ref, v1_ref, v2_ref)
    oa_ref, om_ref = new_refs[N_GROUPS * 2], new_refs[N_GROUPS * 2 + 1]
    bias0_ref, bias0_new_ref, bias1_ref, bias2_ref = new_refs[-4:]
    bias_refs = (None, bias1_ref, bias2_ref)
    quad16 = lax.broadcasted_iota(jnp.int32, (ROWS16, HEAD_DIM), 0) >> 2
    q_scale = SCALE * LOG2E
    win0 = DIL_GROUPS[0][0]

    @pl.when(pl.program_id(0) == 0)
    def _():
        a_idx = lax.broadcasted_iota(jnp.int32, (win0 // 4, ROWS16, HEAD_DIM), 0)
        b_idx = lax.broadcasted_iota(jnp.int32, (win0 // 4, ROWS16, HEAD_DIM), 1) >> 2
        tok = a_idx * 4 + b_idx
        slope0 = sl_ref[0] * LOG2E
        for t in range(t_new):
            bias0_ref[t] = jnp.where(tok >= t, -(slope0[None] * (win0 + t - tok).astype(F32)), NEG_INF)
            bias0_new_ref[t] = jnp.where(quad16 <= t, -(slope0 * (t - quad16).astype(F32)), NEG_INF)
        i_idx = lax.broadcasted_iota(jnp.int32, (BAND, ROWS16, HEAD_DIM), 0)
        for g in (1, 2):
            win, dil = DIL_GROUPS[g]
            bias_refs[g][...] = -((sl_ref[g] * LOG2E)[None] * (win - i_idx * dil).astype(F32))

    outs, lses = [], []

    k3, v3 = k0_ref[0], v0_ref[0]
    o0 = jnp.zeros((ROWS16, HEAD_DIM), F32)
    l0 = jnp.zeros((ROWS16, HEAD_DIM), F32)
    for t in range(t_new):
        o_t, l_t = _quad_attention(k3, v3, qt_ref[0, 0, t] * q_scale, bias0_ref[t],
                                   kn_ref[0, 0], vn_ref[0, 0], bias0_new_ref[t])
        o0 = jnp.where(quad16 == t, o_t, o0)
        l0 = jnp.where(quad16 == t, l_t, l0)
    outs.append(o0)
    lses.append(l0)

    for g in (1, 2):
        q16 = qd_ref[0, g - 1] * q_scale
        k3 = k_refs[g][0, :, 0:ROWS16, :]
        v3 = v_refs[g][0, :, 0:ROWS16, :]
        s = _lane_dot(k3, q16[None]) + bias_refs[g][...]
        s_new = _lane_dot(kn_ref[0, g], q16)
        m = jnp.maximum(jnp.max(s, axis=0), s_new)
        e = jnp.exp2(s - m[None])
        e_new = jnp.exp2(s_new - m)
        l = jnp.sum(e, axis=0) + e_new
        acc = jnp.sum(e * v3, axis=0) + e_new * vn_ref[0, g]
        outs.append(acc / l)
        lses.append(m * LN2 + jnp.log(l))

    m = jnp.maximum(jnp.maximum(lses[0], lses[1]), lses[2])
    w = [jnp.exp(x - m) for x in lses]
    oa_ref[0] = (w[0] * outs[0] + w[1] * outs[1] + w[2] * outs[2]) / (w[0] + w[1] + w[2])

    k3, v3 = mk_ref[0], mv_ref[0]
    om = jnp.zeros((ROWS16, HEAD_DIM), F32)
    for t in range(t_new):
        o_t, _ = _quad_attention(k3, v3, qt_ref[0, 1, t] * q_scale, jnp.zeros((1, ROWS16, HEAD_DIM), F32),
                                 None, None, None)
        om = jnp.where(quad16 == t, o_t, om)
    om_ref[0] = om

    for i, out_ref in enumerate(new_refs[N_GROUPS * 2 + 2:N_GROUPS * 4 + 2]):
        out_ref[0, 0] = (kn_ref, vn_ref)[i % 2][0, i // 2]


def _sample_attention(qd, qt, kn, vn, slopes16, k_states, v_states, shifted, mem_k, mem_v, *, t_new):
    nb = qd.shape[0]
    states = []
    for g in range(N_GROUPS):
        states += [k_states[g], v_states[g]]
    whole = lambda a: pl.BlockSpec((1,) + a.shape[1:], lambda b: (b,) + (0,) * (a.ndim - 1))
    head_rows = lambda a: pl.BlockSpec((1, a.shape[1], ROWS16, HEAD_DIM), lambda b: (b, 0, 0, 0))
    last_rows = lambda a: pl.BlockSpec((1, 1, ROWS16, HEAD_DIM),
                                       lambda b, a=a: (b, a.shape[1] - 1, a.shape[2] // ROWS16 - 1, 0))
    o_spec = pl.BlockSpec((1, ROWS16, HEAD_DIM), lambda b: (b, 0, 0))
    o_shape = jax.ShapeDtypeStruct((nb, ROWS16, HEAD_DIM), F32)
    n_in = 5 + len(states) + 2
    return pl.pallas_call(
        functools.partial(_sample_kernel, t_new=t_new),
        grid=(nb,),
        in_specs=[whole(qd), whole(qt), whole(kn), whole(vn),
                  pl.BlockSpec(slopes16.shape, lambda b: (0, 0, 0))]
                 + [head_rows(s) for s in states] + [whole(mem_k), whole(mem_v)]
                 + [pl.BlockSpec(memory_space=pl.ANY)] * len(shifted),
        out_specs=[o_spec, o_spec] + [last_rows(s) for s in shifted],
        out_shape=[o_shape, o_shape] + [jax.ShapeDtypeStruct(s.shape, F32) for s in shifted],
        input_output_aliases={n_in + i: 2 + i for i in range(len(shifted))},
        scratch_shapes=[pltpu.VMEM((t_new, DIL_GROUPS[0][0] // 4, ROWS16, HEAD_DIM), F32),
                        pltpu.VMEM((t_new, ROWS16, HEAD_DIM), F32),
                        pltpu.VMEM((BAND, ROWS16, HEAD_DIM), F32),
                        pltpu.VMEM((BAND, ROWS16, HEAD_DIM), F32)],
        compiler_params=_params(1),
        name="sample_attn",
    )(qd, qt, kn, vn, slopes16, *states, mem_k, mem_v, *shifted)


def kernel(x_prompt, x_sample, mem_prompt, state_conv, state_k_w128, state_v_w128, state_k_w512, state_v_w512, state_k_w2048, state_v_w2048, cache_mem_k, cache_mem_v, ln1_g, ln1_b, w_ffn1_gu, w_ffn1_down, w_in, conv_w, w_mem_kv, w_br_conv, w_br_attn, w_br_mem, w_o, ln2_g, ln2_b, w_ffn2_gu, w_ffn2_down, ln3_g, ln3_b):
    depth = w_in.shape[0]
    seq = x_prompt.shape[1]
    nb, t_new = x_sample.shape[0], x_sample.shape[1]
    assert depth == 1 and x_prompt.shape[0] == 1 and seq % CHUNK == 0
    assert t_new * HEADS == ROWS16 and t_new <= DIL_GROUPS[1][1]

    wb = lambda w: w[0].astype(BF16)
    wc_b, wa_b, wm_b, wo_b = wb(w_br_conv), wb(w_br_attn), wb(w_br_mem), wb(w_o)
    n_ff_blocks = D_FF // 512
    cw = conv_w[0]
    slopes_lane = np.broadcast_to(SLOPES[:, :, None], (N_GROUPS, HEADS, HEAD_DIM))
    slopes_band = jnp.asarray(np.pad(slopes_lane.transpose(1, 0, 2), ((0, 0), (0, 8 - N_GROUPS), (0, 0))))
    slopes16 = jnp.asarray(np.tile(slopes_lane, (1, ROWS16 // HEADS, 1)))

    rows_s = nb * SAMPLE_PAD
    xs = jnp.pad(x_sample, ((0, 0), (0, SAMPLE_PAD - t_new), (0, 0))).reshape(rows_s, D_MODEL)
    s1, s1b, w1g, w1u, w1d = _ffn(xs, w_ffn1_gu[0], w_ffn1_gu[0], w_ffn1_down[0], ln1_g, ln1_b,
                                  tm=rows_s, tf=512, up_blk0=n_ff_blocks, emit_bf16=True)
    ps_all, win_b = _matmul(s1b, w_in[0], tm=rows_s, tn=1024, out_dtype=F32, name="in_proj_s", emit_bf16=True)

    def tiles(state):
        win = state.shape[2]
        per_tile = 4 if win <= DIL_GROUPS[1][0] else DIL_GROUPS[2][1]
        return state.reshape(nb, win // per_tile, per_tile * HEADS, HEAD_DIM)

    k_states = [tiles(s) for s in (state_k_w128, state_k_w512, state_k_w2048)]
    v_states = [tiles(s) for s in (state_v_w128, state_v_w512, state_v_w2048)]
    h1, h1b, *shifted = _ffn(x_prompt[0], w1g, w1u, w1d, ln1_g, ln1_b, tm=1024, tf=512, manual_rows=True,
                             windows=k_states + v_states)
    shifted = [shifted[g + part * N_GROUPS] for g in range(N_GROUPS) for part in (0, 1)]
    p_all, qkv = _in_proj(h1b, win_b, tm=1024, tn=1024)
    mem_kv = _matmul(mem_prompt[0].astype(BF16), w_mem_kv[0], tm=MEM_TOKENS, tn=512, out_dtype=F32, name="mem_kv")
    o_attn = _band_attention(qkv, slopes_band)
    o_mem = _mem_attention(p_all, mem_kv, tq=512)
    h2, conv_p = _merge(None, o_attn, o_mem, p_all, h1, wc_b, wa_b, wm_b, wo_b, ln2_g, ln2_b, tm=256, conv_w=cw)

    prompt_states = []
    for g, (win, _) in enumerate(DIL_GROUPS):
        for tensor in (1, 2):
            c0 = tensor * N_GROUPS * GROUP_W + g * GROUP_W
            prompt_states.append(qkv[seq - win:, c0:c0 + GROUP_W].reshape(1, 1, win, HEADS, HEAD_DIM))
    mem_k_p = mem_kv[:, :GROUP_W].reshape(1, 1, MEM_TOKENS, HEADS, HEAD_DIM)
    mem_v_p = mem_kv[:, GROUP_W:].reshape(1, 1, MEM_TOKENS, HEADS, HEAD_DIM)

    ps3 = ps_all.reshape(nb, SAMPLE_PAD, IN_W)

    real = ps3[:, :t_new]
    heads = lambda a, n: a.reshape(nb, t_new, n, HEADS, HEAD_DIM)
    q, k, v = (heads(real[..., c:c + N_GROUPS * GROUP_W], N_GROUPS) for c in (COL_Q, COL_K, COL_VA))
    rows_th = lambda a: a.transpose(0, 2, 1, 3, 4).reshape(nb, a.shape[2], ROWS16, HEAD_DIM)
    over_quad = lambda a: jnp.broadcast_to(a[:, :, None], (nb, t_new, ROWS16 // HEADS, HEADS, HEAD_DIM)
                                           ).reshape(nb, t_new, ROWS16, HEAD_DIM)
    qd = rows_th(q[:, :, 1:])
    qm = real[..., COL_QM:COL_QM + GROUP_W].reshape(nb, t_new, HEADS, HEAD_DIM)
    qt = jnp.stack([over_quad(q[:, :, 0]), over_quad(qm)], axis=1)
    kn, vn = rows_th(k), rows_th(v)

    outs = _sample_attention(qd, qt, kn, vn, slopes16, k_states, v_states, shifted,
                             tiles(cache_mem_k), tiles(cache_mem_v), t_new=t_new)
    pad_rows = lambda a, w: jnp.pad(a.reshape(nb, t_new, w), ((0, 0), (0, SAMPLE_PAD - t_new), (0, 0))
                                    ).reshape(rows_s, w).astype(BF16)
    o_attn_s, o_mem_s = pad_rows(outs[0], GROUP_W), pad_rows(outs[1], GROUP_W)
    sample_states = [s.reshape(1, nb, -1, HEADS, HEAD_DIM) for s in outs[2:]]

    o_conv_s, conv_s = _conv(ps3, state_conv[0], cw, tm=SAMPLE_PAD, t_last=t_new)
    s2 = _merge(o_conv_s.reshape(rows_s, CONV_W), o_attn_s, o_mem_s, ps_all, s1,
                wc_b, wa_b, wm_b, wo_b, ln2_g, ln2_b, tm=rows_s)
    y_s, w2g, w2u, w2d = _ffn(s2, w_ffn2_gu[0], w_ffn2_gu[0], w_ffn2_down[0], ln3_g, ln3_b,
                              tm=rows_s, tf=512, up_blk0=n_ff_blocks, emit_bf16=True, bf16_out=False)
    y_sample = y_s.reshape(nb, SAMPLE_PAD, D_MODEL)[:, :t_new]
    (y_prompt,) = _ffn(h2, w2g, w2u, w2d, ln3_g, ln3_b, tm=1024, tf=512, bf16_out=False, row_buffers=2)

    return (y_prompt[None], y_sample, conv_p[None, None], *prompt_states, mem_k_p, mem_v_p,
            conv_s[None], *sample_states)
```

```python
import functools
import math

import jax
import jax.numpy as jnp
import numpy as np
from jax import lax
from jax.experimental import pallas as pl
from jax.experimental.pallas import tpu as pltpu

F32 = jnp.float32
BF16 = jnp.bfloat16

D_MODEL = 2048
D_FF = 5632
HEAD_DIM = 128
HEADS = 4
GROUP_W = HEADS * HEAD_DIM
DIL_GROUPS = ((128, 1), (512, 4), (2048, 16))
N_GROUPS = len(DIL_GROUPS)
BAND = 128
CHUNK = BAND * DIL_GROUPS[-1][1]
CONV_W = 1024
MEM_TOKENS = 256
IN_W = 14336
COL_B, COL_C, COL_V = 0, 1024, 2048
COL_Q, COL_K, COL_VA = 3072, 4608, 6144
COL_QM = 7680
COL_GC, COL_GA, COL_GM = 8192, 10240, 12288
ALPHA = 2.0 ** 0.25
LN_EPS = 1e-5
NEG_INF = -1e30
SCALE = HEAD_DIM ** -0.5
LOG2E = 1.4426950408889634
LN2 = 0.6931471805599453
SLOPES = np.exp2(np.float32(-8.0) * np.arange(1, 13, dtype=np.float32) / np.float32(12)).reshape(3, 4)

SAMPLE_PAD = 8
FFN_ROWS, FFN_COLS = 1024, 512
PROJ_ROWS, PROJ_COLS = 1024, 1024
MERGE_ROWS, MERGE_SUB_ROWS = 512, 256
MEM_Q_ROWS = 512
LN_ROWS = 128
ROWS16 = 16
V7X_VMEM_BYTES = 64 * 1024 * 1024
VALUE_SPILL_BYTES = 8 * 1024 * 1024


def _nbytes(shape, dtype):
    return math.prod(1 if d is None else d for d in shape) * jnp.dtype(dtype).itemsize


def _pallas_call(kernel, operands, *, grid, in_specs, out_specs, out_shape, name, scratch_shapes=(),
                 input_output_aliases=None):
    out_specs, out_shape = list(out_specs), list(out_shape)
    vmem = VALUE_SPILL_BYTES
    for spec, array in list(zip(in_specs, operands)) + list(zip(out_specs, out_shape)):
        if spec.block_shape is not None:
            buffers = 2 if spec.pipeline_mode is None else spec.pipeline_mode.buffer_count
            vmem += buffers * _nbytes(spec.block_shape, array.dtype)
    vmem += sum(_nbytes(s.shape, s.dtype) for s in scratch_shapes if s.memory_space == pltpu.VMEM)
    assert vmem <= V7X_VMEM_BYTES, (name, vmem)
    call = pl.pallas_call(
        kernel, grid=grid, in_specs=in_specs, out_specs=out_specs, out_shape=out_shape,
        scratch_shapes=list(scratch_shapes), input_output_aliases=input_output_aliases or {},
        compiler_params=pltpu.CompilerParams(dimension_semantics=("arbitrary",) * len(grid),
                                             vmem_limit_bytes=vmem),
        name=name)
    return call(*operands)


def _layer_norm_rows(z, g, b):
    mu = jnp.mean(z, axis=-1, keepdims=True)
    zc = z - mu
    var = jnp.mean(zc * zc, axis=-1, keepdims=True)
    return zc * lax.rsqrt(var + LN_EPS) * g + b


def _window_loads(b, src_refs, stage_refs, slot, sems):
    return [pltpu.make_async_copy(src.at[b], stage.at[slot], sems.at[slot, n])
            for n, (src, stage) in enumerate(zip(src_refs, stage_refs))]


def _window_stores(b, stage_refs, dst_refs, slot, sems):
    copies = []
    for stage, dst in zip(stage_refs, dst_refs):
        tiles, rows = dst.shape[1], dst.shape[2]
        keep = rows - ROWS16
        if keep:
            copies.append((stage.at[slot, :, pl.ds(ROWS16, keep), :], dst.at[b, :, pl.ds(0, keep), :]))
        copies.append((stage.at[slot, pl.ds(1, tiles - 1), pl.ds(0, ROWS16), :],
                       dst.at[b, pl.ds(0, tiles - 1), pl.ds(keep, ROWS16), :]))
        copies.append((stage.at[slot, pl.ds(tiles - 1, 1), pl.ds(keep, ROWS16), :],
                       dst.at[b, pl.ds(tiles - 1, 1), pl.ds(keep, ROWS16), :]))
    return [pltpu.make_async_copy(s, d, sems.at[slot, n]) for n, (s, d) in enumerate(copies)]


def _ffn_kernel(*refs, n_ff, row_chunk, emit_bf16, bf16_out, manual_rows, n_windows, n_requests):
    x_ref, wg_ref, wu_ref, wd_ref, g_ref, b_ref = refs[:6]
    win_src = refs[6:6 + n_windows]
    y_ref = refs[6 + n_windows]
    yb_ref = refs[7 + n_windows] if bf16_out else None
    rest = refs[7 + bf16_out + n_windows:]
    if emit_bf16:
        wb_refs, rest = rest[:3], rest[3:]
    win_dst, rest = rest[:n_windows], rest[n_windows:]
    xb_ref, rest = rest[0], rest[1:]
    i, j = pl.program_id(0), pl.program_id(1)
    n_tiles = pl.num_programs(0)
    tm = xb_ref.shape[0]

    if manual_rows:
        y_hbm, yb_hbm = y_ref, yb_ref
        n_ring = 1 + bf16_out
        xbuf_ref, xsem, y_ref = rest[:3]
        ring_refs, out_sems = rest[3:3 + n_ring], rest[3 + n_ring]
        rest = rest[4 + n_ring:]

        def x_copy(tile):
            return pltpu.make_async_copy(x_ref.at[pl.ds(pl.multiple_of(tile * tm, tm), tm), :], xbuf_ref, xsem.at[0])

        @pl.when((i == 0) & (j == 0))
        def _():
            x_copy(0).start()

    if n_windows:
        half = n_windows // 2
        stage_refs, load_sems, store_sems = rest[:half], rest[half], rest[half + 1]
        step = i * n_ff + j
        n_items = 2 * n_requests
        for part in (0, 1):
            srcs, dsts = win_src[part * half:(part + 1) * half], win_dst[part * half:(part + 1) * half]
            mine = step % 2 == part

            @pl.when(mine & (step >= 2) & (step < n_items + 2))
            def _():
                for copy in _window_stores((step - 2) // 2, stage_refs, dsts, part, store_sems):
                    copy.wait()

            @pl.when(mine & (step < n_items))
            def _():
                for copy in _window_loads(step // 2, srcs, stage_refs, part, load_sems):
                    copy.start()

            @pl.when(jnp.logical_not(mine) & (step >= 1) & (step < n_items + 1))
            def _():
                for copy in _window_loads((step - 1) // 2, srcs, stage_refs, part, load_sems):
                    copy.wait()
                for copy in _window_stores((step - 1) // 2, stage_refs, dsts, part, store_sems):
                    copy.start()

    @pl.when(j == 0)
    def _():
        if manual_rows:
            x_copy(i).wait()
        x = xbuf_ref[...] if manual_rows else x_ref[...]
        xb_ref[...] = x.astype(BF16)
        y_ref[...] = (2.0 * ALPHA) * x

    if manual_rows:
        @pl.when((j == 1) & (i + 1 < n_tiles))
        def _():
            x_copy(i + 1).start()

    wg, wu, wd = wg_ref[...], wu_ref[...], wd_ref[...]
    if emit_bf16:
        wg, wu, wd = wg.astype(BF16), wu.astype(BF16), wd.astype(BF16)
        wb_refs[0][...], wb_refs[1][...], wb_refs[2][...] = wg, wu, wd
    xb = xb_ref[...]
    gate = jnp.dot(xb, wg, preferred_element_type=F32)
    up = jnp.dot(xb, wu, preferred_element_type=F32)
    act = (gate * jax.nn.sigmoid(gate) * up).astype(BF16)
    y_ref[...] += jnp.dot(act, wd, preferred_element_type=F32)

    if not manual_rows:
        @pl.when(j == n_ff - 1)
        def _():
            def chunk(c, carry):
                rows = pl.ds(pl.multiple_of(c * row_chunk, row_chunk), row_chunk)
                y = _layer_norm_rows(0.5 * y_ref[rows, :], g_ref[...], b_ref[...])
                y_ref[rows, :] = y
                if bf16_out:
                    yb_ref[rows, :] = y.astype(BF16)
                return carry
            lax.fori_loop(0, tm // row_chunk, chunk, 0)
    else:
        def out_copies(slot, row0):
            dsts = (y_hbm, yb_hbm)[:n_ring]
            return [pltpu.make_async_copy(ring.at[slot], dst.at[pl.ds(row0, row_chunk), :], out_sems.at[slot, n])
                    for n, (ring, dst) in enumerate(zip(ring_refs, dsts))]

        def wait_slot(slot):
            for copy in out_copies(slot, 0):
                copy.wait()

        @pl.when(j == n_ff - 1)
        def _():
            n_chunks = tm // row_chunk
            for c in range(n_chunks):
                slot = c % 2
                if c >= 2:
                    wait_slot(slot)
                else:
                    pl.when(i > 0)(functools.partial(wait_slot, slot))
                y = _layer_norm_rows(0.5 * y_ref[c * row_chunk:(c + 1) * row_chunk, :], g_ref[...], b_ref[...])
                ring_refs[0][slot] = y
                if bf16_out:
                    ring_refs[1][slot] = y.astype(BF16)
                for copy in out_copies(slot, pl.multiple_of(i * tm, tm) + c * row_chunk):
                    copy.start()

            @pl.when(i == n_tiles - 1)
            def _():
                wait_slot(0)
                wait_slot(1)


def _ffn(x, w_gate, w_up, w_down, g, b, *, tm, tf, up_blk0=0, emit_bf16=False, bf16_out=True,
         row_buffers=1, manual_rows=False, windows=()):
    rows = x.shape[0]
    n_ff = D_FF // tf
    n_steps = (rows // tm) * n_ff
    n_requests = windows[0].shape[0] if windows else 0
    assert not emit_bf16 or rows == tm
    assert n_steps >= 2 * n_requests + 2
    row_mode = pl.Buffered(1) if (rows == tm or row_buffers == 1) else None
    hbm = pl.BlockSpec(memory_space=pl.ANY)
    w_col = pl.BlockSpec((D_MODEL, tf), lambda i, j: (0, j))
    w_col_tile = pl.BlockSpec((None, D_MODEL, tf), lambda i, j: (j, 0, 0))
    tiled = w_gate.ndim == 3
    w_row = pl.BlockSpec((tf, D_MODEL), lambda i, j: (j, 0))
    row_tile = pl.BlockSpec((tm, D_MODEL), lambda i, j: (i, 0), pipeline_mode=row_mode)
    out_specs = [hbm if manual_rows else row_tile] * (1 + bf16_out)
    out_shape = [jax.ShapeDtypeStruct((rows, D_MODEL), F32), jax.ShapeDtypeStruct((rows, D_MODEL), BF16)][:1 + bf16_out]
    scratch = [pltpu.VMEM((tm, D_MODEL), BF16)]
    if manual_rows:
        row_chunk = min(tm, LN_ROWS)
        assert n_ff >= 2 and (tm // row_chunk) % 2 == 0
        scratch += [pltpu.VMEM((tm, D_MODEL), F32), pltpu.SemaphoreType.DMA((1,)), pltpu.VMEM((tm, D_MODEL), F32)]
        scratch += [pltpu.VMEM((2, row_chunk, D_MODEL), dt) for dt in (F32, BF16)[:1 + bf16_out]]
        scratch += [pltpu.SemaphoreType.DMA((2, 1 + bf16_out))]
    if emit_bf16:
        out_specs += [w_col_tile, w_col_tile, w_row]
        out_shape += [jax.ShapeDtypeStruct((n_ff, D_MODEL, tf), BF16)] * 2 + [jax.ShapeDtypeStruct((D_FF, D_MODEL), BF16)]
    if windows:
        out_specs += [hbm] * len(windows)
        out_shape += [jax.ShapeDtypeStruct(w.shape, w.dtype) for w in windows]
        half = windows[:len(windows) // 2]
        scratch += [pltpu.VMEM((2,) + w.shape[1:], w.dtype) for w in half]
        scratch += [pltpu.SemaphoreType.DMA((2, len(half))),
                    pltpu.SemaphoreType.DMA((2, sum(2 + (w.shape[2] > ROWS16) for w in half)))]
    return _pallas_call(
        functools.partial(_ffn_kernel, n_ff=n_ff, row_chunk=min(tm, LN_ROWS), emit_bf16=emit_bf16, bf16_out=bf16_out,
                          manual_rows=manual_rows, n_windows=len(windows), n_requests=n_requests),
        (x, w_gate, w_up, w_down, g, b, *windows),
        grid=(rows // tm, n_ff),
        in_specs=[
            hbm if manual_rows else row_tile,
            w_col_tile if tiled else w_col,
            w_col_tile if tiled else pl.BlockSpec((D_MODEL, tf), lambda i, j: (0, j + up_blk0)),
            w_row,
            pl.BlockSpec((1, D_MODEL), lambda i, j: (0, 0)),
            pl.BlockSpec((1, D_MODEL), lambda i, j: (0, 0)),
        ] + [hbm] * len(windows),
        out_specs=out_specs,
        out_shape=out_shape,
        scratch_shapes=scratch,
        name="ffn_cast" if emit_bf16 else "ffn",
    )


def _mm_kernel(x_ref, w_ref, o_ref, *wb_ref):
    w = w_ref[...].astype(BF16)
    if wb_ref:
        wb_ref[0][...] = w
    o_ref[...] = jnp.dot(x_ref[...], w, preferred_element_type=F32).astype(o_ref.dtype)


def _matmul(x, w, *, tm, tn, out_dtype, name, emit_bf16=False):
    rows, k = x.shape
    n_cols = w.shape[1]
    assert not emit_bf16 or rows == tm
    w_spec = pl.BlockSpec((k, tn), lambda i, j: (0, j))
    out_specs = [pl.BlockSpec((tm, tn), lambda i, j: (i, j))]
    out_shape = [jax.ShapeDtypeStruct((rows, n_cols), out_dtype)]
    if emit_bf16:
        out_specs.append(pl.BlockSpec((None, k, tn), lambda i, j: (j, 0, 0)))
        out_shape.append(jax.ShapeDtypeStruct((n_cols // tn, k, tn), BF16))
    out = _pallas_call(
        _mm_kernel,
        (x, w),
        grid=(rows // tm, n_cols // tn),
        in_specs=[pl.BlockSpec((tm, k), lambda i, j: (i, 0)), w_spec],
        out_specs=out_specs,
        out_shape=out_shape,
        name=name,
    )
    return out if emit_bf16 else out[0]


def _in_proj_kernel(x_ref, w_ref, p_ref, qkv_ref, *, n_own):
    j = pl.program_id(1)
    acc = jnp.dot(x_ref[...], w_ref[...], preferred_element_type=F32)
    p_ref[...] = acc.astype(BF16)

    @pl.when(j < n_own)
    def _():
        qkv_ref[...] = acc


def _in_proj(x, w, *, tm, tn):
    rows, k = x.shape
    assert w.shape == (IN_W // tn, k, tn) and COL_Q % tn == 0 and COL_GC % tn == 0
    c_lo, n_own = COL_Q // tn, (COL_GC - COL_Q) // tn

    def col_block(j):
        return jnp.where(j < n_own, j + c_lo, jnp.where(j < n_own + c_lo, j - n_own, j))

    return _pallas_call(
        functools.partial(_in_proj_kernel, n_own=n_own),
        (x, w),
        grid=(rows // tm, IN_W // tn),
        in_specs=[pl.BlockSpec((tm, k), lambda i, j: (i, 0)),
                  pl.BlockSpec((None, k, tn), lambda i, j: (col_block(j), 0, 0))],
        out_specs=[pl.BlockSpec((tm, tn), lambda i, j: (i, col_block(j))),
                   pl.BlockSpec((tm, tn), lambda i, j: (i, jnp.minimum(j, n_own - 1)))],
        out_shape=[jax.ShapeDtypeStruct((rows, IN_W), BF16),
                   jax.ShapeDtypeStruct((rows, COL_GC - COL_Q), F32)],
        name="in_proj",
    )


def _rows(ref, start, dil):
    if dil == 1:
        return ref[start:start + BAND, :]
    return ref[pl.ds(start, BAND, stride=dil), :]


def _band_kernel(sl_ref, q0_ref, k0_ref, v0_ref, q1_ref, k1_ref, v1_ref, q2_ref, k2_ref, v2_ref,
                 o_ref, kt0_ref, vt0_ref, kt1_ref, vt1_ref, kt2_ref, vt2_ref, og_ref, lg_ref):
    c = pl.program_id(1)
    q_refs = (q0_ref, q1_ref, q2_ref)
    k_refs = (k0_ref, k1_ref, k2_ref)
    v_refs = (v0_ref, v1_ref, v2_ref)
    kt_refs = (kt0_ref, kt1_ref, kt2_ref)
    vt_refs = (vt0_ref, vt1_ref, vt2_ref)

    @pl.when(c == 0)
    def _():
        for ref in kt_refs + vt_refs:
            ref[...] = jnp.zeros(ref.shape, F32)

    qi = lax.broadcasted_iota(jnp.int32, (BAND, 2 * BAND), 0)
    ki = lax.broadcasted_iota(jnp.int32, (BAND, 2 * BAND), 1)
    rel = BAND + qi - ki
    valid = (rel >= 0) & (rel <= BAND)
    no_prev = jnp.where(ki < BAND, jnp.where(c == 0, NEG_INF, 0.0), 0.0)

    for g, (_, dil) in enumerate(DIL_GROUPS):
        slope = sl_ref[0, g:g + 1, 0:1] * LOG2E
        bias = jnp.where(valid, -(slope * (rel * dil).astype(F32)), NEG_INF)
        bias_first = bias + no_prev
        span = BAND * dil
        for r in range(dil):
            for n in range(CHUNK // span):
                start = r + n * span
                q = _rows(q_refs[g], start, dil).astype(BF16)
                if n == 0:
                    k_prev, v_prev = _rows(kt_refs[g], r, dil), _rows(vt_refs[g], r, dil)
                else:
                    k_prev, v_prev = _rows(k_refs[g], start - span, dil), _rows(v_refs[g], start - span, dil)
                k = jnp.concatenate([k_prev, _rows(k_refs[g], start, dil)], axis=0).astype(BF16)
                v = jnp.concatenate([v_prev, _rows(v_refs[g], start, dil)], axis=0).astype(BF16)
                s = lax.dot_general(q, k, (((1,), (1,)), ((), ())), preferred_element_type=F32) * (SCALE * LOG2E)
                s = s + (bias_first if n == 0 else bias)
                m = jnp.max(s, axis=-1, keepdims=True)
                p = jnp.exp2(s - m)
                l = jnp.sum(p, axis=-1, keepdims=True)
                o = jnp.dot(p.astype(BF16), v, preferred_element_type=F32) / l
                lse = jnp.broadcast_to(m * LN2 + jnp.log(l), (BAND, HEAD_DIM))
                if dil == 1:
                    og_ref[g, start:start + BAND, :] = o
                    lg_ref[g, start:start + BAND, :] = lse
                else:
                    og_ref[g, pl.ds(start, BAND, stride=dil), :] = o
                    lg_ref[g, pl.ds(start, BAND, stride=dil), :] = lse
        kt_refs[g][...] = k_refs[g][CHUNK - span:CHUNK, :]
        vt_refs[g][...] = v_refs[g][CHUNK - span:CHUNK, :]

    rows_per_step = 256

    def combine(i, carry):
        rows = pl.ds(pl.multiple_of(i * rows_per_step, rows_per_step), rows_per_step)
        l0, l1, l2 = lg_ref[0, rows, :], lg_ref[1, rows, :], lg_ref[2, rows, :]
        m = jnp.maximum(jnp.maximum(l0, l1), l2)
        e0, e1, e2 = jnp.exp(l0 - m), jnp.exp(l1 - m), jnp.exp(l2 - m)
        o = (e0 * og_ref[0, rows, :] + e1 * og_ref[1, rows, :] + e2 * og_ref[2, rows, :]) / (e0 + e1 + e2)
        o_ref[rows, :] = o.astype(o_ref.dtype)
        return carry

    lax.fori_loop(0, CHUNK // rows_per_step, combine, 0)


def _band_attention(qkv, slopes):
    seq = qkv.shape[0]
    blk = (CHUNK, HEAD_DIM)
    per_tensor = N_GROUPS * HEADS
    in_specs = [pl.BlockSpec((1, 8, HEAD_DIM), lambda h, c: (h, 0, 0))]
    for g in range(N_GROUPS):
        for tensor in range(3):
            in_specs.append(pl.BlockSpec(
                blk, lambda h, c, g=g, tensor=tensor: (c, tensor * per_tensor + g * HEADS + h)))
    tails = []
    for _, dil in DIL_GROUPS:
        tails += [pltpu.VMEM((BAND * dil, HEAD_DIM), F32)] * 2
    return _pallas_call(
        _band_kernel,
        (slopes, *([qkv] * 9)),
        grid=(HEADS, seq // CHUNK),
        in_specs=in_specs,
        out_specs=[pl.BlockSpec(blk, lambda h, c: (c, h))],
        out_shape=[jax.ShapeDtypeStruct((seq, GROUP_W), BF16)],
        scratch_shapes=tails + [pltpu.VMEM((N_GROUPS, CHUNK, HEAD_DIM), F32),
                                pltpu.VMEM((N_GROUPS, CHUNK, HEAD_DIM), F32)],
        name="band_attn",
    )[0]


def _mem_attn_kernel(q_ref, mk_ref, mv_ref, o_ref):
    for h in range(HEADS):
        hs = slice(h * HEAD_DIM, (h + 1) * HEAD_DIM)
        q = q_ref[:, hs]
        k = mk_ref[:, hs].astype(BF16)
        v = mv_ref[:, hs].astype(BF16)
        s = lax.dot_general(q, k, (((1,), (1,)), ((), ())), preferred_element_type=F32) * SCALE
        m = jnp.max(s, axis=-1, keepdims=True)
        p = jnp.exp(s - m)
        l = jnp.sum(p, axis=-1, keepdims=True)
        o = jnp.dot(p.astype(BF16), v, preferred_element_type=F32) / l
        o_ref[:, hs] = o.astype(o_ref.dtype)


def _mem_attention(p_all, mem_kv, *, tq):
    seq = p_all.shape[0]
    return _pallas_call(
        _mem_attn_kernel,
        (p_all, mem_kv, mem_kv),
        grid=(seq // tq,),
        in_specs=[pl.BlockSpec((tq, GROUP_W), lambda i: (i, COL_QM // GROUP_W)),
                  pl.BlockSpec((MEM_TOKENS, GROUP_W), lambda i: (0, 0)),
                  pl.BlockSpec((MEM_TOKENS, GROUP_W), lambda i: (0, 1))],
        out_specs=[pl.BlockSpec((tq, GROUP_W), lambda i: (i, 0))],
        out_shape=[jax.ShapeDtypeStruct((seq, GROUP_W), BF16)],
        name="mem_attn",
    )[0]


def _conv_rows(b, c, v, w_ref, carry_ref):
    u = c.astype(F32) * v.astype(F32)
    c2 = carry_ref[0:1, :]
    c1 = carry_ref[1:2, :]
    row = lax.broadcasted_iota(jnp.int32, u.shape, 0)
    u1 = jnp.where(row == 0, c1, pltpu.roll(u, 1, 0))
    u2 = jnp.where(row == 0, c2, jnp.where(row == 1, c1, pltpu.roll(u, 2, 0)))
    return b.astype(F32) * (w_ref[0:1, :] * u2 + w_ref[1:2, :] * u1 + w_ref[2:3, :] * u), u


def _conv_kernel(b_ref, c_ref, v_ref, st_ref, w_ref, o_ref, ns_ref, carry_ref, *, t_last):
    i = pl.program_id(1)

    @pl.when(i == 0)
    def _():
        carry_ref[0:2, :] = st_ref[0]

    o, u = _conv_rows(b_ref[0], c_ref[0], v_ref[0], w_ref, carry_ref)
    o_ref[0] = o.astype(o_ref.dtype)
    last2 = u[t_last - 2:t_last, :]
    carry_ref[0:2, :] = last2
    ns_ref[0] = last2


def _conv(p3, state, conv_w, *, tm, t_last):
    nb, t = p3.shape[0], p3.shape[1]
    blk = (1, tm, CONV_W)
    return _pallas_call(
        functools.partial(_conv_kernel, t_last=t_last),
        (p3, p3, p3, state, conv_w),
        grid=(nb, t // tm),
        in_specs=[
            pl.BlockSpec(blk, lambda b, i: (b, i, COL_B // CONV_W)),
            pl.BlockSpec(blk, lambda b, i: (b, i, COL_C // CONV_W)),
            pl.BlockSpec(blk, lambda b, i: (b, i, COL_V // CONV_W)),
            pl.BlockSpec((1, 2, CONV_W), lambda b, i: (b, 0, 0)),
            pl.BlockSpec((3, CONV_W), lambda b, i: (0, 0)),
        ],
        out_specs=[pl.BlockSpec(blk, lambda b, i: (b, i, 0)),
                   pl.BlockSpec((1, 2, CONV_W), lambda b, i: (b, 0, 0))],
        out_shape=[jax.ShapeDtypeStruct((nb, t, CONV_W), BF16),
                   jax.ShapeDtypeStruct((nb, 2, CONV_W), F32)],
        scratch_shapes=[pltpu.VMEM((8, CONV_W), F32)],
        name="conv",
    )


def _merge_kernel(*refs, sub_rows, fuse_conv):
    if fuse_conv:
        b_ref, c_ref, v_ref, cw_ref = refs[:4]
        refs = refs[4:]
    else:
        oc_ref = refs[0]
        refs = refs[1:]
    oa_ref, om_ref, gc_ref, ga_ref, gm_ref, h_ref, wc_ref, wa_ref, wm_ref, wo_ref, g_ref, beta_ref, y_ref = refs[:13]
    if fuse_conv:
        ns_ref, carry_ref = refs[13:]

        @pl.when(pl.program_id(0) == 0)
        def _():
            carry_ref[...] = jnp.zeros(carry_ref.shape, F32)

    for r0 in range(0, h_ref.shape[0], sub_rows):
        rows = slice(r0, r0 + sub_rows)
        if fuse_conv:
            oc, u = _conv_rows(b_ref[rows, :], c_ref[rows, :], v_ref[rows, :], cw_ref, carry_ref)
            oc = oc.astype(BF16)
            carry_ref[0:2, :] = u[sub_rows - 2:sub_rows, :]
        else:
            oc = oc_ref[rows, :]
        mc = jnp.dot(oc, wc_ref[...], preferred_element_type=F32)
        ma = jnp.dot(oa_ref[rows, :], wa_ref[...], preferred_element_type=F32)
        mm = jnp.dot(om_ref[rows, :], wm_ref[...], preferred_element_type=F32)
        mix = (jax.nn.sigmoid(gc_ref[rows, :].astype(F32)) * mc
               + jax.nn.sigmoid(ga_ref[rows, :].astype(F32)) * ma
               + jax.nn.sigmoid(gm_ref[rows, :].astype(F32)) * mm)
        z = ALPHA * h_ref[rows, :] + jnp.dot(mix.astype(BF16), wo_ref[...], preferred_element_type=F32)
        y_ref[rows, :] = _layer_norm_rows(z, g_ref[...], beta_ref[...])
    if fuse_conv:
        ns_ref[...] = carry_ref[0:2, :]


def _merge(o_conv, o_attn, o_mem, p_all, h, w_c, w_a, w_m, w_o, g, b, *, tm, conv_w=None):
    rows = h.shape[0]
    fuse_conv = o_conv is None
    single = pl.Buffered(1)
    row_blk = lambda w: pl.BlockSpec((tm, w), lambda i: (i, 0))
    col_blk = lambda w, col: pl.BlockSpec((tm, w), lambda i: (i, col // w))
    whole = lambda a: pl.BlockSpec(a.shape, lambda i: (0, 0), pipeline_mode=single)
    if fuse_conv:
        conv_specs = [col_blk(CONV_W, COL_B), col_blk(CONV_W, COL_C), col_blk(CONV_W, COL_V), whole(conv_w)]
        conv_args = [p_all, p_all, p_all, conv_w]
    else:
        conv_specs, conv_args = [row_blk(CONV_W)], [o_conv]
    out = _pallas_call(
        functools.partial(_merge_kernel, sub_rows=min(tm, MERGE_SUB_ROWS), fuse_conv=fuse_conv),
        (*conv_args, o_attn, o_mem, p_all, p_all, p_all, h, w_c, w_a, w_m, w_o, g, b),
        grid=(rows // tm,),
        in_specs=conv_specs + [row_blk(GROUP_W), row_blk(GROUP_W),
                               col_blk(D_MODEL, COL_GC), col_blk(D_MODEL, COL_GA), col_blk(D_MODEL, COL_GM),
                               row_blk(D_MODEL),
                               whole(w_c), whole(w_a), whole(w_m), whole(w_o), whole(g), whole(b)],
        out_specs=[row_blk(D_MODEL)] + ([pl.BlockSpec((2, CONV_W), lambda i: (0, 0))] if fuse_conv else []),
        out_shape=[jax.ShapeDtypeStruct((rows, D_MODEL), F32)]
                  + ([jax.ShapeDtypeStruct((2, CONV_W), F32)] if fuse_conv else []),
        scratch_shapes=[pltpu.VMEM((8, CONV_W), F32)] if fuse_conv else [],
        name="merge",
    )
    return out if fuse_conv else out[0]


def _over_quad(x, op):
    x = op(x, pltpu.roll(x, HEADS, 0))
    return op(x, pltpu.roll(x, 2 * HEADS, 0))


def _lane_dot(k3, q):
    return jnp.sum(k3 * q, axis=-1, keepdims=True)


def _quad_attention(k3, v3, qt, bias3, k_new, v_new, bias_new):
    s = _lane_dot(k3, qt[None]) + bias3
    m = jnp.max(s, axis=0)
    if k_new is not None:
        s_new = _lane_dot(k_new, qt) + bias_new
        m = jnp.maximum(m, s_new)
    m = _over_quad(m, jnp.maximum)
    e = jnp.exp2(s - m[None])
    l = jnp.sum(e, axis=0)
    acc = jnp.sum(e * v3, axis=0)
    if k_new is not None:
        e_new = jnp.exp2(s_new - m)
        l = l + e_new
        acc = acc + e_new * v_new
    l = _over_quad(l, jnp.add)
    acc = _over_quad(acc, jnp.add)
    return acc / l, m * LN2 + jnp.log(l)


def _sample_kernel(qd_ref, qt_ref, kn_ref, vn_ref, sl_ref,
                   k0_ref, v0_ref, k1_ref, v1_ref, k2_ref, v2_ref, mk_ref, mv_ref, *new_refs, t_new):
    k_refs = (k0_ref, k1_ref, k2_ref)
    v_refs = (v0_ref, v1_ref, v2_ref)
    oa_ref, om_ref = new_refs[N_GROUPS * 2], new_refs[N_GROUPS * 2 + 1]
    bias0_ref, bias0_new_ref, bias1_ref, bias2_ref = new_refs[-4:]
    bias_refs = (None, bias1_ref, bias2_ref)
    quad16 = lax.broadcasted_iota(jnp.int32, (ROWS16, HEAD_DIM), 0) >> 2
    q_scale = SCALE * LOG2E
    win0 = DIL_GROUPS[0][0]

    @pl.when(pl.program_id(0) == 0)
    def _():
        a_idx = lax.broadcasted_iota(jnp.int32, (win0 // 4, ROWS16, HEAD_DIM), 0)
        b_idx = lax.broadcasted_iota(jnp.int32, (win0 // 4, ROWS16, HEAD_DIM), 1) >> 2
        tok = a_idx * 4 + b_idx
        slope0 = sl_ref[0] * LOG2E
        for t in range(t_new):
            bias0_ref[t] = jnp.where(tok >= t, -(slope0[None] * (win0 + t - tok).astype(F32)), NEG_INF)
            bias0_new_ref[t] = jnp.where(quad16 <= t, -(slope0 * (t - quad16).astype(F32)), NEG_INF)
        i_idx = lax.broadcasted_iota(jnp.int32, (BAND, ROWS16, HEAD_DIM), 0)
        for g in (1, 2):
            win, dil = DIL_GROUPS[g]
            bias_refs[g][...] = -((sl_ref[g] * LOG2E)[None] * (win - i_idx * dil).astype(F32))

    outs, lses = [], []

    k3, v3 = k0_ref[0], v0_ref[0]
    o0 = jnp.zeros((ROWS16, HEAD_DIM), F32)
    l0 = jnp.zeros((ROWS16, HEAD_DIM), F32)
    for t in range(t_new):
        o_t, l_t = _quad_attention(k3, v3, qt_ref[0, 0, t] * q_scale, bias0_ref[t],
                                   kn_ref[0, 0], vn_ref[0, 0], bias0_new_ref[t])
        o0 = jnp.where(quad16 == t, o_t, o0)
        l0 = jnp.where(quad16 == t, l_t, l0)
    outs.append(o0)
    lses.append(l0)

    for g in (1, 2):
        q16 = qd_ref[0, g - 1] * q_scale
        k3 = k_refs[g][0, :, 0:ROWS16, :]
        v3 = v_refs[g][0, :, 0:ROWS16, :]
        s = _lane_dot(k3, q16[None]) + bias_refs[g][...]
        s_new = _lane_dot(kn_ref[0, g], q16)
        m = jnp.maximum(jnp.max(s, axis=0), s_new)
        e = jnp.exp2(s - m[None])
        e_new = jnp.exp2(s_new - m)
        l = jnp.sum(e, axis=0) + e_new
        acc = jnp.sum(e * v3, axis=0) + e_new * vn_ref[0, g]
        outs.append(acc / l)
        lses.append(m * LN2 + jnp.log(l))

    m = jnp.maximum(jnp.maximum(lses[0], lses[1]), lses[2])
    w = [jnp.exp(x - m) for x in lses]
    oa_ref[0] = (w[0] * outs[0] + w[1] * outs[1] + w[2] * outs[2]) / (w[0] + w[1] + w[2])

    k3, v3 = mk_ref[0], mv_ref[0]
    om = jnp.zeros((ROWS16, HEAD_DIM), F32)
    for t in range(t_new):
        o_t, _ = _quad_attention(k3, v3, qt_ref[0, 1, t] * q_scale, jnp.zeros((1, ROWS16, HEAD_DIM), F32),
                                 None, None, None)
        om = jnp.where(quad16 == t, o_t, om)
    om_ref[0] = om

    for i, out_ref in enumerate(new_refs[N_GROUPS * 2 + 2:N_GROUPS * 4 + 2]):
        out_ref[0, 0] = (kn_ref, vn_ref)[i % 2][0, i // 2]


def _sample_attention(qd, qt, kn, vn, slopes16, k_states, v_states, shifted, mem_k, mem_v, *, t_new):
    nb = qd.shape[0]
    states = []
    for g in range(N_GROUPS):
        states += [k_states[g], v_states[g]]
    whole = lambda a: pl.BlockSpec((1,) + a.shape[1:], lambda b: (b,) + (0,) * (a.ndim - 1))
    head_rows = lambda a: pl.BlockSpec((1, a.shape[1], ROWS16, HEAD_DIM), lambda b: (b, 0, 0, 0))
    last_rows = lambda a: pl.BlockSpec((1, 1, ROWS16, HEAD_DIM),
                                       lambda b, a=a: (b, a.shape[1] - 1, a.shape[2] // ROWS16 - 1, 0))
    o_spec = pl.BlockSpec((1, ROWS16, HEAD_DIM), lambda b: (b, 0, 0))
    o_shape = jax.ShapeDtypeStruct((nb, ROWS16, HEAD_DIM), F32)
    n_in = 5 + len(states) + 2
    return _pallas_call(
        functools.partial(_sample_kernel, t_new=t_new),
        (qd, qt, kn, vn, slopes16, *states, mem_k, mem_v, *shifted),
        grid=(nb,),
        in_specs=[whole(qd), whole(qt), whole(kn), whole(vn),
                  pl.BlockSpec(slopes16.shape, lambda b: (0, 0, 0))]
                 + [head_rows(s) for s in states] + [whole(mem_k), whole(mem_v)]
                 + [pl.BlockSpec(memory_space=pl.ANY)] * len(shifted),
        out_specs=[o_spec, o_spec] + [last_rows(s) for s in shifted],
        out_shape=[o_shape, o_shape] + [jax.ShapeDtypeStruct(s.shape, F32) for s in shifted],
        input_output_aliases={n_in + i: 2 + i for i in range(len(shifted))},
        scratch_shapes=[pltpu.VMEM((t_new, DIL_GROUPS[0][0] // 4, ROWS16, HEAD_DIM), F32),
                        pltpu.VMEM((t_new, ROWS16, HEAD_DIM), F32),
                        pltpu.VMEM((BAND, ROWS16, HEAD_DIM), F32),
                        pltpu.VMEM((BAND, ROWS16, HEAD_DIM), F32)],
        name="sample_attn",
    )


def kernel(x_prompt, x_sample, mem_prompt, state_conv, state_k_w128, state_v_w128, state_k_w512, state_v_w512, state_k_w2048, state_v_w2048, cache_mem_k, cache_mem_v, ln1_g, ln1_b, w_ffn1_gu, w_ffn1_down, w_in, conv_w, w_mem_kv, w_br_conv, w_br_attn, w_br_mem, w_o, ln2_g, ln2_b, w_ffn2_gu, w_ffn2_down, ln3_g, ln3_b):
    depth = w_in.shape[0]
    seq = x_prompt.shape[1]
    nb, t_new = x_sample.shape[0], x_sample.shape[1]
    assert depth == 1 and x_prompt.shape[0] == 1 and seq % CHUNK == 0
    assert t_new * HEADS == ROWS16 and t_new <= DIL_GROUPS[1][1]

    wb = lambda w: w[0].astype(BF16)
    wc_b, wa_b, wm_b, wo_b = wb(w_br_conv), wb(w_br_attn), wb(w_br_mem), wb(w_o)
    n_ff_blocks = D_FF // FFN_COLS
    cw = conv_w[0]
    slopes_lane = np.broadcast_to(SLOPES[:, :, None], (N_GROUPS, HEADS, HEAD_DIM))
    slopes_band = jnp.asarray(np.pad(slopes_lane.transpose(1, 0, 2), ((0, 0), (0, 8 - N_GROUPS), (0, 0))))
    slopes16 = jnp.asarray(np.tile(slopes_lane, (1, ROWS16 // HEADS, 1)))

    rows_s = nb * SAMPLE_PAD
    xs = jnp.pad(x_sample, ((0, 0), (0, SAMPLE_PAD - t_new), (0, 0))).reshape(rows_s, D_MODEL)
    s1, s1b, w1g, w1u, w1d = _ffn(xs, w_ffn1_gu[0], w_ffn1_gu[0], w_ffn1_down[0], ln1_g, ln1_b,
                                  tm=rows_s, tf=FFN_COLS, up_blk0=n_ff_blocks, emit_bf16=True)
    ps_all, win_b = _matmul(s1b, w_in[0], tm=rows_s, tn=PROJ_COLS, out_dtype=F32, name="in_proj_s", emit_bf16=True)

    def tiles(state):
        win = state.shape[2]
        per_tile = 4 if win <= DIL_GROUPS[1][0] else DIL_GROUPS[2][1]
        return state.reshape(nb, win // per_tile, per_tile * HEADS, HEAD_DIM)

    k_states = [tiles(s) for s in (state_k_w128, state_k_w512, state_k_w2048)]
    v_states = [tiles(s) for s in (state_v_w128, state_v_w512, state_v_w2048)]
    h1, h1b, *shifted = _ffn(x_prompt[0], w1g, w1u, w1d, ln1_g, ln1_b, tm=FFN_ROWS, tf=FFN_COLS, manual_rows=True,
                             windows=k_states + v_states)
    shifted = [shifted[g + part * N_GROUPS] for g in range(N_GROUPS) for part in (0, 1)]
    p_all, qkv = _in_proj(h1b, win_b, tm=PROJ_ROWS, tn=PROJ_COLS)
    mem_kv = _matmul(mem_prompt[0].astype(BF16), w_mem_kv[0], tm=MEM_TOKENS, tn=GROUP_W, out_dtype=F32, name="mem_kv")
    o_attn = _band_attention(qkv, slopes_band)
    o_mem = _mem_attention(p_all, mem_kv, tq=MEM_Q_ROWS)
    h2, conv_p = _merge(None, o_attn, o_mem, p_all, h1, wc_b, wa_b, wm_b, wo_b, ln2_g, ln2_b, tm=MERGE_ROWS, conv_w=cw)

    prompt_states = []
    for g, (win, _) in enumerate(DIL_GROUPS):
        for tensor in (1, 2):
            c0 = tensor * N_GROUPS * GROUP_W + g * GROUP_W
            prompt_states.append(qkv[seq - win:, c0:c0 + GROUP_W].reshape(1, 1, win, HEADS, HEAD_DIM))
    mem_k_p = mem_kv[:, :GROUP_W].reshape(1, 1, MEM_TOKENS, HEADS, HEAD_DIM)
    mem_v_p = mem_kv[:, GROUP_W:].reshape(1, 1, MEM_TOKENS, HEADS, HEAD_DIM)

    ps3 = ps_all.reshape(nb, SAMPLE_PAD, IN_W)

    real = ps3[:, :t_new]
    heads = lambda a, n: a.reshape(nb, t_new, n, HEADS, HEAD_DIM)
    q, k, v = (heads(real[..., c:c + N_GROUPS * GROUP_W], N_GROUPS) for c in (COL_Q, COL_K, COL_VA))
    rows_th = lambda a: a.transpose(0, 2, 1, 3, 4).reshape(nb, a.shape[2], ROWS16, HEAD_DIM)
    over_quad = lambda a: jnp.broadcast_to(a[:, :, None], (nb, t_new, ROWS16 // HEADS, HEADS, HEAD_DIM)
                                           ).reshape(nb, t_new, ROWS16, HEAD_DIM)
    qd = rows_th(q[:, :, 1:])
    qm = real[..., COL_QM:COL_QM + GROUP_W].reshape(nb, t_new, HEADS, HEAD_DIM)
    qt = jnp.stack([over_quad(q[:, :, 0]), over_quad(qm)], axis=1)
    kn, vn = rows_th(k), rows_th(v)

    outs = _sample_attention(qd, qt, kn, vn, slopes16, k_states, v_states, shifted,
                             tiles(cache_mem_k), tiles(cache_mem_v), t_new=t_new)
    pad_rows = lambda a, w: jnp.pad(a.reshape(nb, t_new, w), ((0, 0), (0, SAMPLE_PAD - t_new), (0, 0))
                                    ).reshape(rows_s, w).astype(BF16)
    o_attn_s, o_mem_s = pad_rows(outs[0], GROUP_W), pad_rows(outs[1], GROUP_W)
    sample_states = [s.reshape(1, nb, -1, HEADS, HEAD_DIM) for s in outs[2:]]

    o_conv_s, conv_s = _conv(ps3, state_conv[0], cw, tm=SAMPLE_PAD, t_last=t_new)
    s2 = _merge(o_conv_s.reshape(rows_s, CONV_W), o_attn_s, o_mem_s, ps_all, s1,
                wc_b, wa_b, wm_b, wo_b, ln2_g, ln2_b, tm=rows_s)
    y_s, w2g, w2u, w2d = _ffn(s2, w_ffn2_gu[0], w_ffn2_gu[0], w_ffn2_down[0], ln3_g, ln3_b,
                              tm=rows_s, tf=FFN_COLS, up_blk0=n_ff_blocks, emit_bf16=True, bf16_out=False)
    y_sample = y_s.reshape(nb, SAMPLE_PAD, D_MODEL)[:, :t_new]
    (y_prompt,) = _ffn(h2, w2g, w2u, w2d, ln3_g, ln3_b, tm=FFN_ROWS, tf=FFN_COLS, bf16_out=False, row_buffers=2)

    return (y_prompt[None], y_sample, conv_p[None, None], *prompt_states, mem_k_p, mem_v_p,
            conv_s[None], *sample_states)
```

```python
import functools
import math

import jax
import jax.numpy as jnp
import numpy as np
from jax import lax
from jax.experimental import pallas as pl
from jax.experimental.pallas import tpu as pltpu

F32 = jnp.float32
BF16 = jnp.bfloat16

D_MODEL = 2048
D_FF = 5632
HEAD_DIM = 128
HEADS = 4
GROUP_W = HEADS * HEAD_DIM
DIL_GROUPS = ((128, 1), (512, 4), (2048, 16))
N_GROUPS = len(DIL_GROUPS)
BAND = 128
CHUNK = BAND * DIL_GROUPS[-1][1]
CONV_W = 1024
MEM_TOKENS = 256
IN_W = 14336
COL_B, COL_C, COL_V = 0, 1024, 2048
COL_Q, COL_K, COL_VA = 3072, 4608, 6144
COL_QM = 7680
COL_GC, COL_GA, COL_GM = 8192, 10240, 12288
ALPHA = 2.0 ** 0.25
LN_EPS = 1e-5
NEG_INF = -1e30
SCALE = HEAD_DIM ** -0.5
LOG2E = 1.4426950408889634
LN2 = 0.6931471805599453
SLOPES = np.exp2(np.float32(-8.0) * np.arange(1, 13, dtype=np.float32) / np.float32(12)).reshape(3, 4)

SAMPLE_PAD = 8
FFN_ROWS, FFN_COLS = 1024, 512
PROJ_ROWS, PROJ_COLS = 1024, 1024
MERGE_ROWS, MERGE_SUB_ROWS = 512, 256
MEM_Q_ROWS = 512
LN_ROWS = 128
ROWS16 = 16
V7X_VMEM_BYTES = 64 * 1024 * 1024
VALUE_SPILL_BYTES = 8 * 1024 * 1024


def _nbytes(shape, dtype):
    return math.prod(1 if d is None else d for d in shape) * jnp.dtype(dtype).itemsize


def _pallas_call(kernel, operands, *, grid, in_specs, out_specs, out_shape, name, scratch_shapes=(),
                 input_output_aliases=None):
    out_specs, out_shape = list(out_specs), list(out_shape)
    vmem = VALUE_SPILL_BYTES
    for spec, array in list(zip(in_specs, operands)) + list(zip(out_specs, out_shape)):
        if spec.block_shape is not None:
            buffers = 2 if spec.pipeline_mode is None else spec.pipeline_mode.buffer_count
            vmem += buffers * _nbytes(spec.block_shape, array.dtype)
    vmem += sum(_nbytes(s.shape, s.dtype) for s in scratch_shapes if s.memory_space == pltpu.VMEM)
    assert vmem <= V7X_VMEM_BYTES, (name, vmem)
    call = pl.pallas_call(
        kernel, grid=grid, in_specs=in_specs, out_specs=out_specs, out_shape=out_shape,
        scratch_shapes=list(scratch_shapes), input_output_aliases=input_output_aliases or {},
        compiler_params=pltpu.CompilerParams(dimension_semantics=("arbitrary",) * len(grid),
                                             vmem_limit_bytes=vmem),
        name=name)
    return call(*operands)


def _layer_norm_rows(z, g, b):
    mu = jnp.mean(z, axis=-1, keepdims=True)
    zc = z - mu
    var = jnp.mean(zc * zc, axis=-1, keepdims=True)
    return zc * lax.rsqrt(var + LN_EPS) * g + b


def _window_loads(b, src_refs, stage_refs, slot, sems):
    return [pltpu.make_async_copy(src.at[b], stage.at[slot], sems.at[slot, n])
            for n, (src, stage) in enumerate(zip(src_refs, stage_refs))]


def _window_stores(b, stage_refs, dst_refs, slot, sems):
    copies = []
    for stage, dst in zip(stage_refs, dst_refs):
        tiles, rows = dst.shape[1], dst.shape[2]
        keep = rows - ROWS16
        if keep:
            copies.append((stage.at[slot, :, pl.ds(ROWS16, keep), :], dst.at[b, :, pl.ds(0, keep), :]))
        copies.append((stage.at[slot, pl.ds(1, tiles - 1), pl.ds(0, ROWS16), :],
                       dst.at[b, pl.ds(0, tiles - 1), pl.ds(keep, ROWS16), :]))
        copies.append((stage.at[slot, pl.ds(tiles - 1, 1), pl.ds(keep, ROWS16), :],
                       dst.at[b, pl.ds(tiles - 1, 1), pl.ds(keep, ROWS16), :]))
    return [pltpu.make_async_copy(s, d, sems.at[slot, n]) for n, (s, d) in enumerate(copies)]


def _ffn_kernel(*refs, n_ff, row_chunk, emit_bf16, bf16_out, manual_rows, n_windows, n_requests):
    x_ref, wg_ref, wu_ref, wd_ref, g_ref, b_ref = refs[:6]
    win_src = refs[6:6 + n_windows]
    y_ref = refs[6 + n_windows]
    yb_ref = refs[7 + n_windows] if bf16_out else None
    rest = refs[7 + bf16_out + n_windows:]
    if emit_bf16:
        wb_refs, rest = rest[:3], rest[3:]
    win_dst, rest = rest[:n_windows], rest[n_windows:]
    xb_ref, rest = rest[0], rest[1:]
    i, j = pl.program_id(0), pl.program_id(1)
    n_tiles = pl.num_programs(0)
    tm = xb_ref.shape[0]

    if manual_rows:
        y_hbm, yb_hbm = y_ref, yb_ref
        n_ring = 1 + bf16_out
        xbuf_ref, xsem, y_ref = rest[:3]
        ring_refs, out_sems = rest[3:3 + n_ring], rest[3 + n_ring]
        rest = rest[4 + n_ring:]

        def x_copy(tile):
            return pltpu.make_async_copy(x_ref.at[pl.ds(pl.multiple_of(tile * tm, tm), tm), :], xbuf_ref, xsem.at[0])

        @pl.when((i == 0) & (j == 0))
        def _():
            x_copy(0).start()

    if n_windows:
        half = n_windows // 2
        stage_refs, load_sems, store_sems = rest[:half], rest[half], rest[half + 1]
        step = i * n_ff + j
        n_items = 2 * n_requests
        for part in (0, 1):
            srcs, dsts = win_src[part * half:(part + 1) * half], win_dst[part * half:(part + 1) * half]
            mine = step % 2 == part

            @pl.when(mine & (step >= 2) & (step < n_items + 2))
            def _():
                for copy in _window_stores((step - 2) // 2, stage_refs, dsts, part, store_sems):
                    copy.wait()

            @pl.when(mine & (step < n_items))
            def _():
                for copy in _window_loads(step // 2, srcs, stage_refs, part, load_sems):
                    copy.start()

            @pl.when(jnp.logical_not(mine) & (step >= 1) & (step < n_items + 1))
            def _():
                for copy in _window_loads((step - 1) // 2, srcs, stage_refs, part, load_sems):
                    copy.wait()
                for copy in _window_stores((step - 1) // 2, stage_refs, dsts, part, store_sems):
                    copy.start()

    @pl.when(j == 0)
    def _():
        if manual_rows:
            x_copy(i).wait()
        x = xbuf_ref[...] if manual_rows else x_ref[...]
        xb_ref[...] = x.astype(BF16)
        y_ref[...] = (2.0 * ALPHA) * x

    if manual_rows:
        @pl.when((j == 1) & (i + 1 < n_tiles))
        def _():
            x_copy(i + 1).start()

    wg, wu, wd = wg_ref[...], wu_ref[...], wd_ref[...]
    if emit_bf16:
        wg, wu, wd = wg.astype(BF16), wu.astype(BF16), wd.astype(BF16)
        wb_refs[0][...], wb_refs[1][...], wb_refs[2][...] = wg, wu, wd
    xb = xb_ref[...]
    gate = jnp.dot(xb, wg, preferred_element_type=F32)
    up = jnp.dot(xb, wu, preferred_element_type=F32)
    act = (gate * jax.nn.sigmoid(gate) * up).astype(BF16)
    y_ref[...] += jnp.dot(act, wd, preferred_element_type=F32)

    if not manual_rows:
        @pl.when(j == n_ff - 1)
        def _():
            def chunk(c, carry):
                rows = pl.ds(pl.multiple_of(c * row_chunk, row_chunk), row_chunk)
                y = _layer_norm_rows(0.5 * y_ref[rows, :], g_ref[...], b_ref[...])
                y_ref[rows, :] = y
                if bf16_out:
                    yb_ref[rows, :] = y.astype(BF16)
                return carry
            lax.fori_loop(0, tm // row_chunk, chunk, 0)
    else:
        def out_copies(slot, row0):
            dsts = (y_hbm, yb_hbm)[:n_ring]
            return [pltpu.make_async_copy(ring.at[slot], dst.at[pl.ds(row0, row_chunk), :], out_sems.at[slot, n])
                    for n, (ring, dst) in enumerate(zip(ring_refs, dsts))]

        def wait_slot(slot):
            for copy in out_copies(slot, 0):
                copy.wait()

        @pl.when(j == n_ff - 1)
        def _():
            n_chunks = tm // row_chunk
            for c in range(n_chunks):
                slot = c % 2
                if c >= 2:
                    wait_slot(slot)
                else:
                    pl.when(i > 0)(functools.partial(wait_slot, slot))
                y = _layer_norm_rows(0.5 * y_ref[c * row_chunk:(c + 1) * row_chunk, :], g_ref[...], b_ref[...])
                ring_refs[0][slot] = y
                if bf16_out:
                    ring_refs[1][slot] = y.astype(BF16)
                for copy in out_copies(slot, pl.multiple_of(i * tm, tm) + c * row_chunk):
                    copy.start()

            @pl.when(i == n_tiles - 1)
            def _():
                wait_slot(0)
                wait_slot(1)


def _ffn(x, w_gate, w_up, w_down, g, b, *, tm, tf, up_blk0=0, emit_bf16=False, bf16_out=True,
         row_buffers=1, manual_rows=False, windows=()):
    rows = x.shape[0]
    n_ff = D_FF // tf
    n_steps = (rows // tm) * n_ff
    n_requests = windows[0].shape[0] if windows else 0
    assert not emit_bf16 or rows == tm
    assert n_steps >= 2 * n_requests + 2
    row_mode = pl.Buffered(1) if (rows == tm or row_buffers == 1) else None
    hbm = pl.BlockSpec(memory_space=pl.ANY)
    w_col = pl.BlockSpec((D_MODEL, tf), lambda i, j: (0, j))
    w_col_tile = pl.BlockSpec((None, D_MODEL, tf), lambda i, j: (j, 0, 0))
    tiled = w_gate.ndim == 3
    w_row = pl.BlockSpec((tf, D_MODEL), lambda i, j: (j, 0))
    row_tile = pl.BlockSpec((tm, D_MODEL), lambda i, j: (i, 0), pipeline_mode=row_mode)
    out_specs = [hbm if manual_rows else row_tile] * (1 + bf16_out)
    out_shape = [jax.ShapeDtypeStruct((rows, D_MODEL), F32), jax.ShapeDtypeStruct((rows, D_MODEL), BF16)][:1 + bf16_out]
    scratch = [pltpu.VMEM((tm, D_MODEL), BF16)]
    if manual_rows:
        row_chunk = min(tm, LN_ROWS)
        assert n_ff >= 2 and (tm // row_chunk) % 2 == 0
        scratch += [pltpu.VMEM((tm, D_MODEL), F32), pltpu.SemaphoreType.DMA((1,)), pltpu.VMEM((tm, D_MODEL), F32)]
        scratch += [pltpu.VMEM((2, row_chunk, D_MODEL), dt) for dt in (F32, BF16)[:1 + bf16_out]]
        scratch += [pltpu.SemaphoreType.DMA((2, 1 + bf16_out))]
    if emit_bf16:
        out_specs += [w_col_tile, w_col_tile, w_row]
        out_shape += [jax.ShapeDtypeStruct((n_ff, D_MODEL, tf), BF16)] * 2 + [jax.ShapeDtypeStruct((D_FF, D_MODEL), BF16)]
    if windows:
        out_specs += [hbm] * len(windows)
        out_shape += [jax.ShapeDtypeStruct(w.shape, w.dtype) for w in windows]
        half = windows[:len(windows) // 2]
        scratch += [pltpu.VMEM((2,) + w.shape[1:], w.dtype) for w in half]
        scratch += [pltpu.SemaphoreType.DMA((2, len(half))),
                    pltpu.SemaphoreType.DMA((2, sum(2 + (w.shape[2] > ROWS16) for w in half)))]
    return _pallas_call(
        functools.partial(_ffn_kernel, n_ff=n_ff, row_chunk=min(tm, LN_ROWS), emit_bf16=emit_bf16, bf16_out=bf16_out,
                          manual_rows=manual_rows, n_windows=len(windows), n_requests=n_requests),
        (x, w_gate, w_up, w_down, g, b, *windows),
        grid=(rows // tm, n_ff),
        in_specs=[
            hbm if manual_rows else row_tile,
            w_col_tile if tiled else w_col,
            w_col_tile if tiled else pl.BlockSpec((D_MODEL, tf), lambda i, j: (0, j + up_blk0)),
            w_row,
            pl.BlockSpec((1, D_MODEL), lambda i, j: (0, 0)),
            pl.BlockSpec((1, D_MODEL), lambda i, j: (0, 0)),
        ] + [hbm] * len(windows),
        out_specs=out_specs,
        out_shape=out_shape,
        scratch_shapes=scratch,
        name="ffn_cast" if emit_bf16 else "ffn",
    )


def _mm_kernel(x_ref, w_ref, o_ref, *wb_ref):
    w = w_ref[...].astype(BF16)
    if wb_ref:
        wb_ref[0][...] = w
    o_ref[...] = jnp.dot(x_ref[...], w, preferred_element_type=F32).astype(o_ref.dtype)


def _matmul(x, w, *, tm, tn, out_dtype, name, emit_bf16=False):
    rows, k = x.shape
    n_cols = w.shape[1]
    assert not emit_bf16 or rows == tm
    w_spec = pl.BlockSpec((k, tn), lambda i, j: (0, j))
    out_specs = [pl.BlockSpec((tm, tn), lambda i, j: (i, j))]
    out_shape = [jax.ShapeDtypeStruct((rows, n_cols), out_dtype)]
    if emit_bf16:
        out_specs.append(pl.BlockSpec((None, k, tn), lambda i, j: (j, 0, 0)))
        out_shape.append(jax.ShapeDtypeStruct((n_cols // tn, k, tn), BF16))
    out = _pallas_call(
        _mm_kernel,
        (x, w),
        grid=(rows // tm, n_cols // tn),
        in_specs=[pl.BlockSpec((tm, k), lambda i, j: (i, 0)), w_spec],
        out_specs=out_specs,
        out_shape=out_shape,
        name=name,
    )
    return out if emit_bf16 else out[0]


def _in_proj_kernel(x_ref, w_ref, p_ref, qkv_ref, *, n_own):
    j = pl.program_id(1)

    @pl.when(j < n_own)
    def _():
        acc = jnp.dot(x_ref[...], w_ref[...], preferred_element_type=F32)
        qkv_ref[...] = acc
        p_ref[...] = acc.astype(BF16)

    @pl.when(j >= n_own)
    def _():
        p_ref[...] = jnp.dot(x_ref[...], w_ref[...], preferred_element_type=F32).astype(BF16)


def _in_proj(x, w, *, tm, tn):
    rows, k = x.shape
    assert w.shape == (IN_W // tn, k, tn) and COL_Q % tn == 0 and COL_GC % tn == 0
    c_lo, n_own = COL_Q // tn, (COL_GC - COL_Q) // tn

    def col_block(j):
        return jnp.where(j < n_own, j + c_lo, jnp.where(j < n_own + c_lo, j - n_own, j))

    return _pallas_call(
        functools.partial(_in_proj_kernel, n_own=n_own),
        (x, w),
        grid=(rows // tm, IN_W // tn),
        in_specs=[pl.BlockSpec((tm, k), lambda i, j: (i, 0)),
                  pl.BlockSpec((None, k, tn), lambda i, j: (col_block(j), 0, 0))],
        out_specs=[pl.BlockSpec((tm, tn), lambda i, j: (i, col_block(j))),
                   pl.BlockSpec((tm, tn), lambda i, j: (i, jnp.minimum(j, n_own - 1)))],
        out_shape=[jax.ShapeDtypeStruct((rows, IN_W), BF16),
                   jax.ShapeDtypeStruct((rows, COL_GC - COL_Q), F32)],
        name="in_proj",
    )


def _rows(ref, start, dil):
    if dil == 1:
        return ref[start:start + BAND, :]
    return ref[pl.ds(start, BAND, stride=dil), :]


def _band_kernel(sl_ref, q0_ref, k0_ref, v0_ref, q1_ref, k1_ref, v1_ref, q2_ref, k2_ref, v2_ref,
                 o_ref, kt0_ref, vt0_ref, kt1_ref, vt1_ref, kt2_ref, vt2_ref, og_ref, lg_ref):
    c = pl.program_id(1)
    q_refs = (q0_ref, q1_ref, q2_ref)
    k_refs = (k0_ref, k1_ref, k2_ref)
    v_refs = (v0_ref, v1_ref, v2_ref)
    kt_refs = (kt0_ref, kt1_ref, kt2_ref)
    vt_refs = (vt0_ref, vt1_ref, vt2_ref)

    @pl.when(c == 0)
    def _():
        for ref in kt_refs + vt_refs:
            ref[...] = jnp.zeros(ref.shape, F32)

    qi = lax.broadcasted_iota(jnp.int32, (BAND, 2 * BAND), 0)
    ki = lax.broadcasted_iota(jnp.int32, (BAND, 2 * BAND), 1)
    rel = BAND + qi - ki
    valid = (rel >= 0) & (rel <= BAND)
    no_prev = jnp.where(ki < BAND, jnp.where(c == 0, NEG_INF, 0.0), 0.0)

    for g, (_, dil) in enumerate(DIL_GROUPS):
        slope = sl_ref[0, g:g + 1, 0:1] * LOG2E
        bias = jnp.where(valid, -(slope * (rel * dil).astype(F32)), NEG_INF)
        bias_first = bias + no_prev
        span = BAND * dil
        for r in range(dil):
            for n in range(CHUNK // span):
                start = r + n * span
                q = _rows(q_refs[g], start, dil).astype(BF16)
                if n == 0:
                    k_prev, v_prev = _rows(kt_refs[g], r, dil), _rows(vt_refs[g], r, dil)
                else:
                    k_prev, v_prev = _rows(k_refs[g], start - span, dil), _rows(v_refs[g], start - span, dil)
                k = jnp.concatenate([k_prev, _rows(k_refs[g], start, dil)], axis=0).astype(BF16)
                v = jnp.concatenate([v_prev, _rows(v_refs[g], start, dil)], axis=0).astype(BF16)
                s = lax.dot_general(q, k, (((1,), (1,)), ((), ())), preferred_element_type=F32) * (SCALE * LOG2E)
                s = s + (bias_first if n == 0 else bias)
                m = jnp.max(s, axis=-1, keepdims=True)
                p = jnp.exp2(s - m)
                l = jnp.sum(p, axis=-1, keepdims=True)
                o = jnp.dot(p.astype(BF16), v, preferred_element_type=F32) / l
                lse = jnp.broadcast_to(m * LN2 + jnp.log(l), (BAND, HEAD_DIM))
                if dil == 1:
                    og_ref[g, start:start + BAND, :] = o
                    lg_ref[g, start:start + BAND, :] = lse
                else:
                    og_ref[g, pl.ds(start, BAND, stride=dil), :] = o
                    lg_ref[g, pl.ds(start, BAND, stride=dil), :] = lse
        kt_refs[g][...] = k_refs[g][CHUNK - span:CHUNK, :]
        vt_refs[g][...] = v_refs[g][CHUNK - span:CHUNK, :]

    rows_per_step = 256

    def combine(i, carry):
        rows = pl.ds(pl.multiple_of(i * rows_per_step, rows_per_step), rows_per_step)
        l0, l1, l2 = lg_ref[0, rows, :], lg_ref[1, rows, :], lg_ref[2, rows, :]
        m = jnp.maximum(jnp.maximum(l0, l1), l2)
        e0, e1, e2 = jnp.exp(l0 - m), jnp.exp(l1 - m), jnp.exp(l2 - m)
        o = (e0 * og_ref[0, rows, :] + e1 * og_ref[1, rows, :] + e2 * og_ref[2, rows, :]) / (e0 + e1 + e2)
        o_ref[rows, :] = o.astype(o_ref.dtype)
        return carry

    lax.fori_loop(0, CHUNK // rows_per_step, combine, 0)


def _band_attention(qkv, slopes):
    seq = qkv.shape[0]
    blk = (CHUNK, HEAD_DIM)
    per_tensor = N_GROUPS * HEADS
    in_specs = [pl.BlockSpec((1, 8, HEAD_DIM), lambda h, c: (h, 0, 0))]
    for g in range(N_GROUPS):
        for tensor in range(3):
            in_specs.append(pl.BlockSpec(
                blk, lambda h, c, g=g, tensor=tensor: (c, tensor * per_tensor + g * HEADS + h)))
    tails = []
    for _, dil in DIL_GROUPS:
        tails += [pltpu.VMEM((BAND * dil, HEAD_DIM), F32)] * 2
    return _pallas_call(
        _band_kernel,
        (slopes, *([qkv] * 9)),
        grid=(HEADS, seq // CHUNK),
        in_specs=in_specs,
        out_specs=[pl.BlockSpec(blk, lambda h, c: (c, h))],
        out_shape=[jax.ShapeDtypeStruct((seq, GROUP_W), BF16)],
        scratch_shapes=tails + [pltpu.VMEM((N_GROUPS, CHUNK, HEAD_DIM), F32),
                                pltpu.VMEM((N_GROUPS, CHUNK, HEAD_DIM), F32)],
        name="band_attn",
    )[0]


def _mem_attn_kernel(q_ref, mk_ref, mv_ref, o_ref):
    for h in range(HEADS):
        hs = slice(h * HEAD_DIM, (h + 1) * HEAD_DIM)
        q = q_ref[:, hs]
        k = mk_ref[:, hs].astype(BF16)
        v = mv_ref[:, hs].astype(BF16)
        s = lax.dot_general(q, k, (((1,), (1,)), ((), ())), preferred_element_type=F32) * SCALE
        m = jnp.max(s, axis=-1, keepdims=True)
        p = jnp.exp(s - m)
        l = jnp.sum(p, axis=-1, keepdims=True)
        o = jnp.dot(p.astype(BF16), v, preferred_element_type=F32) / l
        o_ref[:, hs] = o.astype(o_ref.dtype)


def _mem_attention(p_all, mem_kv, *, tq):
    seq = p_all.shape[0]
    return _pallas_call(
        _mem_attn_kernel,
        (p_all, mem_kv, mem_kv),
        grid=(seq // tq,),
        in_specs=[pl.BlockSpec((tq, GROUP_W), lambda i: (i, COL_QM // GROUP_W)),
                  pl.BlockSpec((MEM_TOKENS, GROUP_W), lambda i: (0, 0)),
                  pl.BlockSpec((MEM_TOKENS, GROUP_W), lambda i: (0, 1))],
        out_specs=[pl.BlockSpec((tq, GROUP_W), lambda i: (i, 0))],
        out_shape=[jax.ShapeDtypeStruct((seq, GROUP_W), BF16)],
        name="mem_attn",
    )[0]


def _conv_rows(b, c, v, w_ref, carry_ref):
    u = c.astype(F32) * v.astype(F32)
    c2 = carry_ref[0:1, :]
    c1 = carry_ref[1:2, :]
    row = lax.broadcasted_iota(jnp.int32, u.shape, 0)
    u1 = jnp.where(row == 0, c1, pltpu.roll(u, 1, 0))
    u2 = jnp.where(row == 0, c2, jnp.where(row == 1, c1, pltpu.roll(u, 2, 0)))
    return b.astype(F32) * (w_ref[0:1, :] * u2 + w_ref[1:2, :] * u1 + w_ref[2:3, :] * u), u


def _conv_kernel(b_ref, c_ref, v_ref, st_ref, w_ref, o_ref, ns_ref, carry_ref, *, t_last):
    i = pl.program_id(1)

    @pl.when(i == 0)
    def _():
        carry_ref[0:2, :] = st_ref[0]

    o, u = _conv_rows(b_ref[0], c_ref[0], v_ref[0], w_ref, carry_ref)
    o_ref[0] = o.astype(o_ref.dtype)
    last2 = u[t_last - 2:t_last, :]
    carry_ref[0:2, :] = last2
    ns_ref[0] = last2


def _conv(p3, state, conv_w, *, tm, t_last):
    nb, t = p3.shape[0], p3.shape[1]
    blk = (1, tm, CONV_W)
    return _pallas_call(
        functools.partial(_conv_kernel, t_last=t_last),
        (p3, p3, p3, state, conv_w),
        grid=(nb, t // tm),
        in_specs=[
            pl.BlockSpec(blk, lambda b, i: (b, i, COL_B // CONV_W)),
            pl.BlockSpec(blk, lambda b, i: (b, i, COL_C // CONV_W)),
            pl.BlockSpec(blk, lambda b, i: (b, i, COL_V // CONV_W)),
            pl.BlockSpec((1, 2, CONV_W), lambda b, i: (b, 0, 0)),
            pl.BlockSpec((3, CONV_W), lambda b, i: (0, 0)),
        ],
        out_specs=[pl.BlockSpec(blk, lambda b, i: (b, i, 0)),
                   pl.BlockSpec((1, 2, CONV_W), lambda b, i: (b, 0, 0))],
        out_shape=[jax.ShapeDtypeStruct((nb, t, CONV_W), BF16),
                   jax.ShapeDtypeStruct((nb, 2, CONV_W), F32)],
        scratch_shapes=[pltpu.VMEM((8, CONV_W), F32)],
        name="conv",
    )


def _merge_kernel(*refs, sub_rows, fuse_conv):
    if fuse_conv:
        b_ref, c_ref, v_ref, cw_ref = refs[:4]
        refs = refs[4:]
    else:
        oc_ref = refs[0]
        refs = refs[1:]
    oa_ref, om_ref, gc_ref, ga_ref, gm_ref, h_ref, wc_ref, wa_ref, wm_ref, wo_ref, g_ref, beta_ref, y_ref = refs[:13]
    if fuse_conv:
        ns_ref, carry_ref = refs[13:]

        @pl.when(pl.program_id(0) == 0)
        def _():
            carry_ref[...] = jnp.zeros(carry_ref.shape, F32)

    for r0 in range(0, h_ref.shape[0], sub_rows):
        rows = slice(r0, r0 + sub_rows)
        if fuse_conv:
            oc, u = _conv_rows(b_ref[rows, :], c_ref[rows, :], v_ref[rows, :], cw_ref, carry_ref)
            oc = oc.astype(BF16)
            carry_ref[0:2, :] = u[sub_rows - 2:sub_rows, :]
        else:
            oc = oc_ref[rows, :]
        mc = jnp.dot(oc, wc_ref[...], preferred_element_type=F32)
        ma = jnp.dot(oa_ref[rows, :], wa_ref[...], preferred_element_type=F32)
        mm = jnp.dot(om_ref[rows, :], wm_ref[...], preferred_element_type=F32)
        mix = (jax.nn.sigmoid(gc_ref[rows, :].astype(F32)) * mc
               + jax.nn.sigmoid(ga_ref[rows, :].astype(F32)) * ma
               + jax.nn.sigmoid(gm_ref[rows, :].astype(F32)) * mm)
        z = ALPHA * h_ref[rows, :] + jnp.dot(mix.astype(BF16), wo_ref[...], preferred_element_type=F32)
        y_ref[rows, :] = _layer_norm_rows(z, g_ref[...], beta_ref[...])
    if fuse_conv:
        ns_ref[...] = carry_ref[0:2, :]


def _merge(o_conv, o_attn, o_mem, p_all, h, w_c, w_a, w_m, w_o, g, b, *, tm, conv_w=None):
    rows = h.shape[0]
    fuse_conv = o_conv is None
    single = pl.Buffered(1)
    row_blk = lambda w: pl.BlockSpec((tm, w), lambda i: (i, 0))
    col_blk = lambda w, col: pl.BlockSpec((tm, w), lambda i: (i, col // w))
    whole = lambda a: pl.BlockSpec(a.shape, lambda i: (0, 0), pipeline_mode=single)
    if fuse_conv:
        conv_specs = [col_blk(CONV_W, COL_B), col_blk(CONV_W, COL_C), col_blk(CONV_W, COL_V), whole(conv_w)]
        conv_args = [p_all, p_all, p_all, conv_w]
    else:
        conv_specs, conv_args = [row_blk(CONV_W)], [o_conv]
    out = _pallas_call(
        functools.partial(_merge_kernel, sub_rows=min(tm, MERGE_SUB_ROWS), fuse_conv=fuse_conv),
        (*conv_args, o_attn, o_mem, p_all, p_all, p_all, h, w_c, w_a, w_m, w_o, g, b),
        grid=(rows // tm,),
        in_specs=conv_specs + [row_blk(GROUP_W), row_blk(GROUP_W),
                               col_blk(D_MODEL, COL_GC), col_blk(D_MODEL, COL_GA), col_blk(D_MODEL, COL_GM),
                               row_blk(D_MODEL),
                               whole(w_c), whole(w_a), whole(w_m), whole(w_o), whole(g), whole(b)],
        out_specs=[row_blk(D_MODEL)] + ([pl.BlockSpec((2, CONV_W), lambda i: (0, 0))] if fuse_conv else []),
        out_shape=[jax.ShapeDtypeStruct((rows, D_MODEL), F32)]
                  + ([jax.ShapeDtypeStruct((2, CONV_W), F32)] if fuse_conv else []),
        scratch_shapes=[pltpu.VMEM((8, CONV_W), F32)] if fuse_conv else [],
        name="merge",
    )
    return out if fuse_conv else out[0]


def _over_quad(x, op):
    x = op(x, pltpu.roll(x, HEADS, 0))
    return op(x, pltpu.roll(x, 2 * HEADS, 0))


def _lane_dot(k3, q):
    return jnp.sum(k3 * q, axis=-1, keepdims=True)


def _quad_attention(k3, v3, qt, bias3, k_new, v_new, bias_new):
    s = _lane_dot(k3, qt[None]) + bias3
    m = jnp.max(s, axis=0)
    if k_new is not None:
        s_new = _lane_dot(k_new, qt) + bias_new
        m = jnp.maximum(m, s_new)
    m = _over_quad(m, jnp.maximum)
    e = jnp.exp2(s - m[None])
    l = jnp.sum(e, axis=0)
    acc = jnp.sum(e * v3, axis=0)
    if k_new is not None:
        e_new = jnp.exp2(s_new - m)
        l = l + e_new
        acc = acc + e_new * v_new
    l = _over_quad(l, jnp.add)
    acc = _over_quad(acc, jnp.add)
    return acc / l, m * LN2 + jnp.log(l)


def _sample_kernel(qd_ref, qt_ref, kn_ref, vn_ref, sl_ref,
                   k0_ref, v0_ref, k1_ref, v1_ref, k2_ref, v2_ref, mk_ref, mv_ref, *new_refs, t_new):
    k_refs = (k0_ref, k1_ref, k2_ref)
    v_refs = (v0_ref, v1_ref, v2_ref)
    oa_ref, om_ref = new_refs[N_GROUPS * 2], new_refs[N_GROUPS * 2 + 1]
    bias0_ref, bias0_new_ref, bias1_ref, bias2_ref = new_refs[-4:]
    bias_refs = (None, bias1_ref, bias2_ref)
    quad16 = lax.broadcasted_iota(jnp.int32, (ROWS16, HEAD_DIM), 0) >> 2
    q_scale = SCALE * LOG2E
    win0 = DIL_GROUPS[0][0]

    @pl.when(pl.program_id(0) == 0)
    def _():
        a_idx = lax.broadcasted_iota(jnp.int32, (win0 // 4, ROWS16, HEAD_DIM), 0)
        b_idx = lax.broadcasted_iota(jnp.int32, (win0 // 4, ROWS16, HEAD_DIM), 1) >> 2
        tok = a_idx * 4 + b_idx
        slope0 = sl_ref[0] * LOG2E
        for t in range(t_new):
            bias0_ref[t] = jnp.where(tok >= t, -(slope0[None] * (win0 + t - tok).astype(F32)), NEG_INF)
            bias0_new_ref[t] = jnp.where(quad16 <= t, -(slope0 * (t - quad16).astype(F32)), NEG_INF)
        i_idx = lax.broadcasted_iota(jnp.int32, (BAND, ROWS16, HEAD_DIM), 0)
        for g in (1, 2):
            win, dil = DIL_GROUPS[g]
            bias_refs[g][...] = -((sl_ref[g] * LOG2E)[None] * (win - i_idx * dil).astype(F32))

    outs, lses = [], []

    k3, v3 = k0_ref[0], v0_ref[0]
    o0 = jnp.zeros((ROWS16, HEAD_DIM), F32)
    l0 = jnp.zeros((ROWS16, HEAD_DIM), F32)
    for t in range(t_new):
        o_t, l_t = _quad_attention(k3, v3, qt_ref[0, 0, t] * q_scale, bias0_ref[t],
                                   kn_ref[0, 0], vn_ref[0, 0], bias0_new_ref[t])
        o0 = jnp.where(quad16 == t, o_t, o0)
        l0 = jnp.where(quad16 == t, l_t, l0)
    outs.append(o0)
    lses.append(l0)

    for g in (1, 2):
        q16 = qd_ref[0, g - 1] * q_scale
        k3 = k_refs[g][0, :, 0:ROWS16, :]
        v3 = v_refs[g][0, :, 0:ROWS16, :]
        s = _lane_dot(k3, q16[None]) + bias_refs[g][...]
        s_new = _lane_dot(kn_ref[0, g], q16)
        m = jnp.maximum(jnp.max(s, axis=0), s_new)
        e = jnp.exp2(s - m[None])
        e_new = jnp.exp2(s_new - m)
        l = jnp.sum(e, axis=0) + e_new
        acc = jnp.sum(e * v3, axis=0) + e_new * vn_ref[0, g]
        outs.append(acc / l)
        lses.append(m * LN2 + jnp.log(l))

    m = jnp.maximum(jnp.maximum(lses[0], lses[1]), lses[2])
    w = [jnp.exp(x - m) for x in lses]
    oa_ref[0] = (w[0] * outs[0] + w[1] * outs[1] + w[2] * outs[2]) / (w[0] + w[1] + w[2])

    k3, v3 = mk_ref[0], mv_ref[0]
    om = jnp.zeros((ROWS16, HEAD_DIM), F32)
    for t in range(t_new):
        o_t, _ = _quad_attention(k3, v3, qt_ref[0, 1, t] * q_scale, jnp.zeros((1, ROWS16, HEAD_DIM), F32),
                                 None, None, None)
        om = jnp.where(quad16 == t, o_t, om)
    om_ref[0] = om

    for i, out_ref in enumerate(new_refs[N_GROUPS * 2 + 2:N_GROUPS * 4 + 2]):
        out_ref[0, 0] = (kn_ref, vn_ref)[i % 2][0, i // 2]


def _sample_attention(qd, qt, kn, vn, slopes16, k_states, v_states, shifted, mem_k, mem_v, *, t_new):
    nb = qd.shape[0]
    states = []
    for g in range(N_GROUPS):
        states += [k_states[g], v_states[g]]
    whole = lambda a: pl.BlockSpec((1,) + a.shape[1:], lambda b: (b,) + (0,) * (a.ndim - 1))
    head_rows = lambda a: pl.BlockSpec((1, a.shape[1], ROWS16, HEAD_DIM), lambda b: (b, 0, 0, 0))
    last_rows = lambda a: pl.BlockSpec((1, 1, ROWS16, HEAD_DIM),
                                       lambda b, a=a: (b, a.shape[1] - 1, a.shape[2] // ROWS16 - 1, 0))
    o_spec = pl.BlockSpec((1, ROWS16, HEAD_DIM), lambda b: (b, 0, 0))
    o_shape = jax.ShapeDtypeStruct((nb, ROWS16, HEAD_DIM), F32)
    n_in = 5 + len(states) + 2
    return _pallas_call(
        functools.partial(_sample_kernel, t_new=t_new),
        (qd, qt, kn, vn, slopes16, *states, mem_k, mem_v, *shifted),
        grid=(nb,),
        in_specs=[whole(qd), whole(qt), whole(kn), whole(vn),
                  pl.BlockSpec(slopes16.shape, lambda b: (0, 0, 0))]
                 + [head_rows(s) for s in states] + [whole(mem_k), whole(mem_v)]
                 + [pl.BlockSpec(memory_space=pl.ANY)] * len(shifted),
        out_specs=[o_spec, o_spec] + [last_rows(s) for s in shifted],
        out_shape=[o_shape, o_shape] + [jax.ShapeDtypeStruct(s.shape, F32) for s in shifted],
        input_output_aliases={n_in + i: 2 + i for i in range(len(shifted))},
        scratch_shapes=[pltpu.VMEM((t_new, DIL_GROUPS[0][0] // 4, ROWS16, HEAD_DIM), F32),
                        pltpu.VMEM((t_new, ROWS16, HEAD_DIM), F32),
                        pltpu.VMEM((BAND, ROWS16, HEAD_DIM), F32),
                        pltpu.VMEM((BAND, ROWS16, HEAD_DIM), F32)],
        name="sample_attn",
    )


def kernel(x_prompt, x_sample, mem_prompt, state_conv, state_k_w128, state_v_w128, state_k_w512, state_v_w512, state_k_w2048, state_v_w2048, cache_mem_k, cache_mem_v, ln1_g, ln1_b, w_ffn1_gu, w_ffn1_down, w_in, conv_w, w_mem_kv, w_br_conv, w_br_attn, w_br_mem, w_o, ln2_g, ln2_b, w_ffn2_gu, w_ffn2_down, ln3_g, ln3_b):
    depth = w_in.shape[0]
    seq = x_prompt.shape[1]
    nb, t_new = x_sample.shape[0], x_sample.shape[1]
    assert depth == 1 and x_prompt.shape[0] == 1 and seq % CHUNK == 0
    assert t_new * HEADS == ROWS16 and t_new <= DIL_GROUPS[1][1]

    wb = lambda w: w[0].astype(BF16)
    wc_b, wa_b, wm_b, wo_b = wb(w_br_conv), wb(w_br_attn), wb(w_br_mem), wb(w_o)
    n_ff_blocks = D_FF // FFN_COLS
    cw = conv_w[0]
    slopes_lane = np.broadcast_to(SLOPES[:, :, None], (N_GROUPS, HEADS, HEAD_DIM))
    slopes_band = jnp.asarray(np.pad(slopes_lane.transpose(1, 0, 2), ((0, 0), (0, 8 - N_GROUPS), (0, 0))))
    slopes16 = jnp.asarray(np.tile(slopes_lane, (1, ROWS16 // HEADS, 1)))

    rows_s = nb * SAMPLE_PAD
    xs = jnp.pad(x_sample, ((0, 0), (0, SAMPLE_PAD - t_new), (0, 0))).reshape(rows_s, D_MODEL)
    s1, s1b, w1g, w1u, w1d = _ffn(xs, w_ffn1_gu[0], w_ffn1_gu[0], w_ffn1_down[0], ln1_g, ln1_b,
                                  tm=rows_s, tf=FFN_COLS, up_blk0=n_ff_blocks, emit_bf16=True)
    ps_all, win_b = _matmul(s1b, w_in[0], tm=rows_s, tn=PROJ_COLS, out_dtype=F32, name="in_proj_s", emit_bf16=True)

    def tiles(state):
        win = state.shape[2]
        per_tile = 4 if win <= DIL_GROUPS[1][0] else DIL_GROUPS[2][1]
        return state.reshape(nb, win // per_tile, per_tile * HEADS, HEAD_DIM)

    k_states = [tiles(s) for s in (state_k_w128, state_k_w512, state_k_w2048)]
    v_states = [tiles(s) for s in (state_v_w128, state_v_w512, state_v_w2048)]
    h1, h1b, *shifted = _ffn(x_prompt[0], w1g, w1u, w1d, ln1_g, ln1_b, tm=FFN_ROWS, tf=FFN_COLS, manual_rows=True,
                             windows=k_states + v_states)
    shifted = [shifted[g + part * N_GROUPS] for g in range(N_GROUPS) for part in (0, 1)]
    p_all, qkv = _in_proj(h1b, win_b, tm=PROJ_ROWS, tn=PROJ_COLS)
    mem_kv = _matmul(mem_prompt[0].astype(BF16), w_mem_kv[0], tm=MEM_TOKENS, tn=GROUP_W, out_dtype=F32, name="mem_kv")
    o_attn = _band_attention(qkv, slopes_band)
    o_mem = _mem_attention(p_all, mem_kv, tq=MEM_Q_ROWS)
    h2, conv_p = _merge(None, o_attn, o_mem, p_all, h1, wc_b, wa_b, wm_b, wo_b, ln2_g, ln2_b, tm=MERGE_ROWS, conv_w=cw)

    prompt_states = []
    for g, (win, _) in enumerate(DIL_GROUPS):
        for tensor in (1, 2):
            c0 = tensor * N_GROUPS * GROUP_W + g * GROUP_W
            prompt_states.append(qkv[seq - win:, c0:c0 + GROUP_W].reshape(1, 1, win, HEADS, HEAD_DIM))
    mem_k_p = mem_kv[:, :GROUP_W].reshape(1, 1, MEM_TOKENS, HEADS, HEAD_DIM)
    mem_v_p = mem_kv[:, GROUP_W:].reshape(1, 1, MEM_TOKENS, HEADS, HEAD_DIM)

    ps3 = ps_all.reshape(nb, SAMPLE_PAD, IN_W)

    real = ps3[:, :t_new]
    heads = lambda a, n: a.reshape(nb, t_new, n, HEADS, HEAD_DIM)
    q, k, v = (heads(real[..., c:c + N_GROUPS * GROUP_W], N_GROUPS) for c in (COL_Q, COL_K, COL_VA))
    rows_th = lambda a: a.transpose(0, 2, 1, 3, 4).reshape(nb, a.shape[2], ROWS16, HEAD_DIM)
    over_quad = lambda a: jnp.broadcast_to(a[:, :, None], (nb, t_new, ROWS16 // HEADS, HEADS, HEAD_DIM)
                                           ).reshape(nb, t_new, ROWS16, HEAD_DIM)
    qd = rows_th(q[:, :, 1:])
    qm = real[..., COL_QM:COL_QM + GROUP_W].reshape(nb, t_new, HEADS, HEAD_DIM)
    qt = jnp.stack([over_quad(q[:, :, 0]), over_quad(qm)], axis=1)
    kn, vn = rows_th(k), rows_th(v)

    outs = _sample_attention(qd, qt, kn, vn, slopes16, k_states, v_states, shifted,
                             tiles(cache_mem_k), tiles(cache_mem_v), t_new=t_new)
    pad_rows = lambda a, w: jnp.pad(a.reshape(nb, t_new, w), ((0, 0), (0, SAMPLE_PAD - t_new), (0, 0))
                                    ).reshape(rows_s, w).astype(BF16)
    o_attn_s, o_mem_s = pad_rows(outs[0], GROUP_W), pad_rows(outs[1], GROUP_W)
    sample_states = [s.reshape(1, nb, -1, HEADS, HEAD_DIM) for s in outs[2:]]

    o_conv_s, conv_s = _conv(ps3, state_conv[0], cw, tm=SAMPLE_PAD, t_last=t_new)
    s2 = _merge(o_conv_s.reshape(rows_s, CONV_W), o_attn_s, o_mem_s, ps_all, s1,
                wc_b, wa_b, wm_b, wo_b, ln2_g, ln2_b, tm=rows_s)
    y_s, w2g, w2u, w2d = _ffn(s2, w_ffn2_gu[0], w_ffn2_gu[0], w_ffn2_down[0], ln3_g, ln3_b,
                              tm=rows_s, tf=FFN_COLS, up_blk0=n_ff_blocks, emit_bf16=True, bf16_out=False)
    y_sample = y_s.reshape(nb, SAMPLE_PAD, D_MODEL)[:, :t_new]
    (y_prompt,) = _ffn(h2, w2g, w2u, w2d, ln3_g, ln3_b, tm=FFN_ROWS, tf=FFN_COLS, bf16_out=False, row_buffers=2)

    return (y_prompt[None], y_sample, conv_p[None, None], *prompt_states, mem_k_p, mem_v_p,
            conv_s[None], *sample_states)
```

```python
import functools
import math

import jax
import jax.numpy as jnp
import numpy as np
from jax import lax
from jax.experimental import pallas as pl
from jax.experimental.pallas import tpu as pltpu

F32 = jnp.float32
BF16 = jnp.bfloat16

D_MODEL = 2048
D_FF = 5632
HEAD_DIM = 128
HEADS = 4
GROUP_W = HEADS * HEAD_DIM
DIL_GROUPS = ((128, 1), (512, 4), (2048, 16))
N_GROUPS = len(DIL_GROUPS)
BAND = 128
CHUNK = BAND * DIL_GROUPS[-1][1]
CONV_W = 1024
MEM_TOKENS = 256
IN_W = 14336
COL_B, COL_C, COL_V = 0, 1024, 2048
COL_Q, COL_K, COL_VA = 3072, 4608, 6144
COL_QM = 7680
COL_GC, COL_GA, COL_GM = 8192, 10240, 12288
ALPHA = 2.0 ** 0.25
LN_EPS = 1e-5
NEG_INF = -1e30
SCALE = HEAD_DIM ** -0.5
LOG2E = 1.4426950408889634
LN2 = 0.6931471805599453
SLOPES = np.exp2(np.float32(-8.0) * np.arange(1, 13, dtype=np.float32) / np.float32(12)).reshape(3, 4)

SAMPLE_PAD = 8
FFN_ROWS, FFN_COLS = 1024, 512
PROJ_ROWS, PROJ_COLS = 1024, 1024
MERGE_ROWS, MERGE_SUB_ROWS = 512, 256
MEM_Q_ROWS = 512
LN_ROWS = 128
ROWS16 = 16
V7X_VMEM_BYTES = 64 * 1024 * 1024
VALUE_SPILL_BYTES = 8 * 1024 * 1024


def _nbytes(shape, dtype):
    return math.prod(1 if d is None else d for d in shape) * jnp.dtype(dtype).itemsize


def _pallas_call(kernel, operands, *, grid, in_specs, out_specs, out_shape, name, scratch_shapes=(),
                 input_output_aliases=None):
    out_specs, out_shape = list(out_specs), list(out_shape)
    vmem = VALUE_SPILL_BYTES
    for spec, array in list(zip(in_specs, operands)) + list(zip(out_specs, out_shape)):
        if spec.block_shape is not None:
            buffers = 2 if spec.pipeline_mode is None else spec.pipeline_mode.buffer_count
            vmem += buffers * _nbytes(spec.block_shape, array.dtype)
    vmem += sum(_nbytes(s.shape, s.dtype) for s in scratch_shapes if s.memory_space == pltpu.VMEM)
    assert vmem <= V7X_VMEM_BYTES, (name, vmem)
    call = pl.pallas_call(
        kernel, grid=grid, in_specs=in_specs, out_specs=out_specs, out_shape=out_shape,
        scratch_shapes=list(scratch_shapes), input_output_aliases=input_output_aliases or {},
        compiler_params=pltpu.CompilerParams(dimension_semantics=("arbitrary",) * len(grid),
                                             vmem_limit_bytes=vmem),
        name=name)
    return call(*operands)


def _layer_norm_rows(z, g, b):
    mu = jnp.mean(z, axis=-1, keepdims=True)
    zc = z - mu
    var = jnp.mean(zc * zc, axis=-1, keepdims=True)
    return zc * lax.rsqrt(var + LN_EPS) * g + b


def _window_loads(b, src_refs, stage_refs, slot, sems):
    return [pltpu.make_async_copy(src.at[b], stage.at[slot], sems.at[slot, n])
            for n, (src, stage) in enumerate(zip(src_refs, stage_refs))]


def _window_stores(b, stage_refs, dst_refs, slot, sems):
    copies = []
    for stage, dst in zip(stage_refs, dst_refs):
        tiles, rows = dst.shape[1], dst.shape[2]
        keep = rows - ROWS16
        if keep:
            copies.append((stage.at[slot, :, pl.ds(ROWS16, keep), :], dst.at[b, :, pl.ds(0, keep), :]))
        copies.append((stage.at[slot, pl.ds(1, tiles - 1), pl.ds(0, ROWS16), :],
                       dst.at[b, pl.ds(0, tiles - 1), pl.ds(keep, ROWS16), :]))
        copies.append((stage.at[slot, pl.ds(tiles - 1, 1), pl.ds(keep, ROWS16), :],
                       dst.at[b, pl.ds(tiles - 1, 1), pl.ds(keep, ROWS16), :]))
    return [pltpu.make_async_copy(s, d, sems.at[slot, n]) for n, (s, d) in enumerate(copies)]


def _ffn_kernel(*refs, n_ff, row_chunk, fused_gu, emit_bf16, bf16_out, manual_rows, n_windows, n_requests):
    n_w = 2 if fused_gu else 3
    x_ref, w_refs, (g_ref, b_ref) = refs[0], refs[1:1 + n_w], refs[1 + n_w:3 + n_w]
    refs = refs[3 + n_w:]
    win_src = refs[:n_windows]
    y_ref = refs[n_windows]
    yb_ref = refs[1 + n_windows] if bf16_out else None
    rest = refs[1 + bf16_out + n_windows:]
    if emit_bf16:
        wb_refs, rest = rest[:2], rest[2:]
    win_dst, rest = rest[:n_windows], rest[n_windows:]
    xb_ref, rest = rest[0], rest[1:]
    i, j = pl.program_id(0), pl.program_id(1)
    n_tiles = pl.num_programs(0)
    tm = xb_ref.shape[0]

    if manual_rows:
        y_hbm, yb_hbm = y_ref, yb_ref
        n_ring = 1 + bf16_out
        xbuf_ref, xsem, y_ref = rest[:3]
        ring_refs, out_sems = rest[3:3 + n_ring], rest[3 + n_ring]
        rest = rest[4 + n_ring:]

        def x_copy(tile):
            return pltpu.make_async_copy(x_ref.at[pl.ds(pl.multiple_of(tile * tm, tm), tm), :], xbuf_ref, xsem.at[0])

        @pl.when((i == 0) & (j == 0))
        def _():
            x_copy(0).start()

    if n_windows:
        half = n_windows // 2
        stage_refs, load_sems, store_sems = rest[:half], rest[half], rest[half + 1]
        step = i * n_ff + j
        n_items = 2 * n_requests
        for part in (0, 1):
            srcs, dsts = win_src[part * half:(part + 1) * half], win_dst[part * half:(part + 1) * half]
            mine = step % 2 == part

            @pl.when(mine & (step >= 2) & (step < n_items + 2))
            def _():
                for copy in _window_stores((step - 2) // 2, stage_refs, dsts, part, store_sems):
                    copy.wait()

            @pl.when(mine & (step < n_items))
            def _():
                for copy in _window_loads(step // 2, srcs, stage_refs, part, load_sems):
                    copy.start()

            @pl.when(jnp.logical_not(mine) & (step >= 1) & (step < n_items + 1))
            def _():
                for copy in _window_loads((step - 1) // 2, srcs, stage_refs, part, load_sems):
                    copy.wait()
                for copy in _window_stores((step - 1) // 2, stage_refs, dsts, part, store_sems):
                    copy.start()

    @pl.when(j == 0)
    def _():
        if manual_rows:
            x_copy(i).wait()
        x = xbuf_ref[...] if manual_rows else x_ref[...]
        xb_ref[...] = x.astype(BF16)
        y_ref[...] = (2.0 * ALPHA) * x

    if manual_rows:
        @pl.when((j == 1) & (i + 1 < n_tiles))
        def _():
            x_copy(i + 1).start()

    xb = xb_ref[...]
    wd = w_refs[-1][...]
    tf = wd.shape[0]
    if fused_gu:
        gate_up = jnp.dot(xb, w_refs[0][...], preferred_element_type=F32)
        gate, up = gate_up[:, :tf], gate_up[:, tf:]
    else:
        wg, wu = w_refs[0][...], w_refs[1][...]
        if emit_bf16:
            wg, wu, wd = wg.astype(BF16), wu.astype(BF16), wd.astype(BF16)
            wb_refs[0][:, :tf], wb_refs[0][:, tf:], wb_refs[1][...] = wg, wu, wd
        gate = jnp.dot(xb, wg, preferred_element_type=F32)
        up = jnp.dot(xb, wu, preferred_element_type=F32)
    act = (gate * jax.nn.sigmoid(gate) * up).astype(BF16)
    y_ref[...] += jnp.dot(act, wd, preferred_element_type=F32)

    if not manual_rows:
        @pl.when(j == n_ff - 1)
        def _():
            def chunk(c, carry):
                rows = pl.ds(pl.multiple_of(c * row_chunk, row_chunk), row_chunk)
                y = _layer_norm_rows(0.5 * y_ref[rows, :], g_ref[...], b_ref[...])
                y_ref[rows, :] = y
                if bf16_out:
                    yb_ref[rows, :] = y.astype(BF16)
                return carry
            lax.fori_loop(0, tm // row_chunk, chunk, 0)
    else:
        def out_copies(slot, row0):
            dsts = (y_hbm, yb_hbm)[:n_ring]
            return [pltpu.make_async_copy(ring.at[slot], dst.at[pl.ds(row0, row_chunk), :], out_sems.at[slot, n])
                    for n, (ring, dst) in enumerate(zip(ring_refs, dsts))]

        def wait_slot(slot):
            for copy in out_copies(slot, 0):
                copy.wait()

        @pl.when(j == n_ff - 1)
        def _():
            n_chunks = tm // row_chunk
            for c in range(n_chunks):
                slot = c % 2
                if c >= 2:
                    wait_slot(slot)
                else:
                    pl.when(i > 0)(functools.partial(wait_slot, slot))
                y = _layer_norm_rows(0.5 * y_ref[c * row_chunk:(c + 1) * row_chunk, :], g_ref[...], b_ref[...])
                ring_refs[0][slot] = y
                if bf16_out:
                    ring_refs[1][slot] = y.astype(BF16)
                for copy in out_copies(slot, pl.multiple_of(i * tm, tm) + c * row_chunk):
                    copy.start()

            @pl.when(i == n_tiles - 1)
            def _():
                wait_slot(0)
                wait_slot(1)


def _ffn(x, w_gate, w_up, w_down, g, b, *, tm, tf, up_blk0=0, emit_bf16=False, bf16_out=True,
         row_buffers=1, manual_rows=False, windows=()):
    rows = x.shape[0]
    n_ff = D_FF // tf
    n_steps = (rows // tm) * n_ff
    n_requests = windows[0].shape[0] if windows else 0
    assert not emit_bf16 or rows == tm
    assert n_steps >= 2 * n_requests + 2
    row_mode = pl.Buffered(1) if (rows == tm or row_buffers == 1) else None
    hbm = pl.BlockSpec(memory_space=pl.ANY)
    w_col = pl.BlockSpec((D_MODEL, tf), lambda i, j: (0, j))
    w_gu_tile = pl.BlockSpec((None, D_MODEL, 2 * tf), lambda i, j: (j, 0, 0))
    fused_gu = w_up is None
    assert not (fused_gu and emit_bf16)
    w_row = pl.BlockSpec((tf, D_MODEL), lambda i, j: (j, 0))
    row_tile = pl.BlockSpec((tm, D_MODEL), lambda i, j: (i, 0), pipeline_mode=row_mode)
    out_specs = [hbm if manual_rows else row_tile] * (1 + bf16_out)
    out_shape = [jax.ShapeDtypeStruct((rows, D_MODEL), F32), jax.ShapeDtypeStruct((rows, D_MODEL), BF16)][:1 + bf16_out]
    scratch = [pltpu.VMEM((tm, D_MODEL), BF16)]
    if manual_rows:
        row_chunk = min(tm, LN_ROWS)
        assert n_ff >= 2 and (tm // row_chunk) % 2 == 0
        scratch += [pltpu.VMEM((tm, D_MODEL), F32), pltpu.SemaphoreType.DMA((1,)), pltpu.VMEM((tm, D_MODEL), F32)]
        scratch += [pltpu.VMEM((2, row_chunk, D_MODEL), dt) for dt in (F32, BF16)[:1 + bf16_out]]
        scratch += [pltpu.SemaphoreType.DMA((2, 1 + bf16_out))]
    if emit_bf16:
        out_specs += [w_gu_tile, w_row]
        out_shape += [jax.ShapeDtypeStruct((n_ff, D_MODEL, 2 * tf), BF16), jax.ShapeDtypeStruct((D_FF, D_MODEL), BF16)]
    if windows:
        out_specs += [hbm] * len(windows)
        out_shape += [jax.ShapeDtypeStruct(w.shape, w.dtype) for w in windows]
        half = windows[:len(windows) // 2]
        scratch += [pltpu.VMEM((2,) + w.shape[1:], w.dtype) for w in half]
        scratch += [pltpu.SemaphoreType.DMA((2, len(half))),
                    pltpu.SemaphoreType.DMA((2, sum(2 + (w.shape[2] > ROWS16) for w in half)))]
    return _pallas_call(
        functools.partial(_ffn_kernel, n_ff=n_ff, row_chunk=min(tm, LN_ROWS), fused_gu=fused_gu, emit_bf16=emit_bf16,
                          bf16_out=bf16_out, manual_rows=manual_rows, n_windows=len(windows), n_requests=n_requests),
        (x, *((w_gate,) if fused_gu else (w_gate, w_up)), w_down, g, b, *windows),
        grid=(rows // tm, n_ff),
        in_specs=[hbm if manual_rows else row_tile]
                 + ([w_gu_tile] if fused_gu else [w_col, pl.BlockSpec((D_MODEL, tf), lambda i, j: (0, j + up_blk0))])
                 + [w_row,
                    pl.BlockSpec((1, D_MODEL), lambda i, j: (0, 0)),
                    pl.BlockSpec((1, D_MODEL), lambda i, j: (0, 0))]
                 + [hbm] * len(windows),
        out_specs=out_specs,
        out_shape=out_shape,
        scratch_shapes=scratch,
        name="ffn_cast" if emit_bf16 else "ffn",
    )


def _mm_kernel(x_ref, w_ref, o_ref, *wb_ref):
    w = w_ref[...].astype(BF16)
    if wb_ref:
        wb_ref[0][...] = w
    o_ref[...] = jnp.dot(x_ref[...], w, preferred_element_type=F32).astype(o_ref.dtype)


def _matmul(x, w, *, tm, tn, out_dtype, name, emit_bf16=False):
    rows, k = x.shape
    n_cols = w.shape[1]
    assert not emit_bf16 or rows == tm
    w_spec = pl.BlockSpec((k, tn), lambda i, j: (0, j))
    out_specs = [pl.BlockSpec((tm, tn), lambda i, j: (i, j))]
    out_shape = [jax.ShapeDtypeStruct((rows, n_cols), out_dtype)]
    if emit_bf16:
        out_specs.append(pl.BlockSpec((None, k, tn), lambda i, j: (j, 0, 0)))
        out_shape.append(jax.ShapeDtypeStruct((n_cols // tn, k, tn), BF16))
    out = _pallas_call(
        _mm_kernel,
        (x, w),
        grid=(rows // tm, n_cols // tn),
        in_specs=[pl.BlockSpec((tm, k), lambda i, j: (i, 0)), w_spec],
        out_specs=out_specs,
        out_shape=out_shape,
        name=name,
    )
    return out if emit_bf16 else out[0]


def _in_proj_kernel(x_ref, w_ref, p_ref, qkv_ref, *, n_own):
    j = pl.program_id(1)
    acc = jnp.dot(x_ref[...], w_ref[...], preferred_element_type=F32)
    p_ref[...] = acc.astype(BF16)

    @pl.when(j < n_own)
    def _():
        qkv_ref[...] = acc


def _in_proj(x, w, *, tm, tn):
    rows, k = x.shape
    assert w.shape == (IN_W // tn, k, tn) and COL_Q % tn == 0 and COL_GC % tn == 0
    c_lo, n_own = COL_Q // tn, (COL_GC - COL_Q) // tn

    def col_block(j):
        return jnp.where(j < n_own, j + c_lo, jnp.where(j < n_own + c_lo, j - n_own, j))

    return _pallas_call(
        functools.partial(_in_proj_kernel, n_own=n_own),
        (x, w),
        grid=(rows // tm, IN_W // tn),
        in_specs=[pl.BlockSpec((tm, k), lambda i, j: (i, 0)),
                  pl.BlockSpec((None, k, tn), lambda i, j: (col_block(j), 0, 0))],
        out_specs=[pl.BlockSpec((tm, tn), lambda i, j: (i, col_block(j))),
                   pl.BlockSpec((tm, tn), lambda i, j: (i, jnp.minimum(j, n_own - 1)))],
        out_shape=[jax.ShapeDtypeStruct((rows, IN_W), BF16),
                   jax.ShapeDtypeStruct((rows, COL_GC - COL_Q), F32)],
        name="in_proj",
    )


def _rows(ref, start, dil):
    if dil == 1:
        return ref[start:start + BAND, :]
    return ref[pl.ds(start, BAND, stride=dil), :]


def _band_kernel(sl_ref, q0_ref, k0_ref, v0_ref, q1_ref, k1_ref, v1_ref, q2_ref, k2_ref, v2_ref,
                 o_ref, kt0_ref, vt0_ref, kt1_ref, vt1_ref, kt2_ref, vt2_ref, og_ref, lg_ref):
    c = pl.program_id(1)
    q_refs = (q0_ref, q1_ref, q2_ref)
    k_refs = (k0_ref, k1_ref, k2_ref)
    v_refs = (v0_ref, v1_ref, v2_ref)
    kt_refs = (kt0_ref, kt1_ref, kt2_ref)
    vt_refs = (vt0_ref, vt1_ref, vt2_ref)

    @pl.when(c == 0)
    def _():
        for ref in kt_refs + vt_refs:
            ref[...] = jnp.zeros(ref.shape, F32)

    qi = lax.broadcasted_iota(jnp.int32, (BAND, 2 * BAND), 0)
    ki = lax.broadcasted_iota(jnp.int32, (BAND, 2 * BAND), 1)
    rel = BAND + qi - ki
    valid = (rel >= 0) & (rel <= BAND)
    no_prev = jnp.where(ki < BAND, jnp.where(c == 0, NEG_INF, 0.0), 0.0)

    for g, (_, dil) in enumerate(DIL_GROUPS):
        slope = sl_ref[0, g:g + 1, 0:1] * LOG2E
        bias = jnp.where(valid, -(slope * (rel * dil).astype(F32)), NEG_INF)
        bias_first = bias + no_prev
        span = BAND * dil
        for r in range(dil):
            for n in range(CHUNK // span):
                start = r + n * span
                q = _rows(q_refs[g], start, dil).astype(BF16)
                if n == 0:
                    k_prev, v_prev = _rows(kt_refs[g], r, dil), _rows(vt_refs[g], r, dil)
                else:
                    k_prev, v_prev = _rows(k_refs[g], start - span, dil), _rows(v_refs[g], start - span, dil)
                k = jnp.concatenate([k_prev, _rows(k_refs[g], start, dil)], axis=0).astype(BF16)
                v = jnp.concatenate([v_prev, _rows(v_refs[g], start, dil)], axis=0).astype(BF16)
                s = lax.dot_general(q, k, (((1,), (1,)), ((), ())), preferred_element_type=F32) * (SCALE * LOG2E)
                s = s + (bias_first if n == 0 else bias)
                m = jnp.max(s, axis=-1, keepdims=True)
                p = jnp.exp2(s - m)
                l = jnp.sum(p, axis=-1, keepdims=True)
                o = jnp.dot(p.astype(BF16), v, preferred_element_type=F32) / l
                lse = jnp.broadcast_to(m * LN2 + jnp.log(l), (BAND, HEAD_DIM))
                if dil == 1:
                    og_ref[g, start:start + BAND, :] = o
                    lg_ref[g, start:start + BAND, :] = lse
                else:
                    og_ref[g, pl.ds(start, BAND, stride=dil), :] = o
                    lg_ref[g, pl.ds(start, BAND, stride=dil), :] = lse
        kt_refs[g][...] = k_refs[g][CHUNK - span:CHUNK, :]
        vt_refs[g][...] = v_refs[g][CHUNK - span:CHUNK, :]

    rows_per_step = 256

    def combine(i, carry):
        rows = pl.ds(pl.multiple_of(i * rows_per_step, rows_per_step), rows_per_step)
        l0, l1, l2 = lg_ref[0, rows, :], lg_ref[1, rows, :], lg_ref[2, rows, :]
        m = jnp.maximum(jnp.maximum(l0, l1), l2)
        e0, e1, e2 = jnp.exp(l0 - m), jnp.exp(l1 - m), jnp.exp(l2 - m)
        o = (e0 * og_ref[0, rows, :] + e1 * og_ref[1, rows, :] + e2 * og_ref[2, rows, :]) / (e0 + e1 + e2)
        o_ref[rows, :] = o.astype(o_ref.dtype)
        return carry

    lax.fori_loop(0, CHUNK // rows_per_step, combine, 0)


def _band_attention(qkv, slopes):
    seq = qkv.shape[0]
    blk = (CHUNK, HEAD_DIM)
    per_tensor = N_GROUPS * HEADS
    in_specs = [pl.BlockSpec((1, 8, HEAD_DIM), lambda h, c: (h, 0, 0))]
    for g in range(N_GROUPS):
        for tensor in range(3):
            in_specs.append(pl.BlockSpec(
                blk, lambda h, c, g=g, tensor=tensor: (c, tensor * per_tensor + g * HEADS + h)))
    tails = []
    for _, dil in DIL_GROUPS:
        tails += [pltpu.VMEM((BAND * dil, HEAD_DIM), F32)] * 2
    return _pallas_call(
        _band_kernel,
        (slopes, *([qkv] * 9)),
        grid=(HEADS, seq // CHUNK),
        in_specs=in_specs,
        out_specs=[pl.BlockSpec(blk, lambda h, c: (c, h))],
        out_shape=[jax.ShapeDtypeStruct((seq, GROUP_W), BF16)],
        scratch_shapes=tails + [pltpu.VMEM((N_GROUPS, CHUNK, HEAD_DIM), F32),
                                pltpu.VMEM((N_GROUPS, CHUNK, HEAD_DIM), F32)],
        name="band_attn",
    )[0]


def _mem_attn_kernel(q_ref, mk_ref, mv_ref, o_ref):
    for h in range(HEADS):
        hs = slice(h * HEAD_DIM, (h + 1) * HEAD_DIM)
        q = q_ref[:, hs]
        k = mk_ref[:, hs].astype(BF16)
        v = mv_ref[:, hs].astype(BF16)
        s = lax.dot_general(q, k, (((1,), (1,)), ((), ())), preferred_element_type=F32) * SCALE
        m = jnp.max(s, axis=-1, keepdims=True)
        p = jnp.exp(s - m)
        l = jnp.sum(p, axis=-1, keepdims=True)
        o = jnp.dot(p.astype(BF16), v, preferred_element_type=F32) / l
        o_ref[:, hs] = o.astype(o_ref.dtype)


def _mem_attention(p_all, mem_kv, *, tq):
    seq = p_all.shape[0]
    return _pallas_call(
        _mem_attn_kernel,
        (p_all, mem_kv, mem_kv),
        grid=(seq // tq,),
        in_specs=[pl.BlockSpec((tq, GROUP_W), lambda i: (i, COL_QM // GROUP_W)),
                  pl.BlockSpec((MEM_TOKENS, GROUP_W), lambda i: (0, 0)),
                  pl.BlockSpec((MEM_TOKENS, GROUP_W), lambda i: (0, 1))],
        out_specs=[pl.BlockSpec((tq, GROUP_W), lambda i: (i, 0))],
        out_shape=[jax.ShapeDtypeStruct((seq, GROUP_W), BF16)],
        name="mem_attn",
    )[0]


def _conv_rows(b, c, v, w_ref, carry_ref):
    u = c.astype(F32) * v.astype(F32)
    c2 = carry_ref[0:1, :]
    c1 = carry_ref[1:2, :]
    row = lax.broadcasted_iota(jnp.int32, u.shape, 0)
    u1 = jnp.where(row == 0, c1, pltpu.roll(u, 1, 0))
    u2 = jnp.where(row == 0, c2, jnp.where(row == 1, c1, pltpu.roll(u, 2, 0)))
    return b.astype(F32) * (w_ref[0:1, :] * u2 + w_ref[1:2, :] * u1 + w_ref[2:3, :] * u), u


def _conv_kernel(b_ref, c_ref, v_ref, st_ref, w_ref, o_ref, ns_ref, carry_ref, *, t_last):
    i = pl.program_id(1)

    @pl.when(i == 0)
    def _():
        carry_ref[0:2, :] = st_ref[0]

    o, u = _conv_rows(b_ref[0], c_ref[0], v_ref[0], w_ref, carry_ref)
    o_ref[0] = o.astype(o_ref.dtype)
    last2 = u[t_last - 2:t_last, :]
    carry_ref[0:2, :] = last2
    ns_ref[0] = last2


def _conv(p3, state, conv_w, *, tm, t_last):
    nb, t = p3.shape[0], p3.shape[1]
    blk = (1, tm, CONV_W)
    return _pallas_call(
        functools.partial(_conv_kernel, t_last=t_last),
        (p3, p3, p3, state, conv_w),
        grid=(nb, t // tm),
        in_specs=[
            pl.BlockSpec(blk, lambda b, i: (b, i, COL_B // CONV_W)),
            pl.BlockSpec(blk, lambda b, i: (b, i, COL_C // CONV_W)),
            pl.BlockSpec(blk, lambda b, i: (b, i, COL_V // CONV_W)),
            pl.BlockSpec((1, 2, CONV_W), lambda b, i: (b, 0, 0)),
            pl.BlockSpec((3, CONV_W), lambda b, i: (0, 0)),
        ],
        out_specs=[pl.BlockSpec(blk, lambda b, i: (b, i, 0)),
                   pl.BlockSpec((1, 2, CONV_W), lambda b, i: (b, 0, 0))],
        out_shape=[jax.ShapeDtypeStruct((nb, t, CONV_W), BF16),
                   jax.ShapeDtypeStruct((nb, 2, CONV_W), F32)],
        scratch_shapes=[pltpu.VMEM((8, CONV_W), F32)],
        name="conv",
    )


def _merge_kernel(*refs, sub_rows, fuse_conv):
    if fuse_conv:
        b_ref, c_ref, v_ref, cw_ref = refs[:4]
        refs = refs[4:]
    else:
        oc_ref = refs[0]
        refs = refs[1:]
    oa_ref, om_ref, gc_ref, ga_ref, gm_ref, h_ref, wc_ref, wa_ref, wm_ref, wo_ref, g_ref, beta_ref, y_ref = refs[:13]
    if fuse_conv:
        ns_ref, carry_ref = refs[13:]

        @pl.when(pl.program_id(0) == 0)
        def _():
            carry_ref[...] = jnp.zeros(carry_ref.shape, F32)

    for r0 in range(0, h_ref.shape[0], sub_rows):
        rows = slice(r0, r0 + sub_rows)
        if fuse_conv:
            oc, u = _conv_rows(b_ref[rows, :], c_ref[rows, :], v_ref[rows, :], cw_ref, carry_ref)
            oc = oc.astype(BF16)
            carry_ref[0:2, :] = u[sub_rows - 2:sub_rows, :]
        else:
            oc = oc_ref[rows, :]
        mc = jnp.dot(oc, wc_ref[...], preferred_element_type=F32)
        ma = jnp.dot(oa_ref[rows, :], wa_ref[...], preferred_element_type=F32)
        mm = jnp.dot(om_ref[rows, :], wm_ref[...], preferred_element_type=F32)
        mix = (jax.nn.sigmoid(gc_ref[rows, :].astype(F32)) * mc
               + jax.nn.sigmoid(ga_ref[rows, :].astype(F32)) * ma
               + jax.nn.sigmoid(gm_ref[rows, :].astype(F32)) * mm)
        z = ALPHA * h_ref[rows, :] + jnp.dot(mix.astype(BF16), wo_ref[...], preferred_element_type=F32)
        y_ref[rows, :] = _layer_norm_rows(z, g_ref[...], beta_ref[...])
    if fuse_conv:
        ns_ref[...] = carry_ref[0:2, :]


def _merge(o_conv, o_attn, o_mem, p_all, h, w_c, w_a, w_m, w_o, g, b, *, tm, conv_w=None):
    rows = h.shape[0]
    fuse_conv = o_conv is None
    single = pl.Buffered(1)
    row_blk = lambda w: pl.BlockSpec((tm, w), lambda i: (i, 0))
    col_blk = lambda w, col: pl.BlockSpec((tm, w), lambda i: (i, col // w))
    whole = lambda a: pl.BlockSpec(a.shape, lambda i: (0, 0), pipeline_mode=single)
    if fuse_conv:
        conv_specs = [col_blk(CONV_W, COL_B), col_blk(CONV_W, COL_C), col_blk(CONV_W, COL_V), whole(conv_w)]
        conv_args = [p_all, p_all, p_all, conv_w]
    else:
        conv_specs, conv_args = [row_blk(CONV_W)], [o_conv]
    out = _pallas_call(
        functools.partial(_merge_kernel, sub_rows=min(tm, MERGE_SUB_ROWS), fuse_conv=fuse_conv),
        (*conv_args, o_attn, o_mem, p_all, p_all, p_all, h, w_c, w_a, w_m, w_o, g, b),
        grid=(rows // tm,),
        in_specs=conv_specs + [row_blk(GROUP_W), row_blk(GROUP_W),
                               col_blk(D_MODEL, COL_GC), col_blk(D_MODEL, COL_GA), col_blk(D_MODEL, COL_GM),
                               row_blk(D_MODEL),
                               whole(w_c), whole(w_a), whole(w_m), whole(w_o), whole(g), whole(b)],
        out_specs=[row_blk(D_MODEL)] + ([pl.BlockSpec((2, CONV_W), lambda i: (0, 0))] if fuse_conv else []),
        out_shape=[jax.ShapeDtypeStruct((rows, D_MODEL), F32)]
                  + ([jax.ShapeDtypeStruct((2, CONV_W), F32)] if fuse_conv else []),
        scratch_shapes=[pltpu.VMEM((8, CONV_W), F32)] if fuse_conv else [],
        name="merge",
    )
    return out if fuse_conv else out[0]


def _over_quad(x, op):
    x = op(x, pltpu.roll(x, HEADS, 0))
    return op(x, pltpu.roll(x, 2 * HEADS, 0))


def _lane_dot(k3, q):
    return jnp.sum(k3 * q, axis=-1, keepdims=True)


def _quad_attention(k3, v3, qt, bias3, k_new, v_new, bias_new):
    s = _lane_dot(k3, qt[None]) + bias3
    m = jnp.max(s, axis=0)
    if k_new is not None:
        s_new = _lane_dot(k_new, qt) + bias_new
        m = jnp.maximum(m, s_new)
    m = _over_quad(m, jnp.maximum)
    e = jnp.exp2(s - m[None])
    l = jnp.sum(e, axis=0)
    acc = jnp.sum(e * v3, axis=0)
    if k_new is not None:
        e_new = jnp.exp2(s_new - m)
        l = l + e_new
        acc = acc + e_new * v_new
    l = _over_quad(l, jnp.add)
    acc = _over_quad(acc, jnp.add)
    return acc / l, m * LN2 + jnp.log(l)


def _sample_kernel(qd_ref, qt_ref, kn_ref, vn_ref, sl_ref,
                   k0_ref, v0_ref, k1_ref, v1_ref, k2_ref, v2_ref, mk_ref, mv_ref, *new_refs, t_new):
    k_refs = (k0_ref, k1_ref, k2_ref)
    v_refs = (v0_ref, v1_ref, v2_ref)
    oa_ref, om_ref = new_refs[N_GROUPS * 2], new_refs[N_GROUPS * 2 + 1]
    bias0_ref, bias0_new_ref, bias1_ref, bias2_ref = new_refs[-4:]
    bias_refs = (None, bias1_ref, bias2_ref)
    quad16 = lax.broadcasted_iota(jnp.int32, (ROWS16, HEAD_DIM), 0) >> 2
    q_scale = SCALE * LOG2E
    win0 = DIL_GROUPS[0][0]

    @pl.when(pl.program_id(0) == 0)
    def _():
        a_idx = lax.broadcasted_iota(jnp.int32, (win0 // 4, ROWS16, HEAD_DIM), 0)
        b_idx = lax.broadcasted_iota(jnp.int32, (win0 // 4, ROWS16, HEAD_DIM), 1) >> 2
        tok = a_idx * 4 + b_idx
        slope0 = sl_ref[0] * LOG2E
        for t in range(t_new):
            bias0_ref[t] = jnp.where(tok >= t, -(slope0[None] * (win0 + t - tok).astype(F32)), NEG_INF)
            bias0_new_ref[t] = jnp.where(quad16 <= t, -(slope0 * (t - quad16).astype(F32)), NEG_INF)
        i_idx = lax.broadcasted_iota(jnp.int32, (BAND, ROWS16, HEAD_DIM), 0)
        for g in (1, 2):
            win, dil = DIL_GROUPS[g]
            bias_refs[g][...] = -((sl_ref[g] * LOG2E)[None] * (win - i_idx * dil).astype(F32))

    outs, lses = [], []

    k3, v3 = k0_ref[0], v0_ref[0]
    o0 = jnp.zeros((ROWS16, HEAD_DIM), F32)
    l0 = jnp.zeros((ROWS16, HEAD_DIM), F32)
    for t in range(t_new):
        o_t, l_t = _quad_attention(k3, v3, qt_ref[0, 0, t] * q_scale, bias0_ref[t],
                                   kn_ref[0, 0], vn_ref[0, 0], bias0_new_ref[t])
        o0 = jnp.where(quad16 == t, o_t, o0)
        l0 = jnp.where(quad16 == t, l_t, l0)
    outs.append(o0)
    lses.append(l0)

    for g in (1, 2):
        q16 = qd_ref[0, g - 1] * q_scale
        k3 = k_refs[g][0, :, 0:ROWS16, :]
        v3 = v_refs[g][0, :, 0:ROWS16, :]
        s = _lane_dot(k3, q16[None]) + bias_refs[g][...]
        s_new = _lane_dot(kn_ref[0, g], q16)
        m = jnp.maximum(jnp.max(s, axis=0), s_new)
        e = jnp.exp2(s - m[None])
        e_new = jnp.exp2(s_new - m)
        l = jnp.sum(e, axis=0) + e_new
        acc = jnp.sum(e * v3, axis=0) + e_new * vn_ref[0, g]
        outs.append(acc / l)
        lses.append(m * LN2 + jnp.log(l))

    m = jnp.maximum(jnp.maximum(lses[0], lses[1]), lses[2])
    w = [jnp.exp(x - m) for x in lses]
    oa_ref[0] = (w[0] * outs[0] + w[1] * outs[1] + w[2] * outs[2]) / (w[0] + w[1] + w[2])

    k3, v3 = mk_ref[0], mv_ref[0]
    om = jnp.zeros((ROWS16, HEAD_DIM), F32)
    for t in range(t_new):
        o_t, _ = _quad_attention(k3, v3, qt_ref[0, 1, t] * q_scale, jnp.zeros((1, ROWS16, HEAD_DIM), F32),
                                 None, None, None)
        om = jnp.where(quad16 == t, o_t, om)
    om_ref[0] = om

    for i, out_ref in enumerate(new_refs[N_GROUPS * 2 + 2:N_GROUPS * 4 + 2]):
        out_ref[0, 0] = (kn_ref, vn_ref)[i % 2][0, i // 2]


def _sample_attention(qd, qt, kn, vn, slopes16, k_states, v_states, shifted, mem_k, mem_v, *, t_new):
    nb = qd.shape[0]
    states = []
    for g in range(N_GROUPS):
        states += [k_states[g], v_states[g]]
    whole = lambda a: pl.BlockSpec((1,) + a.shape[1:], lambda b: (b,) + (0,) * (a.ndim - 1))
    head_rows = lambda a: pl.BlockSpec((1, a.shape[1], ROWS16, HEAD_DIM), lambda b: (b, 0, 0, 0))
    last_rows = lambda a: pl.BlockSpec((1, 1, ROWS16, HEAD_DIM),
                                       lambda b, a=a: (b, a.shape[1] - 1, a.shape[2] // ROWS16 - 1, 0))
    o_spec = pl.BlockSpec((1, ROWS16, HEAD_DIM), lambda b: (b, 0, 0))
    o_shape = jax.ShapeDtypeStruct((nb, ROWS16, HEAD_DIM), F32)
    n_in = 5 + len(states) + 2
    return _pallas_call(
        functools.partial(_sample_kernel, t_new=t_new),
        (qd, qt, kn, vn, slopes16, *states, mem_k, mem_v, *shifted),
        grid=(nb,),
        in_specs=[whole(qd), whole(qt), whole(kn), whole(vn),
                  pl.BlockSpec(slopes16.shape, lambda b: (0, 0, 0))]
                 + [head_rows(s) for s in states] + [whole(mem_k), whole(mem_v)]
                 + [pl.BlockSpec(memory_space=pl.ANY)] * len(shifted),
        out_specs=[o_spec, o_spec] + [last_rows(s) for s in shifted],
        out_shape=[o_shape, o_shape] + [jax.ShapeDtypeStruct(s.shape, F32) for s in shifted],
        input_output_aliases={n_in + i: 2 + i for i in range(len(shifted))},
        scratch_shapes=[pltpu.VMEM((t_new, DIL_GROUPS[0][0] // 4, ROWS16, HEAD_DIM), F32),
                        pltpu.VMEM((t_new, ROWS16, HEAD_DIM), F32),
                        pltpu.VMEM((BAND, ROWS16, HEAD_DIM), F32),
                        pltpu.VMEM((BAND, ROWS16, HEAD_DIM), F32)],
        name="sample_attn",
    )


def kernel(x_prompt, x_sample, mem_prompt, state_conv, state_k_w128, state_v_w128, state_k_w512, state_v_w512, state_k_w2048, state_v_w2048, cache_mem_k, cache_mem_v, ln1_g, ln1_b, w_ffn1_gu, w_ffn1_down, w_in, conv_w, w_mem_kv, w_br_conv, w_br_attn, w_br_mem, w_o, ln2_g, ln2_b, w_ffn2_gu, w_ffn2_down, ln3_g, ln3_b):
    depth = w_in.shape[0]
    seq = x_prompt.shape[1]
    nb, t_new = x_sample.shape[0], x_sample.shape[1]
    assert depth == 1 and x_prompt.shape[0] == 1 and seq % CHUNK == 0
    assert t_new * HEADS == ROWS16 and t_new <= DIL_GROUPS[1][1]

    wb = lambda w: w[0].astype(BF16)
    wc_b, wa_b, wm_b, wo_b = wb(w_br_conv), wb(w_br_attn), wb(w_br_mem), wb(w_o)
    n_ff_blocks = D_FF // FFN_COLS
    cw = conv_w[0]
    slopes_lane = np.broadcast_to(SLOPES[:, :, None], (N_GROUPS, HEADS, HEAD_DIM))
    slopes_band = jnp.asarray(np.pad(slopes_lane.transpose(1, 0, 2), ((0, 0), (0, 8 - N_GROUPS), (0, 0))))
    slopes16 = jnp.asarray(np.tile(slopes_lane, (1, ROWS16 // HEADS, 1)))

    rows_s = nb * SAMPLE_PAD
    xs = jnp.pad(x_sample, ((0, 0), (0, SAMPLE_PAD - t_new), (0, 0))).reshape(rows_s, D_MODEL)
    s1, s1b, w1gu, w1d = _ffn(xs, w_ffn1_gu[0], w_ffn1_gu[0], w_ffn1_down[0], ln1_g, ln1_b,
                              tm=rows_s, tf=FFN_COLS, up_blk0=n_ff_blocks, emit_bf16=True)
    ps_all, win_b = _matmul(s1b, w_in[0], tm=rows_s, tn=PROJ_COLS, out_dtype=F32, name="in_proj_s", emit_bf16=True)

    def tiles(state):
        win = state.shape[2]
        per_tile = 4 if win <= DIL_GROUPS[1][0] else DIL_GROUPS[2][1]
        return state.reshape(nb, win // per_tile, per_tile * HEADS, HEAD_DIM)

    k_states = [tiles(s) for s in (state_k_w128, state_k_w512, state_k_w2048)]
    v_states = [tiles(s) for s in (state_v_w128, state_v_w512, state_v_w2048)]
    h1, h1b, *shifted = _ffn(x_prompt[0], w1gu, None, w1d, ln1_g, ln1_b, tm=FFN_ROWS, tf=FFN_COLS, manual_rows=True,
                             windows=k_states + v_states)
    shifted = [shifted[g + part * N_GROUPS] for g in range(N_GROUPS) for part in (0, 1)]
    p_all, qkv = _in_proj(h1b, win_b, tm=PROJ_ROWS, tn=PROJ_COLS)
    mem_kv = _matmul(mem_prompt[0].astype(BF16), w_mem_kv[0], tm=MEM_TOKENS, tn=GROUP_W, out_dtype=F32, name="mem_kv")
    o_attn = _band_attention(qkv, slopes_band)
    o_mem = _mem_attention(p_all, mem_kv, tq=MEM_Q_ROWS)
    h2, conv_p = _merge(None, o_attn, o_mem, p_all, h1, wc_b, wa_b, wm_b, wo_b, ln2_g, ln2_b, tm=MERGE_ROWS, conv_w=cw)

    prompt_states = []
    for g, (win, _) in enumerate(DIL_GROUPS):
        for tensor in (1, 2):
            c0 = tensor * N_GROUPS * GROUP_W + g * GROUP_W
            prompt_states.append(qkv[seq - win:, c0:c0 + GROUP_W].reshape(1, 1, win, HEADS, HEAD_DIM))
    mem_k_p = mem_kv[:, :GROUP_W].reshape(1, 1, MEM_TOKENS, HEADS, HEAD_DIM)
    mem_v_p = mem_kv[:, GROUP_W:].reshape(1, 1, MEM_TOKENS, HEADS, HEAD_DIM)

    ps3 = ps_all.reshape(nb, SAMPLE_PAD, IN_W)

    real = ps3[:, :t_new]
    heads = lambda a, n: a.reshape(nb, t_new, n, HEADS, HEAD_DIM)
    q, k, v = (heads(real[..., c:c + N_GROUPS * GROUP_W], N_GROUPS) for c in (COL_Q, COL_K, COL_VA))
    rows_th = lambda a: a.transpose(0, 2, 1, 3, 4).reshape(nb, a.shape[2], ROWS16, HEAD_DIM)
    over_quad = lambda a: jnp.broadcast_to(a[:, :, None], (nb, t_new, ROWS16 // HEADS, HEADS, HEAD_DIM)
                                           ).reshape(nb, t_new, ROWS16, HEAD_DIM)
    qd = rows_th(q[:, :, 1:])
    qm = real[..., COL_QM:COL_QM + GROUP_W].reshape(nb, t_new, HEADS, HEAD_DIM)
    qt = jnp.stack([over_quad(q[:, :, 0]), over_quad(qm)], axis=1)
    kn, vn = rows_th(k), rows_th(v)

    outs = _sample_attention(qd, qt, kn, vn, slopes16, k_states, v_states, shifted,
                             tiles(cache_mem_k), tiles(cache_mem_v), t_new=t_new)
    pad_rows = lambda a, w: jnp.pad(a.reshape(nb, t_new, w), ((0, 0), (0, SAMPLE_PAD - t_new), (0, 0))
                                    ).reshape(rows_s, w).astype(BF16)
    o_attn_s, o_mem_s = pad_rows(outs[0], GROUP_W), pad_rows(outs[1], GROUP_W)
    sample_states = [s.reshape(1, nb, -1, HEADS, HEAD_DIM) for s in outs[2:]]

    o_conv_s, conv_s = _conv(ps3, state_conv[0], cw, tm=SAMPLE_PAD, t_last=t_new)
    s2 = _merge(o_conv_s.reshape(rows_s, CONV_W), o_attn_s, o_mem_s, ps_all, s1,
                wc_b, wa_b, wm_b, wo_b, ln2_g, ln2_b, tm=rows_s)
    y_s, w2gu, w2d = _ffn(s2, w_ffn2_gu[0], w_ffn2_gu[0], w_ffn2_down[0], ln3_g, ln3_b,
                          tm=rows_s, tf=FFN_COLS, up_blk0=n_ff_blocks, emit_bf16=True, bf16_out=False)
    y_sample = y_s.reshape(nb, SAMPLE_PAD, D_MODEL)[:, :t_new]
    (y_prompt,) = _ffn(h2, w2gu, None, w2d, ln3_g, ln3_b, tm=FFN_ROWS, tf=FFN_COLS, bf16_out=False, row_buffers=2)

    return (y_prompt[None], y_sample, conv_p[None, None], *prompt_states, mem_k_p, mem_v_p,
            conv_s[None], *sample_states)
```

```python
import functools
import math

import jax
import jax.numpy as jnp
import numpy as np
from jax import lax
from jax.experimental import pallas as pl
from jax.experimental.pallas import tpu as pltpu

F32 = jnp.float32
BF16 = jnp.bfloat16

D_MODEL = 2048
D_FF = 5632
HEAD_DIM = 128
HEADS = 4
GROUP_W = HEADS * HEAD_DIM
DIL_GROUPS = ((128, 1), (512, 4), (2048, 16))
N_GROUPS = len(DIL_GROUPS)
BAND = 128
CHUNK = BAND * DIL_GROUPS[-1][1]
CONV_W = 1024
MEM_TOKENS = 256
IN_W = 14336
COL_B, COL_C, COL_V = 0, 1024, 2048
COL_Q, COL_K, COL_VA = 3072, 4608, 6144
COL_QM = 7680
COL_GC, COL_GA, COL_GM = 8192, 10240, 12288
ALPHA = 2.0 ** 0.25
LN_EPS = 1e-5
NEG_INF = -1e30
SCALE = HEAD_DIM ** -0.5
LOG2E = 1.4426950408889634
LN2 = 0.6931471805599453
SLOPES = np.exp2(np.float32(-8.0) * np.arange(1, 13, dtype=np.float32) / np.float32(12)).reshape(3, 4)

SAMPLE_PAD = 8
FFN_ROWS, FFN_COLS = 1024, 512
PROJ_ROWS, PROJ_COLS = 1024, 1024
MERGE_ROWS, MERGE_SUB_ROWS = 512, 256
MEM_Q_ROWS = 512
LN_ROWS = 128
ROWS16 = 16
V7X_VMEM_BYTES = 64 * 1024 * 1024
VALUE_SPILL_BYTES = 8 * 1024 * 1024


def _nbytes(shape, dtype):
    return math.prod(1 if d is None else d for d in shape) * jnp.dtype(dtype).itemsize


def _pallas_call(kernel, operands, *, grid, in_specs, out_specs, out_shape, name, scratch_shapes=(),
                 input_output_aliases=None):
    out_specs, out_shape = list(out_specs), list(out_shape)
    vmem = VALUE_SPILL_BYTES
    for spec, array in list(zip(in_specs, operands)) + list(zip(out_specs, out_shape)):
        if spec.block_shape is not None:
            buffers = 2 if spec.pipeline_mode is None else spec.pipeline_mode.buffer_count
            vmem += buffers * _nbytes(spec.block_shape, array.dtype)
    vmem += sum(_nbytes(s.shape, s.dtype) for s in scratch_shapes if s.memory_space == pltpu.VMEM)
    assert vmem <= V7X_VMEM_BYTES, (name, vmem)
    call = pl.pallas_call(
        kernel, grid=grid, in_specs=in_specs, out_specs=out_specs, out_shape=out_shape,
        scratch_shapes=list(scratch_shapes), input_output_aliases=input_output_aliases or {},
        compiler_params=pltpu.CompilerParams(dimension_semantics=("arbitrary",) * len(grid),
                                             vmem_limit_bytes=vmem),
        name=name)
    return call(*operands)


def _layer_norm_rows(z, g, b):
    mu = jnp.mean(z, axis=-1, keepdims=True)
    zc = z - mu
    var = jnp.mean(zc * zc, axis=-1, keepdims=True)
    return zc * lax.rsqrt(var + LN_EPS) * g + b


def _window_loads(b, src_refs, stage_refs, slot, sems):
    return [pltpu.make_async_copy(src.at[b], stage.at[slot], sems.at[slot, n])
            for n, (src, stage) in enumerate(zip(src_refs, stage_refs))]


def _window_stores(b, stage_refs, dst_refs, slot, sems):
    copies = []
    for stage, dst in zip(stage_refs, dst_refs):
        tiles, rows = dst.shape[1], dst.shape[2]
        keep = rows - ROWS16
        if keep:
            copies.append((stage.at[slot, :, pl.ds(ROWS16, keep), :], dst.at[b, :, pl.ds(0, keep), :]))
        copies.append((stage.at[slot, pl.ds(1, tiles - 1), pl.ds(0, ROWS16), :],
                       dst.at[b, pl.ds(0, tiles - 1), pl.ds(keep, ROWS16), :]))
        copies.append((stage.at[slot, pl.ds(tiles - 1, 1), pl.ds(keep, ROWS16), :],
                       dst.at[b, pl.ds(tiles - 1, 1), pl.ds(keep, ROWS16), :]))
    return [pltpu.make_async_copy(s, d, sems.at[slot, n]) for n, (s, d) in enumerate(copies)]


def _ffn_kernel(*refs, n_ff, row_chunk, emit_bf16, bf16_out, manual_rows, n_windows, n_requests):
    x_ref, wg_ref, wu_ref, wd_ref, g_ref, b_ref = refs[:6]
    win_src = refs[6:6 + n_windows]
    y_ref = refs[6 + n_windows]
    yb_ref = refs[7 + n_windows] if bf16_out else None
    rest = refs[7 + bf16_out + n_windows:]
    if emit_bf16:
        wb_refs, rest = rest[:3], rest[3:]
    win_dst, rest = rest[:n_windows], rest[n_windows:]
    xb_ref, rest = rest[0], rest[1:]
    i, j = pl.program_id(0), pl.program_id(1)
    n_tiles = pl.num_programs(0)
    tm = xb_ref.shape[0]

    if manual_rows:
        y_hbm, yb_hbm = y_ref, yb_ref
        n_ring = 1 + bf16_out
        xbuf_ref, xsem, y_ref = rest[:3]
        ring_refs, out_sems = rest[3:3 + n_ring], rest[3 + n_ring]
        rest = rest[4 + n_ring:]

        def x_copy(tile):
            return pltpu.make_async_copy(x_ref.at[pl.ds(pl.multiple_of(tile * tm, tm), tm), :], xbuf_ref, xsem.at[0])

        @pl.when((i == 0) & (j == 0))
        def _():
            x_copy(0).start()

    if n_windows:
        half = n_windows // 2
        stage_refs, load_sems, store_sems = rest[:half], rest[half], rest[half + 1]
        step = i * n_ff + j
        n_items = 2 * n_requests
        for part in (0, 1):
            srcs, dsts = win_src[part * half:(part + 1) * half], win_dst[part * half:(part + 1) * half]
            mine = step % 2 == part

            @pl.when(mine & (step >= 2) & (step < n_items + 2))
            def _():
                for copy in _window_stores((step - 2) // 2, stage_refs, dsts, part, store_sems):
                    copy.wait()

            @pl.when(mine & (step < n_items))
            def _():
                for copy in _window_loads(step // 2, srcs, stage_refs, part, load_sems):
                    copy.start()

            @pl.when(jnp.logical_not(mine) & (step >= 1) & (step < n_items + 1))
            def _():
                for copy in _window_loads((step - 1) // 2, srcs, stage_refs, part, load_sems):
                    copy.wait()
                for copy in _window_stores((step - 1) // 2, stage_refs, dsts, part, store_sems):
                    copy.start()

    @pl.when(j == 0)
    def _():
        if manual_rows:
            x_copy(i).wait()
        x = xbuf_ref[...] if manual_rows else x_ref[...]
        xb_ref[...] = x.astype(BF16)
        y_ref[...] = (2.0 * ALPHA) * x

    if manual_rows:
        @pl.when((j == 1) & (i + 1 < n_tiles))
        def _():
            x_copy(i + 1).start()

    wg, wu, wd = wg_ref[...], wu_ref[...], wd_ref[...]
    if emit_bf16:
        wg, wu, wd = wg.astype(BF16), wu.astype(BF16), wd.astype(BF16)
        wb_refs[0][...], wb_refs[1][...], wb_refs[2][...] = wg, wu, wd
    xb = xb_ref[...]
    gate = jnp.dot(xb, wg, preferred_element_type=F32)
    up = jnp.dot(xb, wu, preferred_element_type=F32)
    act = (gate * jax.nn.sigmoid(gate) * up).astype(BF16)
    y_ref[...] += jnp.dot(act, wd, preferred_element_type=F32)

    if not manual_rows:
        @pl.when(j == n_ff - 1)
        def _():
            def chunk(c, carry):
                rows = pl.ds(pl.multiple_of(c * row_chunk, row_chunk), row_chunk)
                y = _layer_norm_rows(0.5 * y_ref[rows, :], g_ref[...], b_ref[...])
                y_ref[rows, :] = y
                if bf16_out:
                    yb_ref[rows, :] = y.astype(BF16)
                return carry
            lax.fori_loop(0, tm // row_chunk, chunk, 0)
    else:
        def out_copies(slot, row0):
            dsts = (y_hbm, yb_hbm)[:n_ring]
            return [pltpu.make_async_copy(ring.at[slot], dst.at[pl.ds(row0, row_chunk), :], out_sems.at[slot, n])
                    for n, (ring, dst) in enumerate(zip(ring_refs, dsts))]

        def wait_slot(slot):
            for copy in out_copies(slot, 0):
                copy.wait()

        @pl.when(j == n_ff - 1)
        def _():
            n_chunks = tm // row_chunk
            for c in range(n_chunks):
                slot = c % 2
                if c >= 2:
                    wait_slot(slot)
                else:
                    pl.when(i > 0)(functools.partial(wait_slot, slot))
                y = _layer_norm_rows(0.5 * y_ref[c * row_chunk:(c + 1) * row_chunk, :], g_ref[...], b_ref[...])
                ring_refs[0][slot] = y
                if bf16_out:
                    ring_refs[1][slot] = y.astype(BF16)
                for copy in out_copies(slot, pl.multiple_of(i * tm, tm) + c * row_chunk):
                    copy.start()

            @pl.when(i == n_tiles - 1)
            def _():
                wait_slot(0)
                wait_slot(1)


def _ffn(x, w_gate, w_up, w_down, g, b, *, tm, tf, up_blk0=0, emit_bf16=False, bf16_out=True,
         row_buffers=1, manual_rows=False, windows=()):
    rows = x.shape[0]
    n_ff = D_FF // tf
    n_steps = (rows // tm) * n_ff
    n_requests = windows[0].shape[0] if windows else 0
    assert not emit_bf16 or rows == tm
    assert n_steps >= 2 * n_requests + 2
    row_mode = pl.Buffered(1) if (rows == tm or row_buffers == 1) else None
    hbm = pl.BlockSpec(memory_space=pl.ANY)
    w_col = pl.BlockSpec((D_MODEL, tf), lambda i, j: (0, j))
    w_col_tile = pl.BlockSpec((None, D_MODEL, tf), lambda i, j: (j, 0, 0))
    tiled = w_gate.ndim == 3
    w_row = pl.BlockSpec((tf, D_MODEL), lambda i, j: (j, 0))
    row_tile = pl.BlockSpec((tm, D_MODEL), lambda i, j: (i, 0), pipeline_mode=row_mode)
    out_specs = [hbm if manual_rows else row_tile] * (1 + bf16_out)
    out_shape = [jax.ShapeDtypeStruct((rows, D_MODEL), F32), jax.ShapeDtypeStruct((rows, D_MODEL), BF16)][:1 + bf16_out]
    scratch = [pltpu.VMEM((tm, D_MODEL), BF16)]
    if manual_rows:
        row_chunk = min(tm, LN_ROWS)
        assert n_ff >= 2 and (tm // row_chunk) % 2 == 0
        scratch += [pltpu.VMEM((tm, D_MODEL), F32), pltpu.SemaphoreType.DMA((1,)), pltpu.VMEM((tm, D_MODEL), F32)]
        scratch += [pltpu.VMEM((2, row_chunk, D_MODEL), dt) for dt in (F32, BF16)[:1 + bf16_out]]
        scratch += [pltpu.SemaphoreType.DMA((2, 1 + bf16_out))]
    if emit_bf16:
        out_specs += [w_col_tile, w_col_tile, w_row]
        out_shape += [jax.ShapeDtypeStruct((n_ff, D_MODEL, tf), BF16)] * 2 + [jax.ShapeDtypeStruct((D_FF, D_MODEL), BF16)]
    if windows:
        out_specs += [hbm] * len(windows)
        out_shape += [jax.ShapeDtypeStruct(w.shape, w.dtype) for w in windows]
        half = windows[:len(windows) // 2]
        scratch += [pltpu.VMEM((2,) + w.shape[1:], w.dtype) for w in half]
        scratch += [pltpu.SemaphoreType.DMA((2, len(half))),
                    pltpu.SemaphoreType.DMA((2, sum(2 + (w.shape[2] > ROWS16) for w in half)))]
    return _pallas_call(
        functools.partial(_ffn_kernel, n_ff=n_ff, row_chunk=min(tm, LN_ROWS), emit_bf16=emit_bf16, bf16_out=bf16_out,
                          manual_rows=manual_rows, n_windows=len(windows), n_requests=n_requests),
        (x, w_gate, w_up, w_down, g, b, *windows),
        grid=(rows // tm, n_ff),
        in_specs=[
            hbm if manual_rows else row_tile,
            w_col_tile if tiled else w_col,
            w_col_tile if tiled else pl.BlockSpec((D_MODEL, tf), lambda i, j: (0, j + up_blk0)),
            w_row,
            pl.BlockSpec((1, D_MODEL), lambda i, j: (0, 0)),
            pl.BlockSpec((1, D_MODEL), lambda i, j: (0, 0)),
        ] + [hbm] * len(windows),
        out_specs=out_specs,
        out_shape=out_shape,
        scratch_shapes=scratch,
        name="ffn_cast" if emit_bf16 else "ffn",
    )


def _mm_kernel(x_ref, w_ref, o_ref, *wb_ref):
    w = w_ref[...].astype(BF16)
    if wb_ref:
        wb_ref[0][...] = w
    o_ref[...] = jnp.dot(x_ref[...], w, preferred_element_type=F32).astype(o_ref.dtype)


def _matmul(x, w, *, tm, tn, out_dtype, name, emit_bf16=False):
    rows, k = x.shape
    n_cols = w.shape[1]
    assert not emit_bf16 or rows == tm
    w_spec = pl.BlockSpec((k, tn), lambda i, j: (0, j))
    out_specs = [pl.BlockSpec((tm, tn), lambda i, j: (i, j))]
    out_shape = [jax.ShapeDtypeStruct((rows, n_cols), out_dtype)]
    if emit_bf16:
        out_specs.append(pl.BlockSpec((None, k, tn), lambda i, j: (j, 0, 0)))
        out_shape.append(jax.ShapeDtypeStruct((n_cols // tn, k, tn), BF16))
    out = _pallas_call(
        _mm_kernel,
        (x, w),
        grid=(rows // tm, n_cols // tn),
        in_specs=[pl.BlockSpec((tm, k), lambda i, j: (i, 0)), w_spec],
        out_specs=out_specs,
        out_shape=out_shape,
        name=name,
    )
    return out if emit_bf16 else out[0]


def _in_proj_kernel(x_ref, w_ref, p_ref, qkv_ref, *, n_own):
    j = pl.program_id(1)
    acc = jnp.dot(x_ref[...], w_ref[...], preferred_element_type=F32)
    p_ref[...] = acc.astype(BF16)

    @pl.when(j < n_own)
    def _():
        qkv_ref[...] = acc


def _in_proj(x, w, *, tm, tn):
    rows, k = x.shape
    assert w.shape == (IN_W // tn, k, tn) and COL_Q % tn == 0 and COL_GC % tn == 0
    c_lo, n_own = COL_Q // tn, (COL_GC - COL_Q) // tn

    def col_block(j):
        return jnp.where(j < n_own, j + c_lo, jnp.where(j < n_own + c_lo, j - n_own, j))

    return _pallas_call(
        functools.partial(_in_proj_kernel, n_own=n_own),
        (x, w),
        grid=(rows // tm, IN_W // tn),
        in_specs=[pl.BlockSpec((tm, k), lambda i, j: (i, 0)),
                  pl.BlockSpec((None, k, tn), lambda i, j: (col_block(j), 0, 0))],
        out_specs=[pl.BlockSpec((None, tm, tn), lambda i, j: (col_block(j), i, 0)),
                   pl.BlockSpec((tm, tn), lambda i, j: (i, jnp.minimum(j, n_own - 1)))],
        out_shape=[jax.ShapeDtypeStruct((IN_W // tn, rows, tn), BF16),
                   jax.ShapeDtypeStruct((rows, COL_GC - COL_Q), F32)],
        name="in_proj",
    )


def _rows(ref, start, dil):
    if dil == 1:
        return ref[start:start + BAND, :]
    return ref[pl.ds(start, BAND, stride=dil), :]


def _band_kernel(sl_ref, q0_ref, k0_ref, v0_ref, q1_ref, k1_ref, v1_ref, q2_ref, k2_ref, v2_ref,
                 o_ref, kt0_ref, vt0_ref, kt1_ref, vt1_ref, kt2_ref, vt2_ref, og_ref, lg_ref):
    c = pl.program_id(1)
    q_refs = (q0_ref, q1_ref, q2_ref)
    k_refs = (k0_ref, k1_ref, k2_ref)
    v_refs = (v0_ref, v1_ref, v2_ref)
    kt_refs = (kt0_ref, kt1_ref, kt2_ref)
    vt_refs = (vt0_ref, vt1_ref, vt2_ref)

    @pl.when(c == 0)
    def _():
        for ref in kt_refs + vt_refs:
            ref[...] = jnp.zeros(ref.shape, F32)

    qi = lax.broadcasted_iota(jnp.int32, (BAND, 2 * BAND), 0)
    ki = lax.broadcasted_iota(jnp.int32, (BAND, 2 * BAND), 1)
    rel = BAND + qi - ki
    valid = (rel >= 0) & (rel <= BAND)
    no_prev = jnp.where(ki < BAND, jnp.where(c == 0, NEG_INF, 0.0), 0.0)

    for g, (_, dil) in enumerate(DIL_GROUPS):
        slope = sl_ref[0, g:g + 1, 0:1] * LOG2E
        bias = jnp.where(valid, -(slope * (rel * dil).astype(F32)), NEG_INF)
        bias_first = bias + no_prev
        span = BAND * dil
        for r in range(dil):
            for n in range(CHUNK // span):
                start = r + n * span
                q = _rows(q_refs[g], start, dil).astype(BF16)
                if n == 0:
                    k_prev, v_prev = _rows(kt_refs[g], r, dil), _rows(vt_refs[g], r, dil)
                else:
                    k_prev, v_prev = _rows(k_refs[g], start - span, dil), _rows(v_refs[g], start - span, dil)
                k = jnp.concatenate([k_prev, _rows(k_refs[g], start, dil)], axis=0).astype(BF16)
                v = jnp.concatenate([v_prev, _rows(v_refs[g], start, dil)], axis=0).astype(BF16)
                s = lax.dot_general(q, k, (((1,), (1,)), ((), ())), preferred_element_type=F32) * (SCALE * LOG2E)
                s = s + (bias_first if n == 0 else bias)
                m = jnp.max(s, axis=-1, keepdims=True)
                p = jnp.exp2(s - m)
                l = jnp.sum(p, axis=-1, keepdims=True)
                o = jnp.dot(p.astype(BF16), v, preferred_element_type=F32) / l
                lse = jnp.broadcast_to(m * LN2 + jnp.log(l), (BAND, HEAD_DIM))
                if dil == 1:
                    og_ref[g, start:start + BAND, :] = o
                    lg_ref[g, start:start + BAND, :] = lse
                else:
                    og_ref[g, pl.ds(start, BAND, stride=dil), :] = o
                    lg_ref[g, pl.ds(start, BAND, stride=dil), :] = lse
        kt_refs[g][...] = k_refs[g][CHUNK - span:CHUNK, :]
        vt_refs[g][...] = v_refs[g][CHUNK - span:CHUNK, :]

    rows_per_step = 256

    def combine(i, carry):
        rows = pl.ds(pl.multiple_of(i * rows_per_step, rows_per_step), rows_per_step)
        l0, l1, l2 = lg_ref[0, rows, :], lg_ref[1, rows, :], lg_ref[2, rows, :]
        m = jnp.maximum(jnp.maximum(l0, l1), l2)
        e0, e1, e2 = jnp.exp(l0 - m), jnp.exp(l1 - m), jnp.exp(l2 - m)
        o = (e0 * og_ref[0, rows, :] + e1 * og_ref[1, rows, :] + e2 * og_ref[2, rows, :]) / (e0 + e1 + e2)
        o_ref[rows, :] = o.astype(o_ref.dtype)
        return carry

    lax.fori_loop(0, CHUNK // rows_per_step, combine, 0)


def _band_attention(qkv, slopes):
    seq = qkv.shape[0]
    blk = (CHUNK, HEAD_DIM)
    per_tensor = N_GROUPS * HEADS
    in_specs = [pl.BlockSpec((1, 8, HEAD_DIM), lambda h, c: (h, 0, 0))]
    for g in range(N_GROUPS):
        for tensor in range(3):
            in_specs.append(pl.BlockSpec(
                blk, lambda h, c, g=g, tensor=tensor: (c, tensor * per_tensor + g * HEADS + h)))
    tails = []
    for _, dil in DIL_GROUPS:
        tails += [pltpu.VMEM((BAND * dil, HEAD_DIM), F32)] * 2
    return _pallas_call(
        _band_kernel,
        (slopes, *([qkv] * 9)),
        grid=(HEADS, seq // CHUNK),
        in_specs=in_specs,
        out_specs=[pl.BlockSpec(blk, lambda h, c: (c, h))],
        out_shape=[jax.ShapeDtypeStruct((seq, GROUP_W), BF16)],
        scratch_shapes=tails + [pltpu.VMEM((N_GROUPS, CHUNK, HEAD_DIM), F32),
                                pltpu.VMEM((N_GROUPS, CHUNK, HEAD_DIM), F32)],
        name="band_attn",
    )[0]


def _mem_attn_kernel(q_ref, mk_ref, mv_ref, o_ref):
    for h in range(HEADS):
        hs = slice(h * HEAD_DIM, (h + 1) * HEAD_DIM)
        q = q_ref[:, hs]
        k = mk_ref[:, hs].astype(BF16)
        v = mv_ref[:, hs].astype(BF16)
        s = lax.dot_general(q, k, (((1,), (1,)), ((), ())), preferred_element_type=F32) * SCALE
        m = jnp.max(s, axis=-1, keepdims=True)
        p = jnp.exp(s - m)
        l = jnp.sum(p, axis=-1, keepdims=True)
        o = jnp.dot(p.astype(BF16), v, preferred_element_type=F32) / l
        o_ref[:, hs] = o.astype(o_ref.dtype)


def _mem_attention(p_all, mem_kv, *, tq):
    seq, tn = p_all.shape[1], p_all.shape[2]
    return _pallas_call(
        _mem_attn_kernel,
        (p_all, mem_kv, mem_kv),
        grid=(seq // tq,),
        in_specs=[pl.BlockSpec((None, tq, GROUP_W), lambda i: (COL_QM // tn, i, (COL_QM % tn) // GROUP_W)),
                  pl.BlockSpec((MEM_TOKENS, GROUP_W), lambda i: (0, 0)),
                  pl.BlockSpec((MEM_TOKENS, GROUP_W), lambda i: (0, 1))],
        out_specs=[pl.BlockSpec((tq, GROUP_W), lambda i: (i, 0))],
        out_shape=[jax.ShapeDtypeStruct((seq, GROUP_W), BF16)],
        name="mem_attn",
    )[0]


def _conv_rows(b, c, v, w_ref, carry_ref):
    u = c.astype(F32) * v.astype(F32)
    c2 = carry_ref[0:1, :]
    c1 = carry_ref[1:2, :]
    row = lax.broadcasted_iota(jnp.int32, u.shape, 0)
    u1 = jnp.where(row == 0, c1, pltpu.roll(u, 1, 0))
    u2 = jnp.where(row == 0, c2, jnp.where(row == 1, c1, pltpu.roll(u, 2, 0)))
    return b.astype(F32) * (w_ref[0:1, :] * u2 + w_ref[1:2, :] * u1 + w_ref[2:3, :] * u), u


def _conv_kernel(b_ref, c_ref, v_ref, st_ref, w_ref, o_ref, ns_ref, carry_ref, *, t_last):
    i = pl.program_id(1)

    @pl.when(i == 0)
    def _():
        carry_ref[0:2, :] = st_ref[0]

    o, u = _conv_rows(b_ref[0], c_ref[0], v_ref[0], w_ref, carry_ref)
    o_ref[0] = o.astype(o_ref.dtype)
    last2 = u[t_last - 2:t_last, :]
    carry_ref[0:2, :] = last2
    ns_ref[0] = last2


def _conv(p3, state, conv_w, *, tm, t_last):
    nb, t = p3.shape[0], p3.shape[1]
    blk = (1, tm, CONV_W)
    return _pallas_call(
        functools.partial(_conv_kernel, t_last=t_last),
        (p3, p3, p3, state, conv_w),
        grid=(nb, t // tm),
        in_specs=[
            pl.BlockSpec(blk, lambda b, i: (b, i, COL_B // CONV_W)),
            pl.BlockSpec(blk, lambda b, i: (b, i, COL_C // CONV_W)),
            pl.BlockSpec(blk, lambda b, i: (b, i, COL_V // CONV_W)),
            pl.BlockSpec((1, 2, CONV_W), lambda b, i: (b, 0, 0)),
            pl.BlockSpec((3, CONV_W), lambda b, i: (0, 0)),
        ],
        out_specs=[pl.BlockSpec(blk, lambda b, i: (b, i, 0)),
                   pl.BlockSpec((1, 2, CONV_W), lambda b, i: (b, 0, 0))],
        out_shape=[jax.ShapeDtypeStruct((nb, t, CONV_W), BF16),
                   jax.ShapeDtypeStruct((nb, 2, CONV_W), F32)],
        scratch_shapes=[pltpu.VMEM((8, CONV_W), F32)],
        name="conv",
    )


def _merge_kernel(*refs, sub_rows, fuse_conv):
    if fuse_conv:
        b_ref, c_ref, v_ref, cw_ref = refs[:4]
        refs = refs[4:]
    else:
        oc_ref = refs[0]
        refs = refs[1:]
    oa_ref, om_ref = refs[:2]
    gate_refs = refs[2:8]
    h_ref, wc_ref, wa_ref, wm_ref, wo_ref, g_ref, beta_ref, y_ref = refs[8:16]
    if fuse_conv:
        ns_ref, carry_ref = refs[16:]

        @pl.when(pl.program_id(0) == 0)
        def _():
            carry_ref[...] = jnp.zeros(carry_ref.shape, F32)

    for r0 in range(0, h_ref.shape[0], sub_rows):
        rows = slice(r0, r0 + sub_rows)
        if fuse_conv:
            oc, u = _conv_rows(b_ref[rows, :], c_ref[rows, :], v_ref[rows, :], cw_ref, carry_ref)
            oc = oc.astype(BF16)
            carry_ref[0:2, :] = u[sub_rows - 2:sub_rows, :]
        else:
            oc = oc_ref[rows, :]
        mc = jnp.dot(oc, wc_ref[...], preferred_element_type=F32)
        ma = jnp.dot(oa_ref[rows, :], wa_ref[...], preferred_element_type=F32)
        mm = jnp.dot(om_ref[rows, :], wm_ref[...], preferred_element_type=F32)
        gate = lambda k: jax.nn.sigmoid(jnp.concatenate(
            [gate_refs[2 * k][rows, :], gate_refs[2 * k + 1][rows, :]], axis=1).astype(F32))
        mix = gate(0) * mc + gate(1) * ma + gate(2) * mm
        z = ALPHA * h_ref[rows, :] + jnp.dot(mix.astype(BF16), wo_ref[...], preferred_element_type=F32)
        y_ref[rows, :] = _layer_norm_rows(z, g_ref[...], beta_ref[...])
    if fuse_conv:
        ns_ref[...] = carry_ref[0:2, :]


def _merge(o_conv, o_attn, o_mem, p_all, h, w_c, w_a, w_m, w_o, g, b, *, tm, conv_w=None):
    rows = h.shape[0]
    fuse_conv = o_conv is None
    single = pl.Buffered(1)
    row_blk = lambda w: pl.BlockSpec((tm, w), lambda i: (i, 0))
    half = D_MODEL // 2

    def col_blk(w, col):
        if p_all.ndim == 3:
            tn = p_all.shape[2]
            return pl.BlockSpec((None, tm, w), lambda i: (col // tn, i, (col % tn) // w))
        return pl.BlockSpec((tm, w), lambda i: (i, col // w))

    whole = lambda a: pl.BlockSpec(a.shape, lambda i: (0, 0), pipeline_mode=single)
    if fuse_conv:
        conv_specs = [col_blk(CONV_W, COL_B), col_blk(CONV_W, COL_C), col_blk(CONV_W, COL_V), whole(conv_w)]
        conv_args = [p_all, p_all, p_all, conv_w]
    else:
        conv_specs, conv_args = [row_blk(CONV_W)], [o_conv]
    out = _pallas_call(
        functools.partial(_merge_kernel, sub_rows=min(tm, MERGE_SUB_ROWS), fuse_conv=fuse_conv),
        (*conv_args, o_attn, o_mem, *([p_all] * 6), h, w_c, w_a, w_m, w_o, g, b),
        grid=(rows // tm,),
        in_specs=conv_specs + [row_blk(GROUP_W), row_blk(GROUP_W)]
                 + [col_blk(half, col + k * half) for col in (COL_GC, COL_GA, COL_GM) for k in (0, 1)]
                 + [row_blk(D_MODEL),
                               whole(w_c), whole(w_a), whole(w_m), whole(w_o), whole(g), whole(b)],
        out_specs=[row_blk(D_MODEL)] + ([pl.BlockSpec((2, CONV_W), lambda i: (0, 0))] if fuse_conv else []),
        out_shape=[jax.ShapeDtypeStruct((rows, D_MODEL), F32)]
                  + ([jax.ShapeDtypeStruct((2, CONV_W), F32)] if fuse_conv else []),
        scratch_shapes=[pltpu.VMEM((8, CONV_W), F32)] if fuse_conv else [],
        name="merge",
    )
    return out if fuse_conv else out[0]


def _over_quad(x, op):
    x = op(x, pltpu.roll(x, HEADS, 0))
    return op(x, pltpu.roll(x, 2 * HEADS, 0))


def _lane_dot(k3, q):
    return jnp.sum(k3 * q, axis=-1, keepdims=True)


def _quad_attention(k3, v3, qt, bias3, k_new, v_new, bias_new):
    s = _lane_dot(k3, qt[None]) + bias3
    m = jnp.max(s, axis=0)
    if k_new is not None:
        s_new = _lane_dot(k_new, qt) + bias_new
        m = jnp.maximum(m, s_new)
    m = _over_quad(m, jnp.maximum)
    e = jnp.exp2(s - m[None])
    l = jnp.sum(e, axis=0)
    acc = jnp.sum(e * v3, axis=0)
    if k_new is not None:
        e_new = jnp.exp2(s_new - m)
        l = l + e_new
        acc = acc + e_new * v_new
    l = _over_quad(l, jnp.add)
    acc = _over_quad(acc, jnp.add)
    return acc / l, m * LN2 + jnp.log(l)


def _sample_kernel(qd_ref, qt_ref, kn_ref, vn_ref, sl_ref,
                   k0_ref, v0_ref, k1_ref, v1_ref, k2_ref, v2_ref, mk_ref, mv_ref, *new_refs, t_new):
    k_refs = (k0_ref, k1_ref, k2_ref)
    v_refs = (v0_ref, v1_ref, v2_ref)
    oa_ref, om_ref = new_refs[N_GROUPS * 2], new_refs[N_GROUPS * 2 + 1]
    bias0_ref, bias0_new_ref, bias1_ref, bias2_ref = new_refs[-4:]
    bias_refs = (None, bias1_ref, bias2_ref)
    quad16 = lax.broadcasted_iota(jnp.int32, (ROWS16, HEAD_DIM), 0) >> 2
    q_scale = SCALE * LOG2E
    win0 = DIL_GROUPS[0][0]

    @pl.when(pl.program_id(0) == 0)
    def _():
        a_idx = lax.broadcasted_iota(jnp.int32, (win0 // 4, ROWS16, HEAD_DIM), 0)
        b_idx = lax.broadcasted_iota(jnp.int32, (win0 // 4, ROWS16, HEAD_DIM), 1) >> 2
        tok = a_idx * 4 + b_idx
        slope0 = sl_ref[0] * LOG2E
        for t in range(t_new):
            bias0_ref[t] = jnp.where(tok >= t, -(slope0[None] * (win0 + t - tok).astype(F32)), NEG_INF)
            bias0_new_ref[t] = jnp.where(quad16 <= t, -(slope0 * (t - quad16).astype(F32)), NEG_INF)
        i_idx = lax.broadcasted_iota(jnp.int32, (BAND, ROWS16, HEAD_DIM), 0)
        for g in (1, 2):
            win, dil = DIL_GROUPS[g]
            bias_refs[g][...] = -((sl_ref[g] * LOG2E)[None] * (win - i_idx * dil).astype(F32))

    outs, lses = [], []

    k3, v3 = k0_ref[0], v0_ref[0]
    o0 = jnp.zeros((ROWS16, HEAD_DIM), F32)
    l0 = jnp.zeros((ROWS16, HEAD_DIM), F32)
    for t in range(t_new):
        o_t, l_t = _quad_attention(k3, v3, qt_ref[0, 0, t] * q_scale, bias0_ref[t],
                                   kn_ref[0, 0], vn_ref[0, 0], bias0_new_ref[t])
        o0 = jnp.where(quad16 == t, o_t, o0)
        l0 = jnp.where(quad16 == t, l_t, l0)
    outs.append(o0)
    lses.append(l0)

    for g in (1, 2):
        q16 = qd_ref[0, g - 1] * q_scale
        k3 = k_refs[g][0, :, 0:ROWS16, :]
        v3 = v_refs[g][0, :, 0:ROWS16, :]
        s = _lane_dot(k3, q16[None]) + bias_refs[g][...]
        s_new = _lane_dot(kn_ref[0, g], q16)
        m = jnp.maximum(jnp.max(s, axis=0), s_new)
        e = jnp.exp2(s - m[None])
        e_new = jnp.exp2(s_new - m)
        l = jnp.sum(e, axis=0) + e_new
        acc = jnp.sum(e * v3, axis=0) + e_new * vn_ref[0, g]
        outs.append(acc / l)
        lses.append(m * LN2 + jnp.log(l))

    m = jnp.maximum(jnp.maximum(lses[0], lses[1]), lses[2])
    w = [jnp.exp(x - m) for x in lses]
    oa_ref[0] = (w[0] * outs[0] + w[1] * outs[1] + w[2] * outs[2]) / (w[0] + w[1] + w[2])

    k3, v3 = mk_ref[0], mv_ref[0]
    om = jnp.zeros((ROWS16, HEAD_DIM), F32)
    for t in range(t_new):
        o_t, _ = _quad_attention(k3, v3, qt_ref[0, 1, t] * q_scale, jnp.zeros((1, ROWS16, HEAD_DIM), F32),
                                 None, None, None)
        om = jnp.where(quad16 == t, o_t, om)
    om_ref[0] = om

    for i, out_ref in enumerate(new_refs[N_GROUPS * 2 + 2:N_GROUPS * 4 + 2]):
        out_ref[0, 0] = (kn_ref, vn_ref)[i % 2][0, i // 2]


def _sample_attention(qd, qt, kn, vn, slopes16, k_states, v_states, shifted, mem_k, mem_v, *, t_new):
    nb = qd.shape[0]
    states = []
    for g in range(N_GROUPS):
        states += [k_states[g], v_states[g]]
    whole = lambda a: pl.BlockSpec((1,) + a.shape[1:], lambda b: (b,) + (0,) * (a.ndim - 1))
    head_rows = lambda a: pl.BlockSpec((1, a.shape[1], ROWS16, HEAD_DIM), lambda b: (b, 0, 0, 0))
    last_rows = lambda a: pl.BlockSpec((1, 1, ROWS16, HEAD_DIM),
                                       lambda b, a=a: (b, a.shape[1] - 1, a.shape[2] // ROWS16 - 1, 0))
    o_spec = pl.BlockSpec((1, ROWS16, HEAD_DIM), lambda b: (b, 0, 0))
    o_shape = jax.ShapeDtypeStruct((nb, ROWS16, HEAD_DIM), F32)
    n_in = 5 + len(states) + 2
    return _pallas_call(
        functools.partial(_sample_kernel, t_new=t_new),
        (qd, qt, kn, vn, slopes16, *states, mem_k, mem_v, *shifted),
        grid=(nb,),
        in_specs=[whole(qd), whole(qt), whole(kn), whole(vn),
                  pl.BlockSpec(slopes16.shape, lambda b: (0, 0, 0))]
                 + [head_rows(s) for s in states] + [whole(mem_k), whole(mem_v)]
                 + [pl.BlockSpec(memory_space=pl.ANY)] * len(shifted),
        out_specs=[o_spec, o_spec] + [last_rows(s) for s in shifted],
        out_shape=[o_shape, o_shape] + [jax.ShapeDtypeStruct(s.shape, F32) for s in shifted],
        input_output_aliases={n_in + i: 2 + i for i in range(len(shifted))},
        scratch_shapes=[pltpu.VMEM((t_new, DIL_GROUPS[0][0] // 4, ROWS16, HEAD_DIM), F32),
                        pltpu.VMEM((t_new, ROWS16, HEAD_DIM), F32),
                        pltpu.VMEM((BAND, ROWS16, HEAD_DIM), F32),
                        pltpu.VMEM((BAND, ROWS16, HEAD_DIM), F32)],
        name="sample_attn",
    )


def kernel(x_prompt, x_sample, mem_prompt, state_conv, state_k_w128, state_v_w128, state_k_w512, state_v_w512, state_k_w2048, state_v_w2048, cache_mem_k, cache_mem_v, ln1_g, ln1_b, w_ffn1_gu, w_ffn1_down, w_in, conv_w, w_mem_kv, w_br_conv, w_br_attn, w_br_mem, w_o, ln2_g, ln2_b, w_ffn2_gu, w_ffn2_down, ln3_g, ln3_b):
    depth = w_in.shape[0]
    seq = x_prompt.shape[1]
    nb, t_new = x_sample.shape[0], x_sample.shape[1]
    assert depth == 1 and x_prompt.shape[0] == 1 and seq % CHUNK == 0
    assert t_new * HEADS == ROWS16 and t_new <= DIL_GROUPS[1][1]

    wb = lambda w: w[0].astype(BF16)
    wc_b, wa_b, wm_b, wo_b = wb(w_br_conv), wb(w_br_attn), wb(w_br_mem), wb(w_o)
    n_ff_blocks = D_FF // FFN_COLS
    cw = conv_w[0]
    slopes_lane = np.broadcast_to(SLOPES[:, :, None], (N_GROUPS, HEADS, HEAD_DIM))
    slopes_band = jnp.asarray(np.pad(slopes_lane.transpose(1, 0, 2), ((0, 0), (0, 8 - N_GROUPS), (0, 0))))
    slopes16 = jnp.asarray(np.tile(slopes_lane, (1, ROWS16 // HEADS, 1)))

    rows_s = nb * SAMPLE_PAD
    xs = jnp.pad(x_sample, ((0, 0), (0, SAMPLE_PAD - t_new), (0, 0))).reshape(rows_s, D_MODEL)
    s1, s1b, w1g, w1u, w1d = _ffn(xs, w_ffn1_gu[0], w_ffn1_gu[0], w_ffn1_down[0], ln1_g, ln1_b,
                                  tm=rows_s, tf=FFN_COLS, up_blk0=n_ff_blocks, emit_bf16=True)
    ps_all, win_b = _matmul(s1b, w_in[0], tm=rows_s, tn=PROJ_COLS, out_dtype=F32, name="in_proj_s", emit_bf16=True)

    def tiles(state):
        win = state.shape[2]
        per_tile = 4 if win <= DIL_GROUPS[1][0] else DIL_GROUPS[2][1]
        return state.reshape(nb, win // per_tile, per_tile * HEADS, HEAD_DIM)

    k_states = [tiles(s) for s in (state_k_w128, state_k_w512, state_k_w2048)]
    v_states = [tiles(s) for s in (state_v_w128, state_v_w512, state_v_w2048)]
    h1, h1b, *shifted = _ffn(x_prompt[0], w1g, w1u, w1d, ln1_g, ln1_b, tm=FFN_ROWS, tf=FFN_COLS, manual_rows=True,
                             windows=k_states + v_states)
    shifted = [shifted[g + part * N_GROUPS] for g in range(N_GROUPS) for part in (0, 1)]
    p_all, qkv = _in_proj(h1b, win_b, tm=PROJ_ROWS, tn=PROJ_COLS)
    mem_kv = _matmul(mem_prompt[0].astype(BF16), w_mem_kv[0], tm=MEM_TOKENS, tn=GROUP_W, out_dtype=F32, name="mem_kv")
    o_attn = _band_attention(qkv, slopes_band)
    o_mem = _mem_attention(p_all, mem_kv, tq=MEM_Q_ROWS)
    h2, conv_p = _merge(None, o_attn, o_mem, p_all, h1, wc_b, wa_b, wm_b, wo_b, ln2_g, ln2_b, tm=MERGE_ROWS, conv_w=cw)

    prompt_states = []
    for g, (win, _) in enumerate(DIL_GROUPS):
        for tensor in (1, 2):
            c0 = tensor * N_GROUPS * GROUP_W + g * GROUP_W
            prompt_states.append(qkv[seq - win:, c0:c0 + GROUP_W].reshape(1, 1, win, HEADS, HEAD_DIM))
    mem_k_p = mem_kv[:, :GROUP_W].reshape(1, 1, MEM_TOKENS, HEADS, HEAD_DIM)
    mem_v_p = mem_kv[:, GROUP_W:].reshape(1, 1, MEM_TOKENS, HEADS, HEAD_DIM)

    ps3 = ps_all.reshape(nb, SAMPLE_PAD, IN_W)

    real = ps3[:, :t_new]
    heads = lambda a, n: a.reshape(nb, t_new, n, HEADS, HEAD_DIM)
    q, k, v = (heads(real[..., c:c + N_GROUPS * GROUP_W], N_GROUPS) for c in (COL_Q, COL_K, COL_VA))
    rows_th = lambda a: a.transpose(0, 2, 1, 3, 4).reshape(nb, a.shape[2], ROWS16, HEAD_DIM)
    over_quad = lambda a: jnp.broadcast_to(a[:, :, None], (nb, t_new, ROWS16 // HEADS, HEADS, HEAD_DIM)
                                           ).reshape(nb, t_new, ROWS16, HEAD_DIM)
    qd = rows_th(q[:, :, 1:])
    qm = real[..., COL_QM:COL_QM + GROUP_W].reshape(nb, t_new, HEADS, HEAD_DIM)
    qt = jnp.stack([over_quad(q[:, :, 0]), over_quad(qm)], axis=1)
    kn, vn = rows_th(k), rows_th(v)

    outs = _sample_attention(qd, qt, kn, vn, slopes16, k_states, v_states, shifted,
                             tiles(cache_mem_k), tiles(cache_mem_v), t_new=t_new)
    pad_rows = lambda a, w: jnp.pad(a.reshape(nb, t_new, w), ((0, 0), (0, SAMPLE_PAD - t_new), (0, 0))
                                    ).reshape(rows_s, w).astype(BF16)
    o_attn_s, o_mem_s = pad_rows(outs[0], GROUP_W), pad_rows(outs[1], GROUP_W)
    sample_states = [s.reshape(1, nb, -1, HEADS, HEAD_DIM) for s in outs[2:]]

    o_conv_s, conv_s = _conv(ps3, state_conv[0], cw, tm=SAMPLE_PAD, t_last=t_new)
    s2 = _merge(o_conv_s.reshape(rows_s, CONV_W), o_attn_s, o_mem_s, ps_all, s1,
                wc_b, wa_b, wm_b, wo_b, ln2_g, ln2_b, tm=rows_s)
    y_s, w2g, w2u, w2d = _ffn(s2, w_ffn2_gu[0], w_ffn2_gu[0], w_ffn2_down[0], ln3_g, ln3_b,
                              tm=rows_s, tf=FFN_COLS, up_blk0=n_ff_blocks, emit_bf16=True, bf16_out=False)
    y_sample = y_s.reshape(nb, SAMPLE_PAD, D_MODEL)[:, :t_new]
    (y_prompt,) = _ffn(h2, w2g, w2u, w2d, ln3_g, ln3_b, tm=FFN_ROWS, tf=FFN_COLS, bf16_out=False, row_buffers=2)

    return (y_prompt[None], y_sample, conv_p[None, None], *prompt_states, mem_k_p, mem_v_p,
            conv_s[None], *sample_states)
```

```python
import functools
import math

import jax
import jax.numpy as jnp
import numpy as np
from jax import lax
from jax.experimental import pallas as pl
from jax.experimental.pallas import tpu as pltpu

F32 = jnp.float32
BF16 = jnp.bfloat16

D_MODEL = 2048
D_FF = 5632
HEAD_DIM = 128
HEADS = 4
GROUP_W = HEADS * HEAD_DIM
DIL_GROUPS = ((128, 1), (512, 4), (2048, 16))
N_GROUPS = len(DIL_GROUPS)
BAND = 128
CHUNK = BAND * DIL_GROUPS[-1][1]
CONV_W = 1024
MEM_TOKENS = 256
IN_W = 14336
COL_B, COL_C, COL_V = 0, 1024, 2048
COL_Q, COL_K, COL_VA = 3072, 4608, 6144
COL_QM = 7680
COL_GC, COL_GA, COL_GM = 8192, 10240, 12288
ALPHA = 2.0 ** 0.25
LN_EPS = 1e-5
NEG_INF = -1e30
SCALE = HEAD_DIM ** -0.5
LOG2E = 1.4426950408889634
LN2 = 0.6931471805599453
SLOPES = np.exp2(np.float32(-8.0) * np.arange(1, 13, dtype=np.float32) / np.float32(12)).reshape(3, 4)

SAMPLE_PAD = 8
FFN_ROWS, FFN_COLS = 1024, 512
PROJ_ROWS, PROJ_COLS = 1024, 1024
MERGE_ROWS, MERGE_SUB_ROWS = 512, 256
MEM_Q_ROWS = 512
LN_ROWS = 128
WINDOW_DMA_PRIORITY = 1
ROWS16 = 16
V7X_VMEM_BYTES = 64 * 1024 * 1024
VALUE_SPILL_BYTES = 8 * 1024 * 1024


def _nbytes(shape, dtype):
    return math.prod(1 if d is None else d for d in shape) * jnp.dtype(dtype).itemsize


def _pallas_call(kernel, operands, *, grid, in_specs, out_specs, out_shape, name, scratch_shapes=(),
                 input_output_aliases=None):
    out_specs, out_shape = list(out_specs), list(out_shape)
    vmem = VALUE_SPILL_BYTES
    for spec, array in list(zip(in_specs, operands)) + list(zip(out_specs, out_shape)):
        if spec.block_shape is not None:
            buffers = 2 if spec.pipeline_mode is None else spec.pipeline_mode.buffer_count
            vmem += buffers * _nbytes(spec.block_shape, array.dtype)
    vmem += sum(_nbytes(s.shape, s.dtype) for s in scratch_shapes if s.memory_space == pltpu.VMEM)
    assert vmem <= V7X_VMEM_BYTES, (name, vmem)
    call = pl.pallas_call(
        kernel, grid=grid, in_specs=in_specs, out_specs=out_specs, out_shape=out_shape,
        scratch_shapes=list(scratch_shapes), input_output_aliases=input_output_aliases or {},
        compiler_params=pltpu.CompilerParams(dimension_semantics=("arbitrary",) * len(grid),
                                             vmem_limit_bytes=vmem),
        name=name)
    return call(*operands)


def _layer_norm_rows(z, g, b):
    mu = jnp.mean(z, axis=-1, keepdims=True)
    zc = z - mu
    var = jnp.mean(zc * zc, axis=-1, keepdims=True)
    return zc * lax.rsqrt(var + LN_EPS) * g + b


def _window_loads(b, src_refs, stage_refs, slot, sems):
    return [pltpu.make_async_copy(src.at[b], stage.at[slot], sems.at[slot, n])
            for n, (src, stage) in enumerate(zip(src_refs, stage_refs))]


def _window_stores(b, stage_refs, dst_refs, slot, sems):
    copies = []
    for stage, dst in zip(stage_refs, dst_refs):
        tiles, rows = dst.shape[1], dst.shape[2]
        keep = rows - ROWS16
        if keep:
            copies.append((stage.at[slot, :, pl.ds(ROWS16, keep), :], dst.at[b, :, pl.ds(0, keep), :]))
        copies.append((stage.at[slot, pl.ds(1, tiles - 1), pl.ds(0, ROWS16), :],
                       dst.at[b, pl.ds(0, tiles - 1), pl.ds(keep, ROWS16), :]))
        copies.append((stage.at[slot, pl.ds(tiles - 1, 1), pl.ds(keep, ROWS16), :],
                       dst.at[b, pl.ds(tiles - 1, 1), pl.ds(keep, ROWS16), :]))
    return [pltpu.make_async_copy(s, d, sems.at[slot, n]) for n, (s, d) in enumerate(copies)]


def _ffn_kernel(*refs, n_ff, row_chunk, emit_bf16, bf16_out, manual_rows, n_windows, n_requests):
    x_ref, wg_ref, wu_ref, wd_ref, g_ref, b_ref = refs[:6]
    win_src = refs[6:6 + n_windows]
    y_ref = refs[6 + n_windows]
    yb_ref = refs[7 + n_windows] if bf16_out else None
    rest = refs[7 + bf16_out + n_windows:]
    if emit_bf16:
        wb_refs, rest = rest[:3], rest[3:]
    win_dst, rest = rest[:n_windows], rest[n_windows:]
    xb_ref, rest = rest[0], rest[1:]
    i, j = pl.program_id(0), pl.program_id(1)
    n_tiles = pl.num_programs(0)
    tm = xb_ref.shape[0]

    if manual_rows:
        y_hbm, yb_hbm = y_ref, yb_ref
        n_ring = 1 + bf16_out
        xbuf_ref, xsem, y_ref = rest[:3]
        ring_refs, out_sems = rest[3:3 + n_ring], rest[3 + n_ring]
        rest = rest[4 + n_ring:]

        def x_copy(tile):
            return pltpu.make_async_copy(x_ref.at[pl.ds(pl.multiple_of(tile * tm, tm), tm), :], xbuf_ref, xsem.at[0])

        @pl.when((i == 0) & (j == 0))
        def _():
            x_copy(0).start()

    if n_windows:
        half = n_windows // 2
        stage_refs, load_sems, store_sems = rest[:half], rest[half], rest[half + 1]
        step = i * n_ff + j
        n_items = 2 * n_requests
        for part in (0, 1):
            srcs, dsts = win_src[part * half:(part + 1) * half], win_dst[part * half:(part + 1) * half]
            mine = step % 2 == part

            @pl.when(mine & (step >= 2) & (step < n_items + 2))
            def _():
                for copy in _window_stores((step - 2) // 2, stage_refs, dsts, part, store_sems):
                    copy.wait()

            @pl.when(mine & (step < n_items))
            def _():
                for copy in _window_loads(step // 2, srcs, stage_refs, part, load_sems):
                    copy.start(priority=WINDOW_DMA_PRIORITY)

            @pl.when(jnp.logical_not(mine) & (step >= 1) & (step < n_items + 1))
            def _():
                for copy in _window_loads((step - 1) // 2, srcs, stage_refs, part, load_sems):
                    copy.wait()
                for copy in _window_stores((step - 1) // 2, stage_refs, dsts, part, store_sems):
                    copy.start(priority=WINDOW_DMA_PRIORITY)

    @pl.when(j == 0)
    def _():
        if manual_rows:
            x_copy(i).wait()
        x = xbuf_ref[...] if manual_rows else x_ref[...]
        xb_ref[...] = x.astype(BF16)
        y_ref[...] = (2.0 * ALPHA) * x

    if manual_rows:
        @pl.when((j == 1) & (i + 1 < n_tiles))
        def _():
            x_copy(i + 1).start()

    wg, wu, wd = wg_ref[...], wu_ref[...], wd_ref[...]
    if emit_bf16:
        wg, wu, wd = wg.astype(BF16), wu.astype(BF16), wd.astype(BF16)
        wb_refs[0][...], wb_refs[1][...], wb_refs[2][...] = wg, wu, wd
    xb = xb_ref[...]
    gate = jnp.dot(xb, wg, preferred_element_type=F32)
    up = jnp.dot(xb, wu, preferred_element_type=F32)
    act = (gate * jax.nn.sigmoid(gate) * up).astype(BF16)
    y_ref[...] += jnp.dot(act, wd, preferred_element_type=F32)

    if not manual_rows:
        @pl.when(j == n_ff - 1)
        def _():
            def chunk(c, carry):
                rows = pl.ds(pl.multiple_of(c * row_chunk, row_chunk), row_chunk)
                y = _layer_norm_rows(0.5 * y_ref[rows, :], g_ref[...], b_ref[...])
                y_ref[rows, :] = y
                if bf16_out:
                    yb_ref[rows, :] = y.astype(BF16)
                return carry
            lax.fori_loop(0, tm // row_chunk, chunk, 0)
    else:
        def out_copies(slot, row0):
            dsts = (y_hbm, yb_hbm)[:n_ring]
            return [pltpu.make_async_copy(ring.at[slot], dst.at[pl.ds(row0, row_chunk), :], out_sems.at[slot, n])
                    for n, (ring, dst) in enumerate(zip(ring_refs, dsts))]

        def wait_slot(slot):
            for copy in out_copies(slot, 0):
                copy.wait()

        @pl.when(j == n_ff - 1)
        def _():
            n_chunks = tm // row_chunk
            for c in range(n_chunks):
                slot = c % 2
                if c >= 2:
                    wait_slot(slot)
                else:
                    pl.when(i > 0)(functools.partial(wait_slot, slot))
                y = _layer_norm_rows(0.5 * y_ref[c * row_chunk:(c + 1) * row_chunk, :], g_ref[...], b_ref[...])
                ring_refs[0][slot] = y
                if bf16_out:
                    ring_refs[1][slot] = y.astype(BF16)
                for copy in out_copies(slot, pl.multiple_of(i * tm, tm) + c * row_chunk):
                    copy.start()

            @pl.when(i == n_tiles - 1)
            def _():
                wait_slot(0)
                wait_slot(1)


def _ffn(x, w_gate, w_up, w_down, g, b, *, tm, tf, up_blk0=0, emit_bf16=False, bf16_out=True,
         row_buffers=1, manual_rows=False, windows=()):
    rows = x.shape[0]
    n_ff = D_FF // tf
    n_steps = (rows // tm) * n_ff
    n_requests = windows[0].shape[0] if windows else 0
    assert not emit_bf16 or rows == tm
    assert n_steps >= 2 * n_requests + 2
    row_mode = pl.Buffered(1) if (rows == tm or row_buffers == 1) else None
    hbm = pl.BlockSpec(memory_space=pl.ANY)
    w_col = pl.BlockSpec((D_MODEL, tf), lambda i, j: (0, j))
    w_col_tile = pl.BlockSpec((None, D_MODEL, tf), lambda i, j: (j, 0, 0))
    tiled = w_gate.ndim == 3
    w_row = pl.BlockSpec((tf, D_MODEL), lambda i, j: (j, 0))
    row_tile = pl.BlockSpec((tm, D_MODEL), lambda i, j: (i, 0), pipeline_mode=row_mode)
    out_specs = [hbm if manual_rows else row_tile] * (1 + bf16_out)
    out_shape = [jax.ShapeDtypeStruct((rows, D_MODEL), F32), jax.ShapeDtypeStruct((rows, D_MODEL), BF16)][:1 + bf16_out]
    scratch = [pltpu.VMEM((tm, D_MODEL), BF16)]
    if manual_rows:
        row_chunk = min(tm, LN_ROWS)
        assert n_ff >= 2 and (tm // row_chunk) % 2 == 0
        scratch += [pltpu.VMEM((tm, D_MODEL), F32), pltpu.SemaphoreType.DMA((1,)), pltpu.VMEM((tm, D_MODEL), F32)]
        scratch += [pltpu.VMEM((2, row_chunk, D_MODEL), dt) for dt in (F32, BF16)[:1 + bf16_out]]
        scratch += [pltpu.SemaphoreType.DMA((2, 1 + bf16_out))]
    if emit_bf16:
        out_specs += [w_col_tile, w_col_tile, w_row]
        out_shape += [jax.ShapeDtypeStruct((n_ff, D_MODEL, tf), BF16)] * 2 + [jax.ShapeDtypeStruct((D_FF, D_MODEL), BF16)]
    if windows:
        out_specs += [hbm] * len(windows)
        out_shape += [jax.ShapeDtypeStruct(w.shape, w.dtype) for w in windows]
        half = windows[:len(windows) // 2]
        scratch += [pltpu.VMEM((2,) + w.shape[1:], w.dtype) for w in half]
        scratch += [pltpu.SemaphoreType.DMA((2, len(half))),
                    pltpu.SemaphoreType.DMA((2, sum(2 + (w.shape[2] > ROWS16) for w in half)))]
    return _pallas_call(
        functools.partial(_ffn_kernel, n_ff=n_ff, row_chunk=min(tm, LN_ROWS), emit_bf16=emit_bf16, bf16_out=bf16_out,
                          manual_rows=manual_rows, n_windows=len(windows), n_requests=n_requests),
        (x, w_gate, w_up, w_down, g, b, *windows),
        grid=(rows // tm, n_ff),
        in_specs=[
            hbm if manual_rows else row_tile,
            w_col_tile if tiled else w_col,
            w_col_tile if tiled else pl.BlockSpec((D_MODEL, tf), lambda i, j: (0, j + up_blk0)),
            w_row,
            pl.BlockSpec((1, D_MODEL), lambda i, j: (0, 0)),
            pl.BlockSpec((1, D_MODEL), lambda i, j: (0, 0)),
        ] + [hbm] * len(windows),
        out_specs=out_specs,
        out_shape=out_shape,
        scratch_shapes=scratch,
        name="ffn_cast" if emit_bf16 else "ffn",
    )


def _mm_kernel(x_ref, w_ref, o_ref, *wb_ref):
    w = w_ref[...].astype(BF16)
    if wb_ref:
        wb_ref[0][...] = w
    o_ref[...] = jnp.dot(x_ref[...], w, preferred_element_type=F32).astype(o_ref.dtype)


def _matmul(x, w, *, tm, tn, out_dtype, name, emit_bf16=False):
    rows, k = x.shape
    n_cols = w.shape[1]
    assert not emit_bf16 or rows == tm
    w_spec = pl.BlockSpec((k, tn), lambda i, j: (0, j))
    out_specs = [pl.BlockSpec((tm, tn), lambda i, j: (i, j))]
    out_shape = [jax.ShapeDtypeStruct((rows, n_cols), out_dtype)]
    if emit_bf16:
        out_specs.append(pl.BlockSpec((None, k, tn), lambda i, j: (j, 0, 0)))
        out_shape.append(jax.ShapeDtypeStruct((n_cols // tn, k, tn), BF16))
    out = _pallas_call(
        _mm_kernel,
        (x, w),
        grid=(rows // tm, n_cols // tn),
        in_specs=[pl.BlockSpec((tm, k), lambda i, j: (i, 0)), w_spec],
        out_specs=out_specs,
        out_shape=out_shape,
        name=name,
    )
    return out if emit_bf16 else out[0]


def _in_proj_kernel(x_ref, w_ref, p_ref, qkv_ref, *, n_own):
    j = pl.program_id(1)
    acc = jnp.dot(x_ref[...], w_ref[...], preferred_element_type=F32)
    p_ref[...] = acc.astype(BF16)

    @pl.when(j < n_own)
    def _():
        qkv_ref[...] = acc


def _in_proj(x, w, *, tm, tn):
    rows, k = x.shape
    assert w.shape == (IN_W // tn, k, tn) and COL_Q % tn == 0 and COL_GC % tn == 0
    c_lo, n_own = COL_Q // tn, (COL_GC - COL_Q) // tn

    def col_block(j):
        return jnp.where(j < n_own, j + c_lo, jnp.where(j < n_own + c_lo, j - n_own, j))

    return _pallas_call(
        functools.partial(_in_proj_kernel, n_own=n_own),
        (x, w),
        grid=(rows // tm, IN_W // tn),
        in_specs=[pl.BlockSpec((tm, k), lambda i, j: (i, 0)),
                  pl.BlockSpec((None, k, tn), lambda i, j: (col_block(j), 0, 0))],
        out_specs=[pl.BlockSpec((tm, tn), lambda i, j: (i, col_block(j))),
                   pl.BlockSpec((tm, tn), lambda i, j: (i, jnp.minimum(j, n_own - 1)))],
        out_shape=[jax.ShapeDtypeStruct((rows, IN_W), BF16),
                   jax.ShapeDtypeStruct((rows, COL_GC - COL_Q), F32)],
        name="in_proj",
    )


def _rows(ref, start, dil):
    if dil == 1:
        return ref[start:start + BAND, :]
    return ref[pl.ds(start, BAND, stride=dil), :]


def _band_kernel(sl_ref, q0_ref, k0_ref, v0_ref, q1_ref, k1_ref, v1_ref, q2_ref, k2_ref, v2_ref,
                 o_ref, kt0_ref, vt0_ref, kt1_ref, vt1_ref, kt2_ref, vt2_ref, og_ref, lg_ref):
    c = pl.program_id(1)
    q_refs = (q0_ref, q1_ref, q2_ref)
    k_refs = (k0_ref, k1_ref, k2_ref)
    v_refs = (v0_ref, v1_ref, v2_ref)
    kt_refs = (kt0_ref, kt1_ref, kt2_ref)
    vt_refs = (vt0_ref, vt1_ref, vt2_ref)

    @pl.when(c == 0)
    def _():
        for ref in kt_refs + vt_refs:
            ref[...] = jnp.zeros(ref.shape, F32)

    qi = lax.broadcasted_iota(jnp.int32, (BAND, 2 * BAND), 0)
    ki = lax.broadcasted_iota(jnp.int32, (BAND, 2 * BAND), 1)
    rel = BAND + qi - ki
    valid = (rel >= 0) & (rel <= BAND)
    no_prev = jnp.where(ki < BAND, jnp.where(c == 0, NEG_INF, 0.0), 0.0)

    for g, (_, dil) in enumerate(DIL_GROUPS):
        slope = sl_ref[0, g:g + 1, 0:1] * LOG2E
        bias = jnp.where(valid, -(slope * (rel * dil).astype(F32)), NEG_INF)
        bias_first = bias + no_prev
        span = BAND * dil
        for r in range(dil):
            for n in range(CHUNK // span):
                start = r + n * span
                q = _rows(q_refs[g], start, dil).astype(BF16)
                if n == 0:
                    k_prev, v_prev = _rows(kt_refs[g], r, dil), _rows(vt_refs[g], r, dil)
                else:
                    k_prev, v_prev = _rows(k_refs[g], start - span, dil), _rows(v_refs[g], start - span, dil)
                k = jnp.concatenate([k_prev, _rows(k_refs[g], start, dil)], axis=0).astype(BF16)
                v = jnp.concatenate([v_prev, _rows(v_refs[g], start, dil)], axis=0).astype(BF16)
                s = lax.dot_general(q, k, (((1,), (1,)), ((), ())), preferred_element_type=F32) * (SCALE * LOG2E)
                s = s + (bias_first if n == 0 else bias)
                m = jnp.max(s, axis=-1, keepdims=True)
                p = jnp.exp2(s - m)
                l = jnp.sum(p, axis=-1, keepdims=True)
                o = jnp.dot(p.astype(BF16), v, preferred_element_type=F32) / l
                lse = jnp.broadcast_to(m * LN2 + jnp.log(l), (BAND, HEAD_DIM))
                if dil == 1:
                    og_ref[g, start:start + BAND, :] = o
                    lg_ref[g, start:start + BAND, :] = lse
                else:
                    og_ref[g, pl.ds(start, BAND, stride=dil), :] = o
                    lg_ref[g, pl.ds(start, BAND, stride=dil), :] = lse
        kt_refs[g][...] = k_refs[g][CHUNK - span:CHUNK, :]
        vt_refs[g][...] = v_refs[g][CHUNK - span:CHUNK, :]

    rows_per_step = 256

    def combine(i, carry):
        rows = pl.ds(pl.multiple_of(i * rows_per_step, rows_per_step), rows_per_step)
        l0, l1, l2 = lg_ref[0, rows, :], lg_ref[1, rows, :], lg_ref[2, rows, :]
        m = jnp.maximum(jnp.maximum(l0, l1), l2)
        e0, e1, e2 = jnp.exp(l0 - m), jnp.exp(l1 - m), jnp.exp(l2 - m)
        o = (e0 * og_ref[0, rows, :] + e1 * og_ref[1, rows, :] + e2 * og_ref[2, rows, :]) / (e0 + e1 + e2)
        o_ref[rows, :] = o.astype(o_ref.dtype)
        return carry

    lax.fori_loop(0, CHUNK // rows_per_step, combine, 0)


def _band_attention(qkv, slopes):
    seq = qkv.shape[0]
    blk = (CHUNK, HEAD_DIM)
    per_tensor = N_GROUPS * HEADS
    in_specs = [pl.BlockSpec((1, 8, HEAD_DIM), lambda h, c: (h, 0, 0))]
    for g in range(N_GROUPS):
        for tensor in range(3):
            in_specs.append(pl.BlockSpec(
                blk, lambda h, c, g=g, tensor=tensor: (c, tensor * per_tensor + g * HEADS + h)))
    tails = []
    for _, dil in DIL_GROUPS:
        tails += [pltpu.VMEM((BAND * dil, HEAD_DIM), F32)] * 2
    return _pallas_call(
        _band_kernel,
        (slopes, *([qkv] * 9)),
        grid=(HEADS, seq // CHUNK),
        in_specs=in_specs,
        out_specs=[pl.BlockSpec(blk, lambda h, c: (c, h))],
        out_shape=[jax.ShapeDtypeStruct((seq, GROUP_W), BF16)],
        scratch_shapes=tails + [pltpu.VMEM((N_GROUPS, CHUNK, HEAD_DIM), F32),
                                pltpu.VMEM((N_GROUPS, CHUNK, HEAD_DIM), F32)],
        name="band_attn",
    )[0]


def _mem_attn_kernel(q_ref, mk_ref, mv_ref, o_ref):
    for h in range(HEADS):
        hs = slice(h * HEAD_DIM, (h + 1) * HEAD_DIM)
        q = q_ref[:, hs]
        k = mk_ref[:, hs].astype(BF16)
        v = mv_ref[:, hs].astype(BF16)
        s = lax.dot_general(q, k, (((1,), (1,)), ((), ())), preferred_element_type=F32) * SCALE
        m = jnp.max(s, axis=-1, keepdims=True)
        p = jnp.exp(s - m)
        l = jnp.sum(p, axis=-1, keepdims=True)
        o = jnp.dot(p.astype(BF16), v, preferred_element_type=F32) / l
        o_ref[:, hs] = o.astype(o_ref.dtype)


def _mem_attention(p_all, mem_kv, *, tq):
    seq = p_all.shape[0]
    return _pallas_call(
        _mem_attn_kernel,
        (p_all, mem_kv, mem_kv),
        grid=(seq // tq,),
        in_specs=[pl.BlockSpec((tq, GROUP_W), lambda i: (i, COL_QM // GROUP_W)),
                  pl.BlockSpec((MEM_TOKENS, GROUP_W), lambda i: (0, 0)),
                  pl.BlockSpec((MEM_TOKENS, GROUP_W), lambda i: (0, 1))],
        out_specs=[pl.BlockSpec((tq, GROUP_W), lambda i: (i, 0))],
        out_shape=[jax.ShapeDtypeStruct((seq, GROUP_W), BF16)],
        name="mem_attn",
    )[0]


def _conv_rows(b, c, v, w_ref, carry_ref):
    u = c.astype(F32) * v.astype(F32)
    c2 = carry_ref[0:1, :]
    c1 = carry_ref[1:2, :]
    row = lax.broadcasted_iota(jnp.int32, u.shape, 0)
    u1 = jnp.where(row == 0, c1, pltpu.roll(u, 1, 0))
    u2 = jnp.where(row == 0, c2, jnp.where(row == 1, c1, pltpu.roll(u, 2, 0)))
    return b.astype(F32) * (w_ref[0:1, :] * u2 + w_ref[1:2, :] * u1 + w_ref[2:3, :] * u), u


def _conv_kernel(b_ref, c_ref, v_ref, st_ref, w_ref, o_ref, ns_ref, carry_ref, *, t_last):
    i = pl.program_id(1)

    @pl.when(i == 0)
    def _():
        carry_ref[0:2, :] = st_ref[0]

    o, u = _conv_rows(b_ref[0], c_ref[0], v_ref[0], w_ref, carry_ref)
    o_ref[0] = o.astype(o_ref.dtype)
    last2 = u[t_last - 2:t_last, :]
    carry_ref[0:2, :] = last2
    ns_ref[0] = last2


def _conv(p3, state, conv_w, *, tm, t_last):
    nb, t = p3.shape[0], p3.shape[1]
    blk = (1, tm, CONV_W)
    return _pallas_call(
        functools.partial(_conv_kernel, t_last=t_last),
        (p3, p3, p3, state, conv_w),
        grid=(nb, t // tm),
        in_specs=[
            pl.BlockSpec(blk, lambda b, i: (b, i, COL_B // CONV_W)),
            pl.BlockSpec(blk, lambda b, i: (b, i, COL_C // CONV_W)),
            pl.BlockSpec(blk, lambda b, i: (b, i, COL_V // CONV_W)),
            pl.BlockSpec((1, 2, CONV_W), lambda b, i: (b, 0, 0)),
            pl.BlockSpec((3, CONV_W), lambda b, i: (0, 0)),
        ],
        out_specs=[pl.BlockSpec(blk, lambda b, i: (b, i, 0)),
                   pl.BlockSpec((1, 2, CONV_W), lambda b, i: (b, 0, 0))],
        out_shape=[jax.ShapeDtypeStruct((nb, t, CONV_W), BF16),
                   jax.ShapeDtypeStruct((nb, 2, CONV_W), F32)],
        scratch_shapes=[pltpu.VMEM((8, CONV_W), F32)],
        name="conv",
    )


def _merge_kernel(*refs, sub_rows, fuse_conv):
    if fuse_conv:
        b_ref, c_ref, v_ref, cw_ref = refs[:4]
        refs = refs[4:]
    else:
        oc_ref = refs[0]
        refs = refs[1:]
    oa_ref, om_ref, gc_ref, ga_ref, gm_ref, h_ref, wc_ref, wa_ref, wm_ref, wo_ref, g_ref, beta_ref, y_ref = refs[:13]
    if fuse_conv:
        ns_ref, carry_ref = refs[13:]

        @pl.when(pl.program_id(0) == 0)
        def _():
            carry_ref[...] = jnp.zeros(carry_ref.shape, F32)

    for r0 in range(0, h_ref.shape[0], sub_rows):
        rows = slice(r0, r0 + sub_rows)
        if fuse_conv:
            oc, u = _conv_rows(b_ref[rows, :], c_ref[rows, :], v_ref[rows, :], cw_ref, carry_ref)
            oc = oc.astype(BF16)
            carry_ref[0:2, :] = u[sub_rows - 2:sub_rows, :]
        else:
            oc = oc_ref[rows, :]
        mc = jnp.dot(oc, wc_ref[...], preferred_element_type=F32)
        ma = jnp.dot(oa_ref[rows, :], wa_ref[...], preferred_element_type=F32)
        mm = jnp.dot(om_ref[rows, :], wm_ref[...], preferred_element_type=F32)
        mix = (jax.nn.sigmoid(gc_ref[rows, :].astype(F32)) * mc
               + jax.nn.sigmoid(ga_ref[rows, :].astype(F32)) * ma
               + jax.nn.sigmoid(gm_ref[rows, :].astype(F32)) * mm)
        z = ALPHA * h_ref[rows, :] + jnp.dot(mix.astype(BF16), wo_ref[...], preferred_element_type=F32)
        y_ref[rows, :] = _layer_norm_rows(z, g_ref[...], beta_ref[...])
    if fuse_conv:
        ns_ref[...] = carry_ref[0:2, :]


def _merge(o_conv, o_attn, o_mem, p_all, h, w_c, w_a, w_m, w_o, g, b, *, tm, conv_w=None):
    rows = h.shape[0]
    fuse_conv = o_conv is None
    single = pl.Buffered(1)
    row_blk = lambda w: pl.BlockSpec((tm, w), lambda i: (i, 0))
    col_blk = lambda w, col: pl.BlockSpec((tm, w), lambda i: (i, col // w))
    whole = lambda a: pl.BlockSpec(a.shape, lambda i: (0, 0), pipeline_mode=single)
    if fuse_conv:
        conv_specs = [col_blk(CONV_W, COL_B), col_blk(CONV_W, COL_C), col_blk(CONV_W, COL_V), whole(conv_w)]
        conv_args = [p_all, p_all, p_all, conv_w]
    else:
        conv_specs, conv_args = [row_blk(CONV_W)], [o_conv]
    out = _pallas_call(
        functools.partial(_merge_kernel, sub_rows=min(tm, MERGE_SUB_ROWS), fuse_conv=fuse_conv),
        (*conv_args, o_attn, o_mem, p_all, p_all, p_all, h, w_c, w_a, w_m, w_o, g, b),
        grid=(rows // tm,),
        in_specs=conv_specs + [row_blk(GROUP_W), row_blk(GROUP_W),
                               col_blk(D_MODEL, COL_GC), col_blk(D_MODEL, COL_GA), col_blk(D_MODEL, COL_GM),
                               row_blk(D_MODEL),
                               whole(w_c), whole(w_a), whole(w_m), whole(w_o), whole(g), whole(b)],
        out_specs=[row_blk(D_MODEL)] + ([pl.BlockSpec((2, CONV_W), lambda i: (0, 0))] if fuse_conv else []),
        out_shape=[jax.ShapeDtypeStruct((rows, D_MODEL), F32)]
                  + ([jax.ShapeDtypeStruct((2, CONV_W), F32)] if fuse_conv else []),
        scratch_shapes=[pltpu.VMEM((8, CONV_W), F32)] if fuse_conv else [],
        name="merge",
    )
    return out if fuse_conv else out[0]


def _over_quad(x, op):
    x = op(x, pltpu.roll(x, HEADS, 0))
    return op(x, pltpu.roll(x, 2 * HEADS, 0))


def _lane_dot(k3, q):
    return jnp.sum(k3 * q, axis=-1, keepdims=True)


def _quad_attention(k3, v3, qt, bias3, k_new, v_new, bias_new):
    s = _lane_dot(k3, qt[None]) + bias3
    m = jnp.max(s, axis=0)
    if k_new is not None:
        s_new = _lane_dot(k_new, qt) + bias_new
        m = jnp.maximum(m, s_new)
    m = _over_quad(m, jnp.maximum)
    e = jnp.exp2(s - m[None])
    l = jnp.sum(e, axis=0)
    acc = jnp.sum(e * v3, axis=0)
    if k_new is not None:
        e_new = jnp.exp2(s_new - m)
        l = l + e_new
        acc = acc + e_new * v_new
    l = _over_quad(l, jnp.add)
    acc = _over_quad(acc, jnp.add)
    return acc / l, m * LN2 + jnp.log(l)


def _sample_kernel(qd_ref, qt_ref, kn_ref, vn_ref, sl_ref,
                   k0_ref, v0_ref, k1_ref, v1_ref, k2_ref, v2_ref, mk_ref, mv_ref, *new_refs, t_new):
    k_refs = (k0_ref, k1_ref, k2_ref)
    v_refs = (v0_ref, v1_ref, v2_ref)
    oa_ref, om_ref = new_refs[N_GROUPS * 2], new_refs[N_GROUPS * 2 + 1]
    bias0_ref, bias0_new_ref, bias1_ref, bias2_ref = new_refs[-4:]
    bias_refs = (None, bias1_ref, bias2_ref)
    quad16 = lax.broadcasted_iota(jnp.int32, (ROWS16, HEAD_DIM), 0) >> 2
    q_scale = SCALE * LOG2E
    win0 = DIL_GROUPS[0][0]

    @pl.when(pl.program_id(0) == 0)
    def _():
        a_idx = lax.broadcasted_iota(jnp.int32, (win0 // 4, ROWS16, HEAD_DIM), 0)
        b_idx = lax.broadcasted_iota(jnp.int32, (win0 // 4, ROWS16, HEAD_DIM), 1) >> 2
        tok = a_idx * 4 + b_idx
        slope0 = sl_ref[0] * LOG2E
        for t in range(t_new):
            bias0_ref[t] = jnp.where(tok >= t, -(slope0[None] * (win0 + t - tok).astype(F32)), NEG_INF)
            bias0_new_ref[t] = jnp.where(quad16 <= t, -(slope0 * (t - quad16).astype(F32)), NEG_INF)
        i_idx = lax.broadcasted_iota(jnp.int32, (BAND, ROWS16, HEAD_DIM), 0)
        for g in (1, 2):
            win, dil = DIL_GROUPS[g]
            bias_refs[g][...] = -((sl_ref[g] * LOG2E)[None] * (win - i_idx * dil).astype(F32))

    outs, lses = [], []

    k3, v3 = k0_ref[0], v0_ref[0]
    o0 = jnp.zeros((ROWS16, HEAD_DIM), F32)
    l0 = jnp.zeros((ROWS16, HEAD_DIM), F32)
    for t in range(t_new):
        o_t, l_t = _quad_attention(k3, v3, qt_ref[0, 0, t] * q_scale, bias0_ref[t],
                                   kn_ref[0, 0], vn_ref[0, 0], bias0_new_ref[t])
        o0 = jnp.where(quad16 == t, o_t, o0)
        l0 = jnp.where(quad16 == t, l_t, l0)
    outs.append(o0)
    lses.append(l0)

    for g in (1, 2):
        q16 = qd_ref[0, g - 1] * q_scale
        k3 = k_refs[g][0, :, 0:ROWS16, :]
        v3 = v_refs[g][0, :, 0:ROWS16, :]
        s = _lane_dot(k3, q16[None]) + bias_refs[g][...]
        s_new = _lane_dot(kn_ref[0, g], q16)
        m = jnp.maximum(jnp.max(s, axis=0), s_new)
        e = jnp.exp2(s - m[None])
        e_new = jnp.exp2(s_new - m)
        l = jnp.sum(e, axis=0) + e_new
        acc = jnp.sum(e * v3, axis=0) + e_new * vn_ref[0, g]
        outs.append(acc / l)
        lses.append(m * LN2 + jnp.log(l))

    m = jnp.maximum(jnp.maximum(lses[0], lses[1]), lses[2])
    w = [jnp.exp(x - m) for x in lses]
    oa_ref[0] = (w[0] * outs[0] + w[1] * outs[1] + w[2] * outs[2]) / (w[0] + w[1] + w[2])

    k3, v3 = mk_ref[0], mv_ref[0]
    om = jnp.zeros((ROWS16, HEAD_DIM), F32)
    for t in range(t_new):
        o_t, _ = _quad_attention(k3, v3, qt_ref[0, 1, t] * q_scale, jnp.zeros((1, ROWS16, HEAD_DIM), F32),
                                 None, None, None)
        om = jnp.where(quad16 == t, o_t, om)
    om_ref[0] = om

    for i, out_ref in enumerate(new_refs[N_GROUPS * 2 + 2:N_GROUPS * 4 + 2]):
        out_ref[0, 0] = (kn_ref, vn_ref)[i % 2][0, i // 2]


def _sample_attention(qd, qt, kn, vn, slopes16, k_states, v_states, shifted, mem_k, mem_v, *, t_new):
    nb = qd.shape[0]
    states = []
    for g in range(N_GROUPS):
        states += [k_states[g], v_states[g]]
    whole = lambda a: pl.BlockSpec((1,) + a.shape[1:], lambda b: (b,) + (0,) * (a.ndim - 1))
    head_rows = lambda a: pl.BlockSpec((1, a.shape[1], ROWS16, HEAD_DIM), lambda b: (b, 0, 0, 0))
    last_rows = lambda a: pl.BlockSpec((1, 1, ROWS16, HEAD_DIM),
                                       lambda b, a=a: (b, a.shape[1] - 1, a.shape[2] // ROWS16 - 1, 0))
    o_spec = pl.BlockSpec((1, ROWS16, HEAD_DIM), lambda b: (b, 0, 0))
    o_shape = jax.ShapeDtypeStruct((nb, ROWS16, HEAD_DIM), F32)
    n_in = 5 + len(states) + 2
    return _pallas_call(
        functools.partial(_sample_kernel, t_new=t_new),
        (qd, qt, kn, vn, slopes16, *states, mem_k, mem_v, *shifted),
        grid=(nb,),
        in_specs=[whole(qd), whole(qt), whole(kn), whole(vn),
                  pl.BlockSpec(slopes16.shape, lambda b: (0, 0, 0))]
                 + [head_rows(s) for s in states] + [whole(mem_k), whole(mem_v)]
                 + [pl.BlockSpec(memory_space=pl.ANY)] * len(shifted),
        out_specs=[o_spec, o_spec] + [last_rows(s) for s in shifted],
        out_shape=[o_shape, o_shape] + [jax.ShapeDtypeStruct(s.shape, F32) for s in shifted],
        input_output_aliases={n_in + i: 2 + i for i in range(len(shifted))},
        scratch_shapes=[pltpu.VMEM((t_new, DIL_GROUPS[0][0] // 4, ROWS16, HEAD_DIM), F32),
                        pltpu.VMEM((t_new, ROWS16, HEAD_DIM), F32),
                        pltpu.VMEM((BAND, ROWS16, HEAD_DIM), F32),
                        pltpu.VMEM((BAND, ROWS16, HEAD_DIM), F32)],
        name="sample_attn",
    )


def kernel(x_prompt, x_sample, mem_prompt, state_conv, state_k_w128, state_v_w128, state_k_w512, state_v_w512, state_k_w2048, state_v_w2048, cache_mem_k, cache_mem_v, ln1_g, ln1_b, w_ffn1_gu, w_ffn1_down, w_in, conv_w, w_mem_kv, w_br_conv, w_br_attn, w_br_mem, w_o, ln2_g, ln2_b, w_ffn2_gu, w_ffn2_down, ln3_g, ln3_b):
    depth = w_in.shape[0]
    seq = x_prompt.shape[1]
    nb, t_new = x_sample.shape[0], x_sample.shape[1]
    assert depth == 1 and x_prompt.shape[0] == 1 and seq % CHUNK == 0
    assert t_new * HEADS == ROWS16 and t_new <= DIL_GROUPS[1][1]

    wb = lambda w: w[0].astype(BF16)
    wc_b, wa_b, wm_b, wo_b = wb(w_br_conv), wb(w_br_attn), wb(w_br_mem), wb(w_o)
    n_ff_blocks = D_FF // FFN_COLS
    cw = conv_w[0]
    slopes_lane = np.broadcast_to(SLOPES[:, :, None], (N_GROUPS, HEADS, HEAD_DIM))
    slopes_band = jnp.asarray(np.pad(slopes_lane.transpose(1, 0, 2), ((0, 0), (0, 8 - N_GROUPS), (0, 0))))
    slopes16 = jnp.asarray(np.tile(slopes_lane, (1, ROWS16 // HEADS, 1)))

    rows_s = nb * SAMPLE_PAD
    xs = jnp.pad(x_sample, ((0, 0), (0, SAMPLE_PAD - t_new), (0, 0))).reshape(rows_s, D_MODEL)
    s1, s1b, w1g, w1u, w1d = _ffn(xs, w_ffn1_gu[0], w_ffn1_gu[0], w_ffn1_down[0], ln1_g, ln1_b,
                                  tm=rows_s, tf=FFN_COLS, up_blk0=n_ff_blocks, emit_bf16=True)
    ps_all, win_b = _matmul(s1b, w_in[0], tm=rows_s, tn=PROJ_COLS, out_dtype=F32, name="in_proj_s", emit_bf16=True)

    def tiles(state):
        win = state.shape[2]
        per_tile = 4 if win <= DIL_GROUPS[1][0] else DIL_GROUPS[2][1]
        return state.reshape(nb, win // per_tile, per_tile * HEADS, HEAD_DIM)

    k_states = [tiles(s) for s in (state_k_w128, state_k_w512, state_k_w2048)]
    v_states = [tiles(s) for s in (state_v_w128, state_v_w512, state_v_w2048)]
    h1, h1b, *shifted = _ffn(x_prompt[0], w1g, w1u, w1d, ln1_g, ln1_b, tm=FFN_ROWS, tf=FFN_COLS, manual_rows=True,
                             windows=k_states + v_states)
    shifted = [shifted[g + part * N_GROUPS] for g in range(N_GROUPS) for part in (0, 1)]
    p_all, qkv = _in_proj(h1b, win_b, tm=PROJ_ROWS, tn=PROJ_COLS)
    mem_kv = _matmul(mem_prompt[0].astype(BF16), w_mem_kv[0], tm=MEM_TOKENS, tn=GROUP_W, out_dtype=F32, name="mem_kv")
    o_attn = _band_attention(qkv, slopes_band)
    o_mem = _mem_attention(p_all, mem_kv, tq=MEM_Q_ROWS)
    h2, conv_p = _merge(None, o_attn, o_mem, p_all, h1, wc_b, wa_b, wm_b, wo_b, ln2_g, ln2_b, tm=MERGE_ROWS, conv_w=cw)

    prompt_states = []
    for g, (win, _) in enumerate(DIL_GROUPS):
        for tensor in (1, 2):
            c0 = tensor * N_GROUPS * GROUP_W + g * GROUP_W
            prompt_states.append(qkv[seq - win:, c0:c0 + GROUP_W].reshape(1, 1, win, HEADS, HEAD_DIM))
    mem_k_p = mem_kv[:, :GROUP_W].reshape(1, 1, MEM_TOKENS, HEADS, HEAD_DIM)
    mem_v_p = mem_kv[:, GROUP_W:].reshape(1, 1, MEM_TOKENS, HEADS, HEAD_DIM)

    ps3 = ps_all.reshape(nb, SAMPLE_PAD, IN_W)

    real = ps3[:, :t_new]
    heads = lambda a, n: a.reshape(nb, t_new, n, HEADS, HEAD_DIM)
    q, k, v = (heads(real[..., c:c + N_GROUPS * GROUP_W], N_GROUPS) for c in (COL_Q, COL_K, COL_VA))
    rows_th = lambda a: a.transpose(0, 2, 1, 3, 4).reshape(nb, a.shape[2], ROWS16, HEAD_DIM)
    over_quad = lambda a: jnp.broadcast_to(a[:, :, None], (nb, t_new, ROWS16 // HEADS, HEADS, HEAD_DIM)
                                           ).reshape(nb, t_new, ROWS16, HEAD_DIM)
    qd = rows_th(q[:, :, 1:])
    qm = real[..., COL_QM:COL_QM + GROUP_W].reshape(nb, t_new, HEADS, HEAD_DIM)
    qt = jnp.stack([over_quad(q[:, :, 0]), over_quad(qm)], axis=1)
    kn, vn = rows_th(k), rows_th(v)

    outs = _sample_attention(qd, qt, kn, vn, slopes16, k_states, v_states, shifted,
                             tiles(cache_mem_k), tiles(cache_mem_v), t_new=t_new)
    pad_rows = lambda a, w: jnp.pad(a.reshape(nb, t_new, w), ((0, 0), (0, SAMPLE_PAD - t_new), (0, 0))
                                    ).reshape(rows_s, w).astype(BF16)
    o_attn_s, o_mem_s = pad_rows(outs[0], GROUP_W), pad_rows(outs[1], GROUP_W)
    sample_states = [s.reshape(1, nb, -1, HEADS, HEAD_DIM) for s in outs[2:]]

    o_conv_s, conv_s = _conv(ps3, state_conv[0], cw, tm=SAMPLE_PAD, t_last=t_new)
    s2 = _merge(o_conv_s.reshape(rows_s, CONV_W), o_attn_s, o_mem_s, ps_all, s1,
                wc_b, wa_b, wm_b, wo_b, ln2_g, ln2_b, tm=rows_s)
    y_s, w2g, w2u, w2d = _ffn(s2, w_ffn2_gu[0], w_ffn2_gu[0], w_ffn2_down[0], ln3_g, ln3_b,
                              tm=rows_s, tf=FFN_COLS, up_blk0=n_ff_blocks, emit_bf16=True, bf16_out=False)
    y_sample = y_s.reshape(nb, SAMPLE_PAD, D_MODEL)[:, :t_new]
    (y_prompt,) = _ffn(h2, w2g, w2u, w2d, ln3_g, ln3_b, tm=FFN_ROWS, tf=FFN_COLS, bf16_out=False, row_buffers=2)

    return (y_prompt[None], y_sample, conv_p[None, None], *prompt_states, mem_k_p, mem_v_p,
            conv_s[None], *sample_states)
```
